```python
import jax, jax.numpy as jnp
from jax import lax
import numpy as np

D_MODEL = 1024
BATCH = 8
SEQ = 8192
DEPTH = 2

HEAD_DIM = 64
D_MIX = D_MODEL
D_SB = D_MIX // 2
SB_HEADS = D_SB // HEAD_DIM
D_CONV = D_MIX // 4
CONV_GROUPS = D_CONV // HEAD_DIM
CONV_WIDTH = 3
D_POOL = D_MIX - D_SB - D_CONV
POOL_WINDOWS = (2, 4, 8, 16)
POOL_GROUPS = len(POOL_WINDOWS)
POOL_GROUP_DIM = D_POOL // POOL_GROUPS
D_IN = 3 * D_SB + 3 * D_CONV + D_POOL
SPLITS = (D_SB, 2 * D_SB, 3 * D_SB, 3 * D_SB + D_CONV, 3 * D_SB + 2 * D_CONV, 3 * D_SB + 3 * D_CONV)
D_FF = 4 * D_MODEL
Q_BLOCK = 128
DEEPNORM_ALPHA = (2 * DEPTH) ** 0.25
DEEPNORM_BETA = (8 * DEPTH) ** -0.25
LN_EPS = 1e-5
RMS_EPS = 1e-6

kernel_name = "hybrid_sb_attn_shortconv_pool_deepnorm"


def layer_norm(x, g, b):
    xf = x.astype(jnp.float32)
    mu = jnp.mean(xf, axis=-1, keepdims=True)
    xc = xf - mu
    var = jnp.mean(xc * xc, axis=-1, keepdims=True)
    y = xc * lax.rsqrt(var + LN_EPS) * g.astype(jnp.float32) + b.astype(jnp.float32)
    return y.astype(x.dtype)


def head_group_rmsnorm(o, gain):
    B, S, C = o.shape
    of = o.astype(jnp.float32).reshape(B, S, C // HEAD_DIM, HEAD_DIM)
    of = of * lax.rsqrt(jnp.mean(of * of, axis=-1, keepdims=True) + RMS_EPS)
    return (of.reshape(B, S, C) * gain.astype(jnp.float32)).astype(o.dtype)


def stick_breaking_attention(q, k, v):
    B, S, H, Dh = q.shape
    dtype = q.dtype
    scale = Dh ** -0.5
    qf = q.astype(jnp.float32).transpose(0, 2, 1, 3)
    kf = k.astype(jnp.float32).transpose(0, 2, 1, 3)
    vf = v.astype(jnp.float32).transpose(0, 2, 1, 3)
    outs = []
    for start in range(0, S, Q_BLOCK):
        end = start + Q_BLOCK
        qb = qf[:, :, start:end]
        kb = kf[:, :, :end]
        vb = vf[:, :, :end]
        z = jnp.einsum('bhtd,bhsd->bhts', qb, kb) * scale
        t_pos = jnp.arange(start, end)[:, None]
        s_pos = jnp.arange(end)[None, :]
        mask = s_pos < t_pos
        log_om = jnp.where(mask, jax.nn.log_sigmoid(-z), 0.0)
        tail = lax.cumsum(log_om, axis=3, reverse=True) - log_om
        log_a = jax.nn.log_sigmoid(z) + tail
        a = jnp.where(mask, jnp.exp(log_a), 0.0)
        outs.append(jnp.einsum('bhts,bhsd->bhtd', a, vb))
    o = jnp.concatenate(outs, axis=2)
    return o.transpose(0, 2, 1, 3).reshape(B, S, H * Dh).astype(dtype)


def short_conv_mixer(b_gate, c_gate, h, conv_w):
    u = c_gate * h
    S = u.shape[1]
    u_pad = jnp.pad(u, ((0, 0), (CONV_WIDTH - 1, 0), (0, 0)))
    y = u_pad[:, 0:S] * conv_w[0]
    for i in range(1, CONV_WIDTH):
        y = y + u_pad[:, i:i + S] * conv_w[i]
    return b_gate * y


def multiscale_pool_mixer(p, pool_w, pool_scale):
    B, S, C = p.shape
    dtype = p.dtype
    pg = p.astype(jnp.float32).reshape(B, S, POOL_GROUPS, POOL_GROUP_DIM)
    cs = jnp.pad(jnp.cumsum(pg, axis=1), ((0, 0), (1, 0), (0, 0), (0, 0)))
    pos = jnp.arange(S)
    outs = []
    for g, w in enumerate(POOL_WINDOWS):
        lo = jnp.maximum(pos + 1 - w, 0)
        window_sum = cs[:, 1:, g] - cs[:, lo, g]
        count = (pos + 1 - lo).astype(jnp.float32)[None, :, None]
        outs.append(window_sum / count - pg[:, :, g])
    pooled = jnp.stack(outs, axis=2)
    y = jnp.einsum('bsgc,gcd->bsgd', pooled, pool_w.astype(jnp.float32))
    y = y.reshape(B, S, C) * pool_scale.astype(jnp.float32)
    return y.astype(dtype)


def _fwd_setup_inputs(seed: int = 0) -> dict:
    key = jax.random.key(seed)
    ks = jax.random.split(key, 16)
    f32 = jnp.float32
    x = jax.random.normal(ks[0], (BATCH, SEQ, D_MODEL), f32)
    w_in = jax.random.normal(ks[1], (DEPTH, D_MODEL, D_IN), f32) * D_MODEL ** -0.5
    conv_w = jax.random.normal(ks[2], (DEPTH, CONV_WIDTH, D_CONV), f32) * CONV_WIDTH ** -0.5
    pool_w = jax.random.normal(ks[3], (DEPTH, POOL_GROUPS, POOL_GROUP_DIM, POOL_GROUP_DIM), f32) * POOL_GROUP_DIM ** -0.5
    pool_scale = 1.0 + 0.1 * jax.random.normal(ks[4], (DEPTH, D_POOL), f32)
    mix_norm_g = 1.0 + 0.02 * jax.random.normal(ks[5], (DEPTH, D_MIX), f32)
    w_o = jax.random.normal(ks[6], (DEPTH, D_MIX, D_MODEL), f32) * (D_MIX ** -0.5) * DEEPNORM_BETA
    ln1_g = 1.0 + 0.02 * jax.random.normal(ks[7], (DEPTH, D_MODEL), f32)
    ln1_b = 0.02 * jax.random.normal(ks[8], (DEPTH, D_MODEL), f32)
    w_up = jax.random.normal(ks[9], (DEPTH, D_MODEL, D_FF), f32) * D_MODEL ** -0.5
    w_down = jax.random.normal(ks[10], (DEPTH, D_FF, D_MODEL), f32) * (D_FF ** -0.5) * DEEPNORM_BETA
    ln2_g = 1.0 + 0.02 * jax.random.normal(ks[11], (DEPTH, D_MODEL), f32)
    ln2_b = 0.02 * jax.random.normal(ks[12], (DEPTH, D_MODEL), f32)
    return {"x": x, "w_in": w_in, "conv_w": conv_w, "pool_w": pool_w,
            "pool_scale": pool_scale, "mix_norm_g": mix_norm_g, "w_o": w_o,
            "ln1_g": ln1_g, "ln1_b": ln1_b, "w_up": w_up, "w_down": w_down,
            "ln2_g": ln2_g, "ln2_b": ln2_b}


def _fwd_reference(x, w_in, conv_w, pool_w, pool_scale, mix_norm_g, w_o,
              ln1_g, ln1_b, w_up, w_down, ln2_g, ln2_b):
    B, S, _ = x.shape
    for l in range(DEPTH):
        proj = jnp.einsum('bsd,de->bse', x, w_in[l])
        q, k, v, b_gate, c_gate, h, p = jnp.split(proj, SPLITS, axis=-1)
        q = q.reshape(B, S, SB_HEADS, HEAD_DIM)
        k = k.reshape(B, S, SB_HEADS, HEAD_DIM)
        v = v.reshape(B, S, SB_HEADS, HEAD_DIM)
        attn_out = stick_breaking_attention(q, k, v)
        conv_out = short_conv_mixer(b_gate, c_gate, h, conv_w[l])
        pool_out = multiscale_pool_mixer(p, pool_w[l], pool_scale[l])
        mix = jnp.concatenate([attn_out, conv_out, pool_out], axis=-1)
        mix = head_group_rmsnorm(mix, mix_norm_g[l])
        mix = jnp.einsum('bse,ed->bsd', mix, w_o[l])
        x = layer_norm(DEEPNORM_ALPHA * x + mix, ln1_g[l], ln1_b[l])
        hid = jnp.square(jax.nn.relu(jnp.einsum('bsd,df->bsf', x, w_up[l])))
        ff = jnp.einsum('bsf,fd->bsd', hid, w_down[l])
        x = layer_norm(DEEPNORM_ALPHA * x + ff, ln2_g[l], ln2_b[l])
    return x


import jax as _jax
import jax.numpy as _jnp

TWIN_FORMAT = 'train_step'
FWD_PARAMS = ['x', 'w_in', 'conv_w', 'pool_w', 'pool_scale', 'mix_norm_g', 'w_o', 'ln1_g', 'ln1_b', 'w_up', 'w_down', 'ln2_g', 'ln2_b']
TWIN_WEIGHTS = ['w_in', 'conv_w', 'pool_w', 'pool_scale', 'mix_norm_g', 'w_o', 'ln1_g', 'ln1_b', 'w_up', 'w_down', 'ln2_g', 'ln2_b']
TWIN_DIFF_INPUT = 'x'
TWIN_INPUTS = ['x', 'w_in', 'conv_w', 'pool_w', 'pool_scale', 'mix_norm_g', 'w_o', 'ln1_g', 'ln1_b', 'w_up', 'w_down', 'ln2_g', 'ln2_b', 'loss_target', 'm_w_in', 'm_conv_w', 'm_pool_w', 'm_pool_scale', 'm_mix_norm_g', 'm_w_o', 'm_ln1_g', 'm_ln1_b', 'm_w_up', 'm_w_down', 'm_ln2_g', 'm_ln2_b', 'v_w_in', 'v_conv_w', 'v_pool_w', 'v_pool_scale', 'v_mix_norm_g', 'v_w_o', 'v_ln1_g', 'v_ln1_b', 'v_w_up', 'v_w_down', 'v_ln2_g', 'v_ln2_b']
TWIN_OUTPUTS = ['loss', 'grad_x', 'grad_w_in', 'grad_conv_w', 'grad_pool_w', 'grad_pool_scale', 'grad_mix_norm_g', 'grad_w_o', 'grad_ln1_g', 'grad_ln1_b', 'grad_w_up', 'grad_w_down', 'grad_ln2_g', 'grad_ln2_b', 'delta_w_in', 'delta_conv_w', 'delta_pool_w', 'delta_pool_scale', 'delta_mix_norm_g', 'delta_w_o', 'delta_ln1_g', 'delta_ln1_b', 'delta_w_up', 'delta_w_down', 'delta_ln2_g', 'delta_ln2_b', 'new_m_w_in', 'new_m_conv_w', 'new_m_pool_w', 'new_m_pool_scale', 'new_m_mix_norm_g', 'new_m_w_o', 'new_m_ln1_g', 'new_m_ln1_b', 'new_m_w_up', 'new_m_w_down', 'new_m_ln2_g', 'new_m_ln2_b', 'new_v_w_in', 'new_v_conv_w', 'new_v_pool_w', 'new_v_pool_scale', 'new_v_mix_norm_g', 'new_v_w_o', 'new_v_ln1_g', 'new_v_ln1_b', 'new_v_w_up', 'new_v_w_down', 'new_v_ln2_g', 'new_v_ln2_b']
TWIN_LEAF_KINDS = {'loss': 'loss', 'grad_x': 'grad_x', 'grad_w_in': 'grad_w', 'grad_conv_w': 'grad_w', 'grad_pool_w': 'grad_w', 'grad_pool_scale': 'grad_w', 'grad_mix_norm_g': 'grad_w', 'grad_w_o': 'grad_w', 'grad_ln1_g': 'grad_w', 'grad_ln1_b': 'grad_w', 'grad_w_up': 'grad_w', 'grad_w_down': 'grad_w', 'grad_ln2_g': 'grad_w', 'grad_ln2_b': 'grad_w', 'delta_w_in': 'delta_w', 'delta_conv_w': 'delta_w', 'delta_pool_w': 'delta_w', 'delta_pool_scale': 'delta_w', 'delta_mix_norm_g': 'delta_w', 'delta_w_o': 'delta_w', 'delta_ln1_g': 'delta_w', 'delta_ln1_b': 'delta_w', 'delta_w_up': 'delta_w', 'delta_w_down': 'delta_w', 'delta_ln2_g': 'delta_w', 'delta_ln2_b': 'delta_w', 'new_m_w_in': 'new_m', 'new_m_conv_w': 'new_m', 'new_m_pool_w': 'new_m', 'new_m_pool_scale': 'new_m', 'new_m_mix_norm_g': 'new_m', 'new_m_w_o': 'new_m', 'new_m_ln1_g': 'new_m', 'new_m_ln1_b': 'new_m', 'new_m_w_up': 'new_m', 'new_m_w_down': 'new_m', 'new_m_ln2_g': 'new_m', 'new_m_ln2_b': 'new_m', 'new_v_w_in': 'new_v', 'new_v_conv_w': 'new_v', 'new_v_pool_w': 'new_v', 'new_v_pool_scale': 'new_v', 'new_v_mix_norm_g': 'new_v', 'new_v_w_o': 'new_v', 'new_v_ln1_g': 'new_v', 'new_v_ln1_b': 'new_v', 'new_v_w_up': 'new_v', 'new_v_w_down': 'new_v', 'new_v_ln2_g': 'new_v', 'new_v_ln2_b': 'new_v'}


def _forward(args):
    return _fwd_reference(*[args[k] for k in FWD_PARAMS])


def _output_shape():
    def fwd():
        inp = _fwd_setup_inputs(0)
        return _fwd_reference(*[inp[k] for k in FWD_PARAMS])
    out = _jax.eval_shape(fwd)
    return out.shape, out.dtype

N_MICROBATCH = 1
ADAM_LR = 0.001
ADAM_B1 = 0.9
ADAM_B2 = 0.999
ADAM_EPS = 1e-08
ADAM_WD = 0.01
ADAM_STEP = 10
PER_EXAMPLE_BATCH_AXIS = {'x': 0, 'loss_target': 0}
SHARED_INPUTS = []
_WEIGHT_DTYPES = {'w_in': _jnp.float32, 'conv_w': _jnp.float32, 'pool_w': _jnp.float32, 'pool_scale': _jnp.float32, 'mix_norm_g': _jnp.float32, 'w_o': _jnp.float32, 'ln1_g': _jnp.float32, 'ln1_b': _jnp.float32, 'w_up': _jnp.float32, 'w_down': _jnp.float32, 'ln2_g': _jnp.float32, 'ln2_b': _jnp.float32}
MOMENT_SCALE = {'w_in': 7.791751e-02, 'conv_w': 9.320673e-02, 'pool_w': 9.083764e-02, 'pool_scale': 9.247687e-02, 'mix_norm_g': 9.435926e-02, 'w_o': 1.870706e-01, 'ln1_g': 1.541934e+00, 'ln1_b': 9.857739e-01, 'w_up': 6.084944e-02, 'w_down': 3.328510e-01, 'ln2_g': 4.542335e+01, 'ln2_b': 9.928628e+00}


def _to_microbatches(a, axis):
    t = _jnp.moveaxis(a, axis, 0)
    t = t.reshape((N_MICROBATCH, t.shape[0] // N_MICROBATCH) + t.shape[1:])
    return _jnp.moveaxis(t, 1, axis + 1)


def setup_inputs(seed: int = 0) -> dict:
    inp = _fwd_setup_inputs(seed)
    key = _jax.random.fold_in(_jax.random.key(seed), 7919)
    shape, _ = _output_shape()
    out = dict(inp)
    out["loss_target"] = _jax.random.normal(_jax.random.fold_in(key, 0), shape, _jnp.float32)
    for i, name in enumerate(TWIN_WEIGHTS):
        w = inp[name].astype(_jnp.float32)
        if MOMENT_SCALE is None:
            s = _jnp.sqrt(_jnp.mean(_jnp.square(w)) + 1e-30)
        else:
            s = MOMENT_SCALE[name]
        km, kv = _jax.random.split(_jax.random.fold_in(key, i + 1))
        out[name] = w
        out["m_" + name] = s * _jax.random.normal(km, w.shape, _jnp.float32)
        out["v_" + name] = (s * s) * _jax.random.uniform(kv, w.shape, _jnp.float32, 0.5, 1.5)
    if N_MICROBATCH > 1:
        for name, axis in PER_EXAMPLE_BATCH_AXIS.items():
            out[name] = _to_microbatches(out[name], axis)
    return {'x': out['x'], 'w_in': out['w_in'], 'conv_w': out['conv_w'], 'pool_w': out['pool_w'], 'pool_scale': out['pool_scale'], 'mix_norm_g': out['mix_norm_g'], 'w_o': out['w_o'], 'ln1_g': out['ln1_g'], 'ln1_b': out['ln1_b'], 'w_up': out['w_up'], 'w_down': out['w_down'], 'ln2_g': out['ln2_g'], 'ln2_b': out['ln2_b'], 'loss_target': out['loss_target'], 'm_w_in': out['m_w_in'], 'm_conv_w': out['m_conv_w'], 'm_pool_w': out['m_pool_w'], 'm_pool_scale': out['m_pool_scale'], 'm_mix_norm_g': out['m_mix_norm_g'], 'm_w_o': out['m_w_o'], 'm_ln1_g': out['m_ln1_g'], 'm_ln1_b': out['m_ln1_b'], 'm_w_up': out['m_w_up'], 'm_w_down': out['m_w_down'], 'm_ln2_g': out['m_ln2_g'], 'm_ln2_b': out['m_ln2_b'], 'v_w_in': out['v_w_in'], 'v_conv_w': out['v_conv_w'], 'v_pool_w': out['v_pool_w'], 'v_pool_scale': out['v_pool_scale'], 'v_mix_norm_g': out['v_mix_norm_g'], 'v_w_o': out['v_w_o'], 'v_ln1_g': out['v_ln1_g'], 'v_ln1_b': out['v_ln1_b'], 'v_w_up': out['v_w_up'], 'v_w_down': out['v_w_down'], 'v_ln2_g': out['v_ln2_g'], 'v_ln2_b': out['v_ln2_b']}


def _loss(weights, diff, rest, loss_target):
    with _jax.named_scope("forward"):
        args = {**rest, TWIN_DIFF_INPUT: diff, **{k: w.astype(_WEIGHT_DTYPES[k]) for k, w in weights.items()}}
        y = _forward(args)
    with _jax.named_scope("loss_head"):
        err = _jnp.square(y.astype(_jnp.float32) - loss_target)
        return 0.5 * _jnp.sum(_jnp.mean(err, axis=-1)) if err.ndim else 0.5 * err


def _adamw(w, g, m, v):
    m = ADAM_B1 * m + (1.0 - ADAM_B1) * g
    v = ADAM_B2 * v + (1.0 - ADAM_B2) * _jnp.square(g)
    m_hat = m / (1.0 - ADAM_B1 ** ADAM_STEP)
    v_hat = v / (1.0 - ADAM_B2 ** ADAM_STEP)
    delta = -ADAM_LR * (m_hat / (_jnp.sqrt(v_hat) + ADAM_EPS) + ADAM_WD * w)
    return delta, m, v


def reference(x, w_in, conv_w, pool_w, pool_scale, mix_norm_g, w_o, ln1_g, ln1_b, w_up, w_down, ln2_g, ln2_b, loss_target, m_w_in, m_conv_w, m_pool_w, m_pool_scale, m_mix_norm_g, m_w_o, m_ln1_g, m_ln1_b, m_w_up, m_w_down, m_ln2_g, m_ln2_b, v_w_in, v_conv_w, v_pool_w, v_pool_scale, v_mix_norm_g, v_w_o, v_ln1_g, v_ln1_b, v_w_up, v_w_down, v_ln2_g, v_ln2_b):
    given = dict(x=x, w_in=w_in, conv_w=conv_w, pool_w=pool_w, pool_scale=pool_scale, mix_norm_g=mix_norm_g, w_o=w_o, ln1_g=ln1_g, ln1_b=ln1_b, w_up=w_up, w_down=w_down, ln2_g=ln2_g, ln2_b=ln2_b, loss_target=loss_target, m_w_in=m_w_in, m_conv_w=m_conv_w, m_pool_w=m_pool_w, m_pool_scale=m_pool_scale, m_mix_norm_g=m_mix_norm_g, m_w_o=m_w_o, m_ln1_g=m_ln1_g, m_ln1_b=m_ln1_b, m_w_up=m_w_up, m_w_down=m_w_down, m_ln2_g=m_ln2_g, m_ln2_b=m_ln2_b, v_w_in=v_w_in, v_conv_w=v_conv_w, v_pool_w=v_pool_w, v_pool_scale=v_pool_scale, v_mix_norm_g=v_mix_norm_g, v_w_o=v_w_o, v_ln1_g=v_ln1_g, v_ln1_b=v_ln1_b, v_w_up=v_w_up, v_w_down=v_w_down, v_ln2_g=v_ln2_g, v_ln2_b=v_ln2_b)
    weights = {n: given[n] for n in TWIN_WEIGHTS}
    shared = {n: given[n] for n in SHARED_INPUTS}
    per_example = {n: given[n] for n in ['x']}
    grad_fn = _jax.value_and_grad(_loss, argnums=(0, 1))

    def one_microbatch(ex, loss_target):
        ex = dict(ex)
        diff = ex.pop(TWIN_DIFF_INPUT)
        return grad_fn(weights, diff, {**shared, **ex}, loss_target)

    if N_MICROBATCH == 1:
        loss, (grad_w, grad_x) = one_microbatch(per_example, given["loss_target"])
    else:
        def body(carry, xs):
            loss_sum, grad_sum = carry
            l_k, (gw_k, gx_k) = one_microbatch(xs[0], xs[1])
            with _jax.named_scope("update"):
                return (loss_sum + l_k, _jax.tree.map(_jnp.add, grad_sum, gw_k)), gx_k

        init = (_jnp.zeros((), _jnp.float32), _jax.tree.map(_jnp.zeros_like, weights))
        (loss, grad_w), grad_x = _jax.lax.scan(body, init, (per_example, given["loss_target"]))
    with _jax.named_scope("update"):
        delta_w, new_m, new_v = {}, {}, {}
        for n in TWIN_WEIGHTS:
            delta_w[n], new_m[n], new_v[n] = _adamw(weights[n], grad_w[n], given["m_" + n], given["v_" + n])
    return (loss, grad_x, *[grad_w[n] for n in TWIN_WEIGHTS], *[delta_w[n] for n in TWIN_WEIGHTS],
            *[new_m[n] for n in TWIN_WEIGHTS], *[new_v[n] for n in TWIN_WEIGHTS])
```

```python
import functools
import math

import jax
import jax.numpy as jnp
from jax import lax
from jax.experimental import pallas as pl
from jax.experimental.pallas import tpu as pltpu

F32 = jnp.float32
BF16 = jnp.bfloat16
MESH = pl.DeviceIdType.MESH

D_MODEL = 1024
DEPTH = 2
HEAD_DIM = 64
D_SB = 512
D_CONV = 256
D_POOL = 256
D_QKV = 3 * D_SB
D_REST = 3 * D_CONV + D_POOL
D_IN = D_QKV + D_REST
D_FF = 4 * D_MODEL
ALPHA = (2 * DEPTH) ** 0.25
LN_EPS = 1e-5
RMS_EPS = 1e-6
SCALE = HEAD_DIM ** -0.5
N_CHIPS = 4
HALO = 16

ADAM_LR = 0.001
ADAM_B1 = 0.9
ADAM_B2 = 0.999
ADAM_EPS = 1e-08
ADAM_WD = 0.01
ADAM_STEP = 10

VMEM_V7X_BYTES = 64 * 1024 * 1024
VMEM_CAP_BYTES = VMEM_V7X_BYTES - 8 * 1024 * 1024


def _params(sem, block_bytes):
    limit = min(VMEM_CAP_BYTES, max(32 * 1024 * 1024, 3 * block_bytes))
    return pltpu.CompilerParams(dimension_semantics=sem, vmem_limit_bytes=limit)


def _nbytes(shape, dtype):
    return math.prod(shape) * jnp.dtype(dtype).itemsize


def _dot(a, b, dims=(((1,), (0,)), ((), ()))):
    return lax.dot_general(a, b, dims, preferred_element_type=F32)


NT = (((1,), (1,)), ((), ()))
TN = (((0,), (0,)), ((), ()))


def _split(x):
    hi = x.astype(BF16)
    lo = (x - hi.astype(F32)).astype(BF16)
    return hi, lo


def _sum8(x):
    r, c = x.shape
    return x.reshape(r // 8, 8, c).sum(axis=0)


def _matmul(a, b, *, name, tm, tn, tk, ta=False, tb=False, out_dtype=F32,
            a_pro=None, epi=None, e=None, e_scale=1.0):
    M, K = (a.shape[1], a.shape[0]) if ta else a.shape
    N = b.shape[0] if tb else b.shape[1]
    tm, tn, tk = min(tm, M), min(tn, N), min(tk, K)
    assert M % tm == 0 and N % tn == 0 and K % tk == 0, (name, M, N, K)
    nk = K // tk
    dims = (((0 if ta else 1,), (1 if tb else 0,)), ((), ()))

    def body(*refs):
        if epi is None:
            a_ref, b_ref, o_ref, *scr = refs
            e_ref = None
        else:
            a_ref, b_ref, e_ref, o_ref, *scr = refs
        av = a_ref[...]
        if a_pro == "relu2":
            av = jnp.square(jnp.maximum(av.astype(F32), 0.0))
        p = _dot(av.astype(BF16), b_ref[...].astype(BF16), dims)

        def finish(acc):
            if epi == "drelu2":
                acc = acc * (2.0 * jnp.maximum(e_ref[...], 0.0))
            elif epi == "add":
                acc = acc + e_scale * e_ref[...]
            o_ref[...] = acc.astype(out_dtype)

        if nk == 1:
            finish(p)
        else:
            acc_ref = scr[0]
            k = pl.program_id(2)

            @pl.when(k == 0)
            def _():
                acc_ref[...] = p

            @pl.when(k > 0)
            def _():
                acc_ref[...] += p

            @pl.when(k == nk - 1)
            def _():
                finish(acc_ref[...])

    a_spec = pl.BlockSpec((tk, tm), lambda i, j, k: (k, i)) if ta else pl.BlockSpec((tm, tk), lambda i, j, k: (i, k))
    b_spec = pl.BlockSpec((tn, tk), lambda i, j, k: (j, k)) if tb else pl.BlockSpec((tk, tn), lambda i, j, k: (k, j))
    o_spec = pl.BlockSpec((tm, tn), lambda i, j, k: (i, j))
    in_specs = [a_spec, b_spec]
    args = [a, b]
    nbytes = _nbytes((tm, tk), a.dtype) + _nbytes((tk, tn), b.dtype) + 2 * _nbytes((tm, tn), F32)
    if epi is not None:
        in_specs.append(o_spec)
        args.append(e)
        nbytes += _nbytes((tm, tn), e.dtype)
    scratch = [pltpu.VMEM((tm, tn), F32)] if nk > 1 else []
    return pl.pallas_call(
        body, name=name,
        grid=(M // tm, N // tn, nk),
        in_specs=in_specs, out_specs=o_spec,
        out_shape=jax.ShapeDtypeStruct((M, N), out_dtype),
        scratch_shapes=scratch,
        compiler_params=_params(("parallel", "parallel", "arbitrary"), nbytes),
    )(*args)


def _matmul_ln(a, b, xres, g, bias, *, name, tm, tk, a_pro=None):
    M, K = a.shape
    N = b.shape[1]
    tm, tk = min(tm, M), min(tk, K)
    assert M % tm == 0 and K % tk == 0 and N == D_MODEL
    nk = K // tk

    def body(a_ref, b_ref, x_ref, g_ref, bias_ref, y_ref, xh_ref, rs_ref, *scr):
        av = a_ref[...]
        if a_pro == "relu2":
            av = jnp.square(jnp.maximum(av.astype(F32), 0.0))
        p = _dot(av.astype(BF16), b_ref[...].astype(BF16))

        def finish(acc):
            r = ALPHA * x_ref[...] + acc
            mu = jnp.mean(r, axis=-1, keepdims=True)
            xc = r - mu
            var = jnp.mean(xc * xc, axis=-1, keepdims=True)
            rstd = lax.rsqrt(var + LN_EPS)
            xh = xc * rstd
            y_ref[...] = xh * g_ref[...] + bias_ref[...]
            xh_ref[...] = xh
            rs_ref[...] = rstd

        if nk == 1:
            finish(p)
        else:
            acc_ref = scr[0]
            k = pl.program_id(1)

            @pl.when(k == 0)
            def _():
                acc_ref[...] = p

            @pl.when(k > 0)
            def _():
                acc_ref[...] += p

            @pl.when(k == nk - 1)
            def _():
                finish(acc_ref[...])

    row = pl.BlockSpec((tm, N), lambda i, k: (i, 0))
    vec = pl.BlockSpec((1, N), lambda i, k: (0, 0))
    nbytes = _nbytes((tm, tk), a.dtype) + _nbytes((tk, N), b.dtype) + 5 * _nbytes((tm, N), F32)
    scratch = [pltpu.VMEM((tm, N), F32)] if nk > 1 else []
    return pl.pallas_call(
        body, name=name,
        grid=(M // tm, nk),
        in_specs=[pl.BlockSpec((tm, tk), lambda i, k: (i, k)), pl.BlockSpec((tk, N), lambda i, k: (k, 0)), row, vec, vec],
        out_specs=[row, row, pl.BlockSpec((tm, 1), lambda i, k: (i, 0))],
        out_shape=[jax.ShapeDtypeStruct((M, N), F32), jax.ShapeDtypeStruct((M, N), F32),
                   jax.ShapeDtypeStruct((M, 1), F32)],
        scratch_shapes=scratch,
        compiler_params=_params(("parallel", "arbitrary"), nbytes),
    )(a, b, xres, g.reshape(1, N), bias.reshape(1, N))


def _ln_bwd(dy, xhat, rstd, g, *, name, tm):
    M, N = dy.shape
    tm = min(tm, M)

    def body(dy_ref, xh_ref, rs_ref, g_ref, dr_ref, dg_ref, db_ref):
        i = pl.program_id(0)
        dyv = dy_ref[...]
        xh = xh_ref[...]
        dxh = dyv * g_ref[...]
        m1 = jnp.mean(dxh, axis=-1, keepdims=True)
        m2 = jnp.mean(dxh * xh, axis=-1, keepdims=True)
        dr_ref[...] = rs_ref[...] * (dxh - m1 - xh * m2)
        pg = _sum8(dyv * xh)
        pb = _sum8(dyv)

        @pl.when(i == 0)
        def _():
            dg_ref[...] = pg
            db_ref[...] = pb

        @pl.when(i > 0)
        def _():
            dg_ref[...] += pg
            db_ref[...] += pb

    row = pl.BlockSpec((tm, N), lambda i: (i, 0))
    acc = pl.BlockSpec((8, N), lambda i: (0, 0))
    return pl.pallas_call(
        body, name=name, grid=(M // tm,),
        in_specs=[row, row, pl.BlockSpec((tm, 1), lambda i: (i, 0)), pl.BlockSpec((1, N), lambda i: (0, 0))],
        out_specs=[row, acc, acc],
        out_shape=[jax.ShapeDtypeStruct((M, N), F32), jax.ShapeDtypeStruct((8, N), F32),
                   jax.ShapeDtypeStruct((8, N), F32)],
        compiler_params=_params(("arbitrary",), 4 * _nbytes((tm, N), F32)),
    )(dy, xhat, rstd, g.reshape(1, N))


def _loss_grad(y, tgt, *, name, tm):
    M, N = y.shape
    tm = min(tm, M)

    def body(y_ref, t_ref, dy_ref, l_ref):
        i = pl.program_id(0)
        d = y_ref[...] - t_ref[...]
        dy_ref[...] = d * (1.0 / N)
        pl_ = _sum8(d * d) * (0.5 / N)

        @pl.when(i == 0)
        def _():
            l_ref[...] = pl_

        @pl.when(i > 0)
        def _():
            l_ref[...] += pl_

    row = pl.BlockSpec((tm, N), lambda i: (i, 0))
    return pl.pallas_call(
        body, name=name, grid=(M // tm,),
        in_specs=[row, row], out_specs=[row, pl.BlockSpec((8, N), lambda i: (0, 0))],
        out_shape=[jax.ShapeDtypeStruct((M, N), F32), jax.ShapeDtypeStruct((8, N), F32)],
        compiler_params=_params(("arbitrary",), 3 * _nbytes((tm, N), F32)),
    )(y, tgt)


def _tri(n, kind):
    j = lax.broadcasted_iota(jnp.int32, (n, n), 0)
    s = lax.broadcasted_iota(jnp.int32, (n, n), 1)
    return ((j > s) if kind == "after" else (j < s)).astype(BF16)


def _log_terms(z):
    lse = jnp.log(1.0 + jnp.exp(-jnp.abs(z)))
    logsig = jnp.minimum(z, 0.0) - lse
    return logsig, logsig - z


def _attn_fwd(qkv, *, name, tq, tk):
    S = qkv.shape[0]
    tq, tk = min(tq, S), min(tk, S)
    assert S % tq == 0 and tq % tk == 0 and S // tk <= 128
    r = tq // tk
    tri = _tri(tk, "after")

    def body(q_ref, k_ref, v_ref, u_ref, o_ref, c_ref, oacc, cacc, call):
        i = pl.program_id(1)
        lane = lax.broadcasted_iota(jnp.int32, (1, 128), 1)
        q2 = q_ref[...]
        zero = jnp.zeros_like(q2)
        qm = [jnp.where(lane < HEAD_DIM, q2, zero) * SCALE, jnp.where(lane >= HEAD_DIM, q2, zero) * SCALE]
        u = u_ref[...]
        oacc[...] = jnp.zeros_like(oacc)
        cacc[...] = jnp.zeros_like(cacc)
        call[...] = jnp.zeros_like(call)
        row = i * tq + lax.broadcasted_iota(jnp.int32, (tq, tk), 0)
        col = lax.broadcasted_iota(jnp.int32, (tq, tk), 1)

        def step(kb, masked):
            ks = pl.multiple_of(kb * tk, tk)
            kblk = k_ref[pl.ds(ks, tk), :]
            vblk = v_ref[pl.ds(ks, tk), :]
            for h in range(2):
                z = _dot(qm[h], kblk, NT)
                logsig, lom = _log_terms(z)
                if masked:
                    msk = (ks + col) < row
                    lom = jnp.where(msk, lom, 0.0)
                hi, lo = _split(lom)
                tl = _dot(hi, u) + _dot(lo, u)
                c = cacc[h]
                call[h] = jnp.where(lane == kb, c, call[h])
                loga = logsig + tl + c
                if masked:
                    loga = jnp.where(msk, loga, -1e30)
                a = jnp.exp(loga).astype(BF16)
                oacc[h] += _dot(a, vblk)
                cacc[h] = c + tl[:, 0:1] + lom[:, 0:1]

        for d in range(r):
            step(i * r + (r - 1 - d), True)

        def loop(t, carry):
            step(i * r - 1 - t, False)
            return carry

        lax.fori_loop(0, i * r, loop, 0)
        o_ref[...] = jnp.where(lane < HEAD_DIM, oacc[0], oacc[1])
        c_ref[...] = jnp.concatenate([call[0], call[1]], axis=1)

    nbytes = (_nbytes((tq, 128), BF16) + 2 * _nbytes((S, 128), BF16) + _nbytes((tk, tk), BF16)
              + 6 * _nbytes((tq, 128), F32) + 8 * _nbytes((tq, tk), F32))
    return pl.pallas_call(
        body, name=name, grid=(4, S // tq),
        in_specs=[pl.BlockSpec((tq, 128), lambda j, i: (i, j)),
                  pl.BlockSpec((S, 128), lambda j, i: (0, 4 + j)),
                  pl.BlockSpec((S, 128), lambda j, i: (0, 8 + j)),
                  pl.BlockSpec((tk, tk), lambda j, i: (0, 0))],
        out_specs=[pl.BlockSpec((tq, 128), lambda j, i: (i, j)),
                   pl.BlockSpec((tq, 256), lambda j, i: (i, j))],
        out_shape=[jax.ShapeDtypeStruct((S, D_SB), F32), jax.ShapeDtypeStruct((S, 1024), F32)],
        scratch_shapes=[pltpu.VMEM((2, tq, 128), F32), pltpu.VMEM((2, tq, 1), F32), pltpu.VMEM((2, tq, 128), F32)],
        compiler_params=_params(("parallel", "arbitrary"), nbytes),
    )(qkv, qkv, qkv, tri)


def _attn_bwd(qkv, carry, do, *, name, tq, tk):
    S = qkv.shape[0]
    tq, tk = min(tq, S), min(tk, S)
    assert S % tq == 0 and tq % tk == 0 and S // tk <= 128
    r = tq // tk
    nkb = S // tk
    nq = S // tq
    tri_after = _tri(tk, "after")
    tri_before = _tri(tk, "before")

    def body(q_ref, k_ref, v_ref, c_ref, do_ref, ua_ref, ub_ref, dq_ref, dk_ref, dv_ref, dqacc, pacc, dkt, dvt):
        i = pl.program_id(1)
        lane = lax.broadcasted_iota(jnp.int32, (1, 128), 1)
        sub = lax.broadcasted_iota(jnp.int32, (128, 1), 0)
        q2 = q_ref[...]
        zero = jnp.zeros_like(q2)
        qm = [jnp.where(lane < HEAD_DIM, q2, zero) * SCALE, jnp.where(lane >= HEAD_DIM, q2, zero) * SCALE]
        do2 = do_ref[...]
        dom = [jnp.where(lane < HEAD_DIM, do2, 0.0).astype(BF16), jnp.where(lane >= HEAD_DIM, do2, 0.0).astype(BF16)]
        qt = q2.astype(F32).T * SCALE
        dot_ = do2.T
        qmt = [jnp.where(sub < HEAD_DIM, qt, 0.0).astype(BF16), jnp.where(sub >= HEAD_DIM, qt, 0.0).astype(BF16)]
        domt = [jnp.where(sub < HEAD_DIM, dot_, 0.0).astype(BF16), jnp.where(sub >= HEAD_DIM, dot_, 0.0).astype(BF16)]
        ua = ua_ref[...]
        ub = ub_ref[...]
        call = [c_ref[:, 0:128], c_ref[:, 128:256]]

        @pl.when(i == 0)
        def _():
            dkt[...] = jnp.zeros_like(dkt)
            dvt[...] = jnp.zeros_like(dvt)

        dqacc[...] = jnp.zeros_like(dqacc)
        pacc[...] = jnp.zeros_like(pacc)
        row = i * tq + lax.broadcasted_iota(jnp.int32, (tq, tk), 0)
        col = lax.broadcasted_iota(jnp.int32, (tq, tk), 1)

        def step(kb, masked):
            ks = pl.multiple_of(kb * tk, tk)
            kblk = k_ref[pl.ds(ks, tk), :]
            vblk = v_ref[pl.ds(ks, tk), :]
            dk_new = dkt[kb]
            dv_new = dvt[kb]
            for h in range(2):
                z = _dot(qm[h], kblk, NT)
                logsig, lom = _log_terms(z)
                if masked:
                    msk = (ks + col) < row
                    lom = jnp.where(msk, lom, 0.0)
                hi, lo = _split(lom)
                c = jnp.sum(jnp.where(lane == kb, call[h], 0.0), axis=1, keepdims=True)
                loga = logsig + (_dot(hi, ua) + _dot(lo, ua)) + c
                if masked:
                    loga = jnp.where(msk, loga, -1e30)
                a = jnp.exp(loga)
                da = _dot(dom[h], vblk, NT)
                g = a * da
                ghi, glo = _split(g)
                pl_ = _dot(ghi, ub) + _dot(glo, ub)
                pc = pacc[h]
                sig = jnp.exp(logsig)
                dz = g - sig * (g + pl_ + pc)
                if masked:
                    dz = jnp.where(msk, dz, 0.0)
                dzb = dz.astype(BF16)
                ab = a.astype(BF16)
                dqacc[h] += _dot(dzb, kblk)
                dk_new = dk_new + _dot(qmt[h], dzb)
                dv_new = dv_new + _dot(domt[h], ab)
                pacc[h] = pc + pl_[:, tk - 1:tk] + g[:, tk - 1:tk]
            dkt[kb] = dk_new
            dvt[kb] = dv_new

        def loop(t, carry_):
            step(t, False)
            return carry_

        lax.fori_loop(0, i * r, loop, 0)
        for d in range(r):
            step(i * r + d, True)
        dq_ref[...] = jnp.where(lane < HEAD_DIM, dqacc[0], dqacc[1]) * SCALE

        @pl.when(i == nq - 1)
        def _():
            for kb in range(nkb):
                dk_ref[kb * tk:(kb + 1) * tk, :] = dkt[kb].T
                dv_ref[kb * tk:(kb + 1) * tk, :] = dvt[kb].T

    nbytes = (_nbytes((tq, 128), BF16) + 2 * _nbytes((S, 128), BF16) + 2 * _nbytes((tk, tk), BF16)
              + 8 * _nbytes((tq, 128), F32) + 4 * _nbytes((S, 128), F32) + 10 * _nbytes((tq, tk), F32))
    blk = pl.BlockSpec((tq, 128), lambda j, i: (i, j))
    full = pl.BlockSpec((S, 128), lambda j, i: (0, j))
    tri_spec = pl.BlockSpec((tk, tk), lambda j, i: (0, 0))
    dq, dk, dv = pl.pallas_call(
        body, name=name, grid=(4, nq),
        in_specs=[blk,
                  pl.BlockSpec((S, 128), lambda j, i: (0, 4 + j)),
                  pl.BlockSpec((S, 128), lambda j, i: (0, 8 + j)),
                  pl.BlockSpec((tq, 256), lambda j, i: (i, j)),
                  blk, tri_spec, tri_spec],
        out_specs=[blk, full, full],
        out_shape=[jax.ShapeDtypeStruct((S, D_SB), F32)] * 3,
        scratch_shapes=[pltpu.VMEM((2, tq, 128), F32), pltpu.VMEM((2, tq, 1), F32),
                        pltpu.VMEM((nkb, 128, tk), F32), pltpu.VMEM((nkb, 128, tk), F32)],
        compiler_params=_params(("parallel", "arbitrary"), nbytes),
    )(qkv, qkv, qkv, carry, do, tri_after, tri_before)
    return dq, dk, dv


def _group_mats():
    lanes = jnp.arange(D_MODEL) // HEAD_DIM
    gs = (lanes[:, None] == jnp.arange(128)[None, :]).astype(BF16)
    return gs, gs.T


def _group_sum_bcast(x, gs, gb):
    hi, lo = _split(x)
    s = _dot(hi, gs) + _dot(lo, gs)
    return _bcast(s, gb)


def _bcast(s, gb):
    hi, lo = _split(s)
    return _dot(hi, gb) + _dot(lo, gb)


def _pool_lane_consts():
    lane = lax.broadcasted_iota(jnp.int32, (1, D_POOL), 1)
    grp = lane // (D_POOL // 4)
    win = jnp.where(grp == 0, 2, jnp.where(grp == 1, 4, jnp.where(grp == 2, 8, 16)))
    return grp, win


def _by_group(grp, s2, s4, s8, s16):
    return jnp.where(grp == 0, s2, jnp.where(grp == 1, s4, jnp.where(grp == 2, s8, s16)))


def _mixers(i, ts, prev_ref, cur_ref, cw_ref, pw_ref, ps_ref):
    cur = cur_ref[...]
    prev = jnp.where(i == 0, 0.0, prev_ref[...])
    ext = jnp.concatenate([prev, cur], axis=0)
    n = HALO + ts

    def back(a, k):
        return pltpu.roll(a, k, 0)

    u = ext[:, D_CONV:2 * D_CONV] * ext[:, 2 * D_CONV:3 * D_CONV]
    p = ext[:, 3 * D_CONV:]
    cv = (cw_ref[0:1, :] * back(u, 2) + cw_ref[1:2, :] * back(u, 1) + cw_ref[2:3, :] * u)[HALO:]
    s2 = p + back(p, 1)
    s4 = s2 + back(s2, 2)
    s8 = s4 + back(s4, 4)
    s16 = s8 + back(s8, 8)
    grp, win = _pool_lane_consts()
    t1 = i * ts + 1 + lax.broadcasted_iota(jnp.int32, (ts, 1), 0)
    cnt = jnp.minimum(t1, win).astype(F32)
    pooled = _by_group(grp, s2, s4, s8, s16)[HALO:] / cnt - p[HALO:]
    yp = _dot(pooled.astype(BF16), pw_ref[...])
    del n
    return dict(b=cur[:, 0:D_CONV], u=u, cv=cv, pooled=pooled, yp=yp, cnt=cnt,
                conv_out=cur[:, 0:D_CONV] * cv, pool_out=yp * ps_ref[...])


def _halo_specs(ts, S, width):
    nb = ts // HALO
    last = S // HALO - 1
    prev = pl.BlockSpec((HALO, width), lambda i: (jnp.maximum(i * nb - 1, 0), 0))
    nxt = pl.BlockSpec((HALO, width), lambda i: (jnp.minimum((i + 1) * nb, last), 0))
    return prev, nxt


def _mixer_fwd(rest, attn, cw8, pwbd, ps, gain, *, name, ts):
    S = rest.shape[0]
    ts = min(ts, S)
    gs, gb = _group_mats()

    def body(prev_ref, cur_ref, attn_ref, cw_ref, pw_ref, ps_ref, gain_ref, gs_ref, gb_ref, o_ref):
        i = pl.program_id(0)
        f = _mixers(i, ts, prev_ref, cur_ref, cw_ref, pw_ref, ps_ref)
        mix = jnp.concatenate([attn_ref[...], f["conv_out"], f["pool_out"]], axis=1)
        ss = _group_sum_bcast(mix * mix, gs_ref[...], gb_ref[...])
        rinv = lax.rsqrt(ss * (1.0 / HEAD_DIM) + RMS_EPS)
        o_ref[...] = (mix * rinv * gain_ref[...]).astype(BF16)

    prev, _ = _halo_specs(ts, S, D_REST)
    row = lambda w: pl.BlockSpec((ts, w), lambda i: (i, 0))
    const = lambda a: pl.BlockSpec(a.shape, lambda i: (0, 0))
    nbytes = 12 * _nbytes((ts + HALO, D_REST), F32)
    return pl.pallas_call(
        body, name=name, grid=(S // ts,),
        in_specs=[prev, row(D_REST), row(D_SB), const(cw8), const(pwbd), const(ps), const(gain), const(gs), const(gb)],
        out_specs=row(D_MODEL),
        out_shape=jax.ShapeDtypeStruct((S, D_MODEL), BF16),
        compiler_params=_params(("parallel",), nbytes),
    )(rest, rest, attn, cw8, pwbd, ps, gain, gs, gb)


def _mixer_bwd1(dmixn, rest, attn, cw8, pwbd, ps, gain, *, name, ts):
    S = rest.shape[0]
    ts = min(ts, S)
    gs, gb = _group_mats()

    def body(dm_ref, prev_ref, cur_ref, attn_ref, cw_ref, pw_ref, ps_ref, gain_ref, gs_ref, gb_ref,
             da_ref, aux_ref, dg_ref, dsc_ref, dcw_ref, dpw_ref):
        i = pl.program_id(0)
        f = _mixers(i, ts, prev_ref, cur_ref, cw_ref, pw_ref, ps_ref)
        mix = jnp.concatenate([attn_ref[...], f["conv_out"], f["pool_out"]], axis=1)
        gsm, gbm = gs_ref[...], gb_ref[...]
        ss = _group_sum_bcast(mix * mix, gsm, gbm)
        rinv = lax.rsqrt(ss * (1.0 / HEAD_DIM) + RMS_EPS)
        dm = dm_ref[...]
        xn = mix * rinv
        dyg = dm * gain_ref[...]
        gm = _group_sum_bcast(dyg * xn, gsm, gbm) * (1.0 / HEAD_DIM)
        dmix = rinv * (dyg - xn * gm)
        da_ref[...] = dmix[:, 0:D_SB]
        dco = dmix[:, D_SB:D_SB + D_CONV]
        dpo = dmix[:, D_SB + D_CONV:]
        dcv = dco * f["b"]
        dyp = dpo * ps_ref[...]
        dpooled = _dot(dyp.astype(BF16), pw_ref[...], NT)
        aux_ref[...] = jnp.concatenate([dco * f["cv"], dcv, dpooled / f["cnt"], dpooled], axis=1)
        u = f["u"]
        parts = [
            _sum8(dm * xn),
            _sum8(dpo * f["yp"]),
            jnp.concatenate([_sum8(dcv * pltpu.roll(u, 2, 0)[HALO:]), _sum8(dcv * pltpu.roll(u, 1, 0)[HALO:]),
                             _sum8(dcv * u[HALO:])], axis=0),
            _dot(f["pooled"].astype(BF16), dyp.astype(BF16), TN),
        ]
        outs = [dg_ref, dsc_ref, dcw_ref, dpw_ref]

        @pl.when(i == 0)
        def _():
            for o, v in zip(outs, parts):
                o[...] = v

        @pl.when(i > 0)
        def _():
            for o, v in zip(outs, parts):
                o[...] += v

    prev, _ = _halo_specs(ts, S, D_REST)
    row = lambda w: pl.BlockSpec((ts, w), lambda i: (i, 0))
    const = lambda a: pl.BlockSpec(a.shape, lambda i: (0, 0))
    acc = lambda r_, w: pl.BlockSpec((r_, w), lambda i: (0, 0))
    nbytes = 16 * _nbytes((ts + HALO, D_REST), F32)
    return pl.pallas_call(
        body, name=name, grid=(S // ts,),
        in_specs=[row(D_MODEL), prev, row(D_REST), row(D_SB), const(cw8), const(pwbd), const(ps), const(gain),
                  const(gs), const(gb)],
        out_specs=[row(D_SB), row(D_REST), acc(8, D_MODEL), acc(8, D_POOL), acc(24, D_CONV), acc(D_POOL, D_POOL)],
        out_shape=[jax.ShapeDtypeStruct((S, D_SB), F32), jax.ShapeDtypeStruct((S, D_REST), F32),
                   jax.ShapeDtypeStruct((8, D_MODEL), F32), jax.ShapeDtypeStruct((8, D_POOL), F32),
                   jax.ShapeDtypeStruct((24, D_CONV), F32), jax.ShapeDtypeStruct((D_POOL, D_POOL), F32)],
        compiler_params=_params(("arbitrary",), nbytes),
    )(dmixn, rest, rest, attn, cw8, pwbd, ps, gain, gs, gb)


def _mixer_bwd2(aux, rest, cw8, *, name, ts):
    S = rest.shape[0]
    ts = min(ts, S)
    nblk = S // ts

    def body(cur_ref, nxt_ref, rest_ref, cw_ref, o_ref):
        i = pl.program_id(0)
        cur = cur_ref[...]
        nxt = jnp.where(i == nblk - 1, 0.0, nxt_ref[...])
        ext = jnp.concatenate([cur, nxt], axis=0)
        n = ts + HALO

        def fwd(a, k):
            return pltpu.roll(a, n - k, 0)

        dcv = ext[:, D_CONV:2 * D_CONV]
        dps = ext[:, 2 * D_CONV:3 * D_CONV]
        du = (cw_ref[2:3, :] * dcv + cw_ref[1:2, :] * fwd(dcv, 1) + cw_ref[0:1, :] * fwd(dcv, 2))[0:ts]
        f2 = dps + fwd(dps, 1)
        f4 = f2 + fwd(f2, 2)
        f8 = f4 + fwd(f4, 4)
        f16 = f8 + fwd(f8, 8)
        grp, _ = _pool_lane_consts()
        dp = _by_group(grp, f2, f4, f8, f16)[0:ts] - cur[:, 3 * D_CONV:]
        rest_v = rest_ref[...]
        c_gate = rest_v[:, D_CONV:2 * D_CONV]
        h = rest_v[:, 2 * D_CONV:3 * D_CONV]
        o_ref[...] = jnp.concatenate([cur[:, 0:D_CONV], du * h, du * c_gate, dp], axis=1)

    _, nxt = _halo_specs(ts, S, D_REST)
    row = pl.BlockSpec((ts, D_REST), lambda i: (i, 0))
    return pl.pallas_call(
        body, name=name, grid=(nblk,),
        in_specs=[row, nxt, row, pl.BlockSpec(cw8.shape, lambda i: (0, 0))],
        out_specs=row,
        out_shape=jax.ShapeDtypeStruct((S, D_REST), F32),
        compiler_params=_params(("parallel",), 10 * _nbytes((ts + HALO, D_REST), F32)),
    )(aux, aux, rest, cw8)


def _block_diag(pw):
    z = jnp.zeros((4, 64, 4, 64), pw.dtype)
    for g in range(4):
        z = z.at[g, :, g, :].set(pw[g])
    return z.reshape(256, 256)


def _rows8(v, rows=8):
    return jnp.pad(v, ((0, rows - v.shape[0]), (0, 0)))


TILES = dict(tm=512, ts=512, tq=256, tk=256)


def _local_step(x, tgt, w, t=None):
    t = dict(TILES, **(t or {}))
    tm, ts, tq, tk = t["tm"], t["ts"], t["tq"], t["tk"]
    saved = []
    xl = x
    for l in range(DEPTH):
        n = f"l{l}_"
        w_in = w["w_in"][l]
        qkv = _matmul(xl, w_in[:, :D_QKV], name=n + "proj_qkv", tm=1024, tn=512, tk=1024, out_dtype=BF16)
        rest = _matmul(xl, w_in[:, D_QKV:], name=n + "proj_rest", tm=1024, tn=512, tk=1024)
        attn, carry = _attn_fwd(qkv, name=n + "attn_fwd", tq=tq, tk=tk)
        cw8 = _rows8(w["conv_w"][l])
        pwbd = _block_diag(w["pool_w"][l]).astype(BF16)
        ps = w["pool_scale"][l].reshape(1, D_POOL)
        gain = w["mix_norm_g"][l].reshape(1, D_MODEL)
        mixn = _mixer_fwd(rest, attn, cw8, pwbd, ps, gain, name=n + "mixer_fwd", ts=ts)
        x1, xh1, rs1 = _matmul_ln(mixn, w["w_o"][l], xl, w["ln1_g"][l], w["ln1_b"][l], name=n + "wo_ln", tm=tm, tk=1024)
        hpre = _matmul(x1, w["w_up"][l], name=n + "ffn_up", tm=1024, tn=1024, tk=1024)
        x2, xh2, rs2 = _matmul_ln(hpre, w["w_down"][l], x1, w["ln2_g"][l], w["ln2_b"][l], name=n + "ffn_down_ln",
                                  tm=tm, tk=1024, a_pro="relu2")
        saved.append(dict(xin=xl, qkv=qkv, rest=rest, attn=attn, carry=carry, cw8=cw8, pwbd=pwbd, ps=ps, gain=gain,
                          mixn=mixn, x1=x1, xh1=xh1, rs1=rs1, hpre=hpre, xh2=xh2, rs2=rs2))
        xl = x2

    dy, lsum = _loss_grad(xl, tgt, name="loss_grad", tm=tm)
    grads = {k: [None] * DEPTH for k in
             ("w_in", "conv_w", "pool_w", "pool_scale", "mix_norm_g", "w_o", "ln1_g", "ln1_b", "w_up", "w_down",
              "ln2_g", "ln2_b")}
    for l in reversed(range(DEPTH)):
        n = f"l{l}_"
        s = saved[l]
        dr2, dg2, db2 = _ln_bwd(dy, s["xh2"], s["rs2"], w["ln2_g"][l], name=n + "ln2_bwd", tm=tm)
        dhpre = _matmul(dr2, w["w_down"][l], name=n + "ffn_down_dx", tm=1024, tn=1024, tk=1024, tb=True,
                        out_dtype=BF16, epi="drelu2", e=s["hpre"])
        grads["w_down"][l] = _matmul(s["hpre"], dr2, name=n + "ffn_down_dw", tm=1024, tn=1024, tk=512, ta=True,
                                     a_pro="relu2")
        dx1 = _matmul(dhpre, w["w_up"][l], name=n + "ffn_up_dx", tm=1024, tn=1024, tk=1024, tb=True,
                      epi="add", e=dr2, e_scale=ALPHA)
        grads["w_up"][l] = _matmul(s["x1"], dhpre, name=n + "ffn_up_dw", tm=1024, tn=1024, tk=512, ta=True)
        dr1, dg1, db1 = _ln_bwd(dx1, s["xh1"], s["rs1"], w["ln1_g"][l], name=n + "ln1_bwd", tm=tm)
        dmixn = _matmul(dr1, w["w_o"][l], name=n + "wo_dx", tm=1024, tn=1024, tk=1024, tb=True)
        grads["w_o"][l] = _matmul(s["mixn"], dr1, name=n + "wo_dw", tm=1024, tn=1024, tk=512, ta=True)
        d_attn, aux, dgain, dsc, dcw, dpw = _mixer_bwd1(dmixn, s["rest"], s["attn"], s["cw8"], s["pwbd"], s["ps"],
                                                        s["gain"], name=n + "mixer_bwd1", ts=ts)
        drest = _mixer_bwd2(aux, s["rest"], s["cw8"], name=n + "mixer_bwd2", ts=ts)
        dq, dk, dv = _attn_bwd(s["qkv"], s["carry"], d_attn, name=n + "attn_bwd", tq=tq, tk=tk)
        dqkv = jnp.concatenate([dq, dk, dv], axis=1)
        w_in = w["w_in"][l]
        dxa = _matmul(dqkv, w_in[:, :D_QKV], name=n + "proj_qkv_dx", tm=1024, tn=1024, tk=512, tb=True,
                      epi="add", e=dr1, e_scale=ALPHA)
        dy = _matmul(drest, w_in[:, D_QKV:], name=n + "proj_rest_dx", tm=1024, tn=1024, tk=1024, tb=True,
                     epi="add", e=dxa, e_scale=1.0)
        dwq = _matmul(s["xin"], dqkv, name=n + "proj_qkv_dw", tm=1024, tn=512, tk=512, ta=True)
        dwr = _matmul(s["xin"], drest, name=n + "proj_rest_dw", tm=1024, tn=1024, tk=512, ta=True)
        grads["w_in"][l] = jnp.concatenate([dwq, dwr], axis=1)
        grads["ln2_g"][l] = dg2.sum(0)
        grads["ln2_b"][l] = db2.sum(0)
        grads["ln1_g"][l] = dg1.sum(0)
        grads["ln1_b"][l] = db1.sum(0)
        grads["mix_norm_g"][l] = dgain.sum(0)
        grads["pool_scale"][l] = dsc.sum(0)
        grads["conv_w"][l] = dcw.reshape(3, 8, D_CONV).sum(1)
        grads["pool_w"][l] = jnp.stack([dpw[64 * g:64 * g + 64, 64 * g:64 * g + 64] for g in range(4)])
    grads = {k: jnp.stack(v) for k, v in grads.items()}
    return lsum, dy, grads


ANY = pl.BlockSpec(memory_space=pl.ANY)


def _place():
    x, y, c = lax.axis_index("x"), lax.axis_index("y"), lax.axis_index("c")
    chips = [(1 - x, y), (x, 1 - y), (1 - x, 1 - y)]
    return x, y, c, chips


def _remote(src, dst, send_sems, recv_sems, k, to):
    return pltpu.make_async_remote_copy(src_ref=src, dst_ref=dst, send_sem=send_sems.at[k], recv_sem=recv_sems.at[k],
                                        device_id=to, device_id_type=MESH)


def _allgather_chips(pack, *, name):
    R, C = pack.shape
    H = R // 2

    def body(p_ref, o_ref, send_sems, recv_sems, local_sem):
        x, y, c, chips = _place()
        my = 2 * x + y
        sib = (x, y, 1 - c)

        def half(k, hc):
            return o_ref.at[k, pl.ds(hc * H, H), :]

        mine = pltpu.make_async_copy(p_ref, o_ref.at[my], local_sem)
        mine.start()
        first = [_remote(p_ref.at[pl.ds(c * H, H), :], half(my, c), send_sems, recv_sems, j, (cx, cy, c))
                 for j, (cx, cy) in enumerate(chips)]
        for cp in first:
            cp.start()
        passed = []
        for j, (cx, cy) in enumerate(chips):
            k = 2 * cx + cy
            _remote(half(k, c), half(k, c), send_sems, recv_sems, j, sib).wait_recv()
            fwd = _remote(half(k, c), half(k, c), send_sems, recv_sems, 3 + j, sib)
            fwd.start()
            passed.append(fwd)
        for j, (cx, cy) in enumerate(chips):
            k = 2 * cx + cy
            _remote(half(k, 1 - c), half(k, 1 - c), send_sems, recv_sems, 3 + j, sib).wait_recv()
        for cp in first + passed:
            cp.wait_send()
        mine.wait()

    return pl.pallas_call(
        body, name=name, in_specs=[ANY], out_specs=ANY,
        out_shape=jax.ShapeDtypeStruct((N_CHIPS, R, C), pack.dtype),
        scratch_shapes=[pltpu.SemaphoreType.DMA((6,)), pltpu.SemaphoreType.DMA((6,)), pltpu.SemaphoreType.DMA],
    )(pack)


def _swap_halves(gp, *, name):
    K, R, C = gp.shape
    H = R // 2

    def body(g_ref, mine_ref, theirs_ref, send_sems, recv_sems, local_sem):
        x, y, c, _ = _place()
        keep = pltpu.make_async_copy(g_ref.at[:, pl.ds(c * H, H), :], mine_ref, local_sem)
        keep.start()
        cp = _remote(g_ref.at[:, pl.ds((1 - c) * H, H), :], theirs_ref, send_sems, recv_sems, 0, (x, y, 1 - c))
        cp.start()
        cp.wait()
        keep.wait()

    shape = jax.ShapeDtypeStruct((K, H, C), gp.dtype)
    return pl.pallas_call(
        body, name=name, in_specs=[ANY], out_specs=[ANY, ANY], out_shape=[shape, shape],
        scratch_shapes=[pltpu.SemaphoreType.DMA((1,)), pltpu.SemaphoreType.DMA((1,)), pltpu.SemaphoreType.DMA],
    )(gp)


def _scatter_chips(part, *, name):
    K, H, C = part.shape

    def body(p_ref, o_ref, send_sems, recv_sems):
        x, y, c, chips = _place()
        copies = [_remote(p_ref.at[2 * cx + cy], o_ref.at[j], send_sems, recv_sems, j, (cx, cy, c))
                  for j, (cx, cy) in enumerate(chips)]
        for cp in copies:
            cp.start()
        for cp in copies:
            cp.wait()

    return pl.pallas_call(
        body, name=name, in_specs=[ANY], out_specs=ANY,
        out_shape=jax.ShapeDtypeStruct((3, H, C), part.dtype),
        scratch_shapes=[pltpu.SemaphoreType.DMA((3,)), pltpu.SemaphoreType.DMA((3,))],
    )(part)


def _join_halves(half, *, name):
    H, C = half.shape

    def body(h_ref, o_ref, send_sems, recv_sems, local_sem):
        x, y, c, _ = _place()
        rows = o_ref.at[pl.ds(c * H, H), :]
        keep = pltpu.make_async_copy(h_ref, rows, local_sem)
        keep.start()
        cp = _remote(h_ref, rows, send_sems, recv_sems, 0, (x, y, 1 - c))
        cp.start()
        cp.wait_send()
        _remote(h_ref, o_ref.at[pl.ds((1 - c) * H, H), :], send_sems, recv_sems, 0, (x, y, 1 - c)).wait_recv()
        keep.wait()

    return pl.pallas_call(
        body, name=name, in_specs=[ANY], out_specs=ANY,
        out_shape=jax.ShapeDtypeStruct((2 * H, C), half.dtype),
        scratch_shapes=[pltpu.SemaphoreType.DMA((1,)), pltpu.SemaphoreType.DMA((1,)), pltpu.SemaphoreType.DMA],
    )(half)


def _allreduce_small(v, *, name):
    R, C = v.shape
    n_dev = 8

    def body(v_ref, o_ref, gat, send_sems, recv_sems):
        x, y, c, chips = _place()
        sib = (x, y, 1 - c)

        def rows(px, py, pc):
            return gat.at[4 * px + 2 * py + pc]

        gat[4 * x + 2 * y + c] = v_ref[...]
        first = [_remote(v_ref, rows(x, y, c), send_sems, recv_sems, 0, sib)]
        first += [_remote(v_ref, rows(x, y, c), send_sems, recv_sems, 1 + j, (cx, cy, c))
                  for j, (cx, cy) in enumerate(chips)]
        for cp in first:
            cp.start()
        passed = []
        for j, (cx, cy) in enumerate(chips):
            _remote(v_ref, rows(cx, cy, c), send_sems, recv_sems, 1 + j, sib).wait_recv()
            fwd = _remote(rows(cx, cy, c), rows(cx, cy, c), send_sems, recv_sems, 4 + j, sib)
            fwd.start()
            passed.append(fwd)
        _remote(v_ref, rows(x, y, 1 - c), send_sems, recv_sems, 0, sib).wait_recv()
        for j, (cx, cy) in enumerate(chips):
            _remote(v_ref, rows(cx, cy, 1 - c), send_sems, recv_sems, 4 + j, sib).wait_recv()
        for cp in first + passed:
            cp.wait_send()
        acc = gat[0]
        for d in range(1, n_dev):
            acc = acc + gat[d]
        o_ref[...] = acc

    vm = pl.BlockSpec(memory_space=pltpu.VMEM)
    return pl.pallas_call(
        body, name=name, in_specs=[vm], out_specs=vm,
        out_shape=jax.ShapeDtypeStruct((R, C), F32),
        scratch_shapes=[pltpu.VMEM((n_dev, R, C), F32), pltpu.SemaphoreType.DMA((7,)), pltpu.SemaphoreType.DMA((7,))],
    )(v)


def _add_pairs(a, b, *, name, tr):
    K, H, C = a.shape
    tr = min(tr, H)
    assert H % tr == 0

    def body(a_ref, b_ref, o_ref):
        o_ref[...] = (a_ref[...].astype(F32) + b_ref[...].astype(F32)).astype(BF16)

    blk = pl.BlockSpec((1, tr, C), lambda k, i: (k, i, 0))
    return pl.pallas_call(
        body, name=name, grid=(K, H // tr), in_specs=[blk, blk], out_specs=blk,
        out_shape=jax.ShapeDtypeStruct((K, H, C), BF16),
        compiler_params=_params(("parallel", "parallel"), 3 * _nbytes((tr, C), BF16)),
    )(a, b)


def _add_final(a, b, others, *, name, tr):
    H, C = a.shape
    tr = min(tr, H)
    assert H % tr == 0

    def body(a_ref, b_ref, o_ref_in, out_ref):
        acc = a_ref[...].astype(F32) + b_ref[...].astype(F32)
        for j in range(3):
            acc = acc + o_ref_in[j].astype(F32)
        out_ref[...] = acc

    blk = pl.BlockSpec((tr, C), lambda i: (i, 0))
    return pl.pallas_call(
        body, name=name, grid=(H // tr,),
        in_specs=[blk, blk, pl.BlockSpec((3, tr, C), lambda i: (0, i, 0))], out_specs=blk,
        out_shape=jax.ShapeDtypeStruct((H, C), F32),
        compiler_params=_params(("parallel",), 6 * _nbytes((tr, C), F32)),
    )(a, b, others)


def _adamw(w, g, m, v, *, name, tr, row0=0):
    R, C = w.shape
    tr = min(tr, R)
    assert R % tr == 0 and row0 % tr == 0
    off = row0 // tr

    def body(w_ref, g_ref, m_ref, v_ref, go_ref, d_ref, mo_ref, vo_ref):
        gv = g_ref[...]
        m2 = ADAM_B1 * m_ref[...] + (1.0 - ADAM_B1) * gv
        v2 = ADAM_B2 * v_ref[...] + (1.0 - ADAM_B2) * jnp.square(gv)
        m_hat = m2 / (1.0 - ADAM_B1 ** ADAM_STEP)
        v_hat = v2 / (1.0 - ADAM_B2 ** ADAM_STEP)
        d_ref[...] = -ADAM_LR * (m_hat / (jnp.sqrt(v_hat) + ADAM_EPS) + ADAM_WD * w_ref[...])
        go_ref[...] = gv
        mo_ref[...] = m2
        vo_ref[...] = v2

    blk = pl.BlockSpec((tr, C), lambda i: (i, 0))
    shape = jax.ShapeDtypeStruct((R, C), F32)
    return pl.pallas_call(
        body, name=name, grid=(R // tr,),
        in_specs=[blk, pl.BlockSpec((tr, C), lambda i: (i + off, 0)), blk, blk], out_specs=[blk] * 4,
        out_shape=[shape] * 4,
        compiler_params=_params(("parallel",), 8 * _nbytes((tr, C), F32)),
    )(w, g, m, v)


BIG = ("w_in", "w_o", "w_up", "w_down")
BIG_AXIS = dict(w_in=2, w_o=1, w_up=2, w_down=1)
SMALL = ("pool_w", "pool_scale", "mix_norm_g", "ln1_g", "ln1_b", "ln2_g", "ln2_b")
CONV_ROWS = 32
SMALL_ROWS = 48


def _big_rows(shards):
    sizes = [shards[n].size // D_MODEL for n in BIG]
    starts = [sum(sizes[:i]) for i in range(len(sizes))]
    return sizes, starts


def _pad_rows(flat, rows):
    return jnp.pad(flat, (0, rows * D_MODEL - flat.shape[0])).reshape(rows, D_MODEL)


def kernel(x, w_in, conv_w, pool_w, pool_scale, mix_norm_g, w_o, ln1_g, ln1_b, w_up, w_down, ln2_g, ln2_b, loss_target, m_w_in, m_conv_w, m_pool_w, m_pool_scale, m_mix_norm_g, m_w_o, m_ln1_g, m_ln1_b, m_w_up, m_w_down, m_ln2_g, m_ln2_b, v_w_in, v_conv_w, v_pool_w, v_pool_scale, v_mix_norm_g, v_w_o, v_ln1_g, v_ln1_b, v_w_up, v_w_down, v_ln2_g, v_ln2_b):
    wts = dict(w_in=w_in, conv_w=conv_w, pool_w=pool_w, pool_scale=pool_scale, mix_norm_g=mix_norm_g, w_o=w_o,
               ln1_g=ln1_g, ln1_b=ln1_b, w_up=w_up, w_down=w_down, ln2_g=ln2_g, ln2_b=ln2_b)
    mom = dict(w_in=m_w_in, conv_w=m_conv_w, pool_w=m_pool_w, pool_scale=m_pool_scale, mix_norm_g=m_mix_norm_g,
               w_o=m_w_o, ln1_g=m_ln1_g, ln1_b=m_ln1_b, w_up=m_w_up, w_down=m_w_down, ln2_g=m_ln2_g, ln2_b=m_ln2_b)
    var = dict(w_in=v_w_in, conv_w=v_conv_w, pool_w=v_pool_w, pool_scale=v_pool_scale, mix_norm_g=v_mix_norm_g,
               w_o=v_w_o, ln1_g=v_ln1_g, ln1_b=v_ln1_b, w_up=v_w_up, w_down=v_w_down, ln2_g=v_ln2_g, ln2_b=v_ln2_b)
    chip = 2 * lax.axis_index("x") + lax.axis_index("y")
    sizes, starts = _big_rows(wts)
    big_rows = sum(sizes)

    conv_bits = lax.bitcast_convert_type(conv_w.reshape(-1), BF16).reshape(-1)
    pack = jnp.concatenate([wts[n].reshape(-1, D_MODEL).astype(BF16) for n in BIG]
                           + [_pad_rows(conv_bits, CONV_ROWS)], axis=0)
    gathered = _allgather_chips(pack, name="gather_weights")
    full = {}
    for n, size, start in zip(BIG, sizes, starts):
        parts = [gathered[k, start:start + size].reshape(wts[n].shape) for k in range(N_CHIPS)]
        full[n] = jnp.concatenate(parts, axis=BIG_AXIS[n])
    conv_parts = [lax.bitcast_convert_type(gathered[k, big_rows:].reshape(-1)[:2 * conv_w.size].reshape(-1, 2), F32)
                  .reshape(conv_w.shape) for k in range(N_CHIPS)]
    full["conv_w"] = jnp.concatenate(conv_parts, axis=2)
    for n in SMALL:
        full[n] = wts[n]

    lsum, grad_x, grads = _local_step(x[0], loss_target[0], full)

    blocks = []
    for k in range(N_CHIPS):
        rows = []
        for n in BIG:
            ax = BIG_AXIS[n]
            width = wts[n].shape[ax]
            rows.append(lax.slice_in_dim(grads[n], k * width, (k + 1) * width, axis=ax).reshape(-1, D_MODEL))
        blocks.append(jnp.concatenate(rows, axis=0))
    gpack = jnp.stack(blocks).astype(BF16)
    mine, theirs = _swap_halves(gpack, name="grad_swap_cores")
    chip_sum = _add_pairs(mine, theirs, name="grad_add_cores", tr=736)
    from_chips = _scatter_chips(chip_sum, name="grad_scatter_chips")
    mine_k = lax.dynamic_index_in_dim(mine, chip, 0, keepdims=False)
    theirs_k = lax.dynamic_index_in_dim(theirs, chip, 0, keepdims=False)
    half_sum = _add_final(mine_k, theirs_k, from_chips, name="grad_add_chips", tr=736)
    gsum = _join_halves(half_sum, name="grad_join_cores")

    small_flat = jnp.concatenate([grads[n].reshape(-1) for n in SMALL] + [grads["conv_w"].reshape(-1),
                                                                          lsum.sum().reshape(1)])
    small_sum = _allreduce_small(_pad_rows(small_flat, SMALL_ROWS), name="allreduce_small").reshape(-1)
    gsmall = {}
    pos = 0
    for n in SMALL:
        gsmall[n] = small_sum[pos:pos + wts[n].size].reshape(wts[n].shape)
        pos += wts[n].size
    conv_full = small_sum[pos:pos + 4 * conv_w.size].reshape(DEPTH, 3, D_CONV)
    pos += 4 * conv_w.size
    loss = small_sum[pos]
    gsmall["conv_w"] = lax.dynamic_slice_in_dim(conv_full, chip * conv_w.shape[2], conv_w.shape[2], axis=2)

    out_g, out_d, out_m, out_v = {}, {}, {}, {}
    for n, size, start in zip(BIG, sizes, starts):
        shp = wts[n].shape
        res = _adamw(wts[n].reshape(-1, D_MODEL), gsum, mom[n].reshape(-1, D_MODEL), var[n].reshape(-1, D_MODEL),
                     name="adamw_" + n, tr=256, row0=start)
        out_g[n], out_d[n], out_m[n], out_v[n] = [r.reshape(shp) for r in res]
    small_names = SMALL + ("conv_w",)
    packs = [_pad_rows(jnp.concatenate([d[n].reshape(-1) for n in small_names]), SMALL_ROWS)
             for d in (wts, gsmall, mom, var)]
    res = _adamw(*packs, name="adamw_small", tr=SMALL_ROWS)
    pos = 0
    for n in small_names:
        shp = wts[n].shape
        out_g[n], out_d[n], out_m[n], out_v[n] = [r.reshape(-1)[pos:pos + wts[n].size].reshape(shp) for r in res]
        pos += wts[n].size

    order = ("w_in", "conv_w", "pool_w", "pool_scale", "mix_norm_g", "w_o", "ln1_g", "ln1_b", "w_up", "w_down",
             "ln2_g", "ln2_b")
    return (loss, grad_x[None], *[out_g[n] for n in order], *[out_d[n] for n in order],
            *[out_m[n] for n in order], *[out_v[n] for n in order])
```

```python
import functools
import math

import jax
import jax.numpy as jnp
from jax import lax
from jax.experimental import pallas as pl
from jax.experimental.pallas import tpu as pltpu

F32 = jnp.float32
BF16 = jnp.bfloat16
MESH = pl.DeviceIdType.MESH

D_MODEL = 1024
DEPTH = 2
HEAD_DIM = 64
D_SB = 512
D_CONV = 256
D_POOL = 256
D_QKV = 3 * D_SB
D_REST = 3 * D_CONV + D_POOL
D_IN = D_QKV + D_REST
D_FF = 4 * D_MODEL
ALPHA = (2 * DEPTH) ** 0.25
LN_EPS = 1e-5
RMS_EPS = 1e-6
SCALE = HEAD_DIM ** -0.5
N_CHIPS = 4
HALO = 16

ADAM_LR = 0.001
ADAM_B1 = 0.9
ADAM_B2 = 0.999
ADAM_EPS = 1e-08
ADAM_WD = 0.01
ADAM_STEP = 10

VMEM_V7X_BYTES = 64 * 1024 * 1024
VMEM_CAP_BYTES = VMEM_V7X_BYTES - 8 * 1024 * 1024


def _params(sem, block_bytes):
    limit = min(VMEM_CAP_BYTES, max(32 * 1024 * 1024, 3 * block_bytes))
    return pltpu.CompilerParams(dimension_semantics=sem, vmem_limit_bytes=limit)


def _nbytes(shape, dtype):
    return math.prod(shape) * jnp.dtype(dtype).itemsize


def _dot(a, b, dims=(((1,), (0,)), ((), ()))):
    return lax.dot_general(a, b, dims, preferred_element_type=F32)


NT = (((1,), (1,)), ((), ()))
TN = (((0,), (0,)), ((), ()))


def _split(x):
    hi = x.astype(BF16)
    lo = (x - hi.astype(F32)).astype(BF16)
    return hi, lo


def _sum8(x):
    r, c = x.shape
    return x.reshape(r // 8, 8, c).sum(axis=0)


def _matmul(a, b, *, name, tm, tn, tk, ta=False, tb=False, out_dtype=F32,
            a_pro=None, epi=None, e=None, e_scale=1.0):
    M, K = (a.shape[1], a.shape[0]) if ta else a.shape
    N = b.shape[0] if tb else b.shape[1]
    tm, tn, tk = min(tm, M), min(tn, N), min(tk, K)
    assert M % tm == 0 and N % tn == 0 and K % tk == 0, (name, M, N, K)
    nk = K // tk
    dims = (((0 if ta else 1,), (1 if tb else 0,)), ((), ()))

    def body(*refs):
        if epi is None:
            a_ref, b_ref, o_ref, *scr = refs
            e_ref = None
        else:
            a_ref, b_ref, e_ref, o_ref, *scr = refs
        av = a_ref[...]
        if a_pro == "relu2":
            av = jnp.square(jnp.maximum(av.astype(F32), 0.0))
        p = _dot(av.astype(BF16), b_ref[...].astype(BF16), dims)

        def finish(acc):
            if epi == "drelu2":
                acc = acc * (2.0 * jnp.maximum(e_ref[...], 0.0))
            elif epi == "add":
                acc = acc + e_scale * e_ref[...]
            o_ref[...] = acc.astype(out_dtype)

        if nk == 1:
            finish(p)
        else:
            acc_ref = scr[0]
            k = pl.program_id(2)

            @pl.when(k == 0)
            def _():
                acc_ref[...] = p

            @pl.when(k > 0)
            def _():
                acc_ref[...] += p

            @pl.when(k == nk - 1)
            def _():
                finish(acc_ref[...])

    a_spec = pl.BlockSpec((tk, tm), lambda i, j, k: (k, i)) if ta else pl.BlockSpec((tm, tk), lambda i, j, k: (i, k))
    b_spec = pl.BlockSpec((tn, tk), lambda i, j, k: (j, k)) if tb else pl.BlockSpec((tk, tn), lambda i, j, k: (k, j))
    o_spec = pl.BlockSpec((tm, tn), lambda i, j, k: (i, j))
    in_specs = [a_spec, b_spec]
    args = [a, b]
    nbytes = _nbytes((tm, tk), a.dtype) + _nbytes((tk, tn), b.dtype) + 2 * _nbytes((tm, tn), F32)
    if epi is not None:
        in_specs.append(o_spec)
        args.append(e)
        nbytes += _nbytes((tm, tn), e.dtype)
    scratch = [pltpu.VMEM((tm, tn), F32)] if nk > 1 else []
    return pl.pallas_call(
        body, name=name,
        grid=(M // tm, N // tn, nk),
        in_specs=in_specs, out_specs=o_spec,
        out_shape=jax.ShapeDtypeStruct((M, N), out_dtype),
        scratch_shapes=scratch,
        compiler_params=_params(("parallel", "parallel", "arbitrary"), nbytes),
    )(*args)


def _matmul_ln(a, b, xres, g, bias, *, name, tm, tk, a_pro=None):
    M, K = a.shape
    N = b.shape[1]
    tm, tk = min(tm, M), min(tk, K)
    assert M % tm == 0 and K % tk == 0 and N == D_MODEL
    nk = K // tk

    def body(a_ref, b_ref, x_ref, g_ref, bias_ref, y_ref, xh_ref, rs_ref, *scr):
        av = a_ref[...]
        if a_pro == "relu2":
            av = jnp.square(jnp.maximum(av.astype(F32), 0.0))
        p = _dot(av.astype(BF16), b_ref[...].astype(BF16))

        def finish(acc):
            r = ALPHA * x_ref[...] + acc
            mu = jnp.mean(r, axis=-1, keepdims=True)
            xc = r - mu
            var = jnp.mean(xc * xc, axis=-1, keepdims=True)
            rstd = lax.rsqrt(var + LN_EPS)
            xh = xc * rstd
            y_ref[...] = xh * g_ref[...] + bias_ref[...]
            xh_ref[...] = xh
            rs_ref[...] = rstd

        if nk == 1:
            finish(p)
        else:
            acc_ref = scr[0]
            k = pl.program_id(1)

            @pl.when(k == 0)
            def _():
                acc_ref[...] = p

            @pl.when(k > 0)
            def _():
                acc_ref[...] += p

            @pl.when(k == nk - 1)
            def _():
                finish(acc_ref[...])

    row = pl.BlockSpec((tm, N), lambda i, k: (i, 0))
    vec = pl.BlockSpec((1, N), lambda i, k: (0, 0))
    nbytes = _nbytes((tm, tk), a.dtype) + _nbytes((tk, N), b.dtype) + 5 * _nbytes((tm, N), F32)
    scratch = [pltpu.VMEM((tm, N), F32)] if nk > 1 else []
    return pl.pallas_call(
        body, name=name,
        grid=(M // tm, nk),
        in_specs=[pl.BlockSpec((tm, tk), lambda i, k: (i, k)), pl.BlockSpec((tk, N), lambda i, k: (k, 0)), row, vec, vec],
        out_specs=[row, row, pl.BlockSpec((tm, 1), lambda i, k: (i, 0))],
        out_shape=[jax.ShapeDtypeStruct((M, N), F32), jax.ShapeDtypeStruct((M, N), F32),
                   jax.ShapeDtypeStruct((M, 1), F32)],
        scratch_shapes=scratch,
        compiler_params=_params(("parallel", "arbitrary"), nbytes),
    )(a, b, xres, g.reshape(1, N), bias.reshape(1, N))


def _ln_bwd(dy, xhat, rstd, g, *, name, tm):
    M, N = dy.shape
    tm = min(tm, M)

    def body(dy_ref, xh_ref, rs_ref, g_ref, dr_ref, dg_ref, db_ref):
        i = pl.program_id(0)
        dyv = dy_ref[...]
        xh = xh_ref[...]
        dxh = dyv * g_ref[...]
        m1 = jnp.mean(dxh, axis=-1, keepdims=True)
        m2 = jnp.mean(dxh * xh, axis=-1, keepdims=True)
        dr_ref[...] = rs_ref[...] * (dxh - m1 - xh * m2)
        pg = _sum8(dyv * xh)
        pb = _sum8(dyv)

        @pl.when(i == 0)
        def _():
            dg_ref[...] = pg
            db_ref[...] = pb

        @pl.when(i > 0)
        def _():
            dg_ref[...] += pg
            db_ref[...] += pb

    row = pl.BlockSpec((tm, N), lambda i: (i, 0))
    acc = pl.BlockSpec((8, N), lambda i: (0, 0))
    return pl.pallas_call(
        body, name=name, grid=(M // tm,),
        in_specs=[row, row, pl.BlockSpec((tm, 1), lambda i: (i, 0)), pl.BlockSpec((1, N), lambda i: (0, 0))],
        out_specs=[row, acc, acc],
        out_shape=[jax.ShapeDtypeStruct((M, N), F32), jax.ShapeDtypeStruct((8, N), F32),
                   jax.ShapeDtypeStruct((8, N), F32)],
        compiler_params=_params(("arbitrary",), 4 * _nbytes((tm, N), F32)),
    )(dy, xhat, rstd, g.reshape(1, N))


def _loss_grad(y, tgt, *, name, tm):
    M, N = y.shape
    tm = min(tm, M)

    def body(y_ref, t_ref, dy_ref, l_ref):
        i = pl.program_id(0)
        d = y_ref[...] - t_ref[...]
        dy_ref[...] = d * (1.0 / N)
        pl_ = _sum8(d * d) * (0.5 / N)

        @pl.when(i == 0)
        def _():
            l_ref[...] = pl_

        @pl.when(i > 0)
        def _():
            l_ref[...] += pl_

    row = pl.BlockSpec((tm, N), lambda i: (i, 0))
    return pl.pallas_call(
        body, name=name, grid=(M // tm,),
        in_specs=[row, row], out_specs=[row, pl.BlockSpec((8, N), lambda i: (0, 0))],
        out_shape=[jax.ShapeDtypeStruct((M, N), F32), jax.ShapeDtypeStruct((8, N), F32)],
        compiler_params=_params(("arbitrary",), 3 * _nbytes((tm, N), F32)),
    )(y, tgt)


def _tri(n, kind):
    j = lax.broadcasted_iota(jnp.int32, (2 * n, n), 0) % n
    s = lax.broadcasted_iota(jnp.int32, (2 * n, n), 1)
    return ((j > s) if kind == "after" else (j < s)).astype(BF16)


LOG2E = 1.4426950408889634


def _log_terms(z):
    lse = jnp.log(1.0 + jnp.exp2(jnp.abs(z) * (-LOG2E)))
    logsig = jnp.minimum(z, 0.0) - lse
    return logsig, logsig - z


def _cumsum_mm(x, u2_ref):
    hi, lo = _split(x)
    return _dot(jnp.concatenate([hi, lo], axis=1), u2_ref[...])


def _head_rows(x2, scale):
    lane = lax.broadcasted_iota(jnp.int32, (1, 128), 1)
    zero = jnp.zeros_like(x2)
    both = jnp.concatenate([jnp.where(lane < HEAD_DIM, x2, zero), jnp.where(lane >= HEAD_DIM, x2, zero)], axis=0)
    return both * scale


def _causal_mask(i, ks, tq, tk):
    row = lax.broadcasted_iota(jnp.int32, (2 * tq, tk), 0)
    row = i * tq + jnp.where(row >= tq, row - tq, row)
    col = lax.broadcasted_iota(jnp.int32, (2 * tq, tk), 1)
    return (ks + col) < row


def _attn_fwd(qkv, *, name, tq, tk):
    S = qkv.shape[0]
    tq, tk = min(tq, S), min(tk, S)
    assert S % tq == 0 and tq % tk == 0 and S // tk <= 128
    r = tq // tk
    tri = _tri(tk, "after")

    def body(q_ref, k_ref, v_ref, u_ref, o_ref, c_ref, qcat, oacc, cacc, call):
        i = pl.program_id(1)
        lane = lax.broadcasted_iota(jnp.int32, (1, 128), 1)
        qcat[...] = _head_rows(q_ref[...], SCALE)
        oacc[...] = jnp.zeros_like(oacc)
        cacc[...] = jnp.zeros_like(cacc)
        call[...] = jnp.zeros_like(call)

        def step(kb, masked):
            ks = pl.multiple_of(kb * tk, tk)
            kblk = k_ref[pl.ds(ks, tk), :]
            z = _dot(qcat[...], kblk, NT)
            logsig, lom = _log_terms(z)
            if masked:
                msk = _causal_mask(i, ks, tq, tk)
                lom = jnp.where(msk, lom, 0.0)
            tl = _cumsum_mm(lom, u_ref)
            c = cacc[...]
            call[...] = jnp.where(lane == kb, c, call[...])
            loga = logsig + tl + c
            if masked:
                loga = jnp.where(msk, loga, -1e30)
            a = jnp.exp(loga).astype(BF16)
            oacc[...] += _dot(a, v_ref[pl.ds(ks, tk), :])
            cacc[...] = c + tl[:, 0:1] + lom[:, 0:1]

        for d in range(r):
            step(i * r + (r - 1 - d), True)

        def loop(t, carry):
            step(i * r - 1 - t, False)
            return carry

        lax.fori_loop(0, i * r, loop, 0)
        o_ref[...] = jnp.where(lane < HEAD_DIM, oacc[0:tq], oacc[tq:2 * tq])
        c_ref[...] = jnp.concatenate([call[0:tq], call[tq:2 * tq]], axis=1)

    nbytes = (_nbytes((tq, 128), BF16) + 2 * _nbytes((S, 128), BF16) + _nbytes((2 * tk, tk), BF16)
              + 8 * _nbytes((tq, 128), F32) + 12 * _nbytes((2 * tq, tk), F32))
    return pl.pallas_call(
        body, name=name, grid=(4, S // tq),
        in_specs=[pl.BlockSpec((tq, 128), lambda j, i: (i, j)),
                  pl.BlockSpec((S, 128), lambda j, i: (0, 4 + j)),
                  pl.BlockSpec((S, 128), lambda j, i: (0, 8 + j)),
                  pl.BlockSpec((2 * tk, tk), lambda j, i: (0, 0))],
        out_specs=[pl.BlockSpec((tq, 128), lambda j, i: (i, j)),
                   pl.BlockSpec((tq, 256), lambda j, i: (i, j))],
        out_shape=[jax.ShapeDtypeStruct((S, D_SB), F32), jax.ShapeDtypeStruct((S, 1024), F32)],
        scratch_shapes=[pltpu.VMEM((2 * tq, 128), BF16), pltpu.VMEM((2 * tq, 128), F32),
                        pltpu.VMEM((2 * tq, 1), F32), pltpu.VMEM((2 * tq, 128), F32)],
        compiler_params=_params(("parallel", "arbitrary"), nbytes),
    )(qkv, qkv, qkv, tri)


def _attn_bwd(qkv, carry, do, *, name, tq, tk):
    S = qkv.shape[0]
    tq, tk = min(tq, S), min(tk, S)
    assert S % tq == 0 and tq % tk == 0 and S // tk <= 128
    r = tq // tk
    nkb = S // tk
    nq = S // tq
    tri_after = _tri(tk, "after")
    tri_before = _tri(tk, "before")

    def body(q_ref, k_ref, v_ref, c_ref, do_ref, ua_ref, ub_ref, dq_ref, dk_ref, dv_ref,
             qcat, docat, qcat_t, docat_t, ccat, dqacc, pacc, dkt, dvt):
        i = pl.program_id(1)
        lane = lax.broadcasted_iota(jnp.int32, (1, 128), 1)
        sub = lax.broadcasted_iota(jnp.int32, (128, 1), 0)
        q2 = q_ref[...]
        do2 = do_ref[...]
        qcat[...] = _head_rows(q2, SCALE)
        docat[...] = _head_rows(do2, 1.0).astype(BF16)
        qt = q2.astype(F32).T * SCALE
        dot_ = do2.T
        qcat_t[...] = jnp.concatenate([jnp.where(sub < HEAD_DIM, qt, 0.0), jnp.where(sub >= HEAD_DIM, qt, 0.0)],
                                      axis=1).astype(BF16)
        docat_t[...] = jnp.concatenate([jnp.where(sub < HEAD_DIM, dot_, 0.0), jnp.where(sub >= HEAD_DIM, dot_, 0.0)],
                                       axis=1).astype(BF16)
        ccat[0:tq] = c_ref[:, 0:128]
        ccat[tq:2 * tq] = c_ref[:, 128:256]

        @pl.when(i == 0)
        def _():
            dkt[...] = jnp.zeros_like(dkt)
            dvt[...] = jnp.zeros_like(dvt)

        dqacc[...] = jnp.zeros_like(dqacc)
        pacc[...] = jnp.zeros_like(pacc)

        def step(kb, masked):
            ks = pl.multiple_of(kb * tk, tk)
            kblk = k_ref[pl.ds(ks, tk), :]
            z = _dot(qcat[...], kblk, NT)
            logsig, lom = _log_terms(z)
            if masked:
                msk = _causal_mask(i, ks, tq, tk)
                lom = jnp.where(msk, lom, 0.0)
            c = jnp.sum(jnp.where(lane == kb, ccat[...], 0.0), axis=1, keepdims=True)
            loga = logsig + _cumsum_mm(lom, ua_ref) + c
            if masked:
                loga = jnp.where(msk, loga, -1e30)
            a = jnp.exp(loga)
            g = a * _dot(docat[...], v_ref[pl.ds(ks, tk), :], NT)
            before = _cumsum_mm(g, ub_ref)
            pc = pacc[...]
            dz = g - jnp.exp(logsig) * (g + before + pc)
            if masked:
                dz = jnp.where(msk, dz, 0.0)
            dzb = dz.astype(BF16)
            dqacc[...] += _dot(dzb, kblk)
            dkt[kb] += _dot(qcat_t[...], dzb)
            dvt[kb] += _dot(docat_t[...], a.astype(BF16))
            pacc[...] = pc + before[:, tk - 1:tk] + g[:, tk - 1:tk]

        def loop(t, carry_):
            step(t, False)
            return carry_

        lax.fori_loop(0, i * r, loop, 0)
        for d in range(r):
            step(i * r + d, True)
        dq_ref[...] = jnp.where(lane < HEAD_DIM, dqacc[0:tq], dqacc[tq:2 * tq]) * SCALE

        @pl.when(i == nq - 1)
        def _():
            for kb in range(nkb):
                dk_ref[kb * tk:(kb + 1) * tk, :] = dkt[kb].T
                dv_ref[kb * tk:(kb + 1) * tk, :] = dvt[kb].T

    nbytes = (_nbytes((tq, 128), BF16) + 2 * _nbytes((S, 128), BF16) + 2 * _nbytes((2 * tk, tk), BF16)
              + 12 * _nbytes((tq, 128), F32) + 4 * _nbytes((S, 128), F32) + 14 * _nbytes((2 * tq, tk), F32))
    blk = pl.BlockSpec((tq, 128), lambda j, i: (i, j))
    full = pl.BlockSpec((S, 128), lambda j, i: (0, j))
    tri_spec = pl.BlockSpec((2 * tk, tk), lambda j, i: (0, 0))
    dq, dk, dv = pl.pallas_call(
        body, name=name, grid=(4, nq),
        in_specs=[blk,
                  pl.BlockSpec((S, 128), lambda j, i: (0, 4 + j)),
                  pl.BlockSpec((S, 128), lambda j, i: (0, 8 + j)),
                  pl.BlockSpec((tq, 256), lambda j, i: (i, j)),
                  blk, tri_spec, tri_spec],
        out_specs=[blk, full, full],
        out_shape=[jax.ShapeDtypeStruct((S, D_SB), F32)] * 3,
        scratch_shapes=[pltpu.VMEM((2 * tq, 128), BF16), pltpu.VMEM((2 * tq, 128), BF16),
                        pltpu.VMEM((128, 2 * tq), BF16), pltpu.VMEM((128, 2 * tq), BF16),
                        pltpu.VMEM((2 * tq, 128), F32), pltpu.VMEM((2 * tq, 128), F32), pltpu.VMEM((2 * tq, 1), F32),
                        pltpu.VMEM((nkb, 128, tk), F32), pltpu.VMEM((nkb, 128, tk), F32)],
        compiler_params=_params(("parallel", "arbitrary"), nbytes),
    )(qkv, qkv, qkv, carry, do, tri_after, tri_before)
    return dq, dk, dv


def _group_mats():
    lanes = jnp.arange(D_MODEL) // HEAD_DIM
    gs = (lanes[:, None] == jnp.arange(128)[None, :]).astype(BF16)
    return gs, gs.T


def _group_sum_bcast(x, gs, gb):
    hi, lo = _split(x)
    s = _dot(hi, gs) + _dot(lo, gs)
    return _bcast(s, gb)


def _bcast(s, gb):
    hi, lo = _split(s)
    return _dot(hi, gb) + _dot(lo, gb)


def _pool_lane_consts():
    lane = lax.broadcasted_iota(jnp.int32, (1, D_POOL), 1)
    grp = lane // (D_POOL // 4)
    win = jnp.where(grp == 0, 2, jnp.where(grp == 1, 4, jnp.where(grp == 2, 8, 16)))
    return grp, win


def _by_group(grp, s2, s4, s8, s16):
    return jnp.where(grp == 0, s2, jnp.where(grp == 1, s4, jnp.where(grp == 2, s8, s16)))


def _mixers(i, ts, prev_ref, cur_ref, cw_ref, pw_ref, ps_ref):
    cur = cur_ref[...]
    prev = jnp.where(i == 0, 0.0, prev_ref[...])
    ext = jnp.concatenate([prev, cur], axis=0)
    n = HALO + ts

    def back(a, k):
        return pltpu.roll(a, k, 0)

    u = ext[:, D_CONV:2 * D_CONV] * ext[:, 2 * D_CONV:3 * D_CONV]
    p = ext[:, 3 * D_CONV:]
    cv = (cw_ref[0:1, :] * back(u, 2) + cw_ref[1:2, :] * back(u, 1) + cw_ref[2:3, :] * u)[HALO:]
    s2 = p + back(p, 1)
    s4 = s2 + back(s2, 2)
    s8 = s4 + back(s4, 4)
    s16 = s8 + back(s8, 8)
    grp, win = _pool_lane_consts()
    t1 = i * ts + 1 + lax.broadcasted_iota(jnp.int32, (ts, 1), 0)
    cnt = jnp.minimum(t1, win).astype(F32)
    pooled = _by_group(grp, s2, s4, s8, s16)[HALO:] / cnt - p[HALO:]
    yp = _dot(pooled.astype(BF16), pw_ref[...])
    del n
    return dict(b=cur[:, 0:D_CONV], u=u, cv=cv, pooled=pooled, yp=yp, cnt=cnt,
                conv_out=cur[:, 0:D_CONV] * cv, pool_out=yp * ps_ref[...])


def _halo_specs(ts, S, width):
    nb = ts // HALO
    last = S // HALO - 1
    prev = pl.BlockSpec((HALO, width), lambda i: (jnp.maximum(i * nb - 1, 0), 0))
    nxt = pl.BlockSpec((HALO, width), lambda i: (jnp.minimum((i + 1) * nb, last), 0))
    return prev, nxt


def _mixer_fwd(rest, attn, cw8, pwbd, ps, gain, *, name, ts):
    S = rest.shape[0]
    ts = min(ts, S)
    gs, gb = _group_mats()

    def body(prev_ref, cur_ref, attn_ref, cw_ref, pw_ref, ps_ref, gain_ref, gs_ref, gb_ref, o_ref):
        i = pl.program_id(0)
        f = _mixers(i, ts, prev_ref, cur_ref, cw_ref, pw_ref, ps_ref)
        mix = jnp.concatenate([attn_ref[...], f["conv_out"], f["pool_out"]], axis=1)
        ss = _group_sum_bcast(mix * mix, gs_ref[...], gb_ref[...])
        rinv = lax.rsqrt(ss * (1.0 / HEAD_DIM) + RMS_EPS)
        o_ref[...] = (mix * rinv * gain_ref[...]).astype(BF16)

    prev, _ = _halo_specs(ts, S, D_REST)
    row = lambda w: pl.BlockSpec((ts, w), lambda i: (i, 0))
    const = lambda a: pl.BlockSpec(a.shape, lambda i: (0, 0))
    nbytes = 12 * _nbytes((ts + HALO, D_REST), F32)
    return pl.pallas_call(
        body, name=name, grid=(S // ts,),
        in_specs=[prev, row(D_REST), row(D_SB), const(cw8), const(pwbd), const(ps), const(gain), const(gs), const(gb)],
        out_specs=row(D_MODEL),
        out_shape=jax.ShapeDtypeStruct((S, D_MODEL), BF16),
        compiler_params=_params(("parallel",), nbytes),
    )(rest, rest, attn, cw8, pwbd, ps, gain, gs, gb)


def _mixer_bwd1(dmixn, rest, attn, cw8, pwbd, ps, gain, *, name, ts):
    S = rest.shape[0]
    ts = min(ts, S)
    gs, gb = _group_mats()

    def body(dm_ref, prev_ref, cur_ref, attn_ref, cw_ref, pw_ref, ps_ref, gain_ref, gs_ref, gb_ref,
             da_ref, aux_ref, dg_ref, dsc_ref, dcw_ref, dpw_ref):
        i = pl.program_id(0)
        f = _mixers(i, ts, prev_ref, cur_ref, cw_ref, pw_ref, ps_ref)
        mix = jnp.concatenate([attn_ref[...], f["conv_out"], f["pool_out"]], axis=1)
        gsm, gbm = gs_ref[...], gb_ref[...]
        ss = _group_sum_bcast(mix * mix, gsm, gbm)
        rinv = lax.rsqrt(ss * (1.0 / HEAD_DIM) + RMS_EPS)
        dm = dm_ref[...]
        xn = mix * rinv
        dyg = dm * gain_ref[...]
        gm = _group_sum_bcast(dyg * xn, gsm, gbm) * (1.0 / HEAD_DIM)
        dmix = rinv * (dyg - xn * gm)
        da_ref[...] = dmix[:, 0:D_SB]
        dco = dmix[:, D_SB:D_SB + D_CONV]
        dpo = dmix[:, D_SB + D_CONV:]
        dcv = dco * f["b"]
        dyp = dpo * ps_ref[...]
        dpooled = _dot(dyp.astype(BF16), pw_ref[...], NT)
        aux_ref[...] = jnp.concatenate([dco * f["cv"], dcv, dpooled / f["cnt"], dpooled], axis=1)
        u = f["u"]
        parts = [
            _sum8(dm * xn),
            _sum8(dpo * f["yp"]),
            jnp.concatenate([_sum8(dcv * pltpu.roll(u, 2, 0)[HALO:]), _sum8(dcv * pltpu.roll(u, 1, 0)[HALO:]),
                             _sum8(dcv * u[HALO:])], axis=0),
            _dot(f["pooled"].astype(BF16), dyp.astype(BF16), TN),
        ]
        outs = [dg_ref, dsc_ref, dcw_ref, dpw_ref]

        @pl.when(i == 0)
        def _():
            for o, v in zip(outs, parts):
                o[...] = v

        @pl.when(i > 0)
        def _():
            for o, v in zip(outs, parts):
                o[...] += v

    prev, _ = _halo_specs(ts, S, D_REST)
    row = lambda w: pl.BlockSpec((ts, w), lambda i: (i, 0))
    const = lambda a: pl.BlockSpec(a.shape, lambda i: (0, 0))
    acc = lambda r_, w: pl.BlockSpec((r_, w), lambda i: (0, 0))
    nbytes = 16 * _nbytes((ts + HALO, D_REST), F32)
    return pl.pallas_call(
        body, name=name, grid=(S // ts,),
        in_specs=[row(D_MODEL), prev, row(D_REST), row(D_SB), const(cw8), const(pwbd), const(ps), const(gain),
                  const(gs), const(gb)],
        out_specs=[row(D_SB), row(D_REST), acc(8, D_MODEL), acc(8, D_POOL), acc(24, D_CONV), acc(D_POOL, D_POOL)],
        out_shape=[jax.ShapeDtypeStruct((S, D_SB), F32), jax.ShapeDtypeStruct((S, D_REST), F32),
                   jax.ShapeDtypeStruct((8, D_MODEL), F32), jax.ShapeDtypeStruct((8, D_POOL), F32),
                   jax.ShapeDtypeStruct((24, D_CONV), F32), jax.ShapeDtypeStruct((D_POOL, D_POOL), F32)],
        compiler_params=_params(("arbitrary",), nbytes),
    )(dmixn, rest, rest, attn, cw8, pwbd, ps, gain, gs, gb)


def _mixer_bwd2(aux, rest, cw8, *, name, ts):
    S = rest.shape[0]
    ts = min(ts, S)
    nblk = S // ts

    def body(cur_ref, nxt_ref, rest_ref, cw_ref, o_ref):
        i = pl.program_id(0)
        cur = cur_ref[...]
        nxt = jnp.where(i == nblk - 1, 0.0, nxt_ref[...])
        ext = jnp.concatenate([cur, nxt], axis=0)
        n = ts + HALO

        def fwd(a, k):
            return pltpu.roll(a, n - k, 0)

        dcv = ext[:, D_CONV:2 * D_CONV]
        dps = ext[:, 2 * D_CONV:3 * D_CONV]
        du = (cw_ref[2:3, :] * dcv + cw_ref[1:2, :] * fwd(dcv, 1) + cw_ref[0:1, :] * fwd(dcv, 2))[0:ts]
        f2 = dps + fwd(dps, 1)
        f4 = f2 + fwd(f2, 2)
        f8 = f4 + fwd(f4, 4)
        f16 = f8 + fwd(f8, 8)
        grp, _ = _pool_lane_consts()
        dp = _by_group(grp, f2, f4, f8, f16)[0:ts] - cur[:, 3 * D_CONV:]
        rest_v = rest_ref[...]
        c_gate = rest_v[:, D_CONV:2 * D_CONV]
        h = rest_v[:, 2 * D_CONV:3 * D_CONV]
        o_ref[...] = jnp.concatenate([cur[:, 0:D_CONV], du * h, du * c_gate, dp], axis=1)

    _, nxt = _halo_specs(ts, S, D_REST)
    row = pl.BlockSpec((ts, D_REST), lambda i: (i, 0))
    return pl.pallas_call(
        body, name=name, grid=(nblk,),
        in_specs=[row, nxt, row, pl.BlockSpec(cw8.shape, lambda i: (0, 0))],
        out_specs=row,
        out_shape=jax.ShapeDtypeStruct((S, D_REST), F32),
        compiler_params=_params(("parallel",), 10 * _nbytes((ts + HALO, D_REST), F32)),
    )(aux, aux, rest, cw8)


def _block_diag(pw):
    z = jnp.zeros((4, 64, 4, 64), pw.dtype)
    for g in range(4):
        z = z.at[g, :, g, :].set(pw[g])
    return z.reshape(256, 256)


def _rows8(v, rows=8):
    return jnp.pad(v, ((0, rows - v.shape[0]), (0, 0)))


TILES = dict(tm=512, ts=512, tq=512, tk=256)


def _local_step(x, tgt, w, t=None):
    t = dict(TILES, **(t or {}))
    tm, ts, tq, tk = t["tm"], t["ts"], t["tq"], t["tk"]
    saved = []
    xl = x
    for l in range(DEPTH):
        n = f"l{l}_"
        w_in = w["w_in"][l]
        qkv = _matmul(xl, w_in[:, :D_QKV], name=n + "proj_qkv", tm=1024, tn=512, tk=1024, out_dtype=BF16)
        rest = _matmul(xl, w_in[:, D_QKV:], name=n + "proj_rest", tm=1024, tn=512, tk=1024)
        attn, carry = _attn_fwd(qkv, name=n + "attn_fwd", tq=tq, tk=tk)
        cw8 = _rows8(w["conv_w"][l])
        pwbd = _block_diag(w["pool_w"][l]).astype(BF16)
        ps = w["pool_scale"][l].reshape(1, D_POOL)
        gain = w["mix_norm_g"][l].reshape(1, D_MODEL)
        mixn = _mixer_fwd(rest, attn, cw8, pwbd, ps, gain, name=n + "mixer_fwd", ts=ts)
        x1, xh1, rs1 = _matmul_ln(mixn, w["w_o"][l], xl, w["ln1_g"][l], w["ln1_b"][l], name=n + "wo_ln", tm=tm, tk=1024)
        hpre = _matmul(x1, w["w_up"][l], name=n + "ffn_up", tm=1024, tn=1024, tk=1024)
        x2, xh2, rs2 = _matmul_ln(hpre, w["w_down"][l], x1, w["ln2_g"][l], w["ln2_b"][l], name=n + "ffn_down_ln",
                                  tm=tm, tk=1024, a_pro="relu2")
        saved.append(dict(xin=xl, qkv=qkv, rest=rest, attn=attn, carry=carry, cw8=cw8, pwbd=pwbd, ps=ps, gain=gain,
                          mixn=mixn, x1=x1, xh1=xh1, rs1=rs1, hpre=hpre, xh2=xh2, rs2=rs2))
        xl = x2

    dy, lsum = _loss_grad(xl, tgt, name="loss_grad", tm=tm)
    grads = {k: [None] * DEPTH for k in
             ("w_in", "conv_w", "pool_w", "pool_scale", "mix_norm_g", "w_o", "ln1_g", "ln1_b", "w_up", "w_down",
              "ln2_g", "ln2_b")}
    for l in reversed(range(DEPTH)):
        n = f"l{l}_"
        s = saved[l]
        dr2, dg2, db2 = _ln_bwd(dy, s["xh2"], s["rs2"], w["ln2_g"][l], name=n + "ln2_bwd", tm=tm)
        dhpre = _matmul(dr2, w["w_down"][l], name=n + "ffn_down_dx", tm=1024, tn=1024, tk=1024, tb=True,
                        out_dtype=BF16, epi="drelu2", e=s["hpre"])
        grads["w_down"][l] = _matmul(s["hpre"], dr2, name=n + "ffn_down_dw", tm=1024, tn=1024, tk=512, ta=True,
                                     a_pro="relu2")
        dx1 = _matmul(dhpre, w["w_up"][l], name=n + "ffn_up_dx", tm=1024, tn=1024, tk=1024, tb=True,
                      epi="add", e=dr2, e_scale=ALPHA)
        grads["w_up"][l] = _matmul(s["x1"], dhpre, name=n + "ffn_up_dw", tm=1024, tn=1024, tk=512, ta=True)
        dr1, dg1, db1 = _ln_bwd(dx1, s["xh1"], s["rs1"], w["ln1_g"][l], name=n + "ln1_bwd", tm=tm)
        dmixn = _matmul(dr1, w["w_o"][l], name=n + "wo_dx", tm=1024, tn=1024, tk=1024, tb=True)
        grads["w_o"][l] = _matmul(s["mixn"], dr1, name=n + "wo_dw", tm=1024, tn=1024, tk=512, ta=True)
        d_attn, aux, dgain, dsc, dcw, dpw = _mixer_bwd1(dmixn, s["rest"], s["attn"], s["cw8"], s["pwbd"], s["ps"],
                                                        s["gain"], name=n + "mixer_bwd1", ts=ts)
        drest = _mixer_bwd2(aux, s["rest"], s["cw8"], name=n + "mixer_bwd2", ts=ts)
        dq, dk, dv = _attn_bwd(s["qkv"], s["carry"], d_attn, name=n + "attn_bwd", tq=tq, tk=tk)
        dqkv = jnp.concatenate([dq, dk, dv], axis=1)
        w_in = w["w_in"][l]
        dxa = _matmul(dqkv, w_in[:, :D_QKV], name=n + "proj_qkv_dx", tm=1024, tn=1024, tk=512, tb=True,
                      epi="add", e=dr1, e_scale=ALPHA)
        dy = _matmul(drest, w_in[:, D_QKV:], name=n + "proj_rest_dx", tm=1024, tn=1024, tk=1024, tb=True,
                     epi="add", e=dxa, e_scale=1.0)
        dwq = _matmul(s["xin"], dqkv, name=n + "proj_qkv_dw", tm=1024, tn=512, tk=512, ta=True)
        dwr = _matmul(s["xin"], drest, name=n + "proj_rest_dw", tm=1024, tn=1024, tk=512, ta=True)
        grads["w_in"][l] = jnp.concatenate([dwq, dwr], axis=1)
        grads["ln2_g"][l] = dg2.sum(0)
        grads["ln2_b"][l] = db2.sum(0)
        grads["ln1_g"][l] = dg1.sum(0)
        grads["ln1_b"][l] = db1.sum(0)
        grads["mix_norm_g"][l] = dgain.sum(0)
        grads["pool_scale"][l] = dsc.sum(0)
        grads["conv_w"][l] = dcw.reshape(3, 8, D_CONV).sum(1)
        grads["pool_w"][l] = jnp.stack([dpw[64 * g:64 * g + 64, 64 * g:64 * g + 64] for g in range(4)])
    grads = {k: jnp.stack(v) for k, v in grads.items()}
    return lsum, dy, grads


ANY = pl.BlockSpec(memory_space=pl.ANY)


def _place():
    x, y, c = lax.axis_index("x"), lax.axis_index("y"), lax.axis_index("c")
    chips = [(1 - x, y), (x, 1 - y), (1 - x, 1 - y)]
    return x, y, c, chips


def _remote(src, dst, send_sems, recv_sems, k, to):
    return pltpu.make_async_remote_copy(src_ref=src, dst_ref=dst, send_sem=send_sems.at[k], recv_sem=recv_sems.at[k],
                                        device_id=to, device_id_type=MESH)


DMA_CHUNK_BYTES = 1 << 20


def _n_chunks(rows, dtype, width=D_MODEL):
    tiles = rows // 16
    want = max(1, (rows * width * jnp.dtype(dtype).itemsize) // DMA_CHUNK_BYTES)
    best = 1
    for n in range(1, tiles + 1):
        if tiles % n == 0 and n <= want:
            best = n
    return best


class _Chunked:
    def __init__(self, src, dst, send_sems, recv_sems, k, to, n):
        self.args = (send_sems, recv_sems, k, to)
        self.whole = self._one(src, dst)
        rows = src.shape[-2]
        assert rows % n == 0
        step = rows // n
        leads = [()] if len(src.shape) == 2 else [(i,) for i in range(src.shape[0])]
        self.parts = [self._one(src.at[(*lead, pl.ds(t * step, step))], dst.at[(*lead, pl.ds(t * step, step))])
                      for lead in leads for t in range(n)]

    def _one(self, src, dst):
        send_sems, recv_sems, k, to = self.args
        if to is None:
            return pltpu.make_async_copy(src, dst, recv_sems)
        return _remote(src, dst, send_sems, recv_sems, k, to)

    def like(self, src, dst):
        return self._one(src, dst)

    def start(self):
        for p in self.parts:
            p.start()

    def wait(self):
        self.whole.wait()

    def wait_send(self):
        self.whole.wait_send()

    def wait_recv(self):
        self.whole.wait_recv()


def _allgather_chips(pack, *, name):
    R, C = pack.shape
    H = R // 2

    def body(p_ref, o_ref, send_sems, recv_sems, local_sem):
        x, y, c, chips = _place()
        my = 2 * x + y
        sib = (x, y, 1 - c)
        n = _n_chunks(H, pack.dtype)

        def half(k, hc):
            return o_ref.at[k, pl.ds(hc * H, H), :]

        mine = _Chunked(p_ref, o_ref.at[my], None, local_sem, None, None, 2 * n)
        mine.start()
        first = [_Chunked(p_ref.at[pl.ds(c * H, H), :], half(my, c), send_sems, recv_sems, j, (cx, cy, c), n)
                 for j, (cx, cy) in enumerate(chips)]
        for cp in first:
            cp.start()
        passed = []
        for j, (cx, cy) in enumerate(chips):
            k = 2 * cx + cy
            first[j].like(half(k, c), half(k, c)).wait_recv()
            fwd = _Chunked(half(k, c), half(k, c), send_sems, recv_sems, 3 + j, sib, n)
            fwd.start()
            passed.append(fwd)
        for j, (cx, cy) in enumerate(chips):
            k = 2 * cx + cy
            passed[j].like(half(k, 1 - c), half(k, 1 - c)).wait_recv()
        for cp in first + passed:
            cp.wait_send()
        mine.wait()

    return pl.pallas_call(
        body, name=name, in_specs=[ANY], out_specs=ANY,
        out_shape=jax.ShapeDtypeStruct((N_CHIPS, R, C), pack.dtype),
        scratch_shapes=[pltpu.SemaphoreType.DMA((6,)), pltpu.SemaphoreType.DMA((6,)), pltpu.SemaphoreType.DMA],
    )(pack)


def _swap_halves(gp, *, name):
    K, R, C = gp.shape
    H = R // 2

    def body(g_ref, mine_ref, theirs_ref, send_sems, recv_sems, local_sem):
        x, y, c, _ = _place()
        n = _n_chunks(H, gp.dtype)
        keep = _Chunked(g_ref.at[:, pl.ds(c * H, H), :], mine_ref, None, local_sem, None, None, n)
        keep.start()
        cp = _Chunked(g_ref.at[:, pl.ds((1 - c) * H, H), :], theirs_ref, send_sems, recv_sems, 0, (x, y, 1 - c), n)
        cp.start()
        cp.wait()
        keep.wait()

    shape = jax.ShapeDtypeStruct((K, H, C), gp.dtype)
    return pl.pallas_call(
        body, name=name, in_specs=[ANY], out_specs=[ANY, ANY], out_shape=[shape, shape],
        scratch_shapes=[pltpu.SemaphoreType.DMA((1,)), pltpu.SemaphoreType.DMA((1,)), pltpu.SemaphoreType.DMA],
    )(gp)


def _scatter_chips(part, *, name):
    K, H, C = part.shape

    def body(p_ref, o_ref, send_sems, recv_sems):
        x, y, c, chips = _place()
        n = _n_chunks(H, part.dtype)
        copies = [_Chunked(p_ref.at[2 * cx + cy], o_ref.at[j], send_sems, recv_sems, j, (cx, cy, c), n)
                  for j, (cx, cy) in enumerate(chips)]
        for cp in copies:
            cp.start()
        for cp in copies:
            cp.wait()

    return pl.pallas_call(
        body, name=name, in_specs=[ANY], out_specs=ANY,
        out_shape=jax.ShapeDtypeStruct((3, H, C), part.dtype),
        scratch_shapes=[pltpu.SemaphoreType.DMA((3,)), pltpu.SemaphoreType.DMA((3,))],
    )(part)


def _join_halves(half, *, name):
    H, C = half.shape

    def body(h_ref, o_ref, send_sems, recv_sems, local_sem):
        x, y, c, _ = _place()
        n = _n_chunks(H, half.dtype)
        rows = o_ref.at[pl.ds(c * H, H), :]
        keep = _Chunked(h_ref, rows, None, local_sem, None, None, n)
        keep.start()
        cp = _Chunked(h_ref, rows, send_sems, recv_sems, 0, (x, y, 1 - c), n)
        cp.start()
        cp.wait_send()
        cp.like(h_ref, o_ref.at[pl.ds((1 - c) * H, H), :]).wait_recv()
        keep.wait()

    return pl.pallas_call(
        body, name=name, in_specs=[ANY], out_specs=ANY,
        out_shape=jax.ShapeDtypeStruct((2 * H, C), half.dtype),
        scratch_shapes=[pltpu.SemaphoreType.DMA((1,)), pltpu.SemaphoreType.DMA((1,)), pltpu.SemaphoreType.DMA],
    )(half)


def _allreduce_small(v, *, name):
    R, C = v.shape
    n_dev = 8

    def body(v_ref, o_ref, gat, send_sems, recv_sems):
        x, y, c, chips = _place()
        sib = (x, y, 1 - c)

        def rows(px, py, pc):
            return gat.at[4 * px + 2 * py + pc]

        gat[4 * x + 2 * y + c] = v_ref[...]
        first = [_remote(v_ref, rows(x, y, c), send_sems, recv_sems, 0, sib)]
        first += [_remote(v_ref, rows(x, y, c), send_sems, recv_sems, 1 + j, (cx, cy, c))
                  for j, (cx, cy) in enumerate(chips)]
        for cp in first:
            cp.start()
        passed = []
        for j, (cx, cy) in enumerate(chips):
            _remote(v_ref, rows(cx, cy, c), send_sems, recv_sems, 1 + j, sib).wait_recv()
            fwd = _remote(rows(cx, cy, c), rows(cx, cy, c), send_sems, recv_sems, 4 + j, sib)
            fwd.start()
            passed.append(fwd)
        _remote(v_ref, rows(x, y, 1 - c), send_sems, recv_sems, 0, sib).wait_recv()
        for j, (cx, cy) in enumerate(chips):
            _remote(v_ref, rows(cx, cy, 1 - c), send_sems, recv_sems, 4 + j, sib).wait_recv()
        for cp in first + passed:
            cp.wait_send()
        acc = gat[0]
        for d in range(1, n_dev):
            acc = acc + gat[d]
        o_ref[...] = acc

    vm = pl.BlockSpec(memory_space=pltpu.VMEM)
    return pl.pallas_call(
        body, name=name, in_specs=[vm], out_specs=vm,
        out_shape=jax.ShapeDtypeStruct((R, C), F32),
        scratch_shapes=[pltpu.VMEM((n_dev, R, C), F32), pltpu.SemaphoreType.DMA((7,)), pltpu.SemaphoreType.DMA((7,))],
    )(v)


def _add_pairs(a, b, *, name, tr):
    K, H, C = a.shape
    tr = min(tr, H)
    assert H % tr == 0

    def body(a_ref, b_ref, o_ref):
        o_ref[...] = (a_ref[...].astype(F32) + b_ref[...].astype(F32)).astype(BF16)

    blk = pl.BlockSpec((1, tr, C), lambda k, i: (k, i, 0))
    return pl.pallas_call(
        body, name=name, grid=(K, H // tr), in_specs=[blk, blk], out_specs=blk,
        out_shape=jax.ShapeDtypeStruct((K, H, C), BF16),
        compiler_params=_params(("parallel", "parallel"), 3 * _nbytes((tr, C), BF16)),
    )(a, b)


def _add_final(a, b, others, *, name, tr):
    H, C = a.shape
    tr = min(tr, H)
    assert H % tr == 0

    def body(a_ref, b_ref, o_ref_in, out_ref):
        acc = a_ref[...].astype(F32) + b_ref[...].astype(F32)
        for j in range(3):
            acc = acc + o_ref_in[j].astype(F32)
        out_ref[...] = acc

    blk = pl.BlockSpec((tr, C), lambda i: (i, 0))
    return pl.pallas_call(
        body, name=name, grid=(H // tr,),
        in_specs=[blk, blk, pl.BlockSpec((3, tr, C), lambda i: (0, i, 0))], out_specs=blk,
        out_shape=jax.ShapeDtypeStruct((H, C), F32),
        compiler_params=_params(("parallel",), 6 * _nbytes((tr, C), F32)),
    )(a, b, others)


def _adamw(w, g, m, v, *, name, tr, row0=0):
    R, C = w.shape
    tr = min(tr, R)
    assert R % tr == 0 and row0 % tr == 0
    off = row0 // tr

    def body(w_ref, g_ref, m_ref, v_ref, go_ref, d_ref, mo_ref, vo_ref):
        gv = g_ref[...]
        m2 = ADAM_B1 * m_ref[...] + (1.0 - ADAM_B1) * gv
        v2 = ADAM_B2 * v_ref[...] + (1.0 - ADAM_B2) * jnp.square(gv)
        m_hat = m2 / (1.0 - ADAM_B1 ** ADAM_STEP)
        v_hat = v2 / (1.0 - ADAM_B2 ** ADAM_STEP)
        d_ref[...] = -ADAM_LR * (m_hat / (jnp.sqrt(v_hat) + ADAM_EPS) + ADAM_WD * w_ref[...])
        go_ref[...] = gv
        mo_ref[...] = m2
        vo_ref[...] = v2

    blk = pl.BlockSpec((tr, C), lambda i: (i, 0))
    shape = jax.ShapeDtypeStruct((R, C), F32)
    return pl.pallas_call(
        body, name=name, grid=(R // tr,),
        in_specs=[blk, pl.BlockSpec((tr, C), lambda i: (i + off, 0)), blk, blk], out_specs=[blk] * 4,
        out_shape=[shape] * 4,
        compiler_params=_params(("parallel",), 8 * _nbytes((tr, C), F32)),
    )(w, g, m, v)


BIG = ("w_in", "w_o", "w_up", "w_down")
BIG_AXIS = dict(w_in=2, w_o=1, w_up=2, w_down=1)
SMALL = ("pool_w", "pool_scale", "mix_norm_g", "ln1_g", "ln1_b", "ln2_g", "ln2_b")
CONV_ROWS = 32
SMALL_ROWS = 48


def _big_rows(shards):
    sizes = [shards[n].size // D_MODEL for n in BIG]
    starts = [sum(sizes[:i]) for i in range(len(sizes))]
    return sizes, starts


def _pad_rows(flat, rows):
    return jnp.pad(flat, (0, rows * D_MODEL - flat.shape[0])).reshape(rows, D_MODEL)


def kernel(x, w_in, conv_w, pool_w, pool_scale, mix_norm_g, w_o, ln1_g, ln1_b, w_up, w_down, ln2_g, ln2_b, loss_target, m_w_in, m_conv_w, m_pool_w, m_pool_scale, m_mix_norm_g, m_w_o, m_ln1_g, m_ln1_b, m_w_up, m_w_down, m_ln2_g, m_ln2_b, v_w_in, v_conv_w, v_pool_w, v_pool_scale, v_mix_norm_g, v_w_o, v_ln1_g, v_ln1_b, v_w_up, v_w_down, v_ln2_g, v_ln2_b):
    wts = dict(w_in=w_in, conv_w=conv_w, pool_w=pool_w, pool_scale=pool_scale, mix_norm_g=mix_norm_g, w_o=w_o,
               ln1_g=ln1_g, ln1_b=ln1_b, w_up=w_up, w_down=w_down, ln2_g=ln2_g, ln2_b=ln2_b)
    mom = dict(w_in=m_w_in, conv_w=m_conv_w, pool_w=m_pool_w, pool_scale=m_pool_scale, mix_norm_g=m_mix_norm_g,
               w_o=m_w_o, ln1_g=m_ln1_g, ln1_b=m_ln1_b, w_up=m_w_up, w_down=m_w_down, ln2_g=m_ln2_g, ln2_b=m_ln2_b)
    var = dict(w_in=v_w_in, conv_w=v_conv_w, pool_w=v_pool_w, pool_scale=v_pool_scale, mix_norm_g=v_mix_norm_g,
               w_o=v_w_o, ln1_g=v_ln1_g, ln1_b=v_ln1_b, w_up=v_w_up, w_down=v_w_down, ln2_g=v_ln2_g, ln2_b=v_ln2_b)
    chip = 2 * lax.axis_index("x") + lax.axis_index("y")
    sizes, starts = _big_rows(wts)
    big_rows = sum(sizes)

    conv_bits = lax.bitcast_convert_type(conv_w.reshape(-1), BF16).reshape(-1)
    pack = jnp.concatenate([wts[n].reshape(-1, D_MODEL).astype(BF16) for n in BIG]
                           + [_pad_rows(conv_bits, CONV_ROWS)], axis=0)
    gathered = _allgather_chips(pack, name="gather_weights")
    full = {}
    for n, size, start in zip(BIG, sizes, starts):
        parts = [gathered[k, start:start + size].reshape(wts[n].shape) for k in range(N_CHIPS)]
        full[n] = jnp.concatenate(parts, axis=BIG_AXIS[n])
    conv_parts = [lax.bitcast_convert_type(gathered[k, big_rows:].reshape(-1)[:2 * conv_w.size].reshape(-1, 2), F32)
                  .reshape(conv_w.shape) for k in range(N_CHIPS)]
    full["conv_w"] = jnp.concatenate(conv_parts, axis=2)
    for n in SMALL:
        full[n] = wts[n]

    lsum, grad_x, grads = _local_step(x[0], loss_target[0], full)

    blocks = []
    for k in range(N_CHIPS):
        rows = []
        for n in BIG:
            ax = BIG_AXIS[n]
            width = wts[n].shape[ax]
            rows.append(lax.slice_in_dim(grads[n], k * width, (k + 1) * width, axis=ax).reshape(-1, D_MODEL))
        blocks.append(jnp.concatenate(rows, axis=0))
    gpack = jnp.stack(blocks).astype(BF16)
    mine, theirs = _swap_halves(gpack, name="grad_swap_cores")
    chip_sum = _add_pairs(mine, theirs, name="grad_add_cores", tr=736)
    from_chips = _scatter_chips(chip_sum, name="grad_scatter_chips")
    mine_k = lax.dynamic_index_in_dim(mine, chip, 0, keepdims=False)
    theirs_k = lax.dynamic_index_in_dim(theirs, chip, 0, keepdims=False)
    half_sum = _add_final(mine_k, theirs_k, from_chips, name="grad_add_chips", tr=736)
    gsum = _join_halves(half_sum, name="grad_join_cores")

    small_flat = jnp.concatenate([grads[n].reshape(-1) for n in SMALL] + [grads["conv_w"].reshape(-1),
                                                                          lsum.sum().reshape(1)])
    small_sum = _allreduce_small(_pad_rows(small_flat, SMALL_ROWS), name="allreduce_small").reshape(-1)
    gsmall = {}
    pos = 0
    for n in SMALL:
        gsmall[n] = small_sum[pos:pos + wts[n].size].reshape(wts[n].shape)
        pos += wts[n].size
    conv_full = small_sum[pos:pos + 4 * conv_w.size].reshape(DEPTH, 3, D_CONV)
    pos += 4 * conv_w.size
    loss = small_sum[pos]
    gsmall["conv_w"] = lax.dynamic_slice_in_dim(conv_full, chip * conv_w.shape[2], conv_w.shape[2], axis=2)

    out_g, out_d, out_m, out_v = {}, {}, {}, {}
    for n, size, start in zip(BIG, sizes, starts):
        shp = wts[n].shape
        res = _adamw(wts[n].reshape(-1, D_MODEL), gsum, mom[n].reshape(-1, D_MODEL), var[n].reshape(-1, D_MODEL),
                     name="adamw_" + n, tr=256, row0=start)
        out_g[n], out_d[n], out_m[n], out_v[n] = [r.reshape(shp) for r in res]
    small_names = SMALL + ("conv_w",)
    packs = [_pad_rows(jnp.concatenate([d[n].reshape(-1) for n in small_names]), SMALL_ROWS)
             for d in (wts, gsmall, mom, var)]
    res = _adamw(*packs, name="adamw_small", tr=SMALL_ROWS)
    pos = 0
    for n in small_names:
        shp = wts[n].shape
        out_g[n], out_d[n], out_m[n], out_v[n] = [r.reshape(-1)[pos:pos + wts[n].size].reshape(shp) for r in res]
        pos += wts[n].size

    order = ("w_in", "conv_w", "pool_w", "pool_scale", "mix_norm_g", "w_o", "ln1_g", "ln1_b", "w_up", "w_down",
             "ln2_g", "ln2_b")
    return (loss, grad_x[None], *[out_g[n] for n in order], *[out_d[n] for n in order],
            *[out_m[n] for n in order], *[out_v[n] for n in order])
```

```python
import functools
import math

import jax
import jax.numpy as jnp
from jax import lax
from jax.experimental import pallas as pl
from jax.experimental.pallas import tpu as pltpu

F32 = jnp.float32
BF16 = jnp.bfloat16
MESH = pl.DeviceIdType.MESH

D_MODEL = 1024
DEPTH = 2
HEAD_DIM = 64
D_SB = 512
D_CONV = 256
D_POOL = 256
D_QKV = 3 * D_SB
D_REST = 3 * D_CONV + D_POOL
D_IN = D_QKV + D_REST
D_FF = 4 * D_MODEL
ALPHA = (2 * DEPTH) ** 0.25
LN_EPS = 1e-5
RMS_EPS = 1e-6
SCALE = HEAD_DIM ** -0.5
N_CHIPS = 4
HALO = 16

ADAM_LR = 0.001
ADAM_B1 = 0.9
ADAM_B2 = 0.999
ADAM_EPS = 1e-08
ADAM_WD = 0.01
ADAM_STEP = 10

VMEM_V7X_BYTES = 64 * 1024 * 1024
VMEM_CAP_BYTES = VMEM_V7X_BYTES - 8 * 1024 * 1024


def _params(sem, block_bytes):
    limit = min(VMEM_CAP_BYTES, max(32 * 1024 * 1024, 3 * block_bytes))
    return pltpu.CompilerParams(dimension_semantics=sem, vmem_limit_bytes=limit)


def _nbytes(shape, dtype):
    return math.prod(shape) * jnp.dtype(dtype).itemsize


def _dot(a, b, dims=(((1,), (0,)), ((), ()))):
    return lax.dot_general(a, b, dims, preferred_element_type=F32)


NT = (((1,), (1,)), ((), ()))
TN = (((0,), (0,)), ((), ()))


def _split(x):
    hi = x.astype(BF16)
    lo = (x - hi.astype(F32)).astype(BF16)
    return hi, lo


def _sum8(x):
    r, c = x.shape
    return x.reshape(r // 8, 8, c).sum(axis=0)


def _matmul(a, b, *, name, tm, tn, tk, ta=False, tb=False, out_dtype=F32,
            a_pro=None, epi=None, e=None, e_scale=1.0):
    M, K = (a.shape[1], a.shape[0]) if ta else a.shape
    N = b.shape[0] if tb else b.shape[1]
    tm, tn, tk = min(tm, M), min(tn, N), min(tk, K)
    assert M % tm == 0 and N % tn == 0 and K % tk == 0, (name, M, N, K)
    nk = K // tk
    dims = (((0 if ta else 1,), (1 if tb else 0,)), ((), ()))

    def body(*refs):
        if epi is None:
            a_ref, b_ref, o_ref, *scr = refs
            e_ref = None
        else:
            a_ref, b_ref, e_ref, o_ref, *scr = refs
        av = a_ref[...]
        if a_pro == "relu2":
            av = jnp.square(jnp.maximum(av.astype(F32), 0.0))
        p = _dot(av.astype(BF16), b_ref[...].astype(BF16), dims)

        def finish(acc):
            if epi == "drelu2":
                acc = acc * (2.0 * jnp.maximum(e_ref[...], 0.0))
            elif epi == "add":
                acc = acc + e_scale * e_ref[...]
            o_ref[...] = acc.astype(out_dtype)

        if nk == 1:
            finish(p)
        else:
            acc_ref = scr[0]
            k = pl.program_id(2)

            @pl.when(k == 0)
            def _():
                acc_ref[...] = p

            @pl.when(k > 0)
            def _():
                acc_ref[...] += p

            @pl.when(k == nk - 1)
            def _():
                finish(acc_ref[...])

    a_spec = pl.BlockSpec((tk, tm), lambda i, j, k: (k, i)) if ta else pl.BlockSpec((tm, tk), lambda i, j, k: (i, k))
    b_spec = pl.BlockSpec((tn, tk), lambda i, j, k: (j, k)) if tb else pl.BlockSpec((tk, tn), lambda i, j, k: (k, j))
    o_spec = pl.BlockSpec((tm, tn), lambda i, j, k: (i, j))
    in_specs = [a_spec, b_spec]
    args = [a, b]
    nbytes = _nbytes((tm, tk), a.dtype) + _nbytes((tk, tn), b.dtype) + 2 * _nbytes((tm, tn), F32)
    if epi is not None:
        in_specs.append(o_spec)
        args.append(e)
        nbytes += _nbytes((tm, tn), e.dtype)
    scratch = [pltpu.VMEM((tm, tn), F32)] if nk > 1 else []
    return pl.pallas_call(
        body, name=name,
        grid=(M // tm, N // tn, nk),
        in_specs=in_specs, out_specs=o_spec,
        out_shape=jax.ShapeDtypeStruct((M, N), out_dtype),
        scratch_shapes=scratch,
        compiler_params=_params(("parallel", "parallel", "arbitrary"), nbytes),
    )(*args)


def _matmul_ln(a, b, xres, g, bias, *, name, tm, tk, a_pro=None):
    M, K = a.shape
    N = b.shape[1]
    tm, tk = min(tm, M), min(tk, K)
    assert M % tm == 0 and K % tk == 0 and N == D_MODEL
    nk = K // tk

    def body(a_ref, b_ref, x_ref, g_ref, bias_ref, y_ref, xh_ref, rs_ref, *scr):
        av = a_ref[...]
        if a_pro == "relu2":
            av = jnp.square(jnp.maximum(av.astype(F32), 0.0))
        p = _dot(av.astype(BF16), b_ref[...].astype(BF16))

        def finish(acc):
            r = ALPHA * x_ref[...] + acc
            mu = jnp.mean(r, axis=-1, keepdims=True)
            xc = r - mu
            var = jnp.mean(xc * xc, axis=-1, keepdims=True)
            rstd = lax.rsqrt(var + LN_EPS)
            xh = xc * rstd
            y_ref[...] = xh * g_ref[...] + bias_ref[...]
            xh_ref[...] = xh
            rs_ref[...] = rstd

        if nk == 1:
            finish(p)
        else:
            acc_ref = scr[0]
            k = pl.program_id(1)

            @pl.when(k == 0)
            def _():
                acc_ref[...] = p

            @pl.when(k > 0)
            def _():
                acc_ref[...] += p

            @pl.when(k == nk - 1)
            def _():
                finish(acc_ref[...])

    row = pl.BlockSpec((tm, N), lambda i, k: (i, 0))
    vec = pl.BlockSpec((1, N), lambda i, k: (0, 0))
    nbytes = _nbytes((tm, tk), a.dtype) + _nbytes((tk, N), b.dtype) + 5 * _nbytes((tm, N), F32)
    scratch = [pltpu.VMEM((tm, N), F32)] if nk > 1 else []
    return pl.pallas_call(
        body, name=name,
        grid=(M // tm, nk),
        in_specs=[pl.BlockSpec((tm, tk), lambda i, k: (i, k)), pl.BlockSpec((tk, N), lambda i, k: (k, 0)), row, vec, vec],
        out_specs=[row, row, pl.BlockSpec((tm, 1), lambda i, k: (i, 0))],
        out_shape=[jax.ShapeDtypeStruct((M, N), F32), jax.ShapeDtypeStruct((M, N), F32),
                   jax.ShapeDtypeStruct((M, 1), F32)],
        scratch_shapes=scratch,
        compiler_params=_params(("parallel", "arbitrary"), nbytes),
    )(a, b, xres, g.reshape(1, N), bias.reshape(1, N))


def _ln_bwd(dy, xhat, rstd, g, *, name, tm):
    M, N = dy.shape
    tm = min(tm, M)

    def body(dy_ref, xh_ref, rs_ref, g_ref, dr_ref, dg_ref, db_ref):
        i = pl.program_id(0)
        dyv = dy_ref[...]
        xh = xh_ref[...]
        dxh = dyv * g_ref[...]
        m1 = jnp.mean(dxh, axis=-1, keepdims=True)
        m2 = jnp.mean(dxh * xh, axis=-1, keepdims=True)
        dr_ref[...] = rs_ref[...] * (dxh - m1 - xh * m2)
        pg = _sum8(dyv * xh)
        pb = _sum8(dyv)

        @pl.when(i == 0)
        def _():
            dg_ref[...] = pg
            db_ref[...] = pb

        @pl.when(i > 0)
        def _():
            dg_ref[...] += pg
            db_ref[...] += pb

    row = pl.BlockSpec((tm, N), lambda i: (i, 0))
    acc = pl.BlockSpec((8, N), lambda i: (0, 0))
    return pl.pallas_call(
        body, name=name, grid=(M // tm,),
        in_specs=[row, row, pl.BlockSpec((tm, 1), lambda i: (i, 0)), pl.BlockSpec((1, N), lambda i: (0, 0))],
        out_specs=[row, acc, acc],
        out_shape=[jax.ShapeDtypeStruct((M, N), F32), jax.ShapeDtypeStruct((8, N), F32),
                   jax.ShapeDtypeStruct((8, N), F32)],
        compiler_params=_params(("arbitrary",), 4 * _nbytes((tm, N), F32)),
    )(dy, xhat, rstd, g.reshape(1, N))


def _loss_grad(y, tgt, *, name, tm):
    M, N = y.shape
    tm = min(tm, M)

    def body(y_ref, t_ref, dy_ref, l_ref):
        i = pl.program_id(0)
        d = y_ref[...] - t_ref[...]
        dy_ref[...] = d * (1.0 / N)
        pl_ = _sum8(d * d) * (0.5 / N)

        @pl.when(i == 0)
        def _():
            l_ref[...] = pl_

        @pl.when(i > 0)
        def _():
            l_ref[...] += pl_

    row = pl.BlockSpec((tm, N), lambda i: (i, 0))
    return pl.pallas_call(
        body, name=name, grid=(M // tm,),
        in_specs=[row, row], out_specs=[row, pl.BlockSpec((8, N), lambda i: (0, 0))],
        out_shape=[jax.ShapeDtypeStruct((M, N), F32), jax.ShapeDtypeStruct((8, N), F32)],
        compiler_params=_params(("arbitrary",), 3 * _nbytes((tm, N), F32)),
    )(y, tgt)


def _tri(n, kind):
    j = lax.broadcasted_iota(jnp.int32, (2 * n, n), 0) % n
    s = lax.broadcasted_iota(jnp.int32, (2 * n, n), 1)
    return ((j > s) if kind == "after" else (j < s)).astype(BF16)


LOG2E = 1.4426950408889634


def _log_terms(z):
    lse = jnp.log(1.0 + jnp.exp2(jnp.abs(z) * (-LOG2E)))
    logsig = jnp.minimum(z, 0.0) - lse
    return logsig, logsig - z


def _cumsum_mm(x, u2_ref):
    hi, lo = _split(x)
    return _dot(jnp.concatenate([hi, lo], axis=1), u2_ref[...])


def _head_rows(x2, scale):
    lane = lax.broadcasted_iota(jnp.int32, (1, 128), 1)
    zero = jnp.zeros_like(x2)
    both = jnp.concatenate([jnp.where(lane < HEAD_DIM, x2, zero), jnp.where(lane >= HEAD_DIM, x2, zero)], axis=0)
    return both * scale


def _causal_mask(i, ks, tq, tk):
    row = lax.broadcasted_iota(jnp.int32, (2 * tq, tk), 0)
    row = i * tq + jnp.where(row >= tq, row - tq, row)
    col = lax.broadcasted_iota(jnp.int32, (2 * tq, tk), 1)
    return (ks + col) < row


def _attn_fwd(qkv, *, name, tq, tk):
    S = qkv.shape[0]
    tq, tk = min(tq, S), min(tk, S)
    assert S % tq == 0 and tq % (2 * tk) == 0 and S // tk <= 128
    r = tq // tk
    tri = _tri(tk, "after")

    def body(q_ref, k_ref, v_ref, u_ref, o_ref, c_ref, qcat, oacc, cacc, call, ls_buf, tl_buf, l0_buf):
        i = pl.program_id(1)
        lane = lax.broadcasted_iota(jnp.int32, (1, 128), 1)
        qcat[...] = _head_rows(q_ref[...], SCALE)
        oacc[...] = jnp.zeros_like(oacc)
        cacc[...] = jnp.zeros_like(cacc)
        call[...] = jnp.zeros_like(call)
        n = (i + 1) * r

        def scores(j, masked, slot):
            ks = pl.multiple_of((n - 1 - j) * tk, tk)
            z = _dot(qcat[...], k_ref[pl.ds(ks, tk), :], NT)
            logsig, lom = _log_terms(z)
            if masked:
                msk = _causal_mask(i, ks, tq, tk)
                lom = jnp.where(msk, lom, 0.0)
                logsig = jnp.where(msk, logsig, -1e30)
            ls_buf[slot] = logsig
            tl_buf[slot] = _cumsum_mm(lom, u_ref)
            l0_buf[slot] = lom[:, 0:1]

        def weights(j, slot):
            kb = n - 1 - j
            ks = pl.multiple_of(kb * tk, tk)
            tl = tl_buf[slot]
            c = cacc[...]
            call[...] = jnp.where(lane == kb, c, call[...])
            a = jnp.exp(ls_buf[slot] + tl + c).astype(BF16)
            oacc[...] += _dot(a, v_ref[pl.ds(ks, tk), :])
            cacc[...] = c + tl[:, 0:1] + l0_buf[slot]

        scores(0, True, 0)
        for j in range(1, r):
            scores(j, True, j % 2)
            weights(j - 1, (j - 1) % 2)

        def loop(t, carry):
            j = r + 2 * t
            scores(j, False, 0)
            weights(j - 1, 1)
            scores(j + 1, False, 1)
            weights(j, 0)
            return carry

        lax.fori_loop(0, (n - r) // 2, loop, 0)
        weights(n - 1, 1)
        o_ref[...] = jnp.where(lane < HEAD_DIM, oacc[0:tq], oacc[tq:2 * tq])
        c_ref[...] = jnp.concatenate([call[0:tq], call[tq:2 * tq]], axis=1)

    nbytes = (_nbytes((tq, 128), BF16) + 2 * _nbytes((S, 128), BF16) + _nbytes((2 * tk, tk), BF16)
              + 8 * _nbytes((tq, 128), F32) + 14 * _nbytes((2 * tq, tk), F32))
    return pl.pallas_call(
        body, name=name, grid=(4, S // tq),
        in_specs=[pl.BlockSpec((tq, 128), lambda j, i: (i, j)),
                  pl.BlockSpec((S, 128), lambda j, i: (0, 4 + j)),
                  pl.BlockSpec((S, 128), lambda j, i: (0, 8 + j)),
                  pl.BlockSpec((2 * tk, tk), lambda j, i: (0, 0))],
        out_specs=[pl.BlockSpec((tq, 128), lambda j, i: (i, j)),
                   pl.BlockSpec((tq, 256), lambda j, i: (i, j))],
        out_shape=[jax.ShapeDtypeStruct((S, D_SB), F32), jax.ShapeDtypeStruct((S, 1024), F32)],
        scratch_shapes=[pltpu.VMEM((2 * tq, 128), BF16), pltpu.VMEM((2 * tq, 128), F32),
                        pltpu.VMEM((2 * tq, 1), F32), pltpu.VMEM((2 * tq, 128), F32),
                        pltpu.VMEM((2, 2 * tq, tk), F32), pltpu.VMEM((2, 2 * tq, tk), F32),
                        pltpu.VMEM((2, 2 * tq, 1), F32)],
        compiler_params=_params(("parallel", "arbitrary"), nbytes),
    )(qkv, qkv, qkv, tri)


def _attn_bwd(qkv, carry, do, *, name, tq, tk):
    S = qkv.shape[0]
    tq, tk = min(tq, S), min(tk, S)
    assert S % tq == 0 and tq % (2 * tk) == 0 and S // tk <= 128
    r = tq // tk
    nkb = S // tk
    nq = S // tq
    tri_after = _tri(tk, "after")
    tri_before = _tri(tk, "before")

    def body(q_ref, k_ref, v_ref, c_ref, do_ref, ua_ref, ub_ref, dq_ref, dk_ref, dv_ref,
             qcat, docat, qcat_t, docat_t, ccat, dqacc, pacc, dkt, dvt, ls_buf, tl_buf, da_buf):
        i = pl.program_id(1)
        lane = lax.broadcasted_iota(jnp.int32, (1, 128), 1)
        sub = lax.broadcasted_iota(jnp.int32, (128, 1), 0)
        q2 = q_ref[...]
        do2 = do_ref[...]
        qcat[...] = _head_rows(q2, SCALE)
        docat[...] = _head_rows(do2, 1.0).astype(BF16)
        qt = q2.astype(F32).T * SCALE
        dot_ = do2.T
        qcat_t[...] = jnp.concatenate([jnp.where(sub < HEAD_DIM, qt, 0.0), jnp.where(sub >= HEAD_DIM, qt, 0.0)],
                                      axis=1).astype(BF16)
        docat_t[...] = jnp.concatenate([jnp.where(sub < HEAD_DIM, dot_, 0.0), jnp.where(sub >= HEAD_DIM, dot_, 0.0)],
                                       axis=1).astype(BF16)
        ccat[0:tq] = c_ref[:, 0:128]
        ccat[tq:2 * tq] = c_ref[:, 128:256]

        @pl.when(i == 0)
        def _():
            dkt[...] = jnp.zeros_like(dkt)
            dvt[...] = jnp.zeros_like(dvt)

        dqacc[...] = jnp.zeros_like(dqacc)
        pacc[...] = jnp.zeros_like(pacc)

        def scores(kb, masked, slot):
            ks = pl.multiple_of(kb * tk, tk)
            z = _dot(qcat[...], k_ref[pl.ds(ks, tk), :], NT)
            logsig, lom = _log_terms(z)
            if masked:
                msk = _causal_mask(i, ks, tq, tk)
                lom = jnp.where(msk, lom, 0.0)
                logsig = jnp.where(msk, logsig, -1e30)
            ls_buf[slot] = logsig
            tl_buf[slot] = _cumsum_mm(lom, ua_ref)
            da_buf[slot] = _dot(docat[...], v_ref[pl.ds(ks, tk), :], NT)

        def grads(kb, slot):
            ks = pl.multiple_of(kb * tk, tk)
            logsig = ls_buf[slot]
            c = jnp.sum(jnp.where(lane == kb, ccat[...], 0.0), axis=1, keepdims=True)
            a = jnp.exp(logsig + tl_buf[slot] + c)
            g = a * da_buf[slot]
            before = _cumsum_mm(g, ub_ref)
            pc = pacc[...]
            dz = g - jnp.exp(logsig) * (g + before + pc)
            dzb = dz.astype(BF16)
            dqacc[...] += _dot(dzb, k_ref[pl.ds(ks, tk), :])
            dkt[kb] += _dot(qcat_t[...], dzb)
            dvt[kb] += _dot(docat_t[...], a.astype(BF16))
            pacc[...] = pc + before[:, tk - 1:tk] + g[:, tk - 1:tk]

        def pair(kb, masked):
            scores(kb, masked, 0)
            scores(kb + 1, masked, 1)
            grads(kb, 0)
            grads(kb + 1, 1)

        def loop(t, carry_):
            pair(2 * t, False)
            return carry_

        lax.fori_loop(0, (i * r) // 2, loop, 0)
        for d in range(0, r, 2):
            pair(i * r + d, True)
        dq_ref[...] = jnp.where(lane < HEAD_DIM, dqacc[0:tq], dqacc[tq:2 * tq]) * SCALE

        @pl.when(i == nq - 1)
        def _():
            for kb in range(nkb):
                dk_ref[kb * tk:(kb + 1) * tk, :] = dkt[kb].T
                dv_ref[kb * tk:(kb + 1) * tk, :] = dvt[kb].T

    nbytes = (_nbytes((tq, 128), BF16) + 2 * _nbytes((S, 128), BF16) + 2 * _nbytes((2 * tk, tk), BF16)
              + 12 * _nbytes((tq, 128), F32) + 4 * _nbytes((S, 128), F32) + 14 * _nbytes((2 * tq, tk), F32))
    blk = pl.BlockSpec((tq, 128), lambda j, i: (i, j))
    full = pl.BlockSpec((S, 128), lambda j, i: (0, j))
    tri_spec = pl.BlockSpec((2 * tk, tk), lambda j, i: (0, 0))
    dq, dk, dv = pl.pallas_call(
        body, name=name, grid=(4, nq),
        in_specs=[blk,
                  pl.BlockSpec((S, 128), lambda j, i: (0, 4 + j)),
                  pl.BlockSpec((S, 128), lambda j, i: (0, 8 + j)),
                  pl.BlockSpec((tq, 256), lambda j, i: (i, j)),
                  blk, tri_spec, tri_spec],
        out_specs=[blk, full, full],
        out_shape=[jax.ShapeDtypeStruct((S, D_SB), F32)] * 3,
        scratch_shapes=[pltpu.VMEM((2 * tq, 128), BF16), pltpu.VMEM((2 * tq, 128), BF16),
                        pltpu.VMEM((128, 2 * tq), BF16), pltpu.VMEM((128, 2 * tq), BF16),
                        pltpu.VMEM((2 * tq, 128), F32), pltpu.VMEM((2 * tq, 128), F32), pltpu.VMEM((2 * tq, 1), F32),
                        pltpu.VMEM((nkb, 128, tk), F32), pltpu.VMEM((nkb, 128, tk), F32),
                        pltpu.VMEM((2, 2 * tq, tk), F32), pltpu.VMEM((2, 2 * tq, tk), F32),
                        pltpu.VMEM((2, 2 * tq, tk), F32)],
        compiler_params=_params(("parallel", "arbitrary"), nbytes),
    )(qkv, qkv, qkv, carry, do, tri_after, tri_before)
    return dq, dk, dv


def _group_mats():
    lanes = jnp.arange(D_MODEL) // HEAD_DIM
    gs = (lanes[:, None] == jnp.arange(128)[None, :]).astype(BF16)
    return gs, gs.T


def _group_sum_bcast(x, gs, gb):
    hi, lo = _split(x)
    s = _dot(hi, gs) + _dot(lo, gs)
    return _bcast(s, gb)


def _bcast(s, gb):
    hi, lo = _split(s)
    return _dot(hi, gb) + _dot(lo, gb)


def _pool_lane_consts():
    lane = lax.broadcasted_iota(jnp.int32, (1, D_POOL), 1)
    grp = lane // (D_POOL // 4)
    win = jnp.where(grp == 0, 2, jnp.where(grp == 1, 4, jnp.where(grp == 2, 8, 16)))
    return grp, win


def _by_group(grp, s2, s4, s8, s16):
    return jnp.where(grp == 0, s2, jnp.where(grp == 1, s4, jnp.where(grp == 2, s8, s16)))


def _mixers(i, ts, prev_ref, cur_ref, cw_ref, pw_ref, ps_ref):
    cur = cur_ref[...]
    prev = jnp.where(i == 0, 0.0, prev_ref[...])
    ext = jnp.concatenate([prev, cur], axis=0)
    n = HALO + ts

    def back(a, k):
        return pltpu.roll(a, k, 0)

    u = ext[:, D_CONV:2 * D_CONV] * ext[:, 2 * D_CONV:3 * D_CONV]
    p = ext[:, 3 * D_CONV:]
    cv = (cw_ref[0:1, :] * back(u, 2) + cw_ref[1:2, :] * back(u, 1) + cw_ref[2:3, :] * u)[HALO:]
    s2 = p + back(p, 1)
    s4 = s2 + back(s2, 2)
    s8 = s4 + back(s4, 4)
    s16 = s8 + back(s8, 8)
    grp, win = _pool_lane_consts()
    t1 = i * ts + 1 + lax.broadcasted_iota(jnp.int32, (ts, 1), 0)
    cnt = jnp.minimum(t1, win).astype(F32)
    pooled = _by_group(grp, s2, s4, s8, s16)[HALO:] / cnt - p[HALO:]
    yp = _dot(pooled.astype(BF16), pw_ref[...])
    del n
    return dict(b=cur[:, 0:D_CONV], u=u, cv=cv, pooled=pooled, yp=yp, cnt=cnt,
                conv_out=cur[:, 0:D_CONV] * cv, pool_out=yp * ps_ref[...])


def _halo_specs(ts, S, width):
    nb = ts // HALO
    last = S // HALO - 1
    prev = pl.BlockSpec((HALO, width), lambda i: (jnp.maximum(i * nb - 1, 0), 0))
    nxt = pl.BlockSpec((HALO, width), lambda i: (jnp.minimum((i + 1) * nb, last), 0))
    return prev, nxt


def _mixer_fwd(rest, attn, cw8, pwbd, ps, gain, *, name, ts):
    S = rest.shape[0]
    ts = min(ts, S)
    gs, gb = _group_mats()

    def body(prev_ref, cur_ref, attn_ref, cw_ref, pw_ref, ps_ref, gain_ref, gs_ref, gb_ref, o_ref):
        i = pl.program_id(0)
        f = _mixers(i, ts, prev_ref, cur_ref, cw_ref, pw_ref, ps_ref)
        mix = jnp.concatenate([attn_ref[...], f["conv_out"], f["pool_out"]], axis=1)
        ss = _group_sum_bcast(mix * mix, gs_ref[...], gb_ref[...])
        rinv = lax.rsqrt(ss * (1.0 / HEAD_DIM) + RMS_EPS)
        o_ref[...] = (mix * rinv * gain_ref[...]).astype(BF16)

    prev, _ = _halo_specs(ts, S, D_REST)
    row = lambda w: pl.BlockSpec((ts, w), lambda i: (i, 0))
    const = lambda a: pl.BlockSpec(a.shape, lambda i: (0, 0))
    nbytes = 12 * _nbytes((ts + HALO, D_REST), F32)
    return pl.pallas_call(
        body, name=name, grid=(S // ts,),
        in_specs=[prev, row(D_REST), row(D_SB), const(cw8), const(pwbd), const(ps), const(gain), const(gs), const(gb)],
        out_specs=row(D_MODEL),
        out_shape=jax.ShapeDtypeStruct((S, D_MODEL), BF16),
        compiler_params=_params(("parallel",), nbytes),
    )(rest, rest, attn, cw8, pwbd, ps, gain, gs, gb)


def _mixer_bwd1(dmixn, rest, attn, cw8, pwbd, ps, gain, *, name, ts):
    S = rest.shape[0]
    ts = min(ts, S)
    gs, gb = _group_mats()

    def body(dm_ref, prev_ref, cur_ref, attn_ref, cw_ref, pw_ref, ps_ref, gain_ref, gs_ref, gb_ref,
             da_ref, aux_ref, dg_ref, dsc_ref, dcw_ref, dpw_ref):
        i = pl.program_id(0)
        f = _mixers(i, ts, prev_ref, cur_ref, cw_ref, pw_ref, ps_ref)
        mix = jnp.concatenate([attn_ref[...], f["conv_out"], f["pool_out"]], axis=1)
        gsm, gbm = gs_ref[...], gb_ref[...]
        ss = _group_sum_bcast(mix * mix, gsm, gbm)
        rinv = lax.rsqrt(ss * (1.0 / HEAD_DIM) + RMS_EPS)
        dm = dm_ref[...]
        xn = mix * rinv
        dyg = dm * gain_ref[...]
        gm = _group_sum_bcast(dyg * xn, gsm, gbm) * (1.0 / HEAD_DIM)
        dmix = rinv * (dyg - xn * gm)
        da_ref[...] = dmix[:, 0:D_SB]
        dco = dmix[:, D_SB:D_SB + D_CONV]
        dpo = dmix[:, D_SB + D_CONV:]
        dcv = dco * f["b"]
        dyp = dpo * ps_ref[...]
        dpooled = _dot(dyp.astype(BF16), pw_ref[...], NT)
        aux_ref[...] = jnp.concatenate([dco * f["cv"], dcv, dpooled / f["cnt"], dpooled], axis=1)
        u = f["u"]
        parts = [
            _sum8(dm * xn),
            _sum8(dpo * f["yp"]),
            jnp.concatenate([_sum8(dcv * pltpu.roll(u, 2, 0)[HALO:]), _sum8(dcv * pltpu.roll(u, 1, 0)[HALO:]),
                             _sum8(dcv * u[HALO:])], axis=0),
            _dot(f["pooled"].astype(BF16), dyp.astype(BF16), TN),
        ]
        outs = [dg_ref, dsc_ref, dcw_ref, dpw_ref]

        @pl.when(i == 0)
        def _():
            for o, v in zip(outs, parts):
                o[...] = v

        @pl.when(i > 0)
        def _():
            for o, v in zip(outs, parts):
                o[...] += v

    prev, _ = _halo_specs(ts, S, D_REST)
    row = lambda w: pl.BlockSpec((ts, w), lambda i: (i, 0))
    const = lambda a: pl.BlockSpec(a.shape, lambda i: (0, 0))
    acc = lambda r_, w: pl.BlockSpec((r_, w), lambda i: (0, 0))
    nbytes = 16 * _nbytes((ts + HALO, D_REST), F32)
    return pl.pallas_call(
        body, name=name, grid=(S // ts,),
        in_specs=[row(D_MODEL), prev, row(D_REST), row(D_SB), const(cw8), const(pwbd), const(ps), const(gain),
                  const(gs), const(gb)],
        out_specs=[row(D_SB), row(D_REST), acc(8, D_MODEL), acc(8, D_POOL), acc(24, D_CONV), acc(D_POOL, D_POOL)],
        out_shape=[jax.ShapeDtypeStruct((S, D_SB), F32), jax.ShapeDtypeStruct((S, D_REST), F32),
                   jax.ShapeDtypeStruct((8, D_MODEL), F32), jax.ShapeDtypeStruct((8, D_POOL), F32),
                   jax.ShapeDtypeStruct((24, D_CONV), F32), jax.ShapeDtypeStruct((D_POOL, D_POOL), F32)],
        compiler_params=_params(("arbitrary",), nbytes),
    )(dmixn, rest, rest, attn, cw8, pwbd, ps, gain, gs, gb)


def _mixer_bwd2(aux, rest, cw8, *, name, ts):
    S = rest.shape[0]
    ts = min(ts, S)
    nblk = S // ts

    def body(cur_ref, nxt_ref, rest_ref, cw_ref, o_ref):
        i = pl.program_id(0)
        cur = cur_ref[...]
        nxt = jnp.where(i == nblk - 1, 0.0, nxt_ref[...])
        ext = jnp.concatenate([cur, nxt], axis=0)
        n = ts + HALO

        def fwd(a, k):
            return pltpu.roll(a, n - k, 0)

        dcv = ext[:, D_CONV:2 * D_CONV]
        dps = ext[:, 2 * D_CONV:3 * D_CONV]
        du = (cw_ref[2:3, :] * dcv + cw_ref[1:2, :] * fwd(dcv, 1) + cw_ref[0:1, :] * fwd(dcv, 2))[0:ts]
        f2 = dps + fwd(dps, 1)
        f4 = f2 + fwd(f2, 2)
        f8 = f4 + fwd(f4, 4)
        f16 = f8 + fwd(f8, 8)
        grp, _ = _pool_lane_consts()
        dp = _by_group(grp, f2, f4, f8, f16)[0:ts] - cur[:, 3 * D_CONV:]
        rest_v = rest_ref[...]
        c_gate = rest_v[:, D_CONV:2 * D_CONV]
        h = rest_v[:, 2 * D_CONV:3 * D_CONV]
        o_ref[...] = jnp.concatenate([cur[:, 0:D_CONV], du * h, du * c_gate, dp], axis=1)

    _, nxt = _halo_specs(ts, S, D_REST)
    row = pl.BlockSpec((ts, D_REST), lambda i: (i, 0))
    return pl.pallas_call(
        body, name=name, grid=(nblk,),
        in_specs=[row, nxt, row, pl.BlockSpec(cw8.shape, lambda i: (0, 0))],
        out_specs=row,
        out_shape=jax.ShapeDtypeStruct((S, D_REST), F32),
        compiler_params=_params(("parallel",), 10 * _nbytes((ts + HALO, D_REST), F32)),
    )(aux, aux, rest, cw8)


def _block_diag(pw):
    z = jnp.zeros((4, 64, 4, 64), pw.dtype)
    for g in range(4):
        z = z.at[g, :, g, :].set(pw[g])
    return z.reshape(256, 256)


def _rows8(v, rows=8):
    return jnp.pad(v, ((0, rows - v.shape[0]), (0, 0)))


TILES = dict(tm=512, ts=512, tq=512, tk=256)


def _local_step(x, tgt, w, t=None):
    t = dict(TILES, **(t or {}))
    tm, ts, tq, tk = t["tm"], t["ts"], t["tq"], t["tk"]
    saved = []
    xl = x
    for l in range(DEPTH):
        n = f"l{l}_"
        w_in = w["w_in"][l]
        qkv = _matmul(xl, w_in[:, :D_QKV], name=n + "proj_qkv", tm=1024, tn=512, tk=1024, out_dtype=BF16)
        rest = _matmul(xl, w_in[:, D_QKV:], name=n + "proj_rest", tm=1024, tn=512, tk=1024)
        attn, carry = _attn_fwd(qkv, name=n + "attn_fwd", tq=tq, tk=tk)
        cw8 = _rows8(w["conv_w"][l])
        pwbd = _block_diag(w["pool_w"][l]).astype(BF16)
        ps = w["pool_scale"][l].reshape(1, D_POOL)
        gain = w["mix_norm_g"][l].reshape(1, D_MODEL)
        mixn = _mixer_fwd(rest, attn, cw8, pwbd, ps, gain, name=n + "mixer_fwd", ts=ts)
        x1, xh1, rs1 = _matmul_ln(mixn, w["w_o"][l], xl, w["ln1_g"][l], w["ln1_b"][l], name=n + "wo_ln", tm=tm, tk=1024)
        hpre = _matmul(x1, w["w_up"][l], name=n + "ffn_up", tm=1024, tn=1024, tk=1024)
        x2, xh2, rs2 = _matmul_ln(hpre, w["w_down"][l], x1, w["ln2_g"][l], w["ln2_b"][l], name=n + "ffn_down_ln",
                                  tm=tm, tk=1024, a_pro="relu2")
        saved.append(dict(xin=xl, qkv=qkv, rest=rest, attn=attn, carry=carry, cw8=cw8, pwbd=pwbd, ps=ps, gain=gain,
                          mixn=mixn, x1=x1, xh1=xh1, rs1=rs1, hpre=hpre, xh2=xh2, rs2=rs2))
        xl = x2

    dy, lsum = _loss_grad(xl, tgt, name="loss_grad", tm=tm)
    grads = {k: [None] * DEPTH for k in
             ("w_in", "conv_w", "pool_w", "pool_scale", "mix_norm_g", "w_o", "ln1_g", "ln1_b", "w_up", "w_down",
              "ln2_g", "ln2_b")}
    for l in reversed(range(DEPTH)):
        n = f"l{l}_"
        s = saved[l]
        dr2, dg2, db2 = _ln_bwd(dy, s["xh2"], s["rs2"], w["ln2_g"][l], name=n + "ln2_bwd", tm=tm)
        dhpre = _matmul(dr2, w["w_down"][l], name=n + "ffn_down_dx", tm=1024, tn=1024, tk=1024, tb=True,
                        out_dtype=BF16, epi="drelu2", e=s["hpre"])
        grads["w_down"][l] = _matmul(s["hpre"], dr2, name=n + "ffn_down_dw", tm=1024, tn=1024, tk=512, ta=True,
                                     a_pro="relu2")
        dx1 = _matmul(dhpre, w["w_up"][l], name=n + "ffn_up_dx", tm=1024, tn=1024, tk=1024, tb=True,
                      epi="add", e=dr2, e_scale=ALPHA)
        grads["w_up"][l] = _matmul(s["x1"], dhpre, name=n + "ffn_up_dw", tm=1024, tn=1024, tk=512, ta=True)
        dr1, dg1, db1 = _ln_bwd(dx1, s["xh1"], s["rs1"], w["ln1_g"][l], name=n + "ln1_bwd", tm=tm)
        dmixn = _matmul(dr1, w["w_o"][l], name=n + "wo_dx", tm=1024, tn=1024, tk=1024, tb=True)
        grads["w_o"][l] = _matmul(s["mixn"], dr1, name=n + "wo_dw", tm=1024, tn=1024, tk=512, ta=True)
        d_attn, aux, dgain, dsc, dcw, dpw = _mixer_bwd1(dmixn, s["rest"], s["attn"], s["cw8"], s["pwbd"], s["ps"],
                                                        s["gain"], name=n + "mixer_bwd1", ts=ts)
        drest = _mixer_bwd2(aux, s["rest"], s["cw8"], name=n + "mixer_bwd2", ts=ts)
        dq, dk, dv = _attn_bwd(s["qkv"], s["carry"], d_attn, name=n + "attn_bwd", tq=tq, tk=tk)
        dqkv = jnp.concatenate([dq, dk, dv], axis=1)
        w_in = w["w_in"][l]
        dxa = _matmul(dqkv, w_in[:, :D_QKV], name=n + "proj_qkv_dx", tm=1024, tn=1024, tk=512, tb=True,
                      epi="add", e=dr1, e_scale=ALPHA)
        dy = _matmul(drest, w_in[:, D_QKV:], name=n + "proj_rest_dx", tm=1024, tn=1024, tk=1024, tb=True,
                     epi="add", e=dxa, e_scale=1.0)
        dwq = _matmul(s["xin"], dqkv, name=n + "proj_qkv_dw", tm=1024, tn=512, tk=512, ta=True)
        dwr = _matmul(s["xin"], drest, name=n + "proj_rest_dw", tm=1024, tn=1024, tk=512, ta=True)
        grads["w_in"][l] = jnp.concatenate([dwq, dwr], axis=1)
        grads["ln2_g"][l] = dg2.sum(0)
        grads["ln2_b"][l] = db2.sum(0)
        grads["ln1_g"][l] = dg1.sum(0)
        grads["ln1_b"][l] = db1.sum(0)
        grads["mix_norm_g"][l] = dgain.sum(0)
        grads["pool_scale"][l] = dsc.sum(0)
        grads["conv_w"][l] = dcw.reshape(3, 8, D_CONV).sum(1)
        grads["pool_w"][l] = jnp.stack([dpw[64 * g:64 * g + 64, 64 * g:64 * g + 64] for g in range(4)])
    grads = {k: jnp.stack(v) for k, v in grads.items()}
    return lsum, dy, grads


ANY = pl.BlockSpec(memory_space=pl.ANY)


def _place():
    x, y, c = lax.axis_index("x"), lax.axis_index("y"), lax.axis_index("c")
    chips = [(1 - x, y), (x, 1 - y), (1 - x, 1 - y)]
    return x, y, c, chips


def _remote(src, dst, send_sems, recv_sems, k, to):
    return pltpu.make_async_remote_copy(src_ref=src, dst_ref=dst, send_sem=send_sems.at[k], recv_sem=recv_sems.at[k],
                                        device_id=to, device_id_type=MESH)


DMA_CHUNK_BYTES = 1 << 20


def _n_chunks(rows, dtype, width=D_MODEL):
    tiles = rows // 16
    want = max(1, (rows * width * jnp.dtype(dtype).itemsize) // DMA_CHUNK_BYTES)
    best = 1
    for n in range(1, tiles + 1):
        if tiles % n == 0 and n <= want:
            best = n
    return best


class _Chunked:
    def __init__(self, src, dst, send_sems, recv_sems, k, to, n):
        self.args = (send_sems, recv_sems, k, to)
        self.whole = self._one(src, dst)
        rows = src.shape[-2]
        assert rows % n == 0
        step = rows // n
        leads = [()] if len(src.shape) == 2 else [(i,) for i in range(src.shape[0])]
        self.parts = [self._one(src.at[(*lead, pl.ds(t * step, step))], dst.at[(*lead, pl.ds(t * step, step))])
                      for lead in leads for t in range(n)]

    def _one(self, src, dst):
        send_sems, recv_sems, k, to = self.args
        if to is None:
            return pltpu.make_async_copy(src, dst, recv_sems)
        return _remote(src, dst, send_sems, recv_sems, k, to)

    def like(self, src, dst):
        return self._one(src, dst)

    def start(self):
        for p in self.parts:
            p.start()

    def wait(self):
        self.whole.wait()

    def wait_send(self):
        self.whole.wait_send()

    def wait_recv(self):
        self.whole.wait_recv()


def _allgather_chips(pack, *, name):
    R, C = pack.shape
    H = R // 2

    def body(p_ref, o_ref, send_sems, recv_sems):
        x, y, c, chips = _place()
        my = 2 * x + y
        sib = (x, y, 1 - c)
        n = _n_chunks(H, pack.dtype)

        def half(k, hc):
            return o_ref.at[k, pl.ds(hc * H, H), :]

        first = [_Chunked(p_ref.at[pl.ds(c * H, H), :], half(my, c), send_sems, recv_sems, j, (cx, cy, c), n)
                 for j, (cx, cy) in enumerate(chips)]
        for cp in first:
            cp.start()
        passed = []
        for j, (cx, cy) in enumerate(chips):
            k = 2 * cx + cy
            first[j].like(half(k, c), half(k, c)).wait_recv()
            fwd = _Chunked(half(k, c), half(k, c), send_sems, recv_sems, 3 + j, sib, n)
            fwd.start()
            passed.append(fwd)
        for j, (cx, cy) in enumerate(chips):
            k = 2 * cx + cy
            passed[j].like(half(k, 1 - c), half(k, 1 - c)).wait_recv()
        for cp in first + passed:
            cp.wait_send()

    return pl.pallas_call(
        body, name=name, in_specs=[ANY], out_specs=ANY,
        out_shape=jax.ShapeDtypeStruct((N_CHIPS, R, C), pack.dtype),
        scratch_shapes=[pltpu.SemaphoreType.DMA((6,)), pltpu.SemaphoreType.DMA((6,))],
    )(pack)


def _swap_halves(gp, *, name):
    K, R, C = gp.shape
    H = R // 2

    def body(g_ref, theirs_ref, send_sems, recv_sems):
        x, y, c, _ = _place()
        n = _n_chunks(H, gp.dtype)
        cp = _Chunked(g_ref.at[:, pl.ds((1 - c) * H, H), :], theirs_ref, send_sems, recv_sems, 0, (x, y, 1 - c), n)
        cp.start()
        cp.wait()

    return pl.pallas_call(
        body, name=name, in_specs=[ANY], out_specs=ANY, out_shape=jax.ShapeDtypeStruct((K, H, C), gp.dtype),
        scratch_shapes=[pltpu.SemaphoreType.DMA((1,)), pltpu.SemaphoreType.DMA((1,))],
    )(gp)


def _scatter_chips(part, *, name):
    K, H, C = part.shape

    def body(p_ref, o_ref, send_sems, recv_sems):
        x, y, c, chips = _place()
        n = _n_chunks(H, part.dtype)
        copies = [_Chunked(p_ref.at[2 * cx + cy], o_ref.at[j], send_sems, recv_sems, j, (cx, cy, c), n)
                  for j, (cx, cy) in enumerate(chips)]
        for cp in copies:
            cp.start()
        for cp in copies:
            cp.wait()

    return pl.pallas_call(
        body, name=name, in_specs=[ANY], out_specs=ANY,
        out_shape=jax.ShapeDtypeStruct((3, H, C), part.dtype),
        scratch_shapes=[pltpu.SemaphoreType.DMA((3,)), pltpu.SemaphoreType.DMA((3,))],
    )(part)


def _join_halves(half, *, name):
    H, C = half.shape

    def body(h_ref, o_ref, send_sems, recv_sems):
        x, y, c, _ = _place()
        n = _n_chunks(H, half.dtype)
        cp = _Chunked(h_ref, o_ref.at[pl.ds(c * H, H), :], send_sems, recv_sems, 0, (x, y, 1 - c), n)
        cp.start()
        cp.wait_send()
        cp.like(h_ref, o_ref.at[pl.ds((1 - c) * H, H), :]).wait_recv()

    return pl.pallas_call(
        body, name=name, in_specs=[ANY], out_specs=ANY,
        out_shape=jax.ShapeDtypeStruct((2 * H, C), half.dtype),
        scratch_shapes=[pltpu.SemaphoreType.DMA((1,)), pltpu.SemaphoreType.DMA((1,))],
    )(half)


def _allreduce_small(v, *, name):
    R, C = v.shape
    n_dev = 8

    def body(v_ref, o_ref, gat, send_sems, recv_sems):
        x, y, c, chips = _place()
        sib = (x, y, 1 - c)

        def rows(px, py, pc):
            return gat.at[4 * px + 2 * py + pc]

        gat[4 * x + 2 * y + c] = v_ref[...]
        first = [_remote(v_ref, rows(x, y, c), send_sems, recv_sems, 0, sib)]
        first += [_remote(v_ref, rows(x, y, c), send_sems, recv_sems, 1 + j, (cx, cy, c))
                  for j, (cx, cy) in enumerate(chips)]
        for cp in first:
            cp.start()
        passed = []
        for j, (cx, cy) in enumerate(chips):
            _remote(v_ref, rows(cx, cy, c), send_sems, recv_sems, 1 + j, sib).wait_recv()
            fwd = _remote(rows(cx, cy, c), rows(cx, cy, c), send_sems, recv_sems, 4 + j, sib)
            fwd.start()
            passed.append(fwd)
        _remote(v_ref, rows(x, y, 1 - c), send_sems, recv_sems, 0, sib).wait_recv()
        for j, (cx, cy) in enumerate(chips):
            _remote(v_ref, rows(cx, cy, 1 - c), send_sems, recv_sems, 4 + j, sib).wait_recv()
        for cp in first + passed:
            cp.wait_send()
        acc = gat[0]
        for d in range(1, n_dev):
            acc = acc + gat[d]
        o_ref[...] = acc

    vm = pl.BlockSpec(memory_space=pltpu.VMEM)
    return pl.pallas_call(
        body, name=name, in_specs=[vm], out_specs=vm,
        out_shape=jax.ShapeDtypeStruct((R, C), F32),
        scratch_shapes=[pltpu.VMEM((n_dev, R, C), F32), pltpu.SemaphoreType.DMA((7,)), pltpu.SemaphoreType.DMA((7,))],
    )(v)


def _add_pairs(a, b, *, name, tr):
    K, H, C = a.shape
    tr = min(tr, H)
    assert H % tr == 0

    def body(a_ref, b_ref, o_ref):
        o_ref[...] = (a_ref[...].astype(F32) + b_ref[...].astype(F32)).astype(BF16)

    blk = pl.BlockSpec((1, tr, C), lambda k, i: (k, i, 0))
    return pl.pallas_call(
        body, name=name, grid=(K, H // tr), in_specs=[blk, blk], out_specs=blk,
        out_shape=jax.ShapeDtypeStruct((K, H, C), BF16),
        compiler_params=_params(("parallel", "parallel"), 3 * _nbytes((tr, C), BF16)),
    )(a, b)


def _add_final(a, b, others, *, name, tr):
    H, C = a.shape
    tr = min(tr, H)
    assert H % tr == 0

    def body(a_ref, b_ref, o_ref_in, out_ref):
        acc = a_ref[...].astype(F32) + b_ref[...].astype(F32)
        for j in range(3):
            acc = acc + o_ref_in[j].astype(F32)
        out_ref[...] = acc

    blk = pl.BlockSpec((tr, C), lambda i: (i, 0))
    return pl.pallas_call(
        body, name=name, grid=(H // tr,),
        in_specs=[blk, blk, pl.BlockSpec((3, tr, C), lambda i: (0, i, 0))], out_specs=blk,
        out_shape=jax.ShapeDtypeStruct((H, C), F32),
        compiler_params=_params(("parallel",), 6 * _nbytes((tr, C), F32)),
    )(a, b, others)


def _adamw(w, g, m, v, *, name, tr, row0=0):
    R, C = w.shape
    tr = min(tr, R)
    assert R % tr == 0 and row0 % tr == 0
    off = row0 // tr

    def body(w_ref, g_ref, m_ref, v_ref, go_ref, d_ref, mo_ref, vo_ref):
        gv = g_ref[...]
        m2 = ADAM_B1 * m_ref[...] + (1.0 - ADAM_B1) * gv
        v2 = ADAM_B2 * v_ref[...] + (1.0 - ADAM_B2) * jnp.square(gv)
        m_hat = m2 / (1.0 - ADAM_B1 ** ADAM_STEP)
        v_hat = v2 / (1.0 - ADAM_B2 ** ADAM_STEP)
        d_ref[...] = -ADAM_LR * (m_hat / (jnp.sqrt(v_hat) + ADAM_EPS) + ADAM_WD * w_ref[...])
        go_ref[...] = gv
        mo_ref[...] = m2
        vo_ref[...] = v2

    blk = pl.BlockSpec((tr, C), lambda i: (i, 0))
    shape = jax.ShapeDtypeStruct((R, C), F32)
    return pl.pallas_call(
        body, name=name, grid=(R // tr,),
        in_specs=[blk, pl.BlockSpec((tr, C), lambda i: (i + off, 0)), blk, blk], out_specs=[blk] * 4,
        out_shape=[shape] * 4,
        compiler_params=_params(("parallel",), 8 * _nbytes((tr, C), F32)),
    )(w, g, m, v)


BIG = ("w_in", "w_o", "w_up", "w_down")
BIG_AXIS = dict(w_in=2, w_o=1, w_up=2, w_down=1)
SMALL = ("pool_w", "pool_scale", "mix_norm_g", "ln1_g", "ln1_b", "ln2_g", "ln2_b")
CONV_ROWS = 32
SMALL_ROWS = 48


def _big_rows(shards):
    sizes = [shards[n].size // D_MODEL for n in BIG]
    starts = [sum(sizes[:i]) for i in range(len(sizes))]
    return sizes, starts


def _pad_rows(flat, rows):
    return jnp.pad(flat, (0, rows * D_MODEL - flat.shape[0])).reshape(rows, D_MODEL)


def kernel(x, w_in, conv_w, pool_w, pool_scale, mix_norm_g, w_o, ln1_g, ln1_b, w_up, w_down, ln2_g, ln2_b, loss_target, m_w_in, m_conv_w, m_pool_w, m_pool_scale, m_mix_norm_g, m_w_o, m_ln1_g, m_ln1_b, m_w_up, m_w_down, m_ln2_g, m_ln2_b, v_w_in, v_conv_w, v_pool_w, v_pool_scale, v_mix_norm_g, v_w_o, v_ln1_g, v_ln1_b, v_w_up, v_w_down, v_ln2_g, v_ln2_b):
    wts = dict(w_in=w_in, conv_w=conv_w, pool_w=pool_w, pool_scale=pool_scale, mix_norm_g=mix_norm_g, w_o=w_o,
               ln1_g=ln1_g, ln1_b=ln1_b, w_up=w_up, w_down=w_down, ln2_g=ln2_g, ln2_b=ln2_b)
    mom = dict(w_in=m_w_in, conv_w=m_conv_w, pool_w=m_pool_w, pool_scale=m_pool_scale, mix_norm_g=m_mix_norm_g,
               w_o=m_w_o, ln1_g=m_ln1_g, ln1_b=m_ln1_b, w_up=m_w_up, w_down=m_w_down, ln2_g=m_ln2_g, ln2_b=m_ln2_b)
    var = dict(w_in=v_w_in, conv_w=v_conv_w, pool_w=v_pool_w, pool_scale=v_pool_scale, mix_norm_g=v_mix_norm_g,
               w_o=v_w_o, ln1_g=v_ln1_g, ln1_b=v_ln1_b, w_up=v_w_up, w_down=v_w_down, ln2_g=v_ln2_g, ln2_b=v_ln2_b)
    chip = 2 * lax.axis_index("x") + lax.axis_index("y")
    sizes, starts = _big_rows(wts)
    big_rows = sum(sizes)

    conv_bits = lax.bitcast_convert_type(conv_w.reshape(-1), BF16).reshape(-1)
    pack = jnp.concatenate([wts[n].reshape(-1, D_MODEL).astype(BF16) for n in BIG]
                           + [_pad_rows(conv_bits, CONV_ROWS)], axis=0)
    gathered = _allgather_chips(pack, name="gather_weights")
    gathered = lax.dynamic_update_index_in_dim(gathered, pack, chip, 0)
    full = {}
    for n, size, start in zip(BIG, sizes, starts):
        parts = [gathered[k, start:start + size].reshape(wts[n].shape) for k in range(N_CHIPS)]
        full[n] = jnp.concatenate(parts, axis=BIG_AXIS[n])
    conv_parts = [lax.bitcast_convert_type(gathered[k, big_rows:].reshape(-1)[:2 * conv_w.size].reshape(-1, 2), F32)
                  .reshape(conv_w.shape) for k in range(N_CHIPS)]
    full["conv_w"] = jnp.concatenate(conv_parts, axis=2)
    for n in SMALL:
        full[n] = wts[n]

    lsum, grad_x, grads = _local_step(x[0], loss_target[0], full)

    blocks = []
    for k in range(N_CHIPS):
        rows = []
        for n in BIG:
            ax = BIG_AXIS[n]
            width = wts[n].shape[ax]
            rows.append(lax.slice_in_dim(grads[n], k * width, (k + 1) * width, axis=ax).reshape(-1, D_MODEL))
        blocks.append(jnp.concatenate(rows, axis=0))
    gpack = jnp.stack(blocks).astype(BF16)
    core = lax.axis_index("c")
    half_rows = big_rows // 2
    theirs = _swap_halves(gpack, name="grad_swap_cores")
    mine = lax.dynamic_slice_in_dim(gpack, core * half_rows, half_rows, axis=1)
    chip_sum = _add_pairs(mine, theirs, name="grad_add_cores", tr=736)
    from_chips = _scatter_chips(chip_sum, name="grad_scatter_chips")
    mine_k = lax.dynamic_index_in_dim(mine, chip, 0, keepdims=False)
    theirs_k = lax.dynamic_index_in_dim(theirs, chip, 0, keepdims=False)
    half_sum = _add_final(mine_k, theirs_k, from_chips, name="grad_add_chips", tr=736)
    gsum = _join_halves(half_sum, name="grad_join_cores")
    gsum = lax.dynamic_update_slice_in_dim(gsum, half_sum, core * half_rows, axis=0)

    small_flat = jnp.concatenate([grads[n].reshape(-1) for n in SMALL] + [grads["conv_w"].reshape(-1),
                                                                          lsum.sum().reshape(1)])
    small_sum = _allreduce_small(_pad_rows(small_flat, SMALL_ROWS), name="allreduce_small").reshape(-1)
    gsmall = {}
    pos = 0
    for n in SMALL:
        gsmall[n] = small_sum[pos:pos + wts[n].size].reshape(wts[n].shape)
        pos += wts[n].size
    conv_full = small_sum[pos:pos + 4 * conv_w.size].reshape(DEPTH, 3, D_CONV)
    pos += 4 * conv_w.size
    loss = small_sum[pos]
    gsmall["conv_w"] = lax.dynamic_slice_in_dim(conv_full, chip * conv_w.shape[2], conv_w.shape[2], axis=2)

    out_g, out_d, out_m, out_v = {}, {}, {}, {}
    for n, size, start in zip(BIG, sizes, starts):
        shp = wts[n].shape
        res = _adamw(wts[n].reshape(-1, D_MODEL), gsum, mom[n].reshape(-1, D_MODEL), var[n].reshape(-1, D_MODEL),
                     name="adamw_" + n, tr=256, row0=start)
        out_g[n], out_d[n], out_m[n], out_v[n] = [r.reshape(shp) for r in res]
    small_names = SMALL + ("conv_w",)
    packs = [_pad_rows(jnp.concatenate([d[n].reshape(-1) for n in small_names]), SMALL_ROWS)
             for d in (wts, gsmall, mom, var)]
    res = _adamw(*packs, name="adamw_small", tr=SMALL_ROWS)
    pos = 0
    for n in small_names:
        shp = wts[n].shape
        out_g[n], out_d[n], out_m[n], out_v[n] = [r.reshape(-1)[pos:pos + wts[n].size].reshape(shp) for r in res]
        pos += wts[n].size

    order = ("w_in", "conv_w", "pool_w", "pool_scale", "mix_norm_g", "w_o", "ln1_g", "ln1_b", "w_up", "w_down",
             "ln2_g", "ln2_b")
    return (loss, grad_x[None], *[out_g[n] for n in order], *[out_d[n] for n in order],
            *[out_m[n] for n in order], *[out_v[n] for n in order])
```

```python
import functools
import math

import jax
import jax.numpy as jnp
from jax import lax
from jax.experimental import pallas as pl
from jax.experimental.pallas import tpu as pltpu

F32 = jnp.float32
BF16 = jnp.bfloat16
MESH = pl.DeviceIdType.MESH

D_MODEL = 1024
DEPTH = 2
HEAD_DIM = 64
D_SB = 512
D_CONV = 256
D_POOL = 256
D_QKV = 3 * D_SB
D_REST = 3 * D_CONV + D_POOL
D_IN = D_QKV + D_REST
D_FF = 4 * D_MODEL
ALPHA = (2 * DEPTH) ** 0.25
LN_EPS = 1e-5
RMS_EPS = 1e-6
SCALE = HEAD_DIM ** -0.5
N_CHIPS = 4
HALO = 16

ADAM_LR = 0.001
ADAM_B1 = 0.9
ADAM_B2 = 0.999
ADAM_EPS = 1e-08
ADAM_WD = 0.01
ADAM_STEP = 10

VMEM_V7X_BYTES = 64 * 1024 * 1024
VMEM_CAP_BYTES = VMEM_V7X_BYTES - 8 * 1024 * 1024


def _params(sem, block_bytes):
    limit = min(VMEM_CAP_BYTES, max(32 * 1024 * 1024, 3 * block_bytes))
    return pltpu.CompilerParams(dimension_semantics=sem, vmem_limit_bytes=limit)


def _nbytes(shape, dtype):
    return math.prod(shape) * jnp.dtype(dtype).itemsize


def _dot(a, b, dims=(((1,), (0,)), ((), ()))):
    return lax.dot_general(a, b, dims, preferred_element_type=F32)


NT = (((1,), (1,)), ((), ()))
TN = (((0,), (0,)), ((), ()))


def _split(x):
    hi = x.astype(BF16)
    lo = (x - hi.astype(F32)).astype(BF16)
    return hi, lo


def _sum8(x):
    r, c = x.shape
    return x.reshape(r // 8, 8, c).sum(axis=0)


def _matmul(a, b, *, name, tm, tn, tk, ta=False, tb=False, out_dtype=F32,
            a_pro=None, epi=None, e=None, e_scale=1.0):
    M, K = (a.shape[1], a.shape[0]) if ta else a.shape
    N = b.shape[0] if tb else b.shape[1]
    tm, tn, tk = min(tm, M), min(tn, N), min(tk, K)
    assert M % tm == 0 and N % tn == 0 and K % tk == 0, (name, M, N, K)
    nk = K // tk
    dims = (((0 if ta else 1,), (1 if tb else 0,)), ((), ()))

    def body(*refs):
        if epi is None:
            a_ref, b_ref, o_ref, *scr = refs
            e_ref = None
        else:
            a_ref, b_ref, e_ref, o_ref, *scr = refs
        av = a_ref[...]
        if a_pro == "relu2":
            av = jnp.square(jnp.maximum(av.astype(F32), 0.0))
        p = _dot(av.astype(BF16), b_ref[...].astype(BF16), dims)

        def finish(acc):
            if epi == "drelu2":
                acc = acc * (2.0 * jnp.maximum(e_ref[...], 0.0))
            elif epi == "add":
                acc = acc + e_scale * e_ref[...]
            o_ref[...] = acc.astype(out_dtype)

        if nk == 1:
            finish(p)
        else:
            acc_ref = scr[0]
            k = pl.program_id(2)

            @pl.when(k == 0)
            def _():
                acc_ref[...] = p

            @pl.when(k > 0)
            def _():
                acc_ref[...] += p

            @pl.when(k == nk - 1)
            def _():
                finish(acc_ref[...])

    a_spec = pl.BlockSpec((tk, tm), lambda i, j, k: (k, i)) if ta else pl.BlockSpec((tm, tk), lambda i, j, k: (i, k))
    b_spec = pl.BlockSpec((tn, tk), lambda i, j, k: (j, k)) if tb else pl.BlockSpec((tk, tn), lambda i, j, k: (k, j))
    o_spec = pl.BlockSpec((tm, tn), lambda i, j, k: (i, j))
    in_specs = [a_spec, b_spec]
    args = [a, b]
    nbytes = _nbytes((tm, tk), a.dtype) + _nbytes((tk, tn), b.dtype) + 2 * _nbytes((tm, tn), F32)
    if epi is not None:
        in_specs.append(o_spec)
        args.append(e)
        nbytes += _nbytes((tm, tn), e.dtype)
    scratch = [pltpu.VMEM((tm, tn), F32)] if nk > 1 else []
    return pl.pallas_call(
        body, name=name,
        grid=(M // tm, N // tn, nk),
        in_specs=in_specs, out_specs=o_spec,
        out_shape=jax.ShapeDtypeStruct((M, N), out_dtype),
        scratch_shapes=scratch,
        compiler_params=_params(("parallel", "parallel", "arbitrary"), nbytes),
    )(*args)


def _matmul_ln(a, b, xres, g, bias, *, name, tm, tk, a_pro=None):
    M, K = a.shape
    N = b.shape[1]
    tm, tk = min(tm, M), min(tk, K)
    assert M % tm == 0 and K % tk == 0 and N == D_MODEL
    nk = K // tk

    def body(a_ref, b_ref, x_ref, g_ref, bias_ref, y_ref, xh_ref, rs_ref, *scr):
        av = a_ref[...]
        if a_pro == "relu2":
            av = jnp.square(jnp.maximum(av.astype(F32), 0.0))
        p = _dot(av.astype(BF16), b_ref[...].astype(BF16))

        def finish(acc):
            r = ALPHA * x_ref[...] + acc
            mu = jnp.mean(r, axis=-1, keepdims=True)
            xc = r - mu
            var = jnp.mean(xc * xc, axis=-1, keepdims=True)
            rstd = lax.rsqrt(var + LN_EPS)
            xh = xc * rstd
            y_ref[...] = xh * g_ref[...] + bias_ref[...]
            xh_ref[...] = xh
            rs_ref[...] = rstd

        if nk == 1:
            finish(p)
        else:
            acc_ref = scr[0]
            k = pl.program_id(1)

            @pl.when(k == 0)
            def _():
                acc_ref[...] = p

            @pl.when(k > 0)
            def _():
                acc_ref[...] += p

            @pl.when(k == nk - 1)
            def _():
                finish(acc_ref[...])

    row = pl.BlockSpec((tm, N), lambda i, k: (i, 0))
    vec = pl.BlockSpec((1, N), lambda i, k: (0, 0))
    nbytes = _nbytes((tm, tk), a.dtype) + _nbytes((tk, N), b.dtype) + 5 * _nbytes((tm, N), F32)
    scratch = [pltpu.VMEM((tm, N), F32)] if nk > 1 else []
    return pl.pallas_call(
        body, name=name,
        grid=(M // tm, nk),
        in_specs=[pl.BlockSpec((tm, tk), lambda i, k: (i, k)), pl.BlockSpec((tk, N), lambda i, k: (k, 0)), row, vec, vec],
        out_specs=[row, row, pl.BlockSpec((tm, 1), lambda i, k: (i, 0))],
        out_shape=[jax.ShapeDtypeStruct((M, N), F32), jax.ShapeDtypeStruct((M, N), F32),
                   jax.ShapeDtypeStruct((M, 1), F32)],
        scratch_shapes=scratch,
        compiler_params=_params(("parallel", "arbitrary"), nbytes),
    )(a, b, xres, g.reshape(1, N), bias.reshape(1, N))


def _ln_bwd(dy, xhat, rstd, g, *, name, tm):
    M, N = dy.shape
    tm = min(tm, M)

    def body(dy_ref, xh_ref, rs_ref, g_ref, dr_ref, dg_ref, db_ref):
        i = pl.program_id(0)
        dyv = dy_ref[...]
        xh = xh_ref[...]
        dxh = dyv * g_ref[...]
        m1 = jnp.mean(dxh, axis=-1, keepdims=True)
        m2 = jnp.mean(dxh * xh, axis=-1, keepdims=True)
        dr_ref[...] = rs_ref[...] * (dxh - m1 - xh * m2)
        pg = _sum8(dyv * xh)
        pb = _sum8(dyv)

        @pl.when(i == 0)
        def _():
            dg_ref[...] = pg
            db_ref[...] = pb

        @pl.when(i > 0)
        def _():
            dg_ref[...] += pg
            db_ref[...] += pb

    row = pl.BlockSpec((tm, N), lambda i: (i, 0))
    acc = pl.BlockSpec((8, N), lambda i: (0, 0))
    return pl.pallas_call(
        body, name=name, grid=(M // tm,),
        in_specs=[row, row, pl.BlockSpec((tm, 1), lambda i: (i, 0)), pl.BlockSpec((1, N), lambda i: (0, 0))],
        out_specs=[row, acc, acc],
        out_shape=[jax.ShapeDtypeStruct((M, N), F32), jax.ShapeDtypeStruct((8, N), F32),
                   jax.ShapeDtypeStruct((8, N), F32)],
        compiler_params=_params(("arbitrary",), 4 * _nbytes((tm, N), F32)),
    )(dy, xhat, rstd, g.reshape(1, N))


def _loss_grad(y, tgt, *, name, tm):
    M, N = y.shape
    tm = min(tm, M)

    def body(y_ref, t_ref, dy_ref, l_ref):
        i = pl.program_id(0)
        d = y_ref[...] - t_ref[...]
        dy_ref[...] = d * (1.0 / N)
        pl_ = _sum8(d * d) * (0.5 / N)

        @pl.when(i == 0)
        def _():
            l_ref[...] = pl_

        @pl.when(i > 0)
        def _():
            l_ref[...] += pl_

    row = pl.BlockSpec((tm, N), lambda i: (i, 0))
    return pl.pallas_call(
        body, name=name, grid=(M // tm,),
        in_specs=[row, row], out_specs=[row, pl.BlockSpec((8, N), lambda i: (0, 0))],
        out_shape=[jax.ShapeDtypeStruct((M, N), F32), jax.ShapeDtypeStruct((8, N), F32)],
        compiler_params=_params(("arbitrary",), 3 * _nbytes((tm, N), F32)),
    )(y, tgt)


def _tri(n, kind):
    j = lax.broadcasted_iota(jnp.int32, (2 * n, n), 0) % n
    s = lax.broadcasted_iota(jnp.int32, (2 * n, n), 1)
    return ((j > s) if kind == "after" else (j < s)).astype(BF16)


LOG2E = 1.4426950408889634
DEAD = -104.0
NOT_VISITED = -1e30


def _log_terms(z):
    lse = jnp.log(1.0 + jnp.exp2(jnp.abs(z) * (-LOG2E)))
    logsig = jnp.minimum(z, 0.0) - lse
    return logsig, logsig - z


def _cumsum_mm(x, u2_ref):
    hi, lo = _split(x)
    return _dot(jnp.concatenate([hi, lo], axis=1), u2_ref[...])


def _head_rows(x2, scale):
    lane = lax.broadcasted_iota(jnp.int32, (1, 128), 1)
    zero = jnp.zeros_like(x2)
    both = jnp.concatenate([jnp.where(lane < HEAD_DIM, x2, zero), jnp.where(lane >= HEAD_DIM, x2, zero)], axis=0)
    return both * scale


def _causal_mask(i, ks, tq, tk):
    row = lax.broadcasted_iota(jnp.int32, (2 * tq, tk), 0)
    row = i * tq + jnp.where(row >= tq, row - tq, row)
    col = lax.broadcasted_iota(jnp.int32, (2 * tq, tk), 1)
    return (ks + col) < row


def _attn_fwd(qkv, *, name, tq, tk):
    S = qkv.shape[0]
    tq, tk = min(tq, S), min(tk, S)
    assert S % tq == 0 and tq % (2 * tk) == 0 and S // tk <= 128
    r = tq // tk
    tri = _tri(tk, "after")

    def body(q_ref, k_ref, v_ref, u_ref, o_ref, c_ref, qcat, oacc, cacc, call, ls_buf, tl_buf, l0_buf):
        i = pl.program_id(1)
        lane = lax.broadcasted_iota(jnp.int32, (1, 128), 1)
        qcat[...] = _head_rows(q_ref[...], SCALE)
        oacc[...] = jnp.zeros_like(oacc)
        cacc[...] = jnp.zeros_like(cacc)
        call[...] = jnp.full_like(call, NOT_VISITED)

        def scores(kb, masked, slot):
            ks = pl.multiple_of(kb * tk, tk)
            z = _dot(qcat[...], k_ref[pl.ds(ks, tk), :], NT)
            logsig, lom = _log_terms(z)
            if masked:
                msk = _causal_mask(i, ks, tq, tk)
                lom = jnp.where(msk, lom, 0.0)
                logsig = jnp.where(msk, logsig, -1e30)
            ls_buf[slot] = logsig
            tl_buf[slot] = _cumsum_mm(lom, u_ref)
            l0_buf[slot] = lom[:, 0:1]

        def weights(kb, slot):
            ks = pl.multiple_of(kb * tk, tk)
            tl = tl_buf[slot]
            c = cacc[...]
            call[...] = jnp.where(lane == kb, c, call[...])
            a = jnp.exp(ls_buf[slot] + tl + c).astype(BF16)
            oacc[...] += _dot(a, v_ref[pl.ds(ks, tk), :])
            cacc[...] = c + tl[:, 0:1] + l0_buf[slot]

        def pair(kb, masked):
            scores(kb, masked, 0)
            scores(kb - 1, masked, 1)
            weights(kb, 0)
            weights(kb - 1, 1)

        for d in range(0, r, 2):
            pair(i * r + r - 1 - d, True)

        def live(state):
            t, cmax = state
            return jnp.logical_and(t < (i * r) // 2, cmax > DEAD)

        def trip(state):
            t, _ = state
            pair(i * r - 1 - 2 * t, False)
            return t + 1, jnp.max(cacc[...])

        lax.while_loop(live, trip, (0, jnp.max(cacc[...])))
        o_ref[...] = jnp.where(lane < HEAD_DIM, oacc[0:tq], oacc[tq:2 * tq])
        c_ref[...] = jnp.concatenate([call[0:tq], call[tq:2 * tq]], axis=1)

    nbytes = (_nbytes((tq, 128), BF16) + 2 * _nbytes((S, 128), BF16) + _nbytes((2 * tk, tk), BF16)
              + 8 * _nbytes((tq, 128), F32) + 14 * _nbytes((2 * tq, tk), F32))
    return pl.pallas_call(
        body, name=name, grid=(4, S // tq),
        in_specs=[pl.BlockSpec((tq, 128), lambda j, i: (i, j)),
                  pl.BlockSpec((S, 128), lambda j, i: (0, 4 + j)),
                  pl.BlockSpec((S, 128), lambda j, i: (0, 8 + j)),
                  pl.BlockSpec((2 * tk, tk), lambda j, i: (0, 0))],
        out_specs=[pl.BlockSpec((tq, 128), lambda j, i: (i, j)),
                   pl.BlockSpec((tq, 256), lambda j, i: (i, j))],
        out_shape=[jax.ShapeDtypeStruct((S, D_SB), F32), jax.ShapeDtypeStruct((S, 1024), F32)],
        scratch_shapes=[pltpu.VMEM((2 * tq, 128), BF16), pltpu.VMEM((2 * tq, 128), F32),
                        pltpu.VMEM((2 * tq, 1), F32), pltpu.VMEM((2 * tq, 128), F32),
                        pltpu.VMEM((2, 2 * tq, tk), F32), pltpu.VMEM((2, 2 * tq, tk), F32),
                        pltpu.VMEM((2, 2 * tq, 1), F32)],
        compiler_params=_params(("parallel", "arbitrary"), nbytes),
    )(qkv, qkv, qkv, tri)


def _attn_bwd(qkv, carry, do, *, name, tq, tk):
    S = qkv.shape[0]
    tq, tk = min(tq, S), min(tk, S)
    assert S % tq == 0 and tq % (2 * tk) == 0 and S // tk <= 128
    r = tq // tk
    nkb = S // tk
    nq = S // tq
    tri_after = _tri(tk, "after")
    tri_before = _tri(tk, "before")

    def body(q_ref, k_ref, v_ref, c_ref, do_ref, ua_ref, ub_ref, dq_ref, dk_ref, dv_ref,
             qcat, docat, qcat_t, docat_t, ccat, dqacc, pacc, dkt, dvt, ls_buf, tl_buf, da_buf):
        i = pl.program_id(1)
        lane = lax.broadcasted_iota(jnp.int32, (1, 128), 1)
        sub = lax.broadcasted_iota(jnp.int32, (128, 1), 0)
        q2 = q_ref[...]
        do2 = do_ref[...]
        qcat[...] = _head_rows(q2, SCALE)
        docat[...] = _head_rows(do2, 1.0).astype(BF16)
        qt = q2.astype(F32).T * SCALE
        dot_ = do2.T
        qcat_t[...] = jnp.concatenate([jnp.where(sub < HEAD_DIM, qt, 0.0), jnp.where(sub >= HEAD_DIM, qt, 0.0)],
                                      axis=1).astype(BF16)
        docat_t[...] = jnp.concatenate([jnp.where(sub < HEAD_DIM, dot_, 0.0), jnp.where(sub >= HEAD_DIM, dot_, 0.0)],
                                       axis=1).astype(BF16)
        ccat[0:tq] = c_ref[:, 0:128]
        ccat[tq:2 * tq] = c_ref[:, 128:256]

        @pl.when(i == 0)
        def _():
            dkt[...] = jnp.zeros_like(dkt)
            dvt[...] = jnp.zeros_like(dvt)

        dqacc[...] = jnp.zeros_like(dqacc)
        pacc[...] = jnp.zeros_like(pacc)

        def scores(kb, masked, slot):
            ks = pl.multiple_of(kb * tk, tk)
            z = _dot(qcat[...], k_ref[pl.ds(ks, tk), :], NT)
            logsig, lom = _log_terms(z)
            if masked:
                msk = _causal_mask(i, ks, tq, tk)
                lom = jnp.where(msk, lom, 0.0)
                logsig = jnp.where(msk, logsig, -1e30)
            ls_buf[slot] = logsig
            tl_buf[slot] = _cumsum_mm(lom, ua_ref)
            da_buf[slot] = _dot(docat[...], v_ref[pl.ds(ks, tk), :], NT)

        def grads(kb, slot):
            ks = pl.multiple_of(kb * tk, tk)
            logsig = ls_buf[slot]
            c = jnp.sum(jnp.where(lane == kb, ccat[...], 0.0), axis=1, keepdims=True)
            a = jnp.exp(logsig + tl_buf[slot] + c)
            g = a * da_buf[slot]
            before = _cumsum_mm(g, ub_ref)
            pc = pacc[...]
            dz = g - jnp.exp(logsig) * (g + before + pc)
            dzb = dz.astype(BF16)
            dqacc[...] += _dot(dzb, k_ref[pl.ds(ks, tk), :])
            dkt[kb] += _dot(qcat_t[...], dzb)
            dvt[kb] += _dot(docat_t[...], a.astype(BF16))
            pacc[...] = pc + before[:, tk - 1:tk] + g[:, tk - 1:tk]

        def pair(kb, masked):
            scores(kb, masked, 0)
            scores(kb + 1, masked, 1)
            grads(kb, 0)
            grads(kb + 1, 1)

        def loop(t, carry_):
            pair(2 * t, False)
            return carry_

        reach = jnp.max(ccat[...], axis=0, keepdims=True)
        first = jnp.min(jnp.where(reach > DEAD, lane, 128).astype(F32)).astype(jnp.int32)
        lax.fori_loop(jnp.minimum(first, i * r) // 2, (i * r) // 2, loop, 0)
        for d in range(0, r, 2):
            pair(i * r + d, True)
        dq_ref[...] = jnp.where(lane < HEAD_DIM, dqacc[0:tq], dqacc[tq:2 * tq]) * SCALE

        @pl.when(i == nq - 1)
        def _():
            for kb in range(nkb):
                dk_ref[kb * tk:(kb + 1) * tk, :] = dkt[kb].T
                dv_ref[kb * tk:(kb + 1) * tk, :] = dvt[kb].T

    nbytes = (_nbytes((tq, 128), BF16) + 2 * _nbytes((S, 128), BF16) + 2 * _nbytes((2 * tk, tk), BF16)
              + 12 * _nbytes((tq, 128), F32) + 4 * _nbytes((S, 128), F32) + 14 * _nbytes((2 * tq, tk), F32))
    blk = pl.BlockSpec((tq, 128), lambda j, i: (i, j))
    full = pl.BlockSpec((S, 128), lambda j, i: (0, j))
    tri_spec = pl.BlockSpec((2 * tk, tk), lambda j, i: (0, 0))
    dq, dk, dv = pl.pallas_call(
        body, name=name, grid=(4, nq),
        in_specs=[blk,
                  pl.BlockSpec((S, 128), lambda j, i: (0, 4 + j)),
                  pl.BlockSpec((S, 128), lambda j, i: (0, 8 + j)),
                  pl.BlockSpec((tq, 256), lambda j, i: (i, j)),
                  blk, tri_spec, tri_spec],
        out_specs=[blk, full, full],
        out_shape=[jax.ShapeDtypeStruct((S, D_SB), F32)] * 3,
        scratch_shapes=[pltpu.VMEM((2 * tq, 128), BF16), pltpu.VMEM((2 * tq, 128), BF16),
                        pltpu.VMEM((128, 2 * tq), BF16), pltpu.VMEM((128, 2 * tq), BF16),
                        pltpu.VMEM((2 * tq, 128), F32), pltpu.VMEM((2 * tq, 128), F32), pltpu.VMEM((2 * tq, 1), F32),
                        pltpu.VMEM((nkb, 128, tk), F32), pltpu.VMEM((nkb, 128, tk), F32),
                        pltpu.VMEM((2, 2 * tq, tk), F32), pltpu.VMEM((2, 2 * tq, tk), F32),
                        pltpu.VMEM((2, 2 * tq, tk), F32)],
        compiler_params=_params(("parallel", "arbitrary"), nbytes),
    )(qkv, qkv, qkv, carry, do, tri_after, tri_before)
    return dq, dk, dv


def _group_mats():
    lanes = jnp.arange(D_MODEL) // HEAD_DIM
    gs = (lanes[:, None] == jnp.arange(128)[None, :]).astype(BF16)
    return gs, gs.T


def _group_sum_bcast(x, gs, gb):
    hi, lo = _split(x)
    s = _dot(hi, gs) + _dot(lo, gs)
    return _bcast(s, gb)


def _bcast(s, gb):
    hi, lo = _split(s)
    return _dot(hi, gb) + _dot(lo, gb)


def _pool_lane_consts():
    lane = lax.broadcasted_iota(jnp.int32, (1, D_POOL), 1)
    grp = lane // (D_POOL // 4)
    win = jnp.where(grp == 0, 2, jnp.where(grp == 1, 4, jnp.where(grp == 2, 8, 16)))
    return grp, win


def _by_group(grp, s2, s4, s8, s16):
    return jnp.where(grp == 0, s2, jnp.where(grp == 1, s4, jnp.where(grp == 2, s8, s16)))


def _mixers(i, ts, prev_ref, cur_ref, cw_ref, pw_ref, ps_ref):
    cur = cur_ref[...]
    prev = jnp.where(i == 0, 0.0, prev_ref[...])
    ext = jnp.concatenate([prev, cur], axis=0)
    n = HALO + ts

    def back(a, k):
        return pltpu.roll(a, k, 0)

    u = ext[:, D_CONV:2 * D_CONV] * ext[:, 2 * D_CONV:3 * D_CONV]
    p = ext[:, 3 * D_CONV:]
    cv = (cw_ref[0:1, :] * back(u, 2) + cw_ref[1:2, :] * back(u, 1) + cw_ref[2:3, :] * u)[HALO:]
    s2 = p + back(p, 1)
    s4 = s2 + back(s2, 2)
    s8 = s4 + back(s4, 4)
    s16 = s8 + back(s8, 8)
    grp, win = _pool_lane_consts()
    t1 = i * ts + 1 + lax.broadcasted_iota(jnp.int32, (ts, 1), 0)
    cnt = jnp.minimum(t1, win).astype(F32)
    pooled = _by_group(grp, s2, s4, s8, s16)[HALO:] / cnt - p[HALO:]
    yp = _dot(pooled.astype(BF16), pw_ref[...])
    del n
    return dict(b=cur[:, 0:D_CONV], u=u, cv=cv, pooled=pooled, yp=yp, cnt=cnt,
                conv_out=cur[:, 0:D_CONV] * cv, pool_out=yp * ps_ref[...])


def _halo_specs(ts, S, width):
    nb = ts // HALO
    last = S // HALO - 1
    prev = pl.BlockSpec((HALO, width), lambda i: (jnp.maximum(i * nb - 1, 0), 0))
    nxt = pl.BlockSpec((HALO, width), lambda i: (jnp.minimum((i + 1) * nb, last), 0))
    return prev, nxt


def _mixer_fwd(rest, attn, cw8, pwbd, ps, gain, *, name, ts):
    S = rest.shape[0]
    ts = min(ts, S)
    gs, gb = _group_mats()

    def body(prev_ref, cur_ref, attn_ref, cw_ref, pw_ref, ps_ref, gain_ref, gs_ref, gb_ref, o_ref):
        i = pl.program_id(0)
        f = _mixers(i, ts, prev_ref, cur_ref, cw_ref, pw_ref, ps_ref)
        mix = jnp.concatenate([attn_ref[...], f["conv_out"], f["pool_out"]], axis=1)
        ss = _group_sum_bcast(mix * mix, gs_ref[...], gb_ref[...])
        rinv = lax.rsqrt(ss * (1.0 / HEAD_DIM) + RMS_EPS)
        o_ref[...] = (mix * rinv * gain_ref[...]).astype(BF16)

    prev, _ = _halo_specs(ts, S, D_REST)
    row = lambda w: pl.BlockSpec((ts, w), lambda i: (i, 0))
    const = lambda a: pl.BlockSpec(a.shape, lambda i: (0, 0))
    nbytes = 12 * _nbytes((ts + HALO, D_REST), F32)
    return pl.pallas_call(
        body, name=name, grid=(S // ts,),
        in_specs=[prev, row(D_REST), row(D_SB), const(cw8), const(pwbd), const(ps), const(gain), const(gs), const(gb)],
        out_specs=row(D_MODEL),
        out_shape=jax.ShapeDtypeStruct((S, D_MODEL), BF16),
        compiler_params=_params(("parallel",), nbytes),
    )(rest, rest, attn, cw8, pwbd, ps, gain, gs, gb)


def _mixer_bwd1(dmixn, rest, attn, cw8, pwbd, ps, gain, *, name, ts):
    S = rest.shape[0]
    ts = min(ts, S)
    gs, gb = _group_mats()

    def body(dm_ref, prev_ref, cur_ref, attn_ref, cw_ref, pw_ref, ps_ref, gain_ref, gs_ref, gb_ref,
             da_ref, aux_ref, dg_ref, dsc_ref, dcw_ref, dpw_ref):
        i = pl.program_id(0)
        f = _mixers(i, ts, prev_ref, cur_ref, cw_ref, pw_ref, ps_ref)
        mix = jnp.concatenate([attn_ref[...], f["conv_out"], f["pool_out"]], axis=1)
        gsm, gbm = gs_ref[...], gb_ref[...]
        ss = _group_sum_bcast(mix * mix, gsm, gbm)
        rinv = lax.rsqrt(ss * (1.0 / HEAD_DIM) + RMS_EPS)
        dm = dm_ref[...]
        xn = mix * rinv
        dyg = dm * gain_ref[...]
        gm = _group_sum_bcast(dyg * xn, gsm, gbm) * (1.0 / HEAD_DIM)
        dmix = rinv * (dyg - xn * gm)
        da_ref[...] = dmix[:, 0:D_SB]
        dco = dmix[:, D_SB:D_SB + D_CONV]
        dpo = dmix[:, D_SB + D_CONV:]
        dcv = dco * f["b"]
        dyp = dpo * ps_ref[...]
        dpooled = _dot(dyp.astype(BF16), pw_ref[...], NT)
        aux_ref[...] = jnp.concatenate([dco * f["cv"], dcv, dpooled / f["cnt"], dpooled], axis=1)
        u = f["u"]
        parts = [
            _sum8(dm * xn),
            _sum8(dpo * f["yp"]),
            jnp.concatenate([_sum8(dcv * pltpu.roll(u, 2, 0)[HALO:]), _sum8(dcv * pltpu.roll(u, 1, 0)[HALO:]),
                             _sum8(dcv * u[HALO:])], axis=0),
            _dot(f["pooled"].astype(BF16), dyp.astype(BF16), TN),
        ]
        outs = [dg_ref, dsc_ref, dcw_ref, dpw_ref]

        @pl.when(i == 0)
        def _():
            for o, v in zip(outs, parts):
                o[...] = v

        @pl.when(i > 0)
        def _():
            for o, v in zip(outs, parts):
                o[...] += v

    prev, _ = _halo_specs(ts, S, D_REST)
    row = lambda w: pl.BlockSpec((ts, w), lambda i: (i, 0))
    const = lambda a: pl.BlockSpec(a.shape, lambda i: (0, 0))
    acc = lambda r_, w: pl.BlockSpec((r_, w), lambda i: (0, 0))
    nbytes = 16 * _nbytes((ts + HALO, D_REST), F32)
    return pl.pallas_call(
        body, name=name, grid=(S // ts,),
        in_specs=[row(D_MODEL), prev, row(D_REST), row(D_SB), const(cw8), const(pwbd), const(ps), const(gain),
                  const(gs), const(gb)],
        out_specs=[row(D_SB), row(D_REST), acc(8, D_MODEL), acc(8, D_POOL), acc(24, D_CONV), acc(D_POOL, D_POOL)],
        out_shape=[jax.ShapeDtypeStruct((S, D_SB), F32), jax.ShapeDtypeStruct((S, D_REST), F32),
                   jax.ShapeDtypeStruct((8, D_MODEL), F32), jax.ShapeDtypeStruct((8, D_POOL), F32),
                   jax.ShapeDtypeStruct((24, D_CONV), F32), jax.ShapeDtypeStruct((D_POOL, D_POOL), F32)],
        compiler_params=_params(("arbitrary",), nbytes),
    )(dmixn, rest, rest, attn, cw8, pwbd, ps, gain, gs, gb)


def _mixer_bwd2(aux, rest, cw8, *, name, ts):
    S = rest.shape[0]
    ts = min(ts, S)
    nblk = S // ts

    def body(cur_ref, nxt_ref, rest_ref, cw_ref, o_ref):
        i = pl.program_id(0)
        cur = cur_ref[...]
        nxt = jnp.where(i == nblk - 1, 0.0, nxt_ref[...])
        ext = jnp.concatenate([cur, nxt], axis=0)
        n = ts + HALO

        def fwd(a, k):
            return pltpu.roll(a, n - k, 0)

        dcv = ext[:, D_CONV:2 * D_CONV]
        dps = ext[:, 2 * D_CONV:3 * D_CONV]
        du = (cw_ref[2:3, :] * dcv + cw_ref[1:2, :] * fwd(dcv, 1) + cw_ref[0:1, :] * fwd(dcv, 2))[0:ts]
        f2 = dps + fwd(dps, 1)
        f4 = f2 + fwd(f2, 2)
        f8 = f4 + fwd(f4, 4)
        f16 = f8 + fwd(f8, 8)
        grp, _ = _pool_lane_consts()
        dp = _by_group(grp, f2, f4, f8, f16)[0:ts] - cur[:, 3 * D_CONV:]
        rest_v = rest_ref[...]
        c_gate = rest_v[:, D_CONV:2 * D_CONV]
        h = rest_v[:, 2 * D_CONV:3 * D_CONV]
        o_ref[...] = jnp.concatenate([cur[:, 0:D_CONV], du * h, du * c_gate, dp], axis=1)

    _, nxt = _halo_specs(ts, S, D_REST)
    row = pl.BlockSpec((ts, D_REST), lambda i: (i, 0))
    return pl.pallas_call(
        body, name=name, grid=(nblk,),
        in_specs=[row, nxt, row, pl.BlockSpec(cw8.shape, lambda i: (0, 0))],
        out_specs=row,
        out_shape=jax.ShapeDtypeStruct((S, D_REST), F32),
        compiler_params=_params(("parallel",), 10 * _nbytes((ts + HALO, D_REST), F32)),
    )(aux, aux, rest, cw8)


def _block_diag(pw):
    z = jnp.zeros((4, 64, 4, 64), pw.dtype)
    for g in range(4):
        z = z.at[g, :, g, :].set(pw[g])
    return z.reshape(256, 256)


def _rows8(v, rows=8):
    return jnp.pad(v, ((0, rows - v.shape[0]), (0, 0)))


TILES = dict(tm=512, ts=512, tq=512, tk=256)


def _local_step(x, tgt, w, t=None):
    t = dict(TILES, **(t or {}))
    tm, ts, tq, tk = t["tm"], t["ts"], t["tq"], t["tk"]
    saved = []
    xl = x
    for l in range(DEPTH):
        n = f"l{l}_"
        w_in = w["w_in"][l]
        qkv = _matmul(xl, w_in[:, :D_QKV], name=n + "proj_qkv", tm=1024, tn=512, tk=1024, out_dtype=BF16)
        rest = _matmul(xl, w_in[:, D_QKV:], name=n + "proj_rest", tm=1024, tn=512, tk=1024)
        attn, carry = _attn_fwd(qkv, name=n + "attn_fwd", tq=tq, tk=tk)
        cw8 = _rows8(w["conv_w"][l])
        pwbd = _block_diag(w["pool_w"][l]).astype(BF16)
        ps = w["pool_scale"][l].reshape(1, D_POOL)
        gain = w["mix_norm_g"][l].reshape(1, D_MODEL)
        mixn = _mixer_fwd(rest, attn, cw8, pwbd, ps, gain, name=n + "mixer_fwd", ts=ts)
        x1, xh1, rs1 = _matmul_ln(mixn, w["w_o"][l], xl, w["ln1_g"][l], w["ln1_b"][l], name=n + "wo_ln", tm=tm, tk=1024)
        hpre = _matmul(x1, w["w_up"][l], name=n + "ffn_up", tm=1024, tn=1024, tk=1024)
        x2, xh2, rs2 = _matmul_ln(hpre, w["w_down"][l], x1, w["ln2_g"][l], w["ln2_b"][l], name=n + "ffn_down_ln",
                                  tm=tm, tk=1024, a_pro="relu2")
        saved.append(dict(xin=xl, qkv=qkv, rest=rest, attn=attn, carry=carry, cw8=cw8, pwbd=pwbd, ps=ps, gain=gain,
                          mixn=mixn, x1=x1, xh1=xh1, rs1=rs1, hpre=hpre, xh2=xh2, rs2=rs2))
        xl = x2

    dy, lsum = _loss_grad(xl, tgt, name="loss_grad", tm=tm)
    grads = {k: [None] * DEPTH for k in
             ("w_in", "conv_w", "pool_w", "pool_scale", "mix_norm_g", "w_o", "ln1_g", "ln1_b", "w_up", "w_down",
              "ln2_g", "ln2_b")}
    for l in reversed(range(DEPTH)):
        n = f"l{l}_"
        s = saved[l]
        dr2, dg2, db2 = _ln_bwd(dy, s["xh2"], s["rs2"], w["ln2_g"][l], name=n + "ln2_bwd", tm=tm)
        dhpre = _matmul(dr2, w["w_down"][l], name=n + "ffn_down_dx", tm=1024, tn=1024, tk=1024, tb=True,
                        out_dtype=BF16, epi="drelu2", e=s["hpre"])
        grads["w_down"][l] = _matmul(s["hpre"], dr2, name=n + "ffn_down_dw", tm=1024, tn=1024, tk=512, ta=True,
                                     a_pro="relu2")
        dx1 = _matmul(dhpre, w["w_up"][l], name=n + "ffn_up_dx", tm=1024, tn=1024, tk=1024, tb=True,
                      epi="add", e=dr2, e_scale=ALPHA)
        grads["w_up"][l] = _matmul(s["x1"], dhpre, name=n + "ffn_up_dw", tm=1024, tn=1024, tk=512, ta=True)
        dr1, dg1, db1 = _ln_bwd(dx1, s["xh1"], s["rs1"], w["ln1_g"][l], name=n + "ln1_bwd", tm=tm)
        dmixn = _matmul(dr1, w["w_o"][l], name=n + "wo_dx", tm=1024, tn=1024, tk=1024, tb=True)
        grads["w_o"][l] = _matmul(s["mixn"], dr1, name=n + "wo_dw", tm=1024, tn=1024, tk=512, ta=True)
        d_attn, aux, dgain, dsc, dcw, dpw = _mixer_bwd1(dmixn, s["rest"], s["attn"], s["cw8"], s["pwbd"], s["ps"],
                                                        s["gain"], name=n + "mixer_bwd1", ts=ts)
        drest = _mixer_bwd2(aux, s["rest"], s["cw8"], name=n + "mixer_bwd2", ts=ts)
        dq, dk, dv = _attn_bwd(s["qkv"], s["carry"], d_attn, name=n + "attn_bwd", tq=tq, tk=tk)
        dqkv = jnp.concatenate([dq, dk, dv], axis=1)
        w_in = w["w_in"][l]
        dxa = _matmul(dqkv, w_in[:, :D_QKV], name=n + "proj_qkv_dx", tm=1024, tn=1024, tk=512, tb=True,
                      epi="add", e=dr1, e_scale=ALPHA)
        dy = _matmul(drest, w_in[:, D_QKV:], name=n + "proj_rest_dx", tm=1024, tn=1024, tk=1024, tb=True,
                     epi="add", e=dxa, e_scale=1.0)
        dwq = _matmul(s["xin"], dqkv, name=n + "proj_qkv_dw", tm=1024, tn=512, tk=512, ta=True)
        dwr = _matmul(s["xin"], drest, name=n + "proj_rest_dw", tm=1024, tn=1024, tk=512, ta=True)
        grads["w_in"][l] = jnp.concatenate([dwq, dwr], axis=1)
        grads["ln2_g"][l] = dg2.sum(0)
        grads["ln2_b"][l] = db2.sum(0)
        grads["ln1_g"][l] = dg1.sum(0)
        grads["ln1_b"][l] = db1.sum(0)
        grads["mix_norm_g"][l] = dgain.sum(0)
        grads["pool_scale"][l] = dsc.sum(0)
        grads["conv_w"][l] = dcw.reshape(3, 8, D_CONV).sum(1)
        grads["pool_w"][l] = jnp.stack([dpw[64 * g:64 * g + 64, 64 * g:64 * g + 64] for g in range(4)])
    grads = {k: jnp.stack(v) for k, v in grads.items()}
    return lsum, dy, grads


ANY = pl.BlockSpec(memory_space=pl.ANY)


def _place():
    x, y, c = lax.axis_index("x"), lax.axis_index("y"), lax.axis_index("c")
    chips = [(1 - x, y), (x, 1 - y), (1 - x, 1 - y)]
    return x, y, c, chips


def _remote(src, dst, send_sems, recv_sems, k, to):
    return pltpu.make_async_remote_copy(src_ref=src, dst_ref=dst, send_sem=send_sems.at[k], recv_sem=recv_sems.at[k],
                                        device_id=to, device_id_type=MESH)


DMA_CHUNK_BYTES = 1 << 20


def _n_chunks(rows, dtype, width=D_MODEL):
    tiles = rows // 16
    want = max(1, (rows * width * jnp.dtype(dtype).itemsize) // DMA_CHUNK_BYTES)
    best = 1
    for n in range(1, tiles + 1):
        if tiles % n == 0 and n <= want:
            best = n
    return best


class _Chunked:
    def __init__(self, src, dst, send_sems, recv_sems, k, to, n):
        self.args = (send_sems, recv_sems, k, to)
        self.whole = self._one(src, dst)
        rows = src.shape[-2]
        assert rows % n == 0
        step = rows // n
        leads = [()] if len(src.shape) == 2 else [(i,) for i in range(src.shape[0])]
        self.parts = [self._one(src.at[(*lead, pl.ds(t * step, step))], dst.at[(*lead, pl.ds(t * step, step))])
                      for lead in leads for t in range(n)]

    def _one(self, src, dst):
        send_sems, recv_sems, k, to = self.args
        if to is None:
            return pltpu.make_async_copy(src, dst, recv_sems)
        return _remote(src, dst, send_sems, recv_sems, k, to)

    def like(self, src, dst):
        return self._one(src, dst)

    def start(self):
        for p in self.parts:
            p.start()

    def wait(self):
        self.whole.wait()

    def wait_send(self):
        self.whole.wait_send()

    def wait_recv(self):
        self.whole.wait_recv()


def _allgather_chips(pack, *, name):
    R, C = pack.shape
    H = R // 2

    def body(p_ref, o_ref, send_sems, recv_sems):
        x, y, c, chips = _place()
        my = 2 * x + y
        sib = (x, y, 1 - c)
        n = _n_chunks(H, pack.dtype)

        def half(k, hc):
            return o_ref.at[k, pl.ds(hc * H, H), :]

        first = [_Chunked(p_ref.at[pl.ds(c * H, H), :], half(my, c), send_sems, recv_sems, j, (cx, cy, c), n)
                 for j, (cx, cy) in enumerate(chips)]
        for cp in first:
            cp.start()
        passed = []
        for j, (cx, cy) in enumerate(chips):
            k = 2 * cx + cy
            first[j].like(half(k, c), half(k, c)).wait_recv()
            fwd = _Chunked(half(k, c), half(k, c), send_sems, recv_sems, 3 + j, sib, n)
            fwd.start()
            passed.append(fwd)
        for j, (cx, cy) in enumerate(chips):
            k = 2 * cx + cy
            passed[j].like(half(k, 1 - c), half(k, 1 - c)).wait_recv()
        for cp in first + passed:
            cp.wait_send()

    return pl.pallas_call(
        body, name=name, in_specs=[ANY], out_specs=ANY,
        out_shape=jax.ShapeDtypeStruct((N_CHIPS, R, C), pack.dtype),
        scratch_shapes=[pltpu.SemaphoreType.DMA((6,)), pltpu.SemaphoreType.DMA((6,))],
    )(pack)


def _swap_halves(gp, *, name):
    K, R, C = gp.shape
    H = R // 2

    def body(g_ref, theirs_ref, send_sems, recv_sems):
        x, y, c, _ = _place()
        n = _n_chunks(H, gp.dtype)
        cp = _Chunked(g_ref.at[:, pl.ds((1 - c) * H, H), :], theirs_ref, send_sems, recv_sems, 0, (x, y, 1 - c), n)
        cp.start()
        cp.wait()

    return pl.pallas_call(
        body, name=name, in_specs=[ANY], out_specs=ANY, out_shape=jax.ShapeDtypeStruct((K, H, C), gp.dtype),
        scratch_shapes=[pltpu.SemaphoreType.DMA((1,)), pltpu.SemaphoreType.DMA((1,))],
    )(gp)


def _scatter_chips(part, *, name):
    K, H, C = part.shape

    def body(p_ref, o_ref, send_sems, recv_sems):
        x, y, c, chips = _place()
        n = _n_chunks(H, part.dtype)
        copies = [_Chunked(p_ref.at[2 * cx + cy], o_ref.at[j], send_sems, recv_sems, j, (cx, cy, c), n)
                  for j, (cx, cy) in enumerate(chips)]
        for cp in copies:
            cp.start()
        for cp in copies:
            cp.wait()

    return pl.pallas_call(
        body, name=name, in_specs=[ANY], out_specs=ANY,
        out_shape=jax.ShapeDtypeStruct((3, H, C), part.dtype),
        scratch_shapes=[pltpu.SemaphoreType.DMA((3,)), pltpu.SemaphoreType.DMA((3,))],
    )(part)


def _join_halves(half, *, name):
    H, C = half.shape

    def body(h_ref, o_ref, send_sems, recv_sems):
        x, y, c, _ = _place()
        n = _n_chunks(H, half.dtype)
        cp = _Chunked(h_ref, o_ref.at[pl.ds(c * H, H), :], send_sems, recv_sems, 0, (x, y, 1 - c), n)
        cp.start()
        cp.wait_send()
        cp.like(h_ref, o_ref.at[pl.ds((1 - c) * H, H), :]).wait_recv()

    return pl.pallas_call(
        body, name=name, in_specs=[ANY], out_specs=ANY,
        out_shape=jax.ShapeDtypeStruct((2 * H, C), half.dtype),
        scratch_shapes=[pltpu.SemaphoreType.DMA((1,)), pltpu.SemaphoreType.DMA((1,))],
    )(half)


def _allreduce_small(v, *, name):
    R, C = v.shape
    n_dev = 8

    def body(v_ref, o_ref, gat, send_sems, recv_sems):
        x, y, c, chips = _place()
        sib = (x, y, 1 - c)

        def rows(px, py, pc):
            return gat.at[4 * px + 2 * py + pc]

        gat[4 * x + 2 * y + c] = v_ref[...]
        first = [_remote(v_ref, rows(x, y, c), send_sems, recv_sems, 0, sib)]
        first += [_remote(v_ref, rows(x, y, c), send_sems, recv_sems, 1 + j, (cx, cy, c))
                  for j, (cx, cy) in enumerate(chips)]
        for cp in first:
            cp.start()
        passed = []
        for j, (cx, cy) in enumerate(chips):
            _remote(v_ref, rows(cx, cy, c), send_sems, recv_sems, 1 + j, sib).wait_recv()
            fwd = _remote(rows(cx, cy, c), rows(cx, cy, c), send_sems, recv_sems, 4 + j, sib)
            fwd.start()
            passed.append(fwd)
        _remote(v_ref, rows(x, y, 1 - c), send_sems, recv_sems, 0, sib).wait_recv()
        for j, (cx, cy) in enumerate(chips):
            _remote(v_ref, rows(cx, cy, 1 - c), send_sems, recv_sems, 4 + j, sib).wait_recv()
        for cp in first + passed:
            cp.wait_send()
        acc = gat[0]
        for d in range(1, n_dev):
            acc = acc + gat[d]
        o_ref[...] = acc

    vm = pl.BlockSpec(memory_space=pltpu.VMEM)
    return pl.pallas_call(
        body, name=name, in_specs=[vm], out_specs=vm,
        out_shape=jax.ShapeDtypeStruct((R, C), F32),
        scratch_shapes=[pltpu.VMEM((n_dev, R, C), F32), pltpu.SemaphoreType.DMA((7,)), pltpu.SemaphoreType.DMA((7,))],
    )(v)


def _add_pairs(a, b, *, name, tr):
    K, H, C = a.shape
    tr = min(tr, H)
    assert H % tr == 0

    def body(a_ref, b_ref, o_ref):
        o_ref[...] = (a_ref[...].astype(F32) + b_ref[...].astype(F32)).astype(BF16)

    blk = pl.BlockSpec((1, tr, C), lambda k, i: (k, i, 0))
    return pl.pallas_call(
        body, name=name, grid=(K, H // tr), in_specs=[blk, blk], out_specs=blk,
        out_shape=jax.ShapeDtypeStruct((K, H, C), BF16),
        compiler_params=_params(("parallel", "parallel"), 3 * _nbytes((tr, C), BF16)),
    )(a, b)


def _add_final(a, b, others, *, name, tr):
    H, C = a.shape
    tr = min(tr, H)
    assert H % tr == 0

    def body(a_ref, b_ref, o_ref_in, out_ref):
        acc = a_ref[...].astype(F32) + b_ref[...].astype(F32)
        for j in range(3):
            acc = acc + o_ref_in[j].astype(F32)
        out_ref[...] = acc

    blk = pl.BlockSpec((tr, C), lambda i: (i, 0))
    return pl.pallas_call(
        body, name=name, grid=(H // tr,),
        in_specs=[blk, blk, pl.BlockSpec((3, tr, C), lambda i: (0, i, 0))], out_specs=blk,
        out_shape=jax.ShapeDtypeStruct((H, C), F32),
        compiler_params=_params(("parallel",), 6 * _nbytes((tr, C), F32)),
    )(a, b, others)


def _adamw(w, g, m, v, *, name, tr, row0=0):
    R, C = w.shape
    tr = min(tr, R)
    assert R % tr == 0 and row0 % tr == 0
    off = row0 // tr

    def body(w_ref, g_ref, m_ref, v_ref, go_ref, d_ref, mo_ref, vo_ref):
        gv = g_ref[...]
        m2 = ADAM_B1 * m_ref[...] + (1.0 - ADAM_B1) * gv
        v2 = ADAM_B2 * v_ref[...] + (1.0 - ADAM_B2) * jnp.square(gv)
        m_hat = m2 / (1.0 - ADAM_B1 ** ADAM_STEP)
        v_hat = v2 / (1.0 - ADAM_B2 ** ADAM_STEP)
        d_ref[...] = -ADAM_LR * (m_hat / (jnp.sqrt(v_hat) + ADAM_EPS) + ADAM_WD * w_ref[...])
        go_ref[...] = gv
        mo_ref[...] = m2
        vo_ref[...] = v2

    blk = pl.BlockSpec((tr, C), lambda i: (i, 0))
    shape = jax.ShapeDtypeStruct((R, C), F32)
    return pl.pallas_call(
        body, name=name, grid=(R // tr,),
        in_specs=[blk, pl.BlockSpec((tr, C), lambda i: (i + off, 0)), blk, blk], out_specs=[blk] * 4,
        out_shape=[shape] * 4,
        compiler_params=_params(("parallel",), 8 * _nbytes((tr, C), F32)),
    )(w, g, m, v)


BIG = ("w_in", "w_o", "w_up", "w_down")
BIG_AXIS = dict(w_in=2, w_o=1, w_up=2, w_down=1)
SMALL = ("pool_w", "pool_scale", "mix_norm_g", "ln1_g", "ln1_b", "ln2_g", "ln2_b")
CONV_ROWS = 32
SMALL_ROWS = 48


def _big_rows(shards):
    sizes = [shards[n].size // D_MODEL for n in BIG]
    starts = [sum(sizes[:i]) for i in range(len(sizes))]
    return sizes, starts


def _pad_rows(flat, rows):
    return jnp.pad(flat, (0, rows * D_MODEL - flat.shape[0])).reshape(rows, D_MODEL)


def kernel(x, w_in, conv_w, pool_w, pool_scale, mix_norm_g, w_o, ln1_g, ln1_b, w_up, w_down, ln2_g, ln2_b, loss_target, m_w_in, m_conv_w, m_pool_w, m_pool_scale, m_mix_norm_g, m_w_o, m_ln1_g, m_ln1_b, m_w_up, m_w_down, m_ln2_g, m_ln2_b, v_w_in, v_conv_w, v_pool_w, v_pool_scale, v_mix_norm_g, v_w_o, v_ln1_g, v_ln1_b, v_w_up, v_w_down, v_ln2_g, v_ln2_b):
    wts = dict(w_in=w_in, conv_w=conv_w, pool_w=pool_w, pool_scale=pool_scale, mix_norm_g=mix_norm_g, w_o=w_o,
               ln1_g=ln1_g, ln1_b=ln1_b, w_up=w_up, w_down=w_down, ln2_g=ln2_g, ln2_b=ln2_b)
    mom = dict(w_in=m_w_in, conv_w=m_conv_w, pool_w=m_pool_w, pool_scale=m_pool_scale, mix_norm_g=m_mix_norm_g,
               w_o=m_w_o, ln1_g=m_ln1_g, ln1_b=m_ln1_b, w_up=m_w_up, w_down=m_w_down, ln2_g=m_ln2_g, ln2_b=m_ln2_b)
    var = dict(w_in=v_w_in, conv_w=v_conv_w, pool_w=v_pool_w, pool_scale=v_pool_scale, mix_norm_g=v_mix_norm_g,
               w_o=v_w_o, ln1_g=v_ln1_g, ln1_b=v_ln1_b, w_up=v_w_up, w_down=v_w_down, ln2_g=v_ln2_g, ln2_b=v_ln2_b)
    chip = 2 * lax.axis_index("x") + lax.axis_index("y")
    sizes, starts = _big_rows(wts)
    big_rows = sum(sizes)

    conv_bits = lax.bitcast_convert_type(conv_w.reshape(-1), BF16).reshape(-1)
    pack = jnp.concatenate([wts[n].reshape(-1, D_MODEL).astype(BF16) for n in BIG]
                           + [_pad_rows(conv_bits, CONV_ROWS)], axis=0)
    gathered = _allgather_chips(pack, name="gather_weights")
    gathered = lax.dynamic_update_index_in_dim(gathered, pack, chip, 0)
    full = {}
    for n, size, start in zip(BIG, sizes, starts):
        parts = [gathered[k, start:start + size].reshape(wts[n].shape) for k in range(N_CHIPS)]
        full[n] = jnp.concatenate(parts, axis=BIG_AXIS[n])
    conv_parts = [lax.bitcast_convert_type(gathered[k, big_rows:].reshape(-1)[:2 * conv_w.size].reshape(-1, 2), F32)
                  .reshape(conv_w.shape) for k in range(N_CHIPS)]
    full["conv_w"] = jnp.concatenate(conv_parts, axis=2)
    for n in SMALL:
        full[n] = wts[n]

    lsum, grad_x, grads = _local_step(x[0], loss_target[0], full)

    blocks = []
    for k in range(N_CHIPS):
        rows = []
        for n in BIG:
            ax = BIG_AXIS[n]
            width = wts[n].shape[ax]
            rows.append(lax.slice_in_dim(grads[n], k * width, (k + 1) * width, axis=ax).reshape(-1, D_MODEL))
        blocks.append(jnp.concatenate(rows, axis=0))
    gpack = jnp.stack(blocks).astype(BF16)
    core = lax.axis_index("c")
    half_rows = big_rows // 2
    theirs = _swap_halves(gpack, name="grad_swap_cores")
    mine = lax.dynamic_slice_in_dim(gpack, core * half_rows, half_rows, axis=1)
    chip_sum = _add_pairs(mine, theirs, name="grad_add_cores", tr=736)
    from_chips = _scatter_chips(chip_sum, name="grad_scatter_chips")
    mine_k = lax.dynamic_index_in_dim(mine, chip, 0, keepdims=False)
    theirs_k = lax.dynamic_index_in_dim(theirs, chip, 0, keepdims=False)
    half_sum = _add_final(mine_k, theirs_k, from_chips, name="grad_add_chips", tr=736)
    gsum = _join_halves(half_sum, name="grad_join_cores")
    gsum = lax.dynamic_update_slice_in_dim(gsum, half_sum, core * half_rows, axis=0)

    small_flat = jnp.concatenate([grads[n].reshape(-1) for n in SMALL] + [grads["conv_w"].reshape(-1),
                                                                          lsum.sum().reshape(1)])
    small_sum = _allreduce_small(_pad_rows(small_flat, SMALL_ROWS), name="allreduce_small").reshape(-1)
    gsmall = {}
    pos = 0
    for n in SMALL:
        gsmall[n] = small_sum[pos:pos + wts[n].size].reshape(wts[n].shape)
        pos += wts[n].size
    conv_full = small_sum[pos:pos + 4 * conv_w.size].reshape(DEPTH, 3, D_CONV)
    pos += 4 * conv_w.size
    loss = small_sum[pos]
    gsmall["conv_w"] = lax.dynamic_slice_in_dim(conv_full, chip * conv_w.shape[2], conv_w.shape[2], axis=2)

    out_g, out_d, out_m, out_v = {}, {}, {}, {}
    for n, size, start in zip(BIG, sizes, starts):
        shp = wts[n].shape
        res = _adamw(wts[n].reshape(-1, D_MODEL), gsum, mom[n].reshape(-1, D_MODEL), var[n].reshape(-1, D_MODEL),
                     name="adamw_" + n, tr=256, row0=start)
        out_g[n], out_d[n], out_m[n], out_v[n] = [r.reshape(shp) for r in res]
    small_names = SMALL + ("conv_w",)
    packs = [_pad_rows(jnp.concatenate([d[n].reshape(-1) for n in small_names]), SMALL_ROWS)
             for d in (wts, gsmall, mom, var)]
    res = _adamw(*packs, name="adamw_small", tr=SMALL_ROWS)
    pos = 0
    for n in small_names:
        shp = wts[n].shape
        out_g[n], out_d[n], out_m[n], out_v[n] = [r.reshape(-1)[pos:pos + wts[n].size].reshape(shp) for r in res]
        pos += wts[n].size

    order = ("w_in", "conv_w", "pool_w", "pool_scale", "mix_norm_g", "w_o", "ln1_g", "ln1_b", "w_up", "w_down",
             "ln2_g", "ln2_b")
    return (loss, grad_x[None], *[out_g[n] for n in order], *[out_d[n] for n in order],
            *[out_m[n] for n in order], *[out_v[n] for n in order])
```

```python
import functools
import math

import jax
import jax.numpy as jnp
from jax import lax
from jax.experimental import pallas as pl
from jax.experimental.pallas import tpu as pltpu

F32 = jnp.float32
BF16 = jnp.bfloat16
MESH = pl.DeviceIdType.MESH

D_MODEL = 1024
DEPTH = 2
HEAD_DIM = 64
D_SB = 512
D_CONV = 256
D_POOL = 256
D_QKV = 3 * D_SB
D_REST = 3 * D_CONV + D_POOL
D_IN = D_QKV + D_REST
D_FF = 4 * D_MODEL
ALPHA = (2 * DEPTH) ** 0.25
LN_EPS = 1e-5
RMS_EPS = 1e-6
SCALE = HEAD_DIM ** -0.5
N_CHIPS = 4
HALO = 16

ADAM_LR = 0.001
ADAM_B1 = 0.9
ADAM_B2 = 0.999
ADAM_EPS = 1e-08
ADAM_WD = 0.01
ADAM_STEP = 10

VMEM_V7X_BYTES = 64 * 1024 * 1024
VMEM_CAP_BYTES = VMEM_V7X_BYTES - 8 * 1024 * 1024


def _params(sem, block_bytes):
    limit = min(VMEM_CAP_BYTES, max(32 * 1024 * 1024, 3 * block_bytes))
    return pltpu.CompilerParams(dimension_semantics=sem, vmem_limit_bytes=limit)


def _nbytes(shape, dtype):
    return math.prod(shape) * jnp.dtype(dtype).itemsize


def _dot(a, b, dims=(((1,), (0,)), ((), ()))):
    return lax.dot_general(a, b, dims, preferred_element_type=F32)


NT = (((1,), (1,)), ((), ()))
TN = (((0,), (0,)), ((), ()))


def _split(x):
    hi = x.astype(BF16)
    lo = (x - hi.astype(F32)).astype(BF16)
    return hi, lo


def _sum8(x):
    r, c = x.shape
    return x.reshape(r // 8, 8, c).sum(axis=0)


def _matmul(a, b, *, name, tm, tn, tk, ta=False, tb=False, out_dtype=F32,
            epi=None, e=None, e_scale=1.0, relu2_out=False):
    M, K = (a.shape[1], a.shape[0]) if ta else a.shape
    N = b.shape[0] if tb else b.shape[1]
    tm, tn, tk = min(tm, M), min(tn, N), min(tk, K)
    assert M % tm == 0 and N % tn == 0 and K % tk == 0, (name, M, N, K)
    nk = K // tk
    dims = (((0 if ta else 1,), (1 if tb else 0,)), ((), ()))
    n_in = 2 if epi is None else 3
    n_out = 2 if relu2_out else 1

    def body(*refs):
        a_ref, b_ref = refs[0], refs[1]
        e_ref = refs[2] if epi is not None else None
        o_ref = refs[n_in]
        scr = refs[n_in + n_out:]
        p = _dot(a_ref[...].astype(BF16), b_ref[...].astype(BF16), dims)

        def finish(acc):
            if epi == "drelu2":
                acc = acc * (2.0 * jnp.maximum(e_ref[...], 0.0))
            elif epi == "add":
                acc = acc + e_scale * e_ref[...]
            o_ref[...] = acc.astype(out_dtype)
            if relu2_out:
                refs[n_in + 1][...] = jnp.square(jnp.maximum(acc, 0.0)).astype(BF16)

        if nk == 1:
            finish(p)
        else:
            acc_ref = scr[0]
            k = pl.program_id(2)

            @pl.when(k == 0)
            def _():
                acc_ref[...] = p

            @pl.when(k > 0)
            def _():
                acc_ref[...] += p

            @pl.when(k == nk - 1)
            def _():
                finish(acc_ref[...])

    a_spec = pl.BlockSpec((tk, tm), lambda i, j, k: (k, i)) if ta else pl.BlockSpec((tm, tk), lambda i, j, k: (i, k))
    b_spec = pl.BlockSpec((tn, tk), lambda i, j, k: (j, k)) if tb else pl.BlockSpec((tk, tn), lambda i, j, k: (k, j))
    o_spec = pl.BlockSpec((tm, tn), lambda i, j, k: (i, j))
    in_specs = [a_spec, b_spec]
    args = [a, b]
    nbytes = _nbytes((tm, tk), a.dtype) + _nbytes((tk, tn), b.dtype) + 2 * _nbytes((tm, tn), F32)
    if epi is not None:
        in_specs.append(o_spec)
        args.append(e)
        nbytes += _nbytes((tm, tn), e.dtype)
    scratch = [pltpu.VMEM((tm, tn), F32)] if nk > 1 else []
    out_shape = [jax.ShapeDtypeStruct((M, N), out_dtype)]
    if relu2_out:
        out_shape.append(jax.ShapeDtypeStruct((M, N), BF16))
        nbytes += _nbytes((tm, tn), BF16)
    res = pl.pallas_call(
        body, name=name,
        grid=(M // tm, N // tn, nk),
        in_specs=in_specs, out_specs=[o_spec] * n_out,
        out_shape=out_shape,
        scratch_shapes=scratch,
        compiler_params=_params(("parallel", "parallel", "arbitrary"), nbytes),
    )(*args)
    return res if relu2_out else res[0]


def _matmul_ln(a, b, xres, g, bias, *, name, tm, tk):
    M, K = a.shape
    N = b.shape[1]
    tm, tk = min(tm, M), min(tk, K)
    assert M % tm == 0 and K % tk == 0 and N == D_MODEL
    nk = K // tk

    def body(a_ref, b_ref, x_ref, g_ref, bias_ref, y_ref, y16_ref, xh_ref, rs_ref, *scr):
        p = _dot(a_ref[...].astype(BF16), b_ref[...].astype(BF16))

        def finish(acc):
            r = ALPHA * x_ref[...] + acc
            mu = jnp.mean(r, axis=-1, keepdims=True)
            xc = r - mu
            var = jnp.mean(xc * xc, axis=-1, keepdims=True)
            rstd = lax.rsqrt(var + LN_EPS)
            xh = xc * rstd
            y = xh * g_ref[...] + bias_ref[...]
            y_ref[...] = y
            y16_ref[...] = y.astype(BF16)
            xh_ref[...] = xh
            rs_ref[...] = rstd

        if nk == 1:
            finish(p)
        else:
            acc_ref = scr[0]
            k = pl.program_id(1)

            @pl.when(k == 0)
            def _():
                acc_ref[...] = p

            @pl.when(k > 0)
            def _():
                acc_ref[...] += p

            @pl.when(k == nk - 1)
            def _():
                finish(acc_ref[...])

    row = pl.BlockSpec((tm, N), lambda i, k: (i, 0))
    vec = pl.BlockSpec((1, N), lambda i, k: (0, 0))
    nbytes = _nbytes((tm, tk), a.dtype) + _nbytes((tk, N), b.dtype) + 6 * _nbytes((tm, N), F32)
    scratch = [pltpu.VMEM((tm, N), F32)] if nk > 1 else []
    return pl.pallas_call(
        body, name=name,
        grid=(M // tm, nk),
        in_specs=[pl.BlockSpec((tm, tk), lambda i, k: (i, k)), pl.BlockSpec((tk, N), lambda i, k: (k, 0)), row, vec, vec],
        out_specs=[row, row, row, pl.BlockSpec((tm, 1), lambda i, k: (i, 0))],
        out_shape=[jax.ShapeDtypeStruct((M, N), F32), jax.ShapeDtypeStruct((M, N), BF16),
                   jax.ShapeDtypeStruct((M, N), F32), jax.ShapeDtypeStruct((M, 1), F32)],
        scratch_shapes=scratch,
        compiler_params=_params(("parallel", "arbitrary"), nbytes),
    )(a, b, xres, g.reshape(1, N), bias.reshape(1, N))


def _ln_bwd(dy, xhat, rstd, g, *, name, tm):
    M, N = dy.shape
    tm = min(tm, M)

    def body(dy_ref, xh_ref, rs_ref, g_ref, dr_ref, dr16_ref, dg_ref, db_ref):
        i = pl.program_id(0)
        dyv = dy_ref[...]
        xh = xh_ref[...]
        dxh = dyv * g_ref[...]
        m1 = jnp.mean(dxh, axis=-1, keepdims=True)
        m2 = jnp.mean(dxh * xh, axis=-1, keepdims=True)
        dr = rs_ref[...] * (dxh - m1 - xh * m2)
        dr_ref[...] = dr
        dr16_ref[...] = dr.astype(BF16)
        pg = _sum8(dyv * xh)
        pb = _sum8(dyv)

        @pl.when(i == 0)
        def _():
            dg_ref[...] = pg
            db_ref[...] = pb

        @pl.when(i > 0)
        def _():
            dg_ref[...] += pg
            db_ref[...] += pb

    row = pl.BlockSpec((tm, N), lambda i: (i, 0))
    acc = pl.BlockSpec((8, N), lambda i: (0, 0))
    return pl.pallas_call(
        body, name=name, grid=(M // tm,),
        in_specs=[row, row, pl.BlockSpec((tm, 1), lambda i: (i, 0)), pl.BlockSpec((1, N), lambda i: (0, 0))],
        out_specs=[row, row, acc, acc],
        out_shape=[jax.ShapeDtypeStruct((M, N), F32), jax.ShapeDtypeStruct((M, N), BF16),
                   jax.ShapeDtypeStruct((8, N), F32), jax.ShapeDtypeStruct((8, N), F32)],
        compiler_params=_params(("arbitrary",), 5 * _nbytes((tm, N), F32)),
    )(dy, xhat, rstd, g.reshape(1, N))


def _loss_grad(y, tgt, *, name, tm):
    M, N = y.shape
    tm = min(tm, M)

    def body(y_ref, t_ref, dy_ref, l_ref):
        i = pl.program_id(0)
        d = y_ref[...] - t_ref[...]
        dy_ref[...] = d * (1.0 / N)
        pl_ = _sum8(d * d) * (0.5 / N)

        @pl.when(i == 0)
        def _():
            l_ref[...] = pl_

        @pl.when(i > 0)
        def _():
            l_ref[...] += pl_

    row = pl.BlockSpec((tm, N), lambda i: (i, 0))
    return pl.pallas_call(
        body, name=name, grid=(M // tm,),
        in_specs=[row, row], out_specs=[row, pl.BlockSpec((8, N), lambda i: (0, 0))],
        out_shape=[jax.ShapeDtypeStruct((M, N), F32), jax.ShapeDtypeStruct((8, N), F32)],
        compiler_params=_params(("arbitrary",), 3 * _nbytes((tm, N), F32)),
    )(y, tgt)


def _tri(n, kind):
    j = lax.broadcasted_iota(jnp.int32, (2 * n, n), 0) % n
    s = lax.broadcasted_iota(jnp.int32, (2 * n, n), 1)
    return ((j > s) if kind == "after" else (j < s)).astype(BF16)


LOG2E = 1.4426950408889634
DEAD = -104.0
NOT_VISITED = -1e30


def _log_terms(z):
    lse = jnp.log(1.0 + jnp.exp2(jnp.abs(z) * (-LOG2E)))
    logsig = jnp.minimum(z, 0.0) - lse
    return logsig, logsig - z


def _cumsum_mm(x, u2_ref):
    hi, lo = _split(x)
    return _dot(jnp.concatenate([hi, lo], axis=1), u2_ref[...])


def _head_rows(x2, scale):
    lane = lax.broadcasted_iota(jnp.int32, (1, 128), 1)
    zero = jnp.zeros_like(x2)
    both = jnp.concatenate([jnp.where(lane < HEAD_DIM, x2, zero), jnp.where(lane >= HEAD_DIM, x2, zero)], axis=0)
    return both * scale


def _causal_mask(i, ks, tq, tk):
    row = lax.broadcasted_iota(jnp.int32, (2 * tq, tk), 0)
    row = i * tq + jnp.where(row >= tq, row - tq, row)
    col = lax.broadcasted_iota(jnp.int32, (2 * tq, tk), 1)
    return (ks + col) < row


def _attn_fwd(qkv, *, name, tq, tk):
    S = qkv.shape[0]
    tq = tk = min(tq, tk, S)
    assert S % tq == 0 and S // tk <= 128
    tri = _tri(tk, "after")

    def body(q_ref, k_ref, v_ref, u_ref, o_ref, c_ref, qcat, oacc, cacc, call, ls_buf, tl_buf, l0_buf):
        i = pl.program_id(1)
        lane = lax.broadcasted_iota(jnp.int32, (1, 128), 1)
        qcat[...] = _head_rows(q_ref[...], SCALE)
        oacc[...] = jnp.zeros_like(oacc)
        cacc[...] = jnp.zeros_like(cacc)
        call[...] = jnp.full_like(call, NOT_VISITED)

        def scores(kb, masked, slot):
            ks = pl.multiple_of(jnp.maximum(kb, 0) * tk, tk)
            z = _dot(qcat[...], k_ref[pl.ds(ks, tk), :], NT)
            logsig, lom = _log_terms(z)
            if masked:
                msk = jnp.logical_and(_causal_mask(i, ks, tq, tk), kb >= 0)
                lom = jnp.where(msk, lom, 0.0)
                logsig = jnp.where(msk, logsig, -1e30)
            ls_buf[slot] = logsig
            tl_buf[slot] = _cumsum_mm(lom, u_ref)
            l0_buf[slot] = lom[:, 0:1]

        def weights(kb, slot):
            ks = pl.multiple_of(jnp.maximum(kb, 0) * tk, tk)
            tl = tl_buf[slot]
            c = cacc[...]
            call[...] = jnp.where(lane == kb, c, call[...])
            a = jnp.exp(ls_buf[slot] + tl + c).astype(BF16)
            oacc[...] += _dot(a, v_ref[pl.ds(ks, tk), :])
            cacc[...] = c + tl[:, 0:1] + l0_buf[slot]

        def pair(kb, masked):
            scores(kb, masked, 0)
            scores(kb - 1, masked, 1)
            weights(kb, 0)
            weights(kb - 1, 1)

        pair(i, True)

        def live(state):
            t, cmax = state
            return jnp.logical_and(t < (i - 1) // 2, cmax > DEAD)

        def trip(state):
            t, _ = state
            pair(i - 2 - 2 * t, False)
            return t + 1, jnp.max(cacc[...])

        t_end, cmax = lax.while_loop(live, trip, (0, jnp.max(cacc[...])))

        @pl.when(jnp.logical_and(jnp.logical_and(i >= 2, i % 2 == 0), jnp.logical_and(t_end == (i - 1) // 2, cmax > DEAD)))
        def _():
            pair(0, True)

        o_ref[...] = jnp.where(lane < HEAD_DIM, oacc[0:tq], oacc[tq:2 * tq])
        c_ref[...] = jnp.concatenate([call[0:tq], call[tq:2 * tq]], axis=1)

    nbytes = (_nbytes((tq, 128), BF16) + 2 * _nbytes((S, 128), BF16) + _nbytes((2 * tk, tk), BF16)
              + 8 * _nbytes((tq, 128), F32) + 14 * _nbytes((2 * tq, tk), F32))
    return pl.pallas_call(
        body, name=name, grid=(4, S // tq),
        in_specs=[pl.BlockSpec((tq, 128), lambda j, i: (i, j)),
                  pl.BlockSpec((S, 128), lambda j, i: (0, 4 + j)),
                  pl.BlockSpec((S, 128), lambda j, i: (0, 8 + j)),
                  pl.BlockSpec((2 * tk, tk), lambda j, i: (0, 0))],
        out_specs=[pl.BlockSpec((tq, 128), lambda j, i: (i, j)),
                   pl.BlockSpec((tq, 256), lambda j, i: (i, j))],
        out_shape=[jax.ShapeDtypeStruct((S, D_SB), F32), jax.ShapeDtypeStruct((S, 1024), F32)],
        scratch_shapes=[pltpu.VMEM((2 * tq, 128), BF16), pltpu.VMEM((2 * tq, 128), F32),
                        pltpu.VMEM((2 * tq, 1), F32), pltpu.VMEM((2 * tq, 128), F32),
                        pltpu.VMEM((2, 2 * tq, tk), F32), pltpu.VMEM((2, 2 * tq, tk), F32),
                        pltpu.VMEM((2, 2 * tq, 1), F32)],
        compiler_params=_params(("parallel", "arbitrary"), nbytes),
    )(qkv, qkv, qkv, tri)


def _attn_bwd(qkv, carry, do, *, name, tq, tk):
    S = qkv.shape[0]
    tq = tk = min(tq, tk, S)
    assert S % tq == 0 and S // tk <= 128
    nkb = S // tk
    nq = S // tq
    tri_after = _tri(tk, "after")
    tri_before = _tri(tk, "before")

    def body(q_ref, k_ref, v_ref, c_ref, do_ref, ua_ref, ub_ref, dq_ref, dk_ref, dv_ref,
             qcat, docat, qcat_t, docat_t, ccat, dqacc, pacc, dkt, dvt, ls_buf, tl_buf, da_buf):
        i = pl.program_id(1)
        lane = lax.broadcasted_iota(jnp.int32, (1, 128), 1)
        sub = lax.broadcasted_iota(jnp.int32, (128, 1), 0)
        q2 = q_ref[...]
        do2 = do_ref[...]
        qcat[...] = _head_rows(q2, SCALE)
        docat[...] = _head_rows(do2, 1.0).astype(BF16)
        qt = q2.astype(F32).T * SCALE
        dot_ = do2.T
        qcat_t[...] = jnp.concatenate([jnp.where(sub < HEAD_DIM, qt, 0.0), jnp.where(sub >= HEAD_DIM, qt, 0.0)],
                                      axis=1).astype(BF16)
        docat_t[...] = jnp.concatenate([jnp.where(sub < HEAD_DIM, dot_, 0.0), jnp.where(sub >= HEAD_DIM, dot_, 0.0)],
                                       axis=1).astype(BF16)
        ccat[0:tq] = c_ref[:, 0:128]
        ccat[tq:2 * tq] = c_ref[:, 128:256]

        @pl.when(i == 0)
        def _():
            dkt[...] = jnp.zeros_like(dkt)
            dvt[...] = jnp.zeros_like(dvt)

        dqacc[...] = jnp.zeros_like(dqacc)
        pacc[...] = jnp.zeros_like(pacc)

        def scores(kb, masked, slot):
            ks = pl.multiple_of(jnp.maximum(kb, 0) * tk, tk)
            z = _dot(qcat[...], k_ref[pl.ds(ks, tk), :], NT)
            logsig, lom = _log_terms(z)
            if masked:
                msk = jnp.logical_and(_causal_mask(i, ks, tq, tk), kb >= 0)
                lom = jnp.where(msk, lom, 0.0)
                logsig = jnp.where(msk, logsig, -1e30)
            ls_buf[slot] = logsig
            tl_buf[slot] = _cumsum_mm(lom, ua_ref)
            da_buf[slot] = _dot(docat[...], v_ref[pl.ds(ks, tk), :], NT)

        def grads(kb, slot):
            kbc = jnp.maximum(kb, 0)
            ks = pl.multiple_of(kbc * tk, tk)
            logsig = ls_buf[slot]
            c = jnp.sum(jnp.where(lane == kb, ccat[...], 0.0), axis=1, keepdims=True)
            a = jnp.exp(logsig + tl_buf[slot] + c)
            g = a * da_buf[slot]
            before = _cumsum_mm(g, ub_ref)
            pc = pacc[...]
            dz = g - jnp.exp(logsig) * (g + before + pc)
            dzb = dz.astype(BF16)
            dqacc[...] += _dot(dzb, k_ref[pl.ds(ks, tk), :])
            dkt[kbc] += _dot(qcat_t[...], dzb)
            dvt[kbc] += _dot(docat_t[...], a.astype(BF16))
            pacc[...] = pc + before[:, tk - 1:tk] + g[:, tk - 1:tk]

        def pair(kb, masked):
            scores(kb, masked, 0)
            scores(kb + 1, masked, 1)
            grads(kb, 0)
            grads(kb + 1, 1)

        reach = jnp.max(ccat[...], axis=0, keepdims=True)
        first = jnp.min(jnp.where(reach > DEAD, lane, 128).astype(F32)).astype(jnp.int32)
        first = jnp.minimum(first, i)
        start = first - (i - first + 1) % 2

        @pl.when(jnp.logical_and(start < 0, i >= 2))
        def _():
            pair(-1, True)

        k0 = jnp.where(start < 0, 1, start)

        def loop(t, carry_):
            pair(k0 + 2 * t, False)
            return carry_

        lax.fori_loop(0, jnp.maximum((i - 1 - k0) // 2, 0), loop, 0)
        pair(i - 1, True)
        dq_ref[...] = (jnp.where(lane < HEAD_DIM, dqacc[0:tq], dqacc[tq:2 * tq]) * SCALE).astype(BF16)

        @pl.when(i == nq - 1)
        def _():
            for kb in range(nkb):
                dk_ref[kb * tk:(kb + 1) * tk, :] = dkt[kb].T.astype(BF16)
                dv_ref[kb * tk:(kb + 1) * tk, :] = dvt[kb].T.astype(BF16)

    nbytes = (_nbytes((tq, 128), BF16) + 2 * _nbytes((S, 128), BF16) + 2 * _nbytes((2 * tk, tk), BF16)
              + 12 * _nbytes((tq, 128), F32) + 4 * _nbytes((S, 128), F32) + 14 * _nbytes((2 * tq, tk), F32))
    blk = pl.BlockSpec((tq, 128), lambda j, i: (i, j))
    full = pl.BlockSpec((S, 128), lambda j, i: (0, j))
    tri_spec = pl.BlockSpec((2 * tk, tk), lambda j, i: (0, 0))
    dq, dk, dv = pl.pallas_call(
        body, name=name, grid=(4, nq),
        in_specs=[blk,
                  pl.BlockSpec((S, 128), lambda j, i: (0, 4 + j)),
                  pl.BlockSpec((S, 128), lambda j, i: (0, 8 + j)),
                  pl.BlockSpec((tq, 256), lambda j, i: (i, j)),
                  blk, tri_spec, tri_spec],
        out_specs=[blk, full, full],
        out_shape=[jax.ShapeDtypeStruct((S, D_SB), BF16)] * 3,
        scratch_shapes=[pltpu.VMEM((2 * tq, 128), BF16), pltpu.VMEM((2 * tq, 128), BF16),
                        pltpu.VMEM((128, 2 * tq), BF16), pltpu.VMEM((128, 2 * tq), BF16),
                        pltpu.VMEM((2 * tq, 128), F32), pltpu.VMEM((2 * tq, 128), F32), pltpu.VMEM((2 * tq, 1), F32),
                        pltpu.VMEM((nkb, 128, tk), F32), pltpu.VMEM((nkb, 128, tk), F32),
                        pltpu.VMEM((2, 2 * tq, tk), F32), pltpu.VMEM((2, 2 * tq, tk), F32),
                        pltpu.VMEM((2, 2 * tq, tk), F32)],
        compiler_params=_params(("parallel", "arbitrary"), nbytes),
    )(qkv, qkv, qkv, carry, do, tri_after, tri_before)
    return dq, dk, dv


def _group_mats():
    lanes = jnp.arange(D_MODEL) // HEAD_DIM
    gs = (lanes[:, None] == jnp.arange(128)[None, :]).astype(BF16)
    return gs, gs.T


def _group_sum_bcast(x, gs, gb):
    hi, lo = _split(x)
    s = _dot(hi, gs) + _dot(lo, gs)
    return _bcast(s, gb)


def _bcast(s, gb):
    hi, lo = _split(s)
    return _dot(hi, gb) + _dot(lo, gb)


def _pool_lane_consts():
    lane = lax.broadcasted_iota(jnp.int32, (1, D_POOL), 1)
    grp = lane // (D_POOL // 4)
    win = jnp.where(grp == 0, 2, jnp.where(grp == 1, 4, jnp.where(grp == 2, 8, 16)))
    return grp, win


def _by_group(grp, s2, s4, s8, s16):
    return jnp.where(grp == 0, s2, jnp.where(grp == 1, s4, jnp.where(grp == 2, s8, s16)))


def _mixers(i, ts, prev_ref, cur_ref, cw_ref, pw_ref, ps_ref):
    cur = cur_ref[...]
    prev = jnp.where(i == 0, 0.0, prev_ref[...])
    ext = jnp.concatenate([prev, cur], axis=0)
    n = HALO + ts

    def back(a, k):
        return pltpu.roll(a, k, 0)

    u = ext[:, D_CONV:2 * D_CONV] * ext[:, 2 * D_CONV:3 * D_CONV]
    p = ext[:, 3 * D_CONV:]
    cv = (cw_ref[0:1, :] * back(u, 2) + cw_ref[1:2, :] * back(u, 1) + cw_ref[2:3, :] * u)[HALO:]
    s2 = p + back(p, 1)
    s4 = s2 + back(s2, 2)
    s8 = s4 + back(s4, 4)
    s16 = s8 + back(s8, 8)
    grp, win = _pool_lane_consts()
    t1 = i * ts + 1 + lax.broadcasted_iota(jnp.int32, (ts, 1), 0)
    cnt = jnp.minimum(t1, win).astype(F32)
    pooled = _by_group(grp, s2, s4, s8, s16)[HALO:] / cnt - p[HALO:]
    yp = _dot(pooled.astype(BF16), pw_ref[...])
    del n
    return dict(b=cur[:, 0:D_CONV], u=u, cv=cv, pooled=pooled, yp=yp, cnt=cnt,
                conv_out=cur[:, 0:D_CONV] * cv, pool_out=yp * ps_ref[...])


def _halo_specs(ts, S, width):
    nb = ts // HALO
    last = S // HALO - 1
    prev = pl.BlockSpec((HALO, width), lambda i: (jnp.maximum(i * nb - 1, 0), 0))
    nxt = pl.BlockSpec((HALO, width), lambda i: (jnp.minimum((i + 1) * nb, last), 0))
    return prev, nxt


def _mixer_fwd(rest, attn, cw8, pwbd, ps, gain, *, name, ts):
    S = rest.shape[0]
    ts = min(ts, S)
    gs, gb = _group_mats()

    def body(prev_ref, cur_ref, attn_ref, cw_ref, pw_ref, ps_ref, gain_ref, gs_ref, gb_ref, o_ref):
        i = pl.program_id(0)
        f = _mixers(i, ts, prev_ref, cur_ref, cw_ref, pw_ref, ps_ref)
        mix = jnp.concatenate([attn_ref[...], f["conv_out"], f["pool_out"]], axis=1)
        ss = _group_sum_bcast(mix * mix, gs_ref[...], gb_ref[...])
        rinv = lax.rsqrt(ss * (1.0 / HEAD_DIM) + RMS_EPS)
        o_ref[...] = (mix * rinv * gain_ref[...]).astype(BF16)

    prev, _ = _halo_specs(ts, S, D_REST)
    row = lambda w: pl.BlockSpec((ts, w), lambda i: (i, 0))
    const = lambda a: pl.BlockSpec(a.shape, lambda i: (0, 0))
    nbytes = 12 * _nbytes((ts + HALO, D_REST), F32)
    return pl.pallas_call(
        body, name=name, grid=(S // ts,),
        in_specs=[prev, row(D_REST), row(D_SB), const(cw8), const(pwbd), const(ps), const(gain), const(gs), const(gb)],
        out_specs=row(D_MODEL),
        out_shape=jax.ShapeDtypeStruct((S, D_MODEL), BF16),
        compiler_params=_params(("parallel",), nbytes),
    )(rest, rest, attn, cw8, pwbd, ps, gain, gs, gb)


def _mixer_bwd1(dmixn, rest, attn, cw8, pwbd, ps, gain, *, name, ts):
    S = rest.shape[0]
    ts = min(ts, S)
    gs, gb = _group_mats()

    def body(dm_ref, prev_ref, cur_ref, attn_ref, cw_ref, pw_ref, ps_ref, gain_ref, gs_ref, gb_ref,
             da_ref, aux_ref, dg_ref, dsc_ref, dcw_ref, dpw_ref):
        i = pl.program_id(0)
        f = _mixers(i, ts, prev_ref, cur_ref, cw_ref, pw_ref, ps_ref)
        mix = jnp.concatenate([attn_ref[...], f["conv_out"], f["pool_out"]], axis=1)
        gsm, gbm = gs_ref[...], gb_ref[...]
        ss = _group_sum_bcast(mix * mix, gsm, gbm)
        rinv = lax.rsqrt(ss * (1.0 / HEAD_DIM) + RMS_EPS)
        dm = dm_ref[...]
        xn = mix * rinv
        dyg = dm * gain_ref[...]
        gm = _group_sum_bcast(dyg * xn, gsm, gbm) * (1.0 / HEAD_DIM)
        dmix = rinv * (dyg - xn * gm)
        da_ref[...] = dmix[:, 0:D_SB]
        dco = dmix[:, D_SB:D_SB + D_CONV]
        dpo = dmix[:, D_SB + D_CONV:]
        dcv = dco * f["b"]
        dyp = dpo * ps_ref[...]
        dpooled = _dot(dyp.astype(BF16), pw_ref[...], NT)
        aux_ref[...] = jnp.concatenate([dco * f["cv"], dcv, dpooled / f["cnt"], dpooled], axis=1)
        u = f["u"]
        parts = [
            _sum8(dm * xn),
            _sum8(dpo * f["yp"]),
            jnp.concatenate([_sum8(dcv * pltpu.roll(u, 2, 0)[HALO:]), _sum8(dcv * pltpu.roll(u, 1, 0)[HALO:]),
                             _sum8(dcv * u[HALO:])], axis=0),
            _dot(f["pooled"].astype(BF16), dyp.astype(BF16), TN),
        ]
        outs = [dg_ref, dsc_ref, dcw_ref, dpw_ref]

        @pl.when(i == 0)
        def _():
            for o, v in zip(outs, parts):
                o[...] = v

        @pl.when(i > 0)
        def _():
            for o, v in zip(outs, parts):
                o[...] += v

    prev, _ = _halo_specs(ts, S, D_REST)
    row = lambda w: pl.BlockSpec((ts, w), lambda i: (i, 0))
    const = lambda a: pl.BlockSpec(a.shape, lambda i: (0, 0))
    acc = lambda r_, w: pl.BlockSpec((r_, w), lambda i: (0, 0))
    nbytes = 16 * _nbytes((ts + HALO, D_REST), F32)
    return pl.pallas_call(
        body, name=name, grid=(S // ts,),
        in_specs=[row(D_MODEL), prev, row(D_REST), row(D_SB), const(cw8), const(pwbd), const(ps), const(gain),
                  const(gs), const(gb)],
        out_specs=[row(D_SB), row(D_REST), acc(8, D_MODEL), acc(8, D_POOL), acc(24, D_CONV), acc(D_POOL, D_POOL)],
        out_shape=[jax.ShapeDtypeStruct((S, D_SB), F32), jax.ShapeDtypeStruct((S, D_REST), F32),
                   jax.ShapeDtypeStruct((8, D_MODEL), F32), jax.ShapeDtypeStruct((8, D_POOL), F32),
                   jax.ShapeDtypeStruct((24, D_CONV), F32), jax.ShapeDtypeStruct((D_POOL, D_POOL), F32)],
        compiler_params=_params(("arbitrary",), nbytes),
    )(dmixn, rest, rest, attn, cw8, pwbd, ps, gain, gs, gb)


def _mixer_bwd2(aux, rest, cw8, *, name, ts):
    S = rest.shape[0]
    ts = min(ts, S)
    nblk = S // ts

    def body(cur_ref, nxt_ref, rest_ref, cw_ref, o_ref):
        i = pl.program_id(0)
        cur = cur_ref[...]
        nxt = jnp.where(i == nblk - 1, 0.0, nxt_ref[...])
        ext = jnp.concatenate([cur, nxt], axis=0)
        n = ts + HALO

        def fwd(a, k):
            return pltpu.roll(a, n - k, 0)

        dcv = ext[:, D_CONV:2 * D_CONV]
        dps = ext[:, 2 * D_CONV:3 * D_CONV]
        du = (cw_ref[2:3, :] * dcv + cw_ref[1:2, :] * fwd(dcv, 1) + cw_ref[0:1, :] * fwd(dcv, 2))[0:ts]
        f2 = dps + fwd(dps, 1)
        f4 = f2 + fwd(f2, 2)
        f8 = f4 + fwd(f4, 4)
        f16 = f8 + fwd(f8, 8)
        grp, _ = _pool_lane_consts()
        dp = _by_group(grp, f2, f4, f8, f16)[0:ts] - cur[:, 3 * D_CONV:]
        rest_v = rest_ref[...]
        c_gate = rest_v[:, D_CONV:2 * D_CONV]
        h = rest_v[:, 2 * D_CONV:3 * D_CONV]
        o_ref[...] = jnp.concatenate([cur[:, 0:D_CONV], du * h, du * c_gate, dp], axis=1).astype(BF16)

    _, nxt = _halo_specs(ts, S, D_REST)
    row = pl.BlockSpec((ts, D_REST), lambda i: (i, 0))
    return pl.pallas_call(
        body, name=name, grid=(nblk,),
        in_specs=[row, nxt, row, pl.BlockSpec(cw8.shape, lambda i: (0, 0))],
        out_specs=row,
        out_shape=jax.ShapeDtypeStruct((S, D_REST), BF16),
        compiler_params=_params(("parallel",), 10 * _nbytes((ts + HALO, D_REST), F32)),
    )(aux, aux, rest, cw8)


def _block_diag(pw):
    z = jnp.zeros((4, 64, 4, 64), pw.dtype)
    for g in range(4):
        z = z.at[g, :, g, :].set(pw[g])
    return z.reshape(256, 256)


def _rows8(v, rows=8):
    return jnp.pad(v, ((0, rows - v.shape[0]), (0, 0)))


TILES = dict(tm=512, ts=512, tq=256, tk=256)


def _local_step(x, tgt, w, t=None):
    t = dict(TILES, **(t or {}))
    tm, ts, tq, tk = t["tm"], t["ts"], t["tq"], t["tk"]
    big = dict(tm=1024, tn=1024)
    saved = []
    xl, xl16 = x, x.astype(BF16)
    for l in range(DEPTH):
        n = f"l{l}_"
        wq, wr = w["w_in"][l][:, :D_QKV], w["w_in"][l][:, D_QKV:]
        qkv = _matmul(xl16, wq, name=n + "proj_qkv", tm=1024, tn=D_QKV, tk=1024, out_dtype=BF16)
        rest = _matmul(xl16, wr, name=n + "proj_rest", tk=1024, **big)
        attn, carry = _attn_fwd(qkv, name=n + "attn_fwd", tq=tq, tk=tk)
        cw8 = _rows8(w["conv_w"][l])
        pwbd = _block_diag(w["pool_w"][l]).astype(BF16)
        ps = w["pool_scale"][l].reshape(1, D_POOL)
        gain = w["mix_norm_g"][l].reshape(1, D_MODEL)
        mixn = _mixer_fwd(rest, attn, cw8, pwbd, ps, gain, name=n + "mixer_fwd", ts=ts)
        x1, x1_16, xh1, rs1 = _matmul_ln(mixn, w["w_o"][l], xl, w["ln1_g"][l], w["ln1_b"][l], name=n + "wo_ln",
                                         tm=tm, tk=1024)
        hpre, hid = _matmul(x1_16, w["w_up"][l], name=n + "ffn_up", tk=1024, relu2_out=True, **big)
        x2, x2_16, xh2, rs2 = _matmul_ln(hid, w["w_down"][l], x1, w["ln2_g"][l], w["ln2_b"][l],
                                         name=n + "ffn_down_ln", tm=tm // 2, tk=D_FF)
        saved.append(dict(xin16=xl16, wq=wq, wr=wr, qkv=qkv, rest=rest, attn=attn, carry=carry, cw8=cw8, pwbd=pwbd,
                          ps=ps, gain=gain, mixn=mixn, x1_16=x1_16, xh1=xh1, rs1=rs1, hpre=hpre, hid=hid, xh2=xh2,
                          rs2=rs2))
        xl, xl16 = x2, x2_16

    dy, lsum = _loss_grad(xl, tgt, name="loss_grad", tm=tm)
    grads = {k: [None] * DEPTH for k in
             ("w_in", "conv_w", "pool_w", "pool_scale", "mix_norm_g", "w_o", "ln1_g", "ln1_b", "w_up", "w_down",
              "ln2_g", "ln2_b")}
    dw = dict(tk=2048, ta=True, **big)
    for l in reversed(range(DEPTH)):
        n = f"l{l}_"
        s = saved[l]
        dr2, dr2_16, dg2, db2 = _ln_bwd(dy, s["xh2"], s["rs2"], w["ln2_g"][l], name=n + "ln2_bwd", tm=tm)
        dhpre = _matmul(dr2_16, w["w_down"][l], name=n + "ffn_down_dx", tk=1024, tb=True, out_dtype=BF16,
                        epi="drelu2", e=s["hpre"], **big)
        grads["w_down"][l] = _matmul(s["hid"], dr2_16, name=n + "ffn_down_dw", **dw)
        dx1 = _matmul(dhpre, w["w_up"][l], name=n + "ffn_up_dx", tm=512, tn=1024, tk=D_FF, tb=True,
                      epi="add", e=dr2, e_scale=ALPHA)
        grads["w_up"][l] = _matmul(s["x1_16"], dhpre, name=n + "ffn_up_dw", **dw)
        dr1, dr1_16, dg1, db1 = _ln_bwd(dx1, s["xh1"], s["rs1"], w["ln1_g"][l], name=n + "ln1_bwd", tm=tm)
        dmixn = _matmul(dr1_16, w["w_o"][l], name=n + "wo_dx", tk=1024, tb=True, **big)
        grads["w_o"][l] = _matmul(s["mixn"], dr1_16, name=n + "wo_dw", **dw)
        d_attn, aux, dgain, dsc, dcw, dpw = _mixer_bwd1(dmixn, s["rest"], s["attn"], s["cw8"], s["pwbd"], s["ps"],
                                                        s["gain"], name=n + "mixer_bwd1", ts=ts)
        drest = _mixer_bwd2(aux, s["rest"], s["cw8"], name=n + "mixer_bwd2", ts=ts)
        dqkv = jnp.concatenate(_attn_bwd(s["qkv"], s["carry"], d_attn, name=n + "attn_bwd", tq=tq, tk=tk), axis=1)
        dxa = _matmul(dqkv, s["wq"], name=n + "proj_qkv_dx", tk=D_QKV, tb=True, epi="add", e=dr1, e_scale=ALPHA,
                      **big)
        dy = _matmul(drest, s["wr"], name=n + "proj_rest_dx", tk=1024, tb=True, epi="add", e=dxa, e_scale=1.0, **big)
        dwq = _matmul(s["xin16"], dqkv, name=n + "proj_qkv_dw", tm=1024, tn=D_QKV, tk=1024, ta=True)
        dwr = _matmul(s["xin16"], drest, name=n + "proj_rest_dw", **dw)
        grads["w_in"][l] = jnp.concatenate([dwq, dwr], axis=1)
        grads["ln2_g"][l] = dg2.sum(0)
        grads["ln2_b"][l] = db2.sum(0)
        grads["ln1_g"][l] = dg1.sum(0)
        grads["ln1_b"][l] = db1.sum(0)
        grads["mix_norm_g"][l] = dgain.sum(0)
        grads["pool_scale"][l] = dsc.sum(0)
        grads["conv_w"][l] = dcw.reshape(3, 8, D_CONV).sum(1)
        grads["pool_w"][l] = jnp.stack([dpw[64 * g:64 * g + 64, 64 * g:64 * g + 64] for g in range(4)])
    grads = {k: jnp.stack(v) for k, v in grads.items()}
    return lsum, dy, grads


ANY = pl.BlockSpec(memory_space=pl.ANY)


def _place():
    x, y, c = lax.axis_index("x"), lax.axis_index("y"), lax.axis_index("c")
    chips = [(1 - x, y), (x, 1 - y), (1 - x, 1 - y)]
    return x, y, c, chips


def _remote(src, dst, send_sems, recv_sems, k, to):
    return pltpu.make_async_remote_copy(src_ref=src, dst_ref=dst, send_sem=send_sems.at[k], recv_sem=recv_sems.at[k],
                                        device_id=to, device_id_type=MESH)


DMA_CHUNK_BYTES = 1 << 20


def _n_chunks(rows, dtype, width=D_MODEL):
    tiles = rows // 16
    want = max(1, (rows * width * jnp.dtype(dtype).itemsize) // DMA_CHUNK_BYTES)
    best = 1
    for n in range(1, tiles + 1):
        if tiles % n == 0 and n <= want:
            best = n
    return best


class _Chunked:
    def __init__(self, src, dst, send_sems, recv_sems, k, to, n):
        self.args = (send_sems, recv_sems, k, to)
        self.whole = self._one(src, dst)
        rows = src.shape[-2]
        assert rows % n == 0
        step = rows // n
        leads = [()] if len(src.shape) == 2 else [(i,) for i in range(src.shape[0])]
        self.parts = [self._one(src.at[(*lead, pl.ds(t * step, step))], dst.at[(*lead, pl.ds(t * step, step))])
                      for lead in leads for t in range(n)]

    def _one(self, src, dst):
        send_sems, recv_sems, k, to = self.args
        if to is None:
            return pltpu.make_async_copy(src, dst, recv_sems)
        return _remote(src, dst, send_sems, recv_sems, k, to)

    def like(self, src, dst):
        return self._one(src, dst)

    def start(self):
        for p in self.parts:
            p.start()

    def wait(self):
        self.whole.wait()

    def wait_send(self):
        self.whole.wait_send()

    def wait_recv(self):
        self.whole.wait_recv()


def _allgather_chips(pack, *, name):
    R, C = pack.shape
    H = R // 2

    def body(p_ref, o_ref, send_sems, recv_sems):
        x, y, c, chips = _place()
        my = 2 * x + y
        sib = (x, y, 1 - c)
        n = _n_chunks(H, pack.dtype)

        def half(k, hc):
            return o_ref.at[k, pl.ds(hc * H, H), :]

        first = [_Chunked(p_ref.at[pl.ds(c * H, H), :], half(my, c), send_sems, recv_sems, j, (cx, cy, c), n)
                 for j, (cx, cy) in enumerate(chips)]
        for cp in first:
            cp.start()
        passed = []
        for j, (cx, cy) in enumerate(chips):
            k = 2 * cx + cy
            first[j].like(half(k, c), half(k, c)).wait_recv()
            fwd = _Chunked(half(k, c), half(k, c), send_sems, recv_sems, 3 + j, sib, n)
            fwd.start()
            passed.append(fwd)
        for j, (cx, cy) in enumerate(chips):
            k = 2 * cx + cy
            passed[j].like(half(k, 1 - c), half(k, 1 - c)).wait_recv()
        for cp in first + passed:
            cp.wait_send()

    return pl.pallas_call(
        body, name=name, in_specs=[ANY], out_specs=ANY,
        out_shape=jax.ShapeDtypeStruct((N_CHIPS, R, C), pack.dtype),
        scratch_shapes=[pltpu.SemaphoreType.DMA((6,)), pltpu.SemaphoreType.DMA((6,))],
    )(pack)


def _swap_halves(gp, *, name):
    K, R, C = gp.shape
    H = R // 2

    def body(g_ref, theirs_ref, send_sems, recv_sems):
        x, y, c, _ = _place()
        n = _n_chunks(H, gp.dtype)
        cp = _Chunked(g_ref.at[:, pl.ds((1 - c) * H, H), :], theirs_ref, send_sems, recv_sems, 0, (x, y, 1 - c), n)
        cp.start()
        cp.wait()

    return pl.pallas_call(
        body, name=name, in_specs=[ANY], out_specs=ANY, out_shape=jax.ShapeDtypeStruct((K, H, C), gp.dtype),
        scratch_shapes=[pltpu.SemaphoreType.DMA((1,)), pltpu.SemaphoreType.DMA((1,))],
    )(gp)


def _scatter_chips(part, *, name):
    K, H, C = part.shape

    def body(p_ref, o_ref, send_sems, recv_sems):
        x, y, c, chips = _place()
        n = _n_chunks(H, part.dtype)
        copies = [_Chunked(p_ref.at[2 * cx + cy], o_ref.at[j], send_sems, recv_sems, j, (cx, cy, c), n)
                  for j, (cx, cy) in enumerate(chips)]
        for cp in copies:
            cp.start()
        for cp in copies:
            cp.wait()

    return pl.pallas_call(
        body, name=name, in_specs=[ANY], out_specs=ANY,
        out_shape=jax.ShapeDtypeStruct((3, H, C), part.dtype),
        scratch_shapes=[pltpu.SemaphoreType.DMA((3,)), pltpu.SemaphoreType.DMA((3,))],
    )(part)


def _join_halves(half, *, name):
    H, C = half.shape

    def body(h_ref, o_ref, send_sems, recv_sems):
        x, y, c, _ = _place()
        n = _n_chunks(H, half.dtype)
        cp = _Chunked(h_ref, o_ref.at[pl.ds(c * H, H), :], send_sems, recv_sems, 0, (x, y, 1 - c), n)
        cp.start()
        cp.wait_send()
        cp.like(h_ref, o_ref.at[pl.ds((1 - c) * H, H), :]).wait_recv()

    return pl.pallas_call(
        body, name=name, in_specs=[ANY], out_specs=ANY,
        out_shape=jax.ShapeDtypeStruct((2 * H, C), half.dtype),
        scratch_shapes=[pltpu.SemaphoreType.DMA((1,)), pltpu.SemaphoreType.DMA((1,))],
    )(half)


def _allreduce_small(v, *, name):
    R, C = v.shape
    n_dev = 8

    def body(v_ref, o_ref, gat, send_sems, recv_sems):
        x, y, c, chips = _place()
        sib = (x, y, 1 - c)

        def rows(px, py, pc):
            return gat.at[4 * px + 2 * py + pc]

        gat[4 * x + 2 * y + c] = v_ref[...]
        first = [_remote(v_ref, rows(x, y, c), send_sems, recv_sems, 0, sib)]
        first += [_remote(v_ref, rows(x, y, c), send_sems, recv_sems, 1 + j, (cx, cy, c))
                  for j, (cx, cy) in enumerate(chips)]
        for cp in first:
            cp.start()
        passed = []
        for j, (cx, cy) in enumerate(chips):
            _remote(v_ref, rows(cx, cy, c), send_sems, recv_sems, 1 + j, sib).wait_recv()
            fwd = _remote(rows(cx, cy, c), rows(cx, cy, c), send_sems, recv_sems, 4 + j, sib)
            fwd.start()
            passed.append(fwd)
        _remote(v_ref, rows(x, y, 1 - c), send_sems, recv_sems, 0, sib).wait_recv()
        for j, (cx, cy) in enumerate(chips):
            _remote(v_ref, rows(cx, cy, 1 - c), send_sems, recv_sems, 4 + j, sib).wait_recv()
        for cp in first + passed:
            cp.wait_send()
        acc = gat[0]
        for d in range(1, n_dev):
            acc = acc + gat[d]
        o_ref[...] = acc

    vm = pl.BlockSpec(memory_space=pltpu.VMEM)
    return pl.pallas_call(
        body, name=name, in_specs=[vm], out_specs=vm,
        out_shape=jax.ShapeDtypeStruct((R, C), F32),
        scratch_shapes=[pltpu.VMEM((n_dev, R, C), F32), pltpu.SemaphoreType.DMA((7,)), pltpu.SemaphoreType.DMA((7,))],
    )(v)


def _add_pairs(a, b, *, name, tr):
    K, H, C = a.shape
    tr = min(tr, H)
    assert H % tr == 0

    def body(a_ref, b_ref, o_ref):
        o_ref[...] = (a_ref[...].astype(F32) + b_ref[...].astype(F32)).astype(BF16)

    blk = pl.BlockSpec((1, tr, C), lambda k, i: (k, i, 0))
    return pl.pallas_call(
        body, name=name, grid=(K, H // tr), in_specs=[blk, blk], out_specs=blk,
        out_shape=jax.ShapeDtypeStruct((K, H, C), BF16),
        compiler_params=_params(("parallel", "parallel"), 3 * _nbytes((tr, C), BF16)),
    )(a, b)


def _add_final(a, b, others, *, name, tr):
    H, C = a.shape
    tr = min(tr, H)
    assert H % tr == 0

    def body(a_ref, b_ref, o_ref_in, out_ref):
        acc = a_ref[...].astype(F32) + b_ref[...].astype(F32)
        for j in range(3):
            acc = acc + o_ref_in[j].astype(F32)
        out_ref[...] = acc

    blk = pl.BlockSpec((tr, C), lambda i: (i, 0))
    return pl.pallas_call(
        body, name=name, grid=(H // tr,),
        in_specs=[blk, blk, pl.BlockSpec((3, tr, C), lambda i: (0, i, 0))], out_specs=blk,
        out_shape=jax.ShapeDtypeStruct((H, C), F32),
        compiler_params=_params(("parallel",), 6 * _nbytes((tr, C), F32)),
    )(a, b, others)


def _adamw(w, g, m, v, *, name, tr, row0=0):
    R, C = w.shape
    tr = min(tr, R)
    assert R % tr == 0 and row0 % tr == 0
    off = row0 // tr

    def body(w_ref, g_ref, m_ref, v_ref, go_ref, d_ref, mo_ref, vo_ref):
        gv = g_ref[...]
        m2 = ADAM_B1 * m_ref[...] + (1.0 - ADAM_B1) * gv
        v2 = ADAM_B2 * v_ref[...] + (1.0 - ADAM_B2) * jnp.square(gv)
        m_hat = m2 / (1.0 - ADAM_B1 ** ADAM_STEP)
        v_hat = v2 / (1.0 - ADAM_B2 ** ADAM_STEP)
        d_ref[...] = -ADAM_LR * (m_hat / (jnp.sqrt(v_hat) + ADAM_EPS) + ADAM_WD * w_ref[...])
        go_ref[...] = gv
        mo_ref[...] = m2
        vo_ref[...] = v2

    blk = pl.BlockSpec((tr, C), lambda i: (i, 0))
    shape = jax.ShapeDtypeStruct((R, C), F32)
    return pl.pallas_call(
        body, name=name, grid=(R // tr,),
        in_specs=[blk, pl.BlockSpec((tr, C), lambda i: (i + off, 0)), blk, blk], out_specs=[blk] * 4,
        out_shape=[shape] * 4,
        compiler_params=_params(("parallel",), 8 * _nbytes((tr, C), F32)),
    )(w, g, m, v)


BIG = ("w_in", "w_o", "w_up", "w_down")
BIG_AXIS = dict(w_in=2, w_o=1, w_up=2, w_down=1)
SMALL = ("pool_w", "pool_scale", "mix_norm_g", "ln1_g", "ln1_b", "ln2_g", "ln2_b")
CONV_ROWS = 32
SMALL_ROWS = 48


def _big_rows(shards):
    sizes = [shards[n].size // D_MODEL for n in BIG]
    starts = [sum(sizes[:i]) for i in range(len(sizes))]
    return sizes, starts


def _pad_rows(flat, rows):
    return jnp.pad(flat, (0, rows * D_MODEL - flat.shape[0])).reshape(rows, D_MODEL)


def kernel(x, w_in, conv_w, pool_w, pool_scale, mix_norm_g, w_o, ln1_g, ln1_b, w_up, w_down, ln2_g, ln2_b, loss_target, m_w_in, m_conv_w, m_pool_w, m_pool_scale, m_mix_norm_g, m_w_o, m_ln1_g, m_ln1_b, m_w_up, m_w_down, m_ln2_g, m_ln2_b, v_w_in, v_conv_w, v_pool_w, v_pool_scale, v_mix_norm_g, v_w_o, v_ln1_g, v_ln1_b, v_w_up, v_w_down, v_ln2_g, v_ln2_b):
    wts = dict(w_in=w_in, conv_w=conv_w, pool_w=pool_w, pool_scale=pool_scale, mix_norm_g=mix_norm_g, w_o=w_o,
               ln1_g=ln1_g, ln1_b=ln1_b, w_up=w_up, w_down=w_down, ln2_g=ln2_g, ln2_b=ln2_b)
    mom = dict(w_in=m_w_in, conv_w=m_conv_w, pool_w=m_pool_w, pool_scale=m_pool_scale, mix_norm_g=m_mix_norm_g,
               w_o=m_w_o, ln1_g=m_ln1_g, ln1_b=m_ln1_b, w_up=m_w_up, w_down=m_w_down, ln2_g=m_ln2_g, ln2_b=m_ln2_b)
    var = dict(w_in=v_w_in, conv_w=v_conv_w, pool_w=v_pool_w, pool_scale=v_pool_scale, mix_norm_g=v_mix_norm_g,
               w_o=v_w_o, ln1_g=v_ln1_g, ln1_b=v_ln1_b, w_up=v_w_up, w_down=v_w_down, ln2_g=v_ln2_g, ln2_b=v_ln2_b)
    chip = 2 * lax.axis_index("x") + lax.axis_index("y")
    sizes, starts = _big_rows(wts)
    big_rows = sum(sizes)

    conv_bits = lax.bitcast_convert_type(conv_w.reshape(-1), BF16).reshape(-1)
    pack = jnp.concatenate([wts[n].reshape(-1, D_MODEL).astype(BF16) for n in BIG]
                           + [_pad_rows(conv_bits, CONV_ROWS)], axis=0)
    gathered = _allgather_chips(pack, name="gather_weights")
    gathered = lax.dynamic_update_index_in_dim(gathered, pack, chip, 0)
    full = {}
    for n, size, start in zip(BIG, sizes, starts):
        parts = [gathered[k, start:start + size].reshape(wts[n].shape) for k in range(N_CHIPS)]
        full[n] = jnp.concatenate(parts, axis=BIG_AXIS[n])
    conv_parts = [lax.bitcast_convert_type(gathered[k, big_rows:].reshape(-1)[:2 * conv_w.size].reshape(-1, 2), F32)
                  .reshape(conv_w.shape) for k in range(N_CHIPS)]
    full["conv_w"] = jnp.concatenate(conv_parts, axis=2)
    for n in SMALL:
        full[n] = wts[n]

    lsum, grad_x, grads = _local_step(x[0], loss_target[0], full)

    blocks = []
    for k in range(N_CHIPS):
        rows = []
        for n in BIG:
            ax = BIG_AXIS[n]
            width = wts[n].shape[ax]
            rows.append(lax.slice_in_dim(grads[n], k * width, (k + 1) * width, axis=ax).reshape(-1, D_MODEL))
        blocks.append(jnp.concatenate(rows, axis=0))
    gpack = jnp.stack(blocks).astype(BF16)
    core = lax.axis_index("c")
    half_rows = big_rows // 2
    theirs = _swap_halves(gpack, name="grad_swap_cores")
    mine = lax.dynamic_slice_in_dim(gpack, core * half_rows, half_rows, axis=1)
    chip_sum = _add_pairs(mine, theirs, name="grad_add_cores", tr=736)
    from_chips = _scatter_chips(chip_sum, name="grad_scatter_chips")
    mine_k = lax.dynamic_index_in_dim(mine, chip, 0, keepdims=False)
    theirs_k = lax.dynamic_index_in_dim(theirs, chip, 0, keepdims=False)
    half_sum = _add_final(mine_k, theirs_k, from_chips, name="grad_add_chips", tr=736)
    gsum = _join_halves(half_sum, name="grad_join_cores")
    gsum = lax.dynamic_update_slice_in_dim(gsum, half_sum, core * half_rows, axis=0)

    small_flat = jnp.concatenate([grads[n].reshape(-1) for n in SMALL] + [grads["conv_w"].reshape(-1),
                                                                          lsum.sum().reshape(1)])
    small_sum = _allreduce_small(_pad_rows(small_flat, SMALL_ROWS), name="allreduce_small").reshape(-1)
    gsmall = {}
    pos = 0
    for n in SMALL:
        gsmall[n] = small_sum[pos:pos + wts[n].size].reshape(wts[n].shape)
        pos += wts[n].size
    conv_full = small_sum[pos:pos + 4 * conv_w.size].reshape(DEPTH, 3, D_CONV)
    pos += 4 * conv_w.size
    loss = small_sum[pos]
    gsmall["conv_w"] = lax.dynamic_slice_in_dim(conv_full, chip * conv_w.shape[2], conv_w.shape[2], axis=2)

    out_g, out_d, out_m, out_v = {}, {}, {}, {}
    for n, size, start in zip(BIG, sizes, starts):
        shp = wts[n].shape
        res = _adamw(wts[n].reshape(-1, D_MODEL), gsum, mom[n].reshape(-1, D_MODEL), var[n].reshape(-1, D_MODEL),
                     name="adamw_" + n, tr=256, row0=start)
        out_g[n], out_d[n], out_m[n], out_v[n] = [r.reshape(shp) for r in res]
    small_names = SMALL + ("conv_w",)
    packs = [_pad_rows(jnp.concatenate([d[n].reshape(-1) for n in small_names]), SMALL_ROWS)
             for d in (wts, gsmall, mom, var)]
    res = _adamw(*packs, name="adamw_small", tr=SMALL_ROWS)
    pos = 0
    for n in small_names:
        shp = wts[n].shape
        out_g[n], out_d[n], out_m[n], out_v[n] = [r.reshape(-1)[pos:pos + wts[n].size].reshape(shp) for r in res]
        pos += wts[n].size

    order = ("w_in", "conv_w", "pool_w", "pool_scale", "mix_norm_g", "w_o", "ln1_g", "ln1_b", "w_up", "w_down",
             "ln2_g", "ln2_b")
    return (loss, grad_x[None], *[out_g[n] for n in order], *[out_d[n] for n in order],
            *[out_m[n] for n in order], *[out_v[n] for n in order])
```

```python
import functools
import math

import jax
import jax.numpy as jnp
from jax import lax
from jax.experimental import pallas as pl
from jax.experimental.pallas import tpu as pltpu

F32 = jnp.float32
BF16 = jnp.bfloat16
MESH = pl.DeviceIdType.MESH

D_MODEL = 1024
DEPTH = 2
HEAD_DIM = 64
D_SB = 512
D_CONV = 256
D_POOL = 256
D_QKV = 3 * D_SB
D_REST = 3 * D_CONV + D_POOL
D_IN = D_QKV + D_REST
D_FF = 4 * D_MODEL
ALPHA = (2 * DEPTH) ** 0.25
LN_EPS = 1e-5
RMS_EPS = 1e-6
SCALE = HEAD_DIM ** -0.5
N_CHIPS = 4
HALO = 16

ADAM_LR = 0.001
ADAM_B1 = 0.9
ADAM_B2 = 0.999
ADAM_EPS = 1e-08
ADAM_WD = 0.01
ADAM_STEP = 10

VMEM_V7X_BYTES = 64 * 1024 * 1024
VMEM_CAP_BYTES = VMEM_V7X_BYTES - 8 * 1024 * 1024


def _params(sem, block_bytes):
    limit = min(VMEM_CAP_BYTES, max(32 * 1024 * 1024, 3 * block_bytes))
    return pltpu.CompilerParams(dimension_semantics=sem, vmem_limit_bytes=limit)


def _nbytes(shape, dtype):
    return math.prod(shape) * jnp.dtype(dtype).itemsize


def _dot(a, b, dims=(((1,), (0,)), ((), ()))):
    return lax.dot_general(a, b, dims, preferred_element_type=F32)


NT = (((1,), (1,)), ((), ()))
TN = (((0,), (0,)), ((), ()))


def _split(x):
    hi = x.astype(BF16)
    lo = (x - hi.astype(F32)).astype(BF16)
    return hi, lo


def _sum8(x):
    r, c = x.shape
    return x.reshape(r // 8, 8, c).sum(axis=0)


def _matmul(a, b, *, name, tm, tn, tk, ta=False, tb=False, out_dtype=F32,
            epi=None, e=None, e_scale=1.0, relu2_out=False):
    M, K = (a.shape[1], a.shape[0]) if ta else a.shape
    N = b.shape[0] if tb else b.shape[1]
    tm, tn, tk = min(tm, M), min(tn, N), min(tk, K)
    assert M % tm == 0 and N % tn == 0 and K % tk == 0, (name, M, N, K)
    nk = K // tk
    dims = (((0 if ta else 1,), (1 if tb else 0,)), ((), ()))
    n_in = 2 if epi is None else 3
    n_out = 2 if relu2_out else 1

    def body(*refs):
        a_ref, b_ref = refs[0], refs[1]
        e_ref = refs[2] if epi is not None else None
        o_ref = refs[n_in]
        scr = refs[n_in + n_out:]
        p = _dot(a_ref[...].astype(BF16), b_ref[...].astype(BF16), dims)

        def finish(acc):
            if epi == "drelu2":
                acc = acc * (2.0 * jnp.maximum(e_ref[...], 0.0))
            elif epi == "add":
                acc = acc + e_scale * e_ref[...]
            o_ref[...] = acc.astype(out_dtype)
            if relu2_out:
                refs[n_in + 1][...] = jnp.square(jnp.maximum(acc, 0.0)).astype(BF16)

        if nk == 1:
            finish(p)
        else:
            acc_ref = scr[0]
            k = pl.program_id(2)

            @pl.when(k == 0)
            def _():
                acc_ref[...] = p

            @pl.when(k > 0)
            def _():
                acc_ref[...] += p

            @pl.when(k == nk - 1)
            def _():
                finish(acc_ref[...])

    a_spec = pl.BlockSpec((tk, tm), lambda i, j, k: (k, i)) if ta else pl.BlockSpec((tm, tk), lambda i, j, k: (i, k))
    b_spec = pl.BlockSpec((tn, tk), lambda i, j, k: (j, k)) if tb else pl.BlockSpec((tk, tn), lambda i, j, k: (k, j))
    o_spec = pl.BlockSpec((tm, tn), lambda i, j, k: (i, j))
    in_specs = [a_spec, b_spec]
    args = [a, b]
    nbytes = _nbytes((tm, tk), a.dtype) + _nbytes((tk, tn), b.dtype) + 2 * _nbytes((tm, tn), F32)
    if epi is not None:
        in_specs.append(o_spec)
        args.append(e)
        nbytes += _nbytes((tm, tn), e.dtype)
    scratch = [pltpu.VMEM((tm, tn), F32)] if nk > 1 else []
    out_shape = [jax.ShapeDtypeStruct((M, N), out_dtype)]
    if relu2_out:
        out_shape.append(jax.ShapeDtypeStruct((M, N), BF16))
        nbytes += _nbytes((tm, tn), BF16)
    res = pl.pallas_call(
        body, name=name,
        grid=(M // tm, N // tn, nk),
        in_specs=in_specs, out_specs=[o_spec] * n_out,
        out_shape=out_shape,
        scratch_shapes=scratch,
        compiler_params=_params(("parallel", "parallel", "arbitrary"), nbytes),
    )(*args)
    return res if relu2_out else res[0]


def _matmul_ln(a, b, xres, g, bias, *, name, tm, tk):
    M, K = a.shape
    N = b.shape[1]
    tm, tk = min(tm, M), min(tk, K)
    assert M % tm == 0 and K % tk == 0 and N == D_MODEL
    nk = K // tk

    def body(a_ref, b_ref, x_ref, g_ref, bias_ref, y_ref, y16_ref, xh_ref, rs_ref, *scr):
        p = _dot(a_ref[...].astype(BF16), b_ref[...].astype(BF16))

        def finish(acc):
            r = ALPHA * x_ref[...] + acc
            mu = jnp.mean(r, axis=-1, keepdims=True)
            xc = r - mu
            var = jnp.mean(xc * xc, axis=-1, keepdims=True)
            rstd = lax.rsqrt(var + LN_EPS)
            xh = xc * rstd
            y = xh * g_ref[...] + bias_ref[...]
            y_ref[...] = y
            y16_ref[...] = y.astype(BF16)
            xh_ref[...] = xh
            rs_ref[...] = rstd

        if nk == 1:
            finish(p)
        else:
            acc_ref = scr[0]
            k = pl.program_id(1)

            @pl.when(k == 0)
            def _():
                acc_ref[...] = p

            @pl.when(k > 0)
            def _():
                acc_ref[...] += p

            @pl.when(k == nk - 1)
            def _():
                finish(acc_ref[...])

    row = pl.BlockSpec((tm, N), lambda i, k: (i, 0))
    vec = pl.BlockSpec((1, N), lambda i, k: (0, 0))
    nbytes = _nbytes((tm, tk), a.dtype) + _nbytes((tk, N), b.dtype) + 6 * _nbytes((tm, N), F32)
    scratch = [pltpu.VMEM((tm, N), F32)] if nk > 1 else []
    return pl.pallas_call(
        body, name=name,
        grid=(M // tm, nk),
        in_specs=[pl.BlockSpec((tm, tk), lambda i, k: (i, k)), pl.BlockSpec((tk, N), lambda i, k: (k, 0)), row, vec, vec],
        out_specs=[row, row, row, pl.BlockSpec((tm, 1), lambda i, k: (i, 0))],
        out_shape=[jax.ShapeDtypeStruct((M, N), F32), jax.ShapeDtypeStruct((M, N), BF16),
                   jax.ShapeDtypeStruct((M, N), F32), jax.ShapeDtypeStruct((M, 1), F32)],
        scratch_shapes=scratch,
        compiler_params=_params(("parallel", "arbitrary"), nbytes),
    )(a, b, xres, g.reshape(1, N), bias.reshape(1, N))


def _ln_bwd(dy, xhat, rstd, g, *, name, tm):
    M, N = dy.shape
    tm = min(tm, M)

    def body(dy_ref, xh_ref, rs_ref, g_ref, dr_ref, dr16_ref, dg_ref, db_ref):
        i = pl.program_id(0)
        dyv = dy_ref[...]
        xh = xh_ref[...]
        dxh = dyv * g_ref[...]
        m1 = jnp.mean(dxh, axis=-1, keepdims=True)
        m2 = jnp.mean(dxh * xh, axis=-1, keepdims=True)
        dr = rs_ref[...] * (dxh - m1 - xh * m2)
        dr_ref[...] = dr
        dr16_ref[...] = dr.astype(BF16)
        pg = _sum8(dyv * xh)
        pb = _sum8(dyv)

        @pl.when(i == 0)
        def _():
            dg_ref[...] = pg
            db_ref[...] = pb

        @pl.when(i > 0)
        def _():
            dg_ref[...] += pg
            db_ref[...] += pb

    row = pl.BlockSpec((tm, N), lambda i: (i, 0))
    acc = pl.BlockSpec((8, N), lambda i: (0, 0))
    return pl.pallas_call(
        body, name=name, grid=(M // tm,),
        in_specs=[row, row, pl.BlockSpec((tm, 1), lambda i: (i, 0)), pl.BlockSpec((1, N), lambda i: (0, 0))],
        out_specs=[row, row, acc, acc],
        out_shape=[jax.ShapeDtypeStruct((M, N), F32), jax.ShapeDtypeStruct((M, N), BF16),
                   jax.ShapeDtypeStruct((8, N), F32), jax.ShapeDtypeStruct((8, N), F32)],
        compiler_params=_params(("arbitrary",), 5 * _nbytes((tm, N), F32)),
    )(dy, xhat, rstd, g.reshape(1, N))


def _loss_grad(y, tgt, *, name, tm):
    M, N = y.shape
    tm = min(tm, M)

    def body(y_ref, t_ref, dy_ref, l_ref):
        i = pl.program_id(0)
        d = y_ref[...] - t_ref[...]
        dy_ref[...] = d * (1.0 / N)
        pl_ = _sum8(d * d) * (0.5 / N)

        @pl.when(i == 0)
        def _():
            l_ref[...] = pl_

        @pl.when(i > 0)
        def _():
            l_ref[...] += pl_

    row = pl.BlockSpec((tm, N), lambda i: (i, 0))
    return pl.pallas_call(
        body, name=name, grid=(M // tm,),
        in_specs=[row, row], out_specs=[row, pl.BlockSpec((8, N), lambda i: (0, 0))],
        out_shape=[jax.ShapeDtypeStruct((M, N), F32), jax.ShapeDtypeStruct((8, N), F32)],
        compiler_params=_params(("arbitrary",), 3 * _nbytes((tm, N), F32)),
    )(y, tgt)


def _tri(n, kind):
    j = lax.broadcasted_iota(jnp.int32, (2 * n, n), 0) % n
    s = lax.broadcasted_iota(jnp.int32, (2 * n, n), 1)
    return ((j > s) if kind == "after" else (j < s)).astype(BF16)


LOG2E = 1.4426950408889634
DEAD = -104.0
NOT_VISITED = -1e30


def _log_terms(z):
    lse = jnp.log(1.0 + jnp.exp2(jnp.abs(z) * (-LOG2E)))
    logsig = jnp.minimum(z, 0.0) - lse
    return logsig, logsig - z


def _cumsum_mm(x, u2_ref):
    hi, lo = _split(x)
    return _dot(jnp.concatenate([hi, lo], axis=1), u2_ref[...])


def _head_rows(x2, scale):
    lane = lax.broadcasted_iota(jnp.int32, (1, 128), 1)
    zero = jnp.zeros_like(x2)
    both = jnp.concatenate([jnp.where(lane < HEAD_DIM, x2, zero), jnp.where(lane >= HEAD_DIM, x2, zero)], axis=0)
    return both * scale


def _causal_mask(i, ks, tq, tk):
    row = lax.broadcasted_iota(jnp.int32, (2 * tq, tk), 0)
    row = i * tq + jnp.where(row >= tq, row - tq, row)
    col = lax.broadcasted_iota(jnp.int32, (2 * tq, tk), 1)
    return (ks + col) < row


def _attn_fwd(qkv, *, name, tq, tk):
    S = qkv.shape[0]
    tq = tk = min(tq, tk, S)
    assert S % tq == 0 and S // tk <= 128
    tri = _tri(tk, "after")

    def body(q_ref, k_ref, v_ref, u_ref, o_ref, c_ref, qcat, oacc, cacc, call, ls_buf, tl_buf, l0_buf):
        i = pl.program_id(1)
        lane = lax.broadcasted_iota(jnp.int32, (1, 128), 1)
        qcat[...] = _head_rows(q_ref[...], SCALE)
        oacc[...] = jnp.zeros_like(oacc)
        cacc[...] = jnp.zeros_like(cacc)
        call[...] = jnp.full_like(call, NOT_VISITED)

        def scores(kb, masked, slot):
            ks = pl.multiple_of(jnp.maximum(kb, 0) * tk, tk)
            z = _dot(qcat[...], k_ref[pl.ds(ks, tk), :], NT)
            logsig, lom = _log_terms(z)
            if masked:
                msk = jnp.logical_and(_causal_mask(i, ks, tq, tk), kb >= 0)
                lom = jnp.where(msk, lom, 0.0)
                logsig = jnp.where(msk, logsig, -1e30)
            ls_buf[slot] = logsig
            tl_buf[slot] = _cumsum_mm(lom, u_ref)
            l0_buf[slot] = lom[:, 0:1]

        def weights(kb, slot):
            ks = pl.multiple_of(jnp.maximum(kb, 0) * tk, tk)
            tl = tl_buf[slot]
            c = cacc[...]
            call[...] = jnp.where(lane == kb, c, call[...])
            a = jnp.exp(ls_buf[slot] + tl + c).astype(BF16)
            oacc[...] += _dot(a, v_ref[pl.ds(ks, tk), :])
            cacc[...] = c + tl[:, 0:1] + l0_buf[slot]

        def pair(kb, masked):
            scores(kb, masked, 0)
            scores(kb - 1, masked, 1)
            weights(kb, 0)
            weights(kb - 1, 1)

        pair(i, True)

        def live(state):
            t, cmax = state
            return jnp.logical_and(t < (i - 1) // 2, cmax > DEAD)

        def trip(state):
            t, _ = state
            pair(i - 2 - 2 * t, False)
            return t + 1, jnp.max(cacc[...])

        t_end, cmax = lax.while_loop(live, trip, (0, jnp.max(cacc[...])))

        @pl.when(jnp.logical_and(jnp.logical_and(i >= 2, i % 2 == 0), jnp.logical_and(t_end == (i - 1) // 2, cmax > DEAD)))
        def _():
            pair(0, True)

        o_ref[...] = jnp.where(lane < HEAD_DIM, oacc[0:tq], oacc[tq:2 * tq])
        c_ref[...] = jnp.concatenate([call[0:tq], call[tq:2 * tq]], axis=1)

    nbytes = (_nbytes((tq, 128), BF16) + 2 * _nbytes((S, 128), BF16) + _nbytes((2 * tk, tk), BF16)
              + 8 * _nbytes((tq, 128), F32) + 14 * _nbytes((2 * tq, tk), F32))
    return pl.pallas_call(
        body, name=name, grid=(4, S // tq),
        in_specs=[pl.BlockSpec((tq, 128), lambda j, i: (i, j)),
                  pl.BlockSpec((S, 128), lambda j, i: (0, 4 + j)),
                  pl.BlockSpec((S, 128), lambda j, i: (0, 8 + j)),
                  pl.BlockSpec((2 * tk, tk), lambda j, i: (0, 0))],
        out_specs=[pl.BlockSpec((tq, 128), lambda j, i: (i, j)),
                   pl.BlockSpec((tq, 256), lambda j, i: (i, j))],
        out_shape=[jax.ShapeDtypeStruct((S, D_SB), F32), jax.ShapeDtypeStruct((S, 1024), F32)],
        scratch_shapes=[pltpu.VMEM((2 * tq, 128), BF16), pltpu.VMEM((2 * tq, 128), F32),
                        pltpu.VMEM((2 * tq, 1), F32), pltpu.VMEM((2 * tq, 128), F32),
                        pltpu.VMEM((2, 2 * tq, tk), F32), pltpu.VMEM((2, 2 * tq, tk), F32),
                        pltpu.VMEM((2, 2 * tq, 1), F32)],
        compiler_params=_params(("parallel", "arbitrary"), nbytes),
    )(qkv, qkv, qkv, tri)


def _attn_bwd(qkv, carry, do, *, name, tq, tk):
    S = qkv.shape[0]
    tq = tk = min(tq, tk, S)
    assert S % tq == 0 and S // tk <= 128
    nkb = S // tk
    nq = S // tq
    tri_after = _tri(tk, "after")
    tri_before = _tri(tk, "before")

    def body(q_ref, k_ref, v_ref, c_ref, do_ref, ua_ref, ub_ref, dq_ref, dk_ref, dv_ref,
             qcat, docat, qcat_t, docat_t, ccat, dqacc, pacc, dkt, dvt, ls_buf, tl_buf, da_buf):
        i = pl.program_id(1)
        lane = lax.broadcasted_iota(jnp.int32, (1, 128), 1)
        sub = lax.broadcasted_iota(jnp.int32, (128, 1), 0)
        q2 = q_ref[...]
        do2 = do_ref[...]
        qcat[...] = _head_rows(q2, SCALE)
        docat[...] = _head_rows(do2, 1.0).astype(BF16)
        qt = q2.astype(F32).T * SCALE
        dot_ = do2.T
        qcat_t[...] = jnp.concatenate([jnp.where(sub < HEAD_DIM, qt, 0.0), jnp.where(sub >= HEAD_DIM, qt, 0.0)],
                                      axis=1).astype(BF16)
        docat_t[...] = jnp.concatenate([jnp.where(sub < HEAD_DIM, dot_, 0.0), jnp.where(sub >= HEAD_DIM, dot_, 0.0)],
                                       axis=1).astype(BF16)
        ccat[0:tq] = c_ref[:, 0:128]
        ccat[tq:2 * tq] = c_ref[:, 128:256]

        @pl.when(i == 0)
        def _():
            dkt[...] = jnp.zeros_like(dkt)
            dvt[...] = jnp.zeros_like(dvt)

        dqacc[...] = jnp.zeros_like(dqacc)
        pacc[...] = jnp.zeros_like(pacc)

        def scores(kb, masked, slot):
            ks = pl.multiple_of(jnp.maximum(kb, 0) * tk, tk)
            z = _dot(qcat[...], k_ref[pl.ds(ks, tk), :], NT)
            logsig, lom = _log_terms(z)
            if masked:
                msk = jnp.logical_and(_causal_mask(i, ks, tq, tk), kb >= 0)
                lom = jnp.where(msk, lom, 0.0)
                logsig = jnp.where(msk, logsig, -1e30)
            ls_buf[slot] = logsig
            tl_buf[slot] = _cumsum_mm(lom, ua_ref)
            da_buf[slot] = _dot(docat[...], v_ref[pl.ds(ks, tk), :], NT)

        def grads(kb, slot):
            kbc = jnp.maximum(kb, 0)
            ks = pl.multiple_of(kbc * tk, tk)
            logsig = ls_buf[slot]
            c = jnp.sum(jnp.where(lane == kb, ccat[...], 0.0), axis=1, keepdims=True)
            a = jnp.exp(logsig + tl_buf[slot] + c)
            g = a * da_buf[slot]
            before = _cumsum_mm(g, ub_ref)
            pc = pacc[...]
            dz = g - jnp.exp(logsig) * (g + before + pc)
            dzb = dz.astype(BF16)
            dqacc[...] += _dot(dzb, k_ref[pl.ds(ks, tk), :])
            dkt[kbc] += _dot(qcat_t[...], dzb)
            dvt[kbc] += _dot(docat_t[...], a.astype(BF16))
            pacc[...] = pc + before[:, tk - 1:tk] + g[:, tk - 1:tk]

        def pair(kb, masked):
            scores(kb, masked, 0)
            scores(kb + 1, masked, 1)
            grads(kb, 0)
            grads(kb + 1, 1)

        reach = jnp.max(ccat[...], axis=0, keepdims=True)
        first = jnp.min(jnp.where(reach > DEAD, lane, 128).astype(F32)).astype(jnp.int32)
        first = jnp.minimum(first, i)
        start = first - (i - first + 1) % 2

        @pl.when(jnp.logical_and(start < 0, i >= 2))
        def _():
            pair(-1, True)

        k0 = jnp.where(start < 0, 1, start)

        def loop(t, carry_):
            pair(k0 + 2 * t, False)
            return carry_

        lax.fori_loop(0, jnp.maximum((i - 1 - k0) // 2, 0), loop, 0)
        pair(i - 1, True)
        dq_ref[...] = (jnp.where(lane < HEAD_DIM, dqacc[0:tq], dqacc[tq:2 * tq]) * SCALE).astype(BF16)

        @pl.when(i == nq - 1)
        def _():
            for kb in range(nkb):
                dk_ref[kb * tk:(kb + 1) * tk, :] = dkt[kb].T.astype(BF16)
                dv_ref[kb * tk:(kb + 1) * tk, :] = dvt[kb].T.astype(BF16)

    nbytes = (_nbytes((tq, 128), BF16) + 2 * _nbytes((S, 128), BF16) + 2 * _nbytes((2 * tk, tk), BF16)
              + 12 * _nbytes((tq, 128), F32) + 4 * _nbytes((S, 128), F32) + 14 * _nbytes((2 * tq, tk), F32))
    blk = pl.BlockSpec((tq, 128), lambda j, i: (i, j))
    full = pl.BlockSpec((S, 128), lambda j, i: (0, j))
    tri_spec = pl.BlockSpec((2 * tk, tk), lambda j, i: (0, 0))
    dq, dk, dv = pl.pallas_call(
        body, name=name, grid=(4, nq),
        in_specs=[blk,
                  pl.BlockSpec((S, 128), lambda j, i: (0, 4 + j)),
                  pl.BlockSpec((S, 128), lambda j, i: (0, 8 + j)),
                  pl.BlockSpec((tq, 256), lambda j, i: (i, j)),
                  blk, tri_spec, tri_spec],
        out_specs=[blk, full, full],
        out_shape=[jax.ShapeDtypeStruct((S, D_SB), BF16)] * 3,
        scratch_shapes=[pltpu.VMEM((2 * tq, 128), BF16), pltpu.VMEM((2 * tq, 128), BF16),
                        pltpu.VMEM((128, 2 * tq), BF16), pltpu.VMEM((128, 2 * tq), BF16),
                        pltpu.VMEM((2 * tq, 128), F32), pltpu.VMEM((2 * tq, 128), F32), pltpu.VMEM((2 * tq, 1), F32),
                        pltpu.VMEM((nkb, 128, tk), F32), pltpu.VMEM((nkb, 128, tk), F32),
                        pltpu.VMEM((2, 2 * tq, tk), F32), pltpu.VMEM((2, 2 * tq, tk), F32),
                        pltpu.VMEM((2, 2 * tq, tk), F32)],
        compiler_params=_params(("parallel", "arbitrary"), nbytes),
    )(qkv, qkv, qkv, carry, do, tri_after, tri_before)
    return dq, dk, dv


def _group_mats():
    lanes = jnp.arange(D_MODEL) // HEAD_DIM
    gs = (lanes[:, None] == jnp.arange(128)[None, :]).astype(BF16)
    return gs, gs.T


def _group_sum_bcast(x, gs, gb):
    hi, lo = _split(x)
    s = _dot(hi, gs) + _dot(lo, gs)
    return _bcast(s, gb)


def _bcast(s, gb):
    hi, lo = _split(s)
    return _dot(hi, gb) + _dot(lo, gb)


def _pool_lane_consts():
    lane = lax.broadcasted_iota(jnp.int32, (1, D_POOL), 1)
    grp = lane // (D_POOL // 4)
    win = jnp.where(grp == 0, 2, jnp.where(grp == 1, 4, jnp.where(grp == 2, 8, 16)))
    return grp, win


def _by_group(grp, s2, s4, s8, s16):
    return jnp.where(grp == 0, s2, jnp.where(grp == 1, s4, jnp.where(grp == 2, s8, s16)))


def _mixers(i, ts, prev_ref, cur_ref, cw_ref, pw_ref, ps_ref):
    cur = cur_ref[...]
    prev = jnp.where(i == 0, 0.0, prev_ref[...])
    ext = jnp.concatenate([prev, cur], axis=0)
    n = HALO + ts

    def back(a, k):
        return pltpu.roll(a, k, 0)

    u = ext[:, D_CONV:2 * D_CONV] * ext[:, 2 * D_CONV:3 * D_CONV]
    p = ext[:, 3 * D_CONV:]
    cv = (cw_ref[0:1, :] * back(u, 2) + cw_ref[1:2, :] * back(u, 1) + cw_ref[2:3, :] * u)[HALO:]
    s2 = p + back(p, 1)
    s4 = s2 + back(s2, 2)
    s8 = s4 + back(s4, 4)
    s16 = s8 + back(s8, 8)
    grp, win = _pool_lane_consts()
    t1 = i * ts + 1 + lax.broadcasted_iota(jnp.int32, (ts, 1), 0)
    cnt = jnp.minimum(t1, win).astype(F32)
    pooled = _by_group(grp, s2, s4, s8, s16)[HALO:] / cnt - p[HALO:]
    yp = _dot(pooled.astype(BF16), pw_ref[...])
    del n
    return dict(b=cur[:, 0:D_CONV], u=u, cv=cv, pooled=pooled, yp=yp, cnt=cnt,
                conv_out=cur[:, 0:D_CONV] * cv, pool_out=yp * ps_ref[...])


def _halo_specs(ts, S, width):
    nb = ts // HALO
    last = S // HALO - 1
    prev = pl.BlockSpec((HALO, width), lambda i: (jnp.maximum(i * nb - 1, 0), 0))
    nxt = pl.BlockSpec((HALO, width), lambda i: (jnp.minimum((i + 1) * nb, last), 0))
    return prev, nxt


def _mixer_fwd(rest, attn, cw8, pwbd, ps, gain, *, name, ts):
    S = rest.shape[0]
    ts = min(ts, S)
    gs, gb = _group_mats()

    def body(prev_ref, cur_ref, attn_ref, cw_ref, pw_ref, ps_ref, gain_ref, gs_ref, gb_ref, o_ref):
        i = pl.program_id(0)
        f = _mixers(i, ts, prev_ref, cur_ref, cw_ref, pw_ref, ps_ref)
        mix = jnp.concatenate([attn_ref[...], f["conv_out"], f["pool_out"]], axis=1)
        ss = _group_sum_bcast(mix * mix, gs_ref[...], gb_ref[...])
        rinv = lax.rsqrt(ss * (1.0 / HEAD_DIM) + RMS_EPS)
        o_ref[...] = (mix * rinv * gain_ref[...]).astype(BF16)

    prev, _ = _halo_specs(ts, S, D_REST)
    row = lambda w: pl.BlockSpec((ts, w), lambda i: (i, 0))
    const = lambda a: pl.BlockSpec(a.shape, lambda i: (0, 0))
    nbytes = 12 * _nbytes((ts + HALO, D_REST), F32)
    return pl.pallas_call(
        body, name=name, grid=(S // ts,),
        in_specs=[prev, row(D_REST), row(D_SB), const(cw8), const(pwbd), const(ps), const(gain), const(gs), const(gb)],
        out_specs=row(D_MODEL),
        out_shape=jax.ShapeDtypeStruct((S, D_MODEL), BF16),
        compiler_params=_params(("parallel",), nbytes),
    )(rest, rest, attn, cw8, pwbd, ps, gain, gs, gb)


def _mixer_bwd1(dmixn, rest, attn, cw8, pwbd, ps, gain, *, name, ts):
    S = rest.shape[0]
    ts = min(ts, S)
    gs, gb = _group_mats()

    def body(dm_ref, prev_ref, cur_ref, attn_ref, cw_ref, pw_ref, ps_ref, gain_ref, gs_ref, gb_ref,
             da_ref, aux_ref, dg_ref, dsc_ref, dcw_ref, dpw_ref):
        i = pl.program_id(0)
        f = _mixers(i, ts, prev_ref, cur_ref, cw_ref, pw_ref, ps_ref)
        mix = jnp.concatenate([attn_ref[...], f["conv_out"], f["pool_out"]], axis=1)
        gsm, gbm = gs_ref[...], gb_ref[...]
        ss = _group_sum_bcast(mix * mix, gsm, gbm)
        rinv = lax.rsqrt(ss * (1.0 / HEAD_DIM) + RMS_EPS)
        dm = dm_ref[...]
        xn = mix * rinv
        dyg = dm * gain_ref[...]
        gm = _group_sum_bcast(dyg * xn, gsm, gbm) * (1.0 / HEAD_DIM)
        dmix = rinv * (dyg - xn * gm)
        da_ref[...] = dmix[:, 0:D_SB]
        dco = dmix[:, D_SB:D_SB + D_CONV]
        dpo = dmix[:, D_SB + D_CONV:]
        dcv = dco * f["b"]
        dyp = dpo * ps_ref[...]
        dpooled = _dot(dyp.astype(BF16), pw_ref[...], NT)
        aux_ref[...] = jnp.concatenate([dco * f["cv"], dcv, dpooled / f["cnt"], dpooled], axis=1)
        u = f["u"]
        parts = [
            _sum8(dm * xn),
            _sum8(dpo * f["yp"]),
            jnp.concatenate([_sum8(dcv * pltpu.roll(u, 2, 0)[HALO:]), _sum8(dcv * pltpu.roll(u, 1, 0)[HALO:]),
                             _sum8(dcv * u[HALO:])], axis=0),
            _dot(f["pooled"].astype(BF16), dyp.astype(BF16), TN),
        ]
        outs = [dg_ref, dsc_ref, dcw_ref, dpw_ref]

        @pl.when(i == 0)
        def _():
            for o, v in zip(outs, parts):
                o[...] = v

        @pl.when(i > 0)
        def _():
            for o, v in zip(outs, parts):
                o[...] += v

    prev, _ = _halo_specs(ts, S, D_REST)
    row = lambda w: pl.BlockSpec((ts, w), lambda i: (i, 0))
    const = lambda a: pl.BlockSpec(a.shape, lambda i: (0, 0))
    acc = lambda r_, w: pl.BlockSpec((r_, w), lambda i: (0, 0))
    nbytes = 16 * _nbytes((ts + HALO, D_REST), F32)
    return pl.pallas_call(
        body, name=name, grid=(S // ts,),
        in_specs=[row(D_MODEL), prev, row(D_REST), row(D_SB), const(cw8), const(pwbd), const(ps), const(gain),
                  const(gs), const(gb)],
        out_specs=[row(D_SB), row(D_REST), acc(8, D_MODEL), acc(8, D_POOL), acc(24, D_CONV), acc(D_POOL, D_POOL)],
        out_shape=[jax.ShapeDtypeStruct((S, D_SB), F32), jax.ShapeDtypeStruct((S, D_REST), F32),
                   jax.ShapeDtypeStruct((8, D_MODEL), F32), jax.ShapeDtypeStruct((8, D_POOL), F32),
                   jax.ShapeDtypeStruct((24, D_CONV), F32), jax.ShapeDtypeStruct((D_POOL, D_POOL), F32)],
        compiler_params=_params(("arbitrary",), nbytes),
    )(dmixn, rest, rest, attn, cw8, pwbd, ps, gain, gs, gb)


def _mixer_bwd2(aux, rest, cw8, *, name, ts):
    S = rest.shape[0]
    ts = min(ts, S)
    nblk = S // ts

    def body(cur_ref, nxt_ref, rest_ref, cw_ref, o_ref):
        i = pl.program_id(0)
        cur = cur_ref[...]
        nxt = jnp.where(i == nblk - 1, 0.0, nxt_ref[...])
        ext = jnp.concatenate([cur, nxt], axis=0)
        n = ts + HALO

        def fwd(a, k):
            return pltpu.roll(a, n - k, 0)

        dcv = ext[:, D_CONV:2 * D_CONV]
        dps = ext[:, 2 * D_CONV:3 * D_CONV]
        du = (cw_ref[2:3, :] * dcv + cw_ref[1:2, :] * fwd(dcv, 1) + cw_ref[0:1, :] * fwd(dcv, 2))[0:ts]
        f2 = dps + fwd(dps, 1)
        f4 = f2 + fwd(f2, 2)
        f8 = f4 + fwd(f4, 4)
        f16 = f8 + fwd(f8, 8)
        grp, _ = _pool_lane_consts()
        dp = _by_group(grp, f2, f4, f8, f16)[0:ts] - cur[:, 3 * D_CONV:]
        rest_v = rest_ref[...]
        c_gate = rest_v[:, D_CONV:2 * D_CONV]
        h = rest_v[:, 2 * D_CONV:3 * D_CONV]
        o_ref[...] = jnp.concatenate([cur[:, 0:D_CONV], du * h, du * c_gate, dp], axis=1).astype(BF16)

    _, nxt = _halo_specs(ts, S, D_REST)
    row = pl.BlockSpec((ts, D_REST), lambda i: (i, 0))
    return pl.pallas_call(
        body, name=name, grid=(nblk,),
        in_specs=[row, nxt, row, pl.BlockSpec(cw8.shape, lambda i: (0, 0))],
        out_specs=row,
        out_shape=jax.ShapeDtypeStruct((S, D_REST), BF16),
        compiler_params=_params(("parallel",), 10 * _nbytes((ts + HALO, D_REST), F32)),
    )(aux, aux, rest, cw8)


def _block_diag(pw):
    wide = jnp.tile(pw.reshape(256, 64), (1, 4))
    grp = jnp.arange(256) // 64
    return jnp.where(grp[:, None] == grp[None, :], wide, 0.0)


def _rows8(v, rows=8):
    return jnp.pad(v, ((0, rows - v.shape[0]), (0, 0)))


TILES = dict(tm=512, ts=512, tq=256, tk=256)


def _local_step(x, tgt, w, t=None):
    t = dict(TILES, **(t or {}))
    tm, ts, tq, tk = t["tm"], t["ts"], t["tq"], t["tk"]
    big = dict(tm=1024, tn=1024)
    saved = []
    xl, xl16 = x, x.astype(BF16)
    for l in range(DEPTH):
        n = f"l{l}_"
        wq, wr = w["w_in"][l][:, :D_QKV], w["w_in"][l][:, D_QKV:]
        qkv = _matmul(xl16, wq, name=n + "proj_qkv", tm=1024, tn=D_QKV, tk=1024, out_dtype=BF16)
        rest = _matmul(xl16, wr, name=n + "proj_rest", tk=1024, **big)
        attn, carry = _attn_fwd(qkv, name=n + "attn_fwd", tq=tq, tk=tk)
        cw8 = _rows8(w["conv_w"][l])
        pwbd = _block_diag(w["pool_w"][l]).astype(BF16)
        ps = w["pool_scale"][l].reshape(1, D_POOL)
        gain = w["mix_norm_g"][l].reshape(1, D_MODEL)
        mixn = _mixer_fwd(rest, attn, cw8, pwbd, ps, gain, name=n + "mixer_fwd", ts=ts)
        x1, x1_16, xh1, rs1 = _matmul_ln(mixn, w["w_o"][l], xl, w["ln1_g"][l], w["ln1_b"][l], name=n + "wo_ln",
                                         tm=tm, tk=1024)
        hpre, hid = _matmul(x1_16, w["w_up"][l], name=n + "ffn_up", tk=1024, relu2_out=True, **big)
        x2, x2_16, xh2, rs2 = _matmul_ln(hid, w["w_down"][l], x1, w["ln2_g"][l], w["ln2_b"][l],
                                         name=n + "ffn_down_ln", tm=tm // 2, tk=D_FF)
        saved.append(dict(xin16=xl16, wq=wq, wr=wr, qkv=qkv, rest=rest, attn=attn, carry=carry, cw8=cw8, pwbd=pwbd,
                          ps=ps, gain=gain, mixn=mixn, x1_16=x1_16, xh1=xh1, rs1=rs1, hpre=hpre, hid=hid, xh2=xh2,
                          rs2=rs2))
        xl, xl16 = x2, x2_16

    dy, lsum = _loss_grad(xl, tgt, name="loss_grad", tm=tm)
    grads = {k: [None] * DEPTH for k in
             ("w_in", "conv_w", "pool_w", "pool_scale", "mix_norm_g", "w_o", "ln1_g", "ln1_b", "w_up", "w_down",
              "ln2_g", "ln2_b")}
    dw = dict(tk=2048, ta=True, out_dtype=BF16, **big)
    for l in reversed(range(DEPTH)):
        n = f"l{l}_"
        s = saved[l]
        dr2, dr2_16, dg2, db2 = _ln_bwd(dy, s["xh2"], s["rs2"], w["ln2_g"][l], name=n + "ln2_bwd", tm=tm)
        dhpre = _matmul(dr2_16, w["w_down"][l], name=n + "ffn_down_dx", tk=1024, tb=True, out_dtype=BF16,
                        epi="drelu2", e=s["hpre"], **big)
        grads["w_down"][l] = _matmul(s["hid"], dr2_16, name=n + "ffn_down_dw", **dw)
        dx1 = _matmul(dhpre, w["w_up"][l], name=n + "ffn_up_dx", tm=512, tn=1024, tk=D_FF, tb=True,
                      epi="add", e=dr2, e_scale=ALPHA)
        grads["w_up"][l] = _matmul(s["x1_16"], dhpre, name=n + "ffn_up_dw", **dw)
        dr1, dr1_16, dg1, db1 = _ln_bwd(dx1, s["xh1"], s["rs1"], w["ln1_g"][l], name=n + "ln1_bwd", tm=tm)
        dmixn = _matmul(dr1_16, w["w_o"][l], name=n + "wo_dx", tk=1024, tb=True, **big)
        grads["w_o"][l] = _matmul(s["mixn"], dr1_16, name=n + "wo_dw", **dw)
        d_attn, aux, dgain, dsc, dcw, dpw = _mixer_bwd1(dmixn, s["rest"], s["attn"], s["cw8"], s["pwbd"], s["ps"],
                                                        s["gain"], name=n + "mixer_bwd1", ts=ts)
        drest = _mixer_bwd2(aux, s["rest"], s["cw8"], name=n + "mixer_bwd2", ts=ts)
        dqkv = jnp.concatenate(_attn_bwd(s["qkv"], s["carry"], d_attn, name=n + "attn_bwd", tq=tq, tk=tk), axis=1)
        dxa = _matmul(dqkv, s["wq"], name=n + "proj_qkv_dx", tk=D_QKV, tb=True, epi="add", e=dr1, e_scale=ALPHA,
                      **big)
        dy = _matmul(drest, s["wr"], name=n + "proj_rest_dx", tk=1024, tb=True, epi="add", e=dxa, e_scale=1.0, **big)
        dwq = _matmul(s["xin16"], dqkv, name=n + "proj_qkv_dw", tm=1024, tn=D_QKV, tk=1024, ta=True, out_dtype=BF16)
        dwr = _matmul(s["xin16"], drest, name=n + "proj_rest_dw", **dw)
        grads["w_in"][l] = jnp.concatenate([dwq, dwr], axis=1)
        grads["ln2_g"][l] = dg2.sum(0)
        grads["ln2_b"][l] = db2.sum(0)
        grads["ln1_g"][l] = dg1.sum(0)
        grads["ln1_b"][l] = db1.sum(0)
        grads["mix_norm_g"][l] = dgain.sum(0)
        grads["pool_scale"][l] = dsc.sum(0)
        grads["conv_w"][l] = dcw.reshape(3, 8, D_CONV).sum(1)
        grads["pool_w"][l] = jnp.stack([dpw[64 * g:64 * g + 64, 64 * g:64 * g + 64] for g in range(4)])
    grads = {k: jnp.stack(v) for k, v in grads.items()}
    return lsum, dy, grads


ANY = pl.BlockSpec(memory_space=pl.ANY)


def _place():
    x, y, c = lax.axis_index("x"), lax.axis_index("y"), lax.axis_index("c")
    chips = [(1 - x, y), (x, 1 - y), (1 - x, 1 - y)]
    return x, y, c, chips


def _remote(src, dst, send_sems, recv_sems, k, to):
    return pltpu.make_async_remote_copy(src_ref=src, dst_ref=dst, send_sem=send_sems.at[k], recv_sem=recv_sems.at[k],
                                        device_id=to, device_id_type=MESH)


DMA_CHUNK_BYTES = 1 << 20


def _n_chunks(rows, dtype, width=D_MODEL):
    tiles = rows // 16
    want = max(1, (rows * width * jnp.dtype(dtype).itemsize) // DMA_CHUNK_BYTES)
    best = 1
    for n in range(1, tiles + 1):
        if tiles % n == 0 and n <= want:
            best = n
    return best


class _Chunked:
    def __init__(self, src, dst, send_sems, recv_sems, k, to, n):
        self.args = (send_sems, recv_sems, k, to)
        self.whole = self._one(src, dst)
        rows = src.shape[-2]
        assert rows % n == 0
        step = rows // n
        leads = [()] if len(src.shape) == 2 else [(i,) for i in range(src.shape[0])]
        self.parts = [self._one(src.at[(*lead, pl.ds(t * step, step))], dst.at[(*lead, pl.ds(t * step, step))])
                      for lead in leads for t in range(n)]

    def _one(self, src, dst):
        send_sems, recv_sems, k, to = self.args
        if to is None:
            return pltpu.make_async_copy(src, dst, recv_sems)
        return _remote(src, dst, send_sems, recv_sems, k, to)

    def like(self, src, dst):
        return self._one(src, dst)

    def start(self):
        for p in self.parts:
            p.start()

    def wait(self):
        self.whole.wait()

    def wait_send(self):
        self.whole.wait_send()

    def wait_recv(self):
        self.whole.wait_recv()


def _allgather_chips(pack, *, name):
    R, C = pack.shape
    H = R // 2
    Q = H // 2
    assert R % 64 == 0
    A, B = 0, 1

    def body(p_ref, o_ref, send_sems, recv_sems):
        x, y, c, _ = _place()
        my, kx, ky, kd = 2 * x + y, 2 * (1 - x) + y, 2 * x + (1 - y), 2 * (1 - x) + (1 - y)
        xn, yn, sib = (1 - x, y, c), (x, 1 - y, c), (x, y, 1 - c)
        n = _n_chunks(Q, pack.dtype)

        def own(ab):
            return p_ref.at[pl.ds(c * H + ab * Q, Q), :]

        def quarter(k, hc, ab):
            return o_ref.at[k, pl.ds(hc * H + ab * Q, Q), :]

        def send(src, k, ab, sem, to):
            cp = _Chunked(src, quarter(k, c, ab), send_sems, recv_sems, sem, to, n)
            cp.start()
            return cp

        def landed(sent, k, hc, ab):
            sent.like(quarter(k, hc, ab), quarter(k, hc, ab)).wait_recv()

        a_x = send(own(A), my, A, 0, xn)
        b_y = send(own(B), my, B, 3, yn)
        b_x = send(own(B), my, B, 1, xn)
        a_y = send(own(A), my, A, 4, yn)
        landed(b_y, ky, c, B)
        fb = send(quarter(ky, c, B), ky, B, 2, xn)
        landed(a_x, kx, c, A)
        fa = send(quarter(kx, c, A), kx, A, 5, yn)
        arrivals = [(kx, A, None), (ky, B, None), (kx, B, b_x), (ky, A, a_y), (kd, B, fb), (kd, A, fa)]
        passed = []
        for j, (k, ab, sent) in enumerate(arrivals):
            if sent is not None:
                landed(sent, k, c, ab)
            passed.append(_Chunked(quarter(k, c, ab), quarter(k, c, ab), send_sems, recv_sems, 6 + j, sib, n))
            passed[-1].start()
        for j, (k, ab, _) in enumerate(arrivals):
            landed(passed[j], k, 1 - c, ab)
        for cp in [a_x, b_y, b_x, a_y, fb, fa] + passed:
            cp.wait_send()

    return pl.pallas_call(
        body, name=name, in_specs=[ANY], out_specs=ANY,
        out_shape=jax.ShapeDtypeStruct((N_CHIPS, R, C), pack.dtype),
        scratch_shapes=[pltpu.SemaphoreType.DMA((12,)), pltpu.SemaphoreType.DMA((12,))],
    )(pack)


def _swap_halves(gp, *, name):
    K, R, C = gp.shape
    H = R // 2

    def body(g_ref, theirs_ref, send_sems, recv_sems):
        x, y, c, _ = _place()
        n = _n_chunks(H, gp.dtype)
        cp = _Chunked(g_ref.at[:, pl.ds((1 - c) * H, H), :], theirs_ref, send_sems, recv_sems, 0, (x, y, 1 - c), n)
        cp.start()
        cp.wait()

    return pl.pallas_call(
        body, name=name, in_specs=[ANY], out_specs=ANY, out_shape=jax.ShapeDtypeStruct((K, H, C), gp.dtype),
        scratch_shapes=[pltpu.SemaphoreType.DMA((1,)), pltpu.SemaphoreType.DMA((1,))],
    )(gp)


def _scatter_chips(part, *, name):
    K, H, C = part.shape

    def body(p_ref, o_ref, send_sems, recv_sems):
        x, y, c, chips = _place()
        n = _n_chunks(H, part.dtype)
        copies = [_Chunked(p_ref.at[2 * cx + cy], o_ref.at[j], send_sems, recv_sems, j, (cx, cy, c), n)
                  for j, (cx, cy) in enumerate(chips)]
        for cp in copies:
            cp.start()
        for cp in copies:
            cp.wait()

    return pl.pallas_call(
        body, name=name, in_specs=[ANY], out_specs=ANY,
        out_shape=jax.ShapeDtypeStruct((3, H, C), part.dtype),
        scratch_shapes=[pltpu.SemaphoreType.DMA((3,)), pltpu.SemaphoreType.DMA((3,))],
    )(part)


def _join_halves(both, *, name):
    H, C = both.shape[0] // 2, both.shape[1]

    def body(in_ref, o_ref, send_sems, recv_sems):
        x, y, c, _ = _place()
        n = _n_chunks(H, both.dtype)
        mine = pl.ds(c * H, H)
        theirs = pl.ds((1 - c) * H, H)
        cp = _Chunked(in_ref.at[mine, :], o_ref.at[mine, :], send_sems, recv_sems, 0, (x, y, 1 - c), n)
        cp.start()
        cp.wait_send()
        cp.like(in_ref.at[theirs, :], o_ref.at[theirs, :]).wait_recv()

    return pl.pallas_call(
        body, name=name, in_specs=[ANY], out_specs=ANY, input_output_aliases={0: 0},
        out_shape=jax.ShapeDtypeStruct(both.shape, both.dtype),
        scratch_shapes=[pltpu.SemaphoreType.DMA((1,)), pltpu.SemaphoreType.DMA((1,))],
    )(both)


def _allreduce_small(v, *, name):
    R, C = v.shape
    n_dev = 8

    def body(v_ref, o_ref, gat, send_sems, recv_sems):
        x, y, c, chips = _place()
        sib = (x, y, 1 - c)

        def rows(px, py, pc):
            return gat.at[4 * px + 2 * py + pc]

        gat[4 * x + 2 * y + c] = v_ref[...]
        first = [_remote(v_ref, rows(x, y, c), send_sems, recv_sems, 0, sib)]
        first += [_remote(v_ref, rows(x, y, c), send_sems, recv_sems, 1 + j, (cx, cy, c))
                  for j, (cx, cy) in enumerate(chips)]
        for cp in first:
            cp.start()
        passed = []
        for j, (cx, cy) in enumerate(chips):
            _remote(v_ref, rows(cx, cy, c), send_sems, recv_sems, 1 + j, sib).wait_recv()
            fwd = _remote(rows(cx, cy, c), rows(cx, cy, c), send_sems, recv_sems, 4 + j, sib)
            fwd.start()
            passed.append(fwd)
        _remote(v_ref, rows(x, y, 1 - c), send_sems, recv_sems, 0, sib).wait_recv()
        for j, (cx, cy) in enumerate(chips):
            _remote(v_ref, rows(cx, cy, 1 - c), send_sems, recv_sems, 4 + j, sib).wait_recv()
        for cp in first + passed:
            cp.wait_send()
        acc = gat[0]
        for d in range(1, n_dev):
            acc = acc + gat[d]
        o_ref[...] = acc

    vm = pl.BlockSpec(memory_space=pltpu.VMEM)
    return pl.pallas_call(
        body, name=name, in_specs=[vm], out_specs=vm,
        out_shape=jax.ShapeDtypeStruct((R, C), F32),
        scratch_shapes=[pltpu.VMEM((n_dev, R, C), F32), pltpu.SemaphoreType.DMA((7,)), pltpu.SemaphoreType.DMA((7,))],
    )(v)


def _add_pairs(gp, theirs, place, *, name, tr):
    K, H, C = theirs.shape
    tr = min(tr, H)
    assert H % tr == 0
    nb = H // tr

    def body(place_ref, a_ref, b_ref, o_ref):
        o_ref[...] = (a_ref[...].astype(F32) + b_ref[...].astype(F32)).astype(BF16)

    blk = pl.BlockSpec((1, tr, C), lambda k, i, p: (k, i, 0))
    mine = pl.BlockSpec((1, tr, C), lambda k, i, p: (k, i + p[1] * nb, 0))
    return pl.pallas_call(
        body, name=name,
        grid_spec=pltpu.PrefetchScalarGridSpec(num_scalar_prefetch=1, grid=(K, nb), in_specs=[mine, blk],
                                               out_specs=blk),
        out_shape=jax.ShapeDtypeStruct((K, H, C), BF16),
        compiler_params=_params(("parallel", "parallel"), 3 * _nbytes((tr, C), BF16)),
    )(place, gp, theirs)


def _add_final(gp, theirs, others, place, *, name, tr):
    K, H, C = theirs.shape
    tr = min(tr, H)
    assert H % tr == 0
    nb = H // tr

    def body(place_ref, a_ref, b_ref, o_ref_in, out_ref):
        acc = a_ref[0].astype(F32) + b_ref[0].astype(F32)
        for j in range(3):
            acc = acc + o_ref_in[j].astype(F32)
        out_ref[...] = acc

    return pl.pallas_call(
        body, name=name,
        grid_spec=pltpu.PrefetchScalarGridSpec(
            num_scalar_prefetch=1, grid=(nb,),
            in_specs=[pl.BlockSpec((1, tr, C), lambda i, p: (p[0], i + p[1] * nb, 0)),
                      pl.BlockSpec((1, tr, C), lambda i, p: (p[0], i, 0)),
                      pl.BlockSpec((3, tr, C), lambda i, p: (0, i, 0))],
            out_specs=pl.BlockSpec((tr, C), lambda i, p: (i + p[1] * nb, 0))),
        out_shape=jax.ShapeDtypeStruct((2 * H, C), F32),
        compiler_params=_params(("parallel",), 6 * _nbytes((tr, C), F32)),
    )(place, gp, theirs, others)


def _adamw(w, g, m, v, *, name, tr, row0=0):
    R, C = w.shape
    tr = min(tr, R)
    assert R % tr == 0 and row0 % tr == 0
    off = row0 // tr

    def body(w_ref, g_ref, m_ref, v_ref, go_ref, d_ref, mo_ref, vo_ref):
        gv = g_ref[...]
        m2 = ADAM_B1 * m_ref[...] + (1.0 - ADAM_B1) * gv
        v2 = ADAM_B2 * v_ref[...] + (1.0 - ADAM_B2) * jnp.square(gv)
        m_hat = m2 / (1.0 - ADAM_B1 ** ADAM_STEP)
        v_hat = v2 / (1.0 - ADAM_B2 ** ADAM_STEP)
        d_ref[...] = -ADAM_LR * (m_hat / (jnp.sqrt(v_hat) + ADAM_EPS) + ADAM_WD * w_ref[...])
        go_ref[...] = gv
        mo_ref[...] = m2
        vo_ref[...] = v2

    blk = pl.BlockSpec((tr, C), lambda i: (i, 0))
    shape = jax.ShapeDtypeStruct((R, C), F32)
    return pl.pallas_call(
        body, name=name, grid=(R // tr,),
        in_specs=[blk, pl.BlockSpec((tr, C), lambda i: (i + off, 0)), blk, blk], out_specs=[blk] * 4,
        out_shape=[shape] * 4,
        compiler_params=_params(("parallel",), 8 * _nbytes((tr, C), F32)),
    )(w, g, m, v)


BIG = ("w_in", "w_o", "w_up", "w_down")
BIG_AXIS = dict(w_in=2, w_o=1, w_up=2, w_down=1)
SMALL = ("pool_w", "pool_scale", "mix_norm_g", "ln1_g", "ln1_b", "ln2_g", "ln2_b")
CONV_ROWS = 64
SMALL_ROWS = 48


def _big_rows(shards):
    sizes = [shards[n].size // D_MODEL for n in BIG]
    starts = [sum(sizes[:i]) for i in range(len(sizes))]
    return sizes, starts


def _to_chips(a, axis):
    shape = list(a.shape)
    shape[axis:axis + 1] = [N_CHIPS, shape[axis] // N_CHIPS]
    return jnp.moveaxis(a.reshape(shape), axis, 0)


def _from_chips(a, axis):
    a = jnp.moveaxis(a, 0, axis)
    shape = list(a.shape)
    shape[axis:axis + 2] = [shape[axis] * shape[axis + 1]]
    return a.reshape(shape)


def _pad_rows(flat, rows):
    return jnp.pad(flat, (0, rows * D_MODEL - flat.shape[0])).reshape(rows, D_MODEL)


def kernel(x, w_in, conv_w, pool_w, pool_scale, mix_norm_g, w_o, ln1_g, ln1_b, w_up, w_down, ln2_g, ln2_b, loss_target, m_w_in, m_conv_w, m_pool_w, m_pool_scale, m_mix_norm_g, m_w_o, m_ln1_g, m_ln1_b, m_w_up, m_w_down, m_ln2_g, m_ln2_b, v_w_in, v_conv_w, v_pool_w, v_pool_scale, v_mix_norm_g, v_w_o, v_ln1_g, v_ln1_b, v_w_up, v_w_down, v_ln2_g, v_ln2_b):
    wts = dict(w_in=w_in, conv_w=conv_w, pool_w=pool_w, pool_scale=pool_scale, mix_norm_g=mix_norm_g, w_o=w_o,
               ln1_g=ln1_g, ln1_b=ln1_b, w_up=w_up, w_down=w_down, ln2_g=ln2_g, ln2_b=ln2_b)
    mom = dict(w_in=m_w_in, conv_w=m_conv_w, pool_w=m_pool_w, pool_scale=m_pool_scale, mix_norm_g=m_mix_norm_g,
               w_o=m_w_o, ln1_g=m_ln1_g, ln1_b=m_ln1_b, w_up=m_w_up, w_down=m_w_down, ln2_g=m_ln2_g, ln2_b=m_ln2_b)
    var = dict(w_in=v_w_in, conv_w=v_conv_w, pool_w=v_pool_w, pool_scale=v_pool_scale, mix_norm_g=v_mix_norm_g,
               w_o=v_w_o, ln1_g=v_ln1_g, ln1_b=v_ln1_b, w_up=v_w_up, w_down=v_w_down, ln2_g=v_ln2_g, ln2_b=v_ln2_b)
    chip = 2 * lax.axis_index("x") + lax.axis_index("y")
    sizes, starts = _big_rows(wts)
    big_rows = sum(sizes)

    conv_bits = lax.bitcast_convert_type(conv_w.reshape(-1), BF16).reshape(-1)
    pack = jnp.concatenate([wts[n].reshape(-1, D_MODEL).astype(BF16) for n in BIG]
                           + [_pad_rows(conv_bits, CONV_ROWS)], axis=0)
    gathered = _allgather_chips(pack, name="gather_weights")
    gathered = lax.dynamic_update_index_in_dim(gathered, pack, chip, 0)
    full = {}
    for n, size, start in zip(BIG, sizes, starts):
        full[n] = _from_chips(gathered[:, start:start + size].reshape((N_CHIPS,) + wts[n].shape), BIG_AXIS[n])
    conv_parts = [lax.bitcast_convert_type(gathered[k, big_rows:].reshape(-1)[:2 * conv_w.size].reshape(-1, 2), F32)
                  .reshape(conv_w.shape) for k in range(N_CHIPS)]
    full["conv_w"] = jnp.concatenate(conv_parts, axis=2)
    for n in SMALL:
        full[n] = wts[n]

    lsum, grad_x, grads = _local_step(x[0], loss_target[0], full)

    gpack = jnp.concatenate([_to_chips(grads[n], BIG_AXIS[n]).reshape(N_CHIPS, -1, D_MODEL) for n in BIG],
                            axis=1).astype(BF16)
    place = jnp.stack([chip, lax.axis_index("c")]).astype(jnp.int32)
    theirs = _swap_halves(gpack, name="grad_swap_cores")
    chip_sum = _add_pairs(gpack, theirs, place, name="grad_add_cores", tr=736)
    from_chips = _scatter_chips(chip_sum, name="grad_scatter_chips")
    half_sum = _add_final(gpack, theirs, from_chips, place, name="grad_add_chips", tr=736)
    gsum = _join_halves(half_sum, name="grad_join_cores")

    small_flat = jnp.concatenate([grads[n].reshape(-1) for n in SMALL] + [grads["conv_w"].reshape(-1),
                                                                          lsum.sum().reshape(1)])
    small_sum = _allreduce_small(_pad_rows(small_flat, SMALL_ROWS), name="allreduce_small").reshape(-1)
    gsmall = {}
    pos = 0
    for n in SMALL:
        gsmall[n] = small_sum[pos:pos + wts[n].size].reshape(wts[n].shape)
        pos += wts[n].size
    conv_full = small_sum[pos:pos + 4 * conv_w.size].reshape(DEPTH, 3, D_CONV)
    pos += 4 * conv_w.size
    loss = small_sum[pos]
    gsmall["conv_w"] = lax.dynamic_slice_in_dim(conv_full, chip * conv_w.shape[2], conv_w.shape[2], axis=2)

    out_g, out_d, out_m, out_v = {}, {}, {}, {}
    for n, size, start in zip(BIG, sizes, starts):
        shp = wts[n].shape
        g, row0 = gsum, start
        if shp[-1] != D_MODEL:
            g, row0 = gsum[start:start + size].reshape(-1, shp[-1]), 0
        res = _adamw(wts[n].reshape(-1, shp[-1]), g, mom[n].reshape(-1, shp[-1]), var[n].reshape(-1, shp[-1]),
                     name="adamw_" + n, tr=256, row0=row0)
        out_g[n], out_d[n], out_m[n], out_v[n] = [r.reshape(shp) for r in res]
    small_names = SMALL + ("conv_w",)
    packs = [_pad_rows(jnp.concatenate([d[n].reshape(-1) for n in small_names]), SMALL_ROWS)
             for d in (wts, gsmall, mom, var)]
    res = _adamw(*packs, name="adamw_small", tr=SMALL_ROWS)
    pos = 0
    for n in small_names:
        shp = wts[n].shape
        out_g[n], out_d[n], out_m[n], out_v[n] = [r.reshape(-1)[pos:pos + wts[n].size].reshape(shp) for r in res]
        pos += wts[n].size

    order = ("w_in", "conv_w", "pool_w", "pool_scale", "mix_norm_g", "w_o", "ln1_g", "ln1_b", "w_up", "w_down",
             "ln2_g", "ln2_b")
    return (loss, grad_x[None], *[out_g[n] for n in order], *[out_d[n] for n in order],
            *[out_m[n] for n in order], *[out_v[n] for n in order])
```

```python
import functools
import math
from typing import NamedTuple

import jax
import jax.numpy as jnp
from jax import lax
from jax.experimental import pallas as pl
from jax.experimental.pallas import tpu as pltpu

F32 = jnp.float32
BF16 = jnp.bfloat16
MESH = pl.DeviceIdType.MESH

D_MODEL = 1024
DEPTH = 2
HEAD_DIM = 64
D_SB = 512
D_CONV = 256
D_POOL = 256
D_QKV = 3 * D_SB
D_REST = 3 * D_CONV + D_POOL
D_IN = D_QKV + D_REST
D_FF = 4 * D_MODEL
ALPHA = (2 * DEPTH) ** 0.25
LN_EPS = 1e-5
RMS_EPS = 1e-6
SCALE = HEAD_DIM ** -0.5
N_CHIPS = 4
HALO = 16

ADAM_LR = 0.001
ADAM_B1 = 0.9
ADAM_B2 = 0.999
ADAM_EPS = 1e-08
ADAM_WD = 0.01
ADAM_STEP = 10

VMEM_V7X_BYTES = 64 * 1024 * 1024
VMEM_CAP_BYTES = VMEM_V7X_BYTES - 8 * 1024 * 1024


def _params(sem, block_bytes):
    limit = min(VMEM_CAP_BYTES, max(32 * 1024 * 1024, 3 * block_bytes))
    return pltpu.CompilerParams(dimension_semantics=sem, vmem_limit_bytes=limit)


def _nbytes(shape, dtype):
    return math.prod(shape) * jnp.dtype(dtype).itemsize


def _dot(a, b, dims=(((1,), (0,)), ((), ()))):
    return lax.dot_general(a, b, dims, preferred_element_type=F32)


NT = (((1,), (1,)), ((), ()))
TN = (((0,), (0,)), ((), ()))


def _split(x):
    hi = x.astype(BF16)
    lo = (x - hi.astype(F32)).astype(BF16)
    return hi, lo


def _sum8(x):
    r, c = x.shape
    return x.reshape(r // 8, 8, c).sum(axis=0)


class _ChipWeight(NamedTuple):
    arr: jax.Array
    rb: int
    along: str


CHIP_BLOCK = 1024


def _matmul(a, b, *, name, tm, tn, tk, ta=False, tb=False, out_dtype=F32,
            epi=None, e=None, e_scale=1.0, relu2_out=False, out_chips=None):
    M, K = (a.shape[1], a.shape[0]) if ta else a.shape
    chips = isinstance(b, _ChipWeight)
    split_k = chips and ((b.along == "cols") == tb)
    if chips:
        N = CHIP_BLOCK if split_k else N_CHIPS * CHIP_BLOCK
        assert K == (N_CHIPS * CHIP_BLOCK if split_k else CHIP_BLOCK) and not ta, (name, K)
        tn, tk = CHIP_BLOCK, K
    else:
        N = b.shape[0] if tb else b.shape[1]
    tm, tn, tk = min(tm, M), min(tn, N), min(tk, K)
    assert M % tm == 0 and N % tn == 0 and K % tk == 0, (name, M, N, K)
    nk = K // tk
    dims = (((0 if ta else 1,), (1 if tb else 0,)), ((), ()))
    n_in = 2 + (epi is not None) + (out_chips is not None)

    def body(*refs):
        a_ref, b_ref = refs[0], refs[1]
        e_ref = refs[2] if epi is not None else None
        o_ref = refs[n_in]
        scr = refs[-1:]
        if not chips:
            p = _dot(a_ref[...].astype(BF16), b_ref[...].astype(BF16), dims)
        elif split_k:
            p = _dot(a_ref[:, 0:CHIP_BLOCK].astype(BF16), b_ref[0], dims)
            for c in range(1, N_CHIPS):
                p = p + _dot(a_ref[:, c * CHIP_BLOCK:(c + 1) * CHIP_BLOCK].astype(BF16), b_ref[c], dims)
        else:
            p = _dot(a_ref[...].astype(BF16), b_ref[0], dims)

        def finish(acc):
            if epi == "drelu2":
                acc = acc * (2.0 * jnp.maximum(e_ref[...], 0.0))
            elif epi == "add":
                acc = acc + e_scale * e_ref[...]
            if out_chips is None:
                o_ref[...] = acc.astype(out_dtype)
            else:
                o_ref[0] = acc.astype(out_dtype)
            if relu2_out:
                refs[n_in + 1][...] = jnp.square(jnp.maximum(acc, 0.0)).astype(BF16)

        if nk == 1:
            finish(p)
        else:
            acc_ref = scr[0]
            k = pl.program_id(2)

            @pl.when(k == 0)
            def _():
                acc_ref[...] = p

            @pl.when(k > 0)
            def _():
                acc_ref[...] += p

            @pl.when(k == nk - 1)
            def _():
                finish(acc_ref[...])

    a_spec = pl.BlockSpec((tk, tm), lambda i, j, k: (k, i)) if ta else pl.BlockSpec((tm, tk), lambda i, j, k: (i, k))
    if chips:
        b_arr, rb = b.arr, b.rb
        nblk = N_CHIPS if split_k else 1
        b_spec = pl.BlockSpec((nblk, CHIP_BLOCK, CHIP_BLOCK),
                              (lambda i, j, k: (0, rb, 0)) if split_k else (lambda i, j, k: (j, rb, 0)))
    else:
        b_arr = b
        b_spec = pl.BlockSpec((tn, tk), lambda i, j, k: (j, k)) if tb else pl.BlockSpec((tk, tn), lambda i, j, k: (k, j))
    o_spec = pl.BlockSpec((tm, tn), lambda i, j, k: (i, j))
    in_specs = [a_spec, b_spec]
    args = [a, b_arr]
    nbytes = _nbytes((tm, tk), a.dtype) + _nbytes((tk, tn), b_arr.dtype) + 2 * _nbytes((tm, tn), F32)
    if epi is not None:
        in_specs.append(o_spec)
        args.append(e)
        nbytes += _nbytes((tm, tn), e.dtype)
    scratch = [pltpu.VMEM((tm, tn), F32)] if nk > 1 else []
    out_shape = [jax.ShapeDtypeStruct((M, N), out_dtype)]
    out_specs = [o_spec]
    aliases = {}
    if out_chips is not None:
        assert tm == tn == CHIP_BLOCK and not relu2_out and out_chips.arr.dtype == out_dtype
        orb = out_chips.rb
        out_specs = [pl.BlockSpec((1, CHIP_BLOCK, CHIP_BLOCK),
                                  (lambda i, j, k: (j, orb, 0)) if out_chips.along == "cols" else
                                  (lambda i, j, k: (i, orb, 0)))]
        out_shape = [jax.ShapeDtypeStruct(out_chips.arr.shape, out_dtype)]
        in_specs.append(pl.BlockSpec(memory_space=pl.ANY))
        args.append(out_chips.arr)
        aliases = {len(args) - 1: 0}
    if relu2_out:
        out_shape.append(jax.ShapeDtypeStruct((M, N), BF16))
        out_specs.append(o_spec)
        nbytes += _nbytes((tm, tn), BF16)
    res = pl.pallas_call(
        body, name=name,
        grid=(M // tm, N // tn, nk),
        in_specs=in_specs, out_specs=out_specs,
        out_shape=out_shape,
        scratch_shapes=scratch,
        input_output_aliases=aliases,
        compiler_params=_params(("parallel", "parallel", "arbitrary"), nbytes),
    )(*args)
    return res if relu2_out else res[0]


def _matmul_ln(a, b, xres, g, bias, *, name, tm, tk):
    M, K = a.shape
    chips = isinstance(b, _ChipWeight)
    if chips:
        assert b.along == "rows" and K == N_CHIPS * CHIP_BLOCK
        N, tk = CHIP_BLOCK, K
    else:
        N = b.shape[1]
    tm, tk = min(tm, M), min(tk, K)
    assert M % tm == 0 and K % tk == 0 and N == D_MODEL
    nk = K // tk

    def body(a_ref, b_ref, x_ref, g_ref, bias_ref, y_ref, y16_ref, xh_ref, rs_ref, *scr):
        if chips:
            p = _dot(a_ref[:, 0:CHIP_BLOCK].astype(BF16), b_ref[0])
            for c in range(1, N_CHIPS):
                p = p + _dot(a_ref[:, c * CHIP_BLOCK:(c + 1) * CHIP_BLOCK].astype(BF16), b_ref[c])
        else:
            p = _dot(a_ref[...].astype(BF16), b_ref[...].astype(BF16))

        def finish(acc):
            r = ALPHA * x_ref[...] + acc
            mu = jnp.mean(r, axis=-1, keepdims=True)
            xc = r - mu
            var = jnp.mean(xc * xc, axis=-1, keepdims=True)
            rstd = lax.rsqrt(var + LN_EPS)
            xh = xc * rstd
            y = xh * g_ref[...] + bias_ref[...]
            y_ref[...] = y
            y16_ref[...] = y.astype(BF16)
            xh_ref[...] = xh
            rs_ref[...] = rstd

        if nk == 1:
            finish(p)
        else:
            acc_ref = scr[0]
            k = pl.program_id(1)

            @pl.when(k == 0)
            def _():
                acc_ref[...] = p

            @pl.when(k > 0)
            def _():
                acc_ref[...] += p

            @pl.when(k == nk - 1)
            def _():
                finish(acc_ref[...])

    row = pl.BlockSpec((tm, N), lambda i, k: (i, 0))
    vec = pl.BlockSpec((1, N), lambda i, k: (0, 0))
    if chips:
        b_arr, rb = b.arr, b.rb
        b_spec = pl.BlockSpec((N_CHIPS, CHIP_BLOCK, CHIP_BLOCK), lambda i, k: (0, rb, 0))
    else:
        b_arr = b
        b_spec = pl.BlockSpec((tk, N), lambda i, k: (k, 0))
    nbytes = _nbytes((tm, tk), a.dtype) + _nbytes((tk, N), b_arr.dtype) + 6 * _nbytes((tm, N), F32)
    scratch = [pltpu.VMEM((tm, N), F32)] if nk > 1 else []
    return pl.pallas_call(
        body, name=name,
        grid=(M // tm, nk),
        in_specs=[pl.BlockSpec((tm, tk), lambda i, k: (i, k)), b_spec, row, vec, vec],
        out_specs=[row, row, row, pl.BlockSpec((tm, 1), lambda i, k: (i, 0))],
        out_shape=[jax.ShapeDtypeStruct((M, N), F32), jax.ShapeDtypeStruct((M, N), BF16),
                   jax.ShapeDtypeStruct((M, N), F32), jax.ShapeDtypeStruct((M, 1), F32)],
        scratch_shapes=scratch,
        compiler_params=_params(("parallel", "arbitrary"), nbytes),
    )(a, b_arr, xres, g.reshape(1, N), bias.reshape(1, N))


def _ln_bwd(dy, xhat, rstd, g, *, name, tm):
    M, N = dy.shape
    tm = min(tm, M)

    def body(dy_ref, xh_ref, rs_ref, g_ref, dr_ref, dr16_ref, dg_ref, db_ref):
        i = pl.program_id(0)
        dyv = dy_ref[...]
        xh = xh_ref[...]
        dxh = dyv * g_ref[...]
        m1 = jnp.mean(dxh, axis=-1, keepdims=True)
        m2 = jnp.mean(dxh * xh, axis=-1, keepdims=True)
        dr = rs_ref[...] * (dxh - m1 - xh * m2)
        dr_ref[...] = dr
        dr16_ref[...] = dr.astype(BF16)
        pg = _sum8(dyv * xh)
        pb = _sum8(dyv)

        @pl.when(i == 0)
        def _():
            dg_ref[...] = pg
            db_ref[...] = pb

        @pl.when(i > 0)
        def _():
            dg_ref[...] += pg
            db_ref[...] += pb

    row = pl.BlockSpec((tm, N), lambda i: (i, 0))
    acc = pl.BlockSpec((8, N), lambda i: (0, 0))
    return pl.pallas_call(
        body, name=name, grid=(M // tm,),
        in_specs=[row, row, pl.BlockSpec((tm, 1), lambda i: (i, 0)), pl.BlockSpec((1, N), lambda i: (0, 0))],
        out_specs=[row, row, acc, acc],
        out_shape=[jax.ShapeDtypeStruct((M, N), F32), jax.ShapeDtypeStruct((M, N), BF16),
                   jax.ShapeDtypeStruct((8, N), F32), jax.ShapeDtypeStruct((8, N), F32)],
        compiler_params=_params(("arbitrary",), 5 * _nbytes((tm, N), F32)),
    )(dy, xhat, rstd, g.reshape(1, N))


def _loss_grad(y, tgt, *, name, tm):
    M, N = y.shape
    tm = min(tm, M)

    def body(y_ref, t_ref, dy_ref, l_ref):
        i = pl.program_id(0)
        d = y_ref[...] - t_ref[...]
        dy_ref[...] = d * (1.0 / N)
        pl_ = _sum8(d * d) * (0.5 / N)

        @pl.when(i == 0)
        def _():
            l_ref[...] = pl_

        @pl.when(i > 0)
        def _():
            l_ref[...] += pl_

    row = pl.BlockSpec((tm, N), lambda i: (i, 0))
    return pl.pallas_call(
        body, name=name, grid=(M // tm,),
        in_specs=[row, row], out_specs=[row, pl.BlockSpec((8, N), lambda i: (0, 0))],
        out_shape=[jax.ShapeDtypeStruct((M, N), F32), jax.ShapeDtypeStruct((8, N), F32)],
        compiler_params=_params(("arbitrary",), 3 * _nbytes((tm, N), F32)),
    )(y, tgt)


def _tri(n, kind):
    j = lax.broadcasted_iota(jnp.int32, (2 * n, n), 0) % n
    s = lax.broadcasted_iota(jnp.int32, (2 * n, n), 1)
    return ((j > s) if kind == "after" else (j < s)).astype(BF16)


LOG2E = 1.4426950408889634
DEAD = -104.0
NOT_VISITED = -1e30


def _log_terms(z):
    lse = jnp.log(1.0 + jnp.exp2(jnp.abs(z) * (-LOG2E)))
    logsig = jnp.minimum(z, 0.0) - lse
    return logsig, logsig - z


def _cumsum_mm(x, u2_ref):
    hi, lo = _split(x)
    return _dot(jnp.concatenate([hi, lo], axis=1), u2_ref[...])


def _head_rows(x2, scale):
    lane = lax.broadcasted_iota(jnp.int32, (1, 128), 1)
    zero = jnp.zeros_like(x2)
    both = jnp.concatenate([jnp.where(lane < HEAD_DIM, x2, zero), jnp.where(lane >= HEAD_DIM, x2, zero)], axis=0)
    return both * scale


def _causal_mask(i, ks, tq, tk):
    row = lax.broadcasted_iota(jnp.int32, (2 * tq, tk), 0)
    row = i * tq + jnp.where(row >= tq, row - tq, row)
    col = lax.broadcasted_iota(jnp.int32, (2 * tq, tk), 1)
    return (ks + col) < row


def _attn_fwd(qkv, *, name, tq, tk):
    S = qkv.shape[0]
    tq = tk = min(tq, tk, S)
    assert S % tq == 0 and S // tk <= 128
    tri = _tri(tk, "after")

    def body(q_ref, k_ref, v_ref, u_ref, o_ref, c_ref, qcat, oacc, cacc, call, ls_buf, tl_buf, l0_buf):
        i = pl.program_id(1)
        lane = lax.broadcasted_iota(jnp.int32, (1, 128), 1)
        qcat[...] = _head_rows(q_ref[...], SCALE)
        oacc[...] = jnp.zeros_like(oacc)
        cacc[...] = jnp.zeros_like(cacc)
        call[...] = jnp.full_like(call, NOT_VISITED)

        def scores(kb, masked, slot):
            ks = pl.multiple_of(jnp.maximum(kb, 0) * tk, tk)
            z = _dot(qcat[...], k_ref[pl.ds(ks, tk), :], NT)
            logsig, lom = _log_terms(z)
            if masked:
                msk = jnp.logical_and(_causal_mask(i, ks, tq, tk), kb >= 0)
                lom = jnp.where(msk, lom, 0.0)
                logsig = jnp.where(msk, logsig, -1e30)
            ls_buf[slot] = logsig
            tl_buf[slot] = _cumsum_mm(lom, u_ref)
            l0_buf[slot] = lom[:, 0:1]

        def weights(kb, slot):
            ks = pl.multiple_of(jnp.maximum(kb, 0) * tk, tk)
            tl = tl_buf[slot]
            c = cacc[...]
            call[...] = jnp.where(lane == kb, c, call[...])
            a = jnp.exp(ls_buf[slot] + tl + c).astype(BF16)
            oacc[...] += _dot(a, v_ref[pl.ds(ks, tk), :])
            cacc[...] = c + tl[:, 0:1] + l0_buf[slot]

        def pair(kb, masked):
            scores(kb, masked, 0)
            scores(kb - 1, masked, 1)
            weights(kb, 0)
            weights(kb - 1, 1)

        pair(i, True)

        def live(state):
            t, cmax = state
            return jnp.logical_and(t < (i - 1) // 2, cmax > DEAD)

        def trip(state):
            t, _ = state
            pair(i - 2 - 2 * t, False)
            return t + 1, jnp.max(cacc[...])

        t_end, cmax = lax.while_loop(live, trip, (0, jnp.max(cacc[...])))

        @pl.when(jnp.logical_and(jnp.logical_and(i >= 2, i % 2 == 0), jnp.logical_and(t_end == (i - 1) // 2, cmax > DEAD)))
        def _():
            pair(0, True)

        o_ref[...] = jnp.where(lane < HEAD_DIM, oacc[0:tq], oacc[tq:2 * tq])
        c_ref[...] = jnp.concatenate([call[0:tq], call[tq:2 * tq]], axis=1)

    nbytes = (_nbytes((tq, 128), BF16) + 2 * _nbytes((S, 128), BF16) + _nbytes((2 * tk, tk), BF16)
              + 8 * _nbytes((tq, 128), F32) + 14 * _nbytes((2 * tq, tk), F32))
    return pl.pallas_call(
        body, name=name, grid=(4, S // tq),
        in_specs=[pl.BlockSpec((tq, 128), lambda j, i: (i, j)),
                  pl.BlockSpec((S, 128), lambda j, i: (0, 4 + j)),
                  pl.BlockSpec((S, 128), lambda j, i: (0, 8 + j)),
                  pl.BlockSpec((2 * tk, tk), lambda j, i: (0, 0))],
        out_specs=[pl.BlockSpec((tq, 128), lambda j, i: (i, j)),
                   pl.BlockSpec((tq, 256), lambda j, i: (i, j))],
        out_shape=[jax.ShapeDtypeStruct((S, D_SB), F32), jax.ShapeDtypeStruct((S, 1024), F32)],
        scratch_shapes=[pltpu.VMEM((2 * tq, 128), BF16), pltpu.VMEM((2 * tq, 128), F32),
                        pltpu.VMEM((2 * tq, 1), F32), pltpu.VMEM((2 * tq, 128), F32),
                        pltpu.VMEM((2, 2 * tq, tk), F32), pltpu.VMEM((2, 2 * tq, tk), F32),
                        pltpu.VMEM((2, 2 * tq, 1), F32)],
        compiler_params=_params(("parallel", "arbitrary"), nbytes),
    )(qkv, qkv, qkv, tri)


def _attn_bwd(qkv, carry, do, *, name, tq, tk):
    S = qkv.shape[0]
    tq = tk = min(tq, tk, S)
    assert S % tq == 0 and S // tk <= 128
    nkb = S // tk
    nq = S // tq
    tri_after = _tri(tk, "after")
    tri_before = _tri(tk, "before")

    def body(q_ref, k_ref, v_ref, c_ref, do_ref, ua_ref, ub_ref, dq_ref, dk_ref, dv_ref,
             qcat, docat, qcat_t, docat_t, ccat, dqacc, pacc, dkt, dvt, ls_buf, tl_buf, da_buf):
        i = pl.program_id(1)
        lane = lax.broadcasted_iota(jnp.int32, (1, 128), 1)
        sub = lax.broadcasted_iota(jnp.int32, (128, 1), 0)
        q2 = q_ref[...]
        do2 = do_ref[...]
        qcat[...] = _head_rows(q2, SCALE)
        docat[...] = _head_rows(do2, 1.0).astype(BF16)
        qt = q2.astype(F32).T * SCALE
        dot_ = do2.T
        qcat_t[...] = jnp.concatenate([jnp.where(sub < HEAD_DIM, qt, 0.0), jnp.where(sub >= HEAD_DIM, qt, 0.0)],
                                      axis=1).astype(BF16)
        docat_t[...] = jnp.concatenate([jnp.where(sub < HEAD_DIM, dot_, 0.0), jnp.where(sub >= HEAD_DIM, dot_, 0.0)],
                                       axis=1).astype(BF16)
        ccat[0:tq] = c_ref[:, 0:128]
        ccat[tq:2 * tq] = c_ref[:, 128:256]

        @pl.when(i == 0)
        def _():
            dkt[...] = jnp.zeros_like(dkt)
            dvt[...] = jnp.zeros_like(dvt)

        dqacc[...] = jnp.zeros_like(dqacc)
        pacc[...] = jnp.zeros_like(pacc)

        def scores(kb, masked, slot):
            ks = pl.multiple_of(jnp.maximum(kb, 0) * tk, tk)
            z = _dot(qcat[...], k_ref[pl.ds(ks, tk), :], NT)
            logsig, lom = _log_terms(z)
            if masked:
                msk = jnp.logical_and(_causal_mask(i, ks, tq, tk), kb >= 0)
                lom = jnp.where(msk, lom, 0.0)
                logsig = jnp.where(msk, logsig, -1e30)
            ls_buf[slot] = logsig
            tl_buf[slot] = _cumsum_mm(lom, ua_ref)
            da_buf[slot] = _dot(docat[...], v_ref[pl.ds(ks, tk), :], NT)

        def grads(kb, slot):
            kbc = jnp.maximum(kb, 0)
            ks = pl.multiple_of(kbc * tk, tk)
            logsig = ls_buf[slot]
            c = jnp.sum(jnp.where(lane == kb, ccat[...], 0.0), axis=1, keepdims=True)
            a = jnp.exp(logsig + tl_buf[slot] + c)
            g = a * da_buf[slot]
            before = _cumsum_mm(g, ub_ref)
            pc = pacc[...]
            dz = g - jnp.exp(logsig) * (g + before + pc)
            dzb = dz.astype(BF16)
            dqacc[...] += _dot(dzb, k_ref[pl.ds(ks, tk), :])
            dkt[kbc] += _dot(qcat_t[...], dzb)
            dvt[kbc] += _dot(docat_t[...], a.astype(BF16))
            pacc[...] = pc + before[:, tk - 1:tk] + g[:, tk - 1:tk]

        def pair(kb, masked):
            scores(kb, masked, 0)
            scores(kb + 1, masked, 1)
            grads(kb, 0)
            grads(kb + 1, 1)

        reach = jnp.max(ccat[...], axis=0, keepdims=True)
        first = jnp.min(jnp.where(reach > DEAD, lane, 128).astype(F32)).astype(jnp.int32)
        first = jnp.minimum(first, i)
        start = first - (i - first + 1) % 2

        @pl.when(jnp.logical_and(start < 0, i >= 2))
        def _():
            pair(-1, True)

        k0 = jnp.where(start < 0, 1, start)

        def loop(t, carry_):
            pair(k0 + 2 * t, False)
            return carry_

        lax.fori_loop(0, jnp.maximum((i - 1 - k0) // 2, 0), loop, 0)
        pair(i - 1, True)
        dq_ref[...] = (jnp.where(lane < HEAD_DIM, dqacc[0:tq], dqacc[tq:2 * tq]) * SCALE).astype(BF16)

        @pl.when(i == nq - 1)
        def _():
            for kb in range(nkb):
                dk_ref[kb * tk:(kb + 1) * tk, :] = dkt[kb].T.astype(BF16)
                dv_ref[kb * tk:(kb + 1) * tk, :] = dvt[kb].T.astype(BF16)

    nbytes = (_nbytes((tq, 128), BF16) + 2 * _nbytes((S, 128), BF16) + 2 * _nbytes((2 * tk, tk), BF16)
              + 12 * _nbytes((tq, 128), F32) + 4 * _nbytes((S, 128), F32) + 14 * _nbytes((2 * tq, tk), F32))
    blk = pl.BlockSpec((tq, 128), lambda j, i: (i, j))
    full = pl.BlockSpec((S, 128), lambda j, i: (0, j))
    tri_spec = pl.BlockSpec((2 * tk, tk), lambda j, i: (0, 0))
    dq, dk, dv = pl.pallas_call(
        body, name=name, grid=(4, nq),
        in_specs=[blk,
                  pl.BlockSpec((S, 128), lambda j, i: (0, 4 + j)),
                  pl.BlockSpec((S, 128), lambda j, i: (0, 8 + j)),
                  pl.BlockSpec((tq, 256), lambda j, i: (i, j)),
                  blk, tri_spec, tri_spec],
        out_specs=[blk, full, full],
        out_shape=[jax.ShapeDtypeStruct((S, D_SB), BF16)] * 3,
        scratch_shapes=[pltpu.VMEM((2 * tq, 128), BF16), pltpu.VMEM((2 * tq, 128), BF16),
                        pltpu.VMEM((128, 2 * tq), BF16), pltpu.VMEM((128, 2 * tq), BF16),
                        pltpu.VMEM((2 * tq, 128), F32), pltpu.VMEM((2 * tq, 128), F32), pltpu.VMEM((2 * tq, 1), F32),
                        pltpu.VMEM((nkb, 128, tk), F32), pltpu.VMEM((nkb, 128, tk), F32),
                        pltpu.VMEM((2, 2 * tq, tk), F32), pltpu.VMEM((2, 2 * tq, tk), F32),
                        pltpu.VMEM((2, 2 * tq, tk), F32)],
        compiler_params=_params(("parallel", "arbitrary"), nbytes),
    )(qkv, qkv, qkv, carry, do, tri_after, tri_before)
    return dq, dk, dv


def _group_mats():
    lanes = jnp.arange(D_MODEL) // HEAD_DIM
    gs = (lanes[:, None] == jnp.arange(128)[None, :]).astype(BF16)
    return gs, gs.T


def _group_sum_bcast(x, gs, gb):
    hi, lo = _split(x)
    s = _dot(hi, gs) + _dot(lo, gs)
    return _bcast(s, gb)


def _bcast(s, gb):
    hi, lo = _split(s)
    return _dot(hi, gb) + _dot(lo, gb)


def _pool_lane_consts():
    lane = lax.broadcasted_iota(jnp.int32, (1, D_POOL), 1)
    grp = lane // (D_POOL // 4)
    win = jnp.where(grp == 0, 2, jnp.where(grp == 1, 4, jnp.where(grp == 2, 8, 16)))
    return grp, win


def _by_group(grp, s2, s4, s8, s16):
    return jnp.where(grp == 0, s2, jnp.where(grp == 1, s4, jnp.where(grp == 2, s8, s16)))


def _mixers(i, ts, prev_ref, cur_ref, cw_ref, pw_ref, ps_ref):
    cur = cur_ref[...]
    prev = jnp.where(i == 0, 0.0, prev_ref[...])
    ext = jnp.concatenate([prev, cur], axis=0)
    n = HALO + ts

    def back(a, k):
        return pltpu.roll(a, k, 0)

    u = ext[:, D_CONV:2 * D_CONV] * ext[:, 2 * D_CONV:3 * D_CONV]
    p = ext[:, 3 * D_CONV:]
    cv = (cw_ref[0:1, :] * back(u, 2) + cw_ref[1:2, :] * back(u, 1) + cw_ref[2:3, :] * u)[HALO:]
    s2 = p + back(p, 1)
    s4 = s2 + back(s2, 2)
    s8 = s4 + back(s4, 4)
    s16 = s8 + back(s8, 8)
    grp, win = _pool_lane_consts()
    t1 = i * ts + 1 + lax.broadcasted_iota(jnp.int32, (ts, 1), 0)
    cnt = jnp.minimum(t1, win).astype(F32)
    pooled = _by_group(grp, s2, s4, s8, s16)[HALO:] / cnt - p[HALO:]
    yp = _dot(pooled.astype(BF16), pw_ref[...])
    del n
    return dict(b=cur[:, 0:D_CONV], u=u, cv=cv, pooled=pooled, yp=yp, cnt=cnt,
                conv_out=cur[:, 0:D_CONV] * cv, pool_out=yp * ps_ref[...])


def _halo_specs(ts, S, width):
    nb = ts // HALO
    last = S // HALO - 1
    prev = pl.BlockSpec((HALO, width), lambda i: (jnp.maximum(i * nb - 1, 0), 0))
    nxt = pl.BlockSpec((HALO, width), lambda i: (jnp.minimum((i + 1) * nb, last), 0))
    return prev, nxt


def _mixer_fwd(rest, attn, cw8, pwbd, ps, gain, *, name, ts):
    S = rest.shape[0]
    ts = min(ts, S)
    gs, gb = _group_mats()

    def body(prev_ref, cur_ref, attn_ref, cw_ref, pw_ref, ps_ref, gain_ref, gs_ref, gb_ref, o_ref):
        i = pl.program_id(0)
        f = _mixers(i, ts, prev_ref, cur_ref, cw_ref, pw_ref, ps_ref)
        mix = jnp.concatenate([attn_ref[...], f["conv_out"], f["pool_out"]], axis=1)
        ss = _group_sum_bcast(mix * mix, gs_ref[...], gb_ref[...])
        rinv = lax.rsqrt(ss * (1.0 / HEAD_DIM) + RMS_EPS)
        o_ref[...] = (mix * rinv * gain_ref[...]).astype(BF16)

    prev, _ = _halo_specs(ts, S, D_REST)
    row = lambda w: pl.BlockSpec((ts, w), lambda i: (i, 0))
    const = lambda a: pl.BlockSpec(a.shape, lambda i: (0, 0))
    nbytes = 12 * _nbytes((ts + HALO, D_REST), F32)
    return pl.pallas_call(
        body, name=name, grid=(S // ts,),
        in_specs=[prev, row(D_REST), row(D_SB), const(cw8), const(pwbd), const(ps), const(gain), const(gs), const(gb)],
        out_specs=row(D_MODEL),
        out_shape=jax.ShapeDtypeStruct((S, D_MODEL), BF16),
        compiler_params=_params(("parallel",), nbytes),
    )(rest, rest, attn, cw8, pwbd, ps, gain, gs, gb)


def _mixer_bwd1(dmixn, rest, attn, cw8, pwbd, ps, gain, *, name, ts):
    S = rest.shape[0]
    ts = min(ts, S)
    gs, gb = _group_mats()

    def body(dm_ref, prev_ref, cur_ref, attn_ref, cw_ref, pw_ref, ps_ref, gain_ref, gs_ref, gb_ref,
             da_ref, aux_ref, dg_ref, dsc_ref, dcw_ref, dpw_ref):
        i = pl.program_id(0)
        f = _mixers(i, ts, prev_ref, cur_ref, cw_ref, pw_ref, ps_ref)
        mix = jnp.concatenate([attn_ref[...], f["conv_out"], f["pool_out"]], axis=1)
        gsm, gbm = gs_ref[...], gb_ref[...]
        ss = _group_sum_bcast(mix * mix, gsm, gbm)
        rinv = lax.rsqrt(ss * (1.0 / HEAD_DIM) + RMS_EPS)
        dm = dm_ref[...]
        xn = mix * rinv
        dyg = dm * gain_ref[...]
        gm = _group_sum_bcast(dyg * xn, gsm, gbm) * (1.0 / HEAD_DIM)
        dmix = rinv * (dyg - xn * gm)
        da_ref[...] = dmix[:, 0:D_SB]
        dco = dmix[:, D_SB:D_SB + D_CONV]
        dpo = dmix[:, D_SB + D_CONV:]
        dcv = dco * f["b"]
        dyp = dpo * ps_ref[...]
        dpooled = _dot(dyp.astype(BF16), pw_ref[...], NT)
        aux_ref[...] = jnp.concatenate([dco * f["cv"], dcv, dpooled / f["cnt"], dpooled], axis=1)
        u = f["u"]
        parts = [
            _sum8(dm * xn),
            _sum8(dpo * f["yp"]),
            jnp.concatenate([_sum8(dcv * pltpu.roll(u, 2, 0)[HALO:]), _sum8(dcv * pltpu.roll(u, 1, 0)[HALO:]),
                             _sum8(dcv * u[HALO:])], axis=0),
            _dot(f["pooled"].astype(BF16), dyp.astype(BF16), TN),
        ]
        outs = [dg_ref, dsc_ref, dcw_ref, dpw_ref]

        @pl.when(i == 0)
        def _():
            for o, v in zip(outs, parts):
                o[...] = v

        @pl.when(i > 0)
        def _():
            for o, v in zip(outs, parts):
                o[...] += v

    prev, _ = _halo_specs(ts, S, D_REST)
    row = lambda w: pl.BlockSpec((ts, w), lambda i: (i, 0))
    const = lambda a: pl.BlockSpec(a.shape, lambda i: (0, 0))
    acc = lambda r_, w: pl.BlockSpec((r_, w), lambda i: (0, 0))
    nbytes = 16 * _nbytes((ts + HALO, D_REST), F32)
    return pl.pallas_call(
        body, name=name, grid=(S // ts,),
        in_specs=[row(D_MODEL), prev, row(D_REST), row(D_SB), const(cw8), const(pwbd), const(ps), const(gain),
                  const(gs), const(gb)],
        out_specs=[row(D_SB), row(D_REST), acc(8, D_MODEL), acc(8, D_POOL), acc(24, D_CONV), acc(D_POOL, D_POOL)],
        out_shape=[jax.ShapeDtypeStruct((S, D_SB), F32), jax.ShapeDtypeStruct((S, D_REST), F32),
                   jax.ShapeDtypeStruct((8, D_MODEL), F32), jax.ShapeDtypeStruct((8, D_POOL), F32),
                   jax.ShapeDtypeStruct((24, D_CONV), F32), jax.ShapeDtypeStruct((D_POOL, D_POOL), F32)],
        compiler_params=_params(("arbitrary",), nbytes),
    )(dmixn, rest, rest, attn, cw8, pwbd, ps, gain, gs, gb)


def _mixer_bwd2(aux, rest, cw8, *, name, ts):
    S = rest.shape[0]
    ts = min(ts, S)
    nblk = S // ts

    def body(cur_ref, nxt_ref, rest_ref, cw_ref, o_ref):
        i = pl.program_id(0)
        cur = cur_ref[...]
        nxt = jnp.where(i == nblk - 1, 0.0, nxt_ref[...])
        ext = jnp.concatenate([cur, nxt], axis=0)
        n = ts + HALO

        def fwd(a, k):
            return pltpu.roll(a, n - k, 0)

        dcv = ext[:, D_CONV:2 * D_CONV]
        dps = ext[:, 2 * D_CONV:3 * D_CONV]
        du = (cw_ref[2:3, :] * dcv + cw_ref[1:2, :] * fwd(dcv, 1) + cw_ref[0:1, :] * fwd(dcv, 2))[0:ts]
        f2 = dps + fwd(dps, 1)
        f4 = f2 + fwd(f2, 2)
        f8 = f4 + fwd(f4, 4)
        f16 = f8 + fwd(f8, 8)
        grp, _ = _pool_lane_consts()
        dp = _by_group(grp, f2, f4, f8, f16)[0:ts] - cur[:, 3 * D_CONV:]
        rest_v = rest_ref[...]
        c_gate = rest_v[:, D_CONV:2 * D_CONV]
        h = rest_v[:, 2 * D_CONV:3 * D_CONV]
        o_ref[...] = jnp.concatenate([cur[:, 0:D_CONV], du * h, du * c_gate, dp], axis=1).astype(BF16)

    _, nxt = _halo_specs(ts, S, D_REST)
    row = pl.BlockSpec((ts, D_REST), lambda i: (i, 0))
    return pl.pallas_call(
        body, name=name, grid=(nblk,),
        in_specs=[row, nxt, row, pl.BlockSpec(cw8.shape, lambda i: (0, 0))],
        out_specs=row,
        out_shape=jax.ShapeDtypeStruct((S, D_REST), BF16),
        compiler_params=_params(("parallel",), 10 * _nbytes((ts + HALO, D_REST), F32)),
    )(aux, aux, rest, cw8)


def _block_diag(pw):
    wide = jnp.tile(pw.reshape(256, 64), (1, 4))
    grp = jnp.arange(256) // 64
    return jnp.where(grp[:, None] == grp[None, :], wide, 0.0)


def _rows8(v, rows=8):
    return jnp.pad(v, ((0, rows - v.shape[0]), (0, 0)))


TILES = dict(tm=512, ts=512, tq=256, tk=256)


def _local_step(x, tgt, w, t=None, grad_pack=None):
    t = dict(TILES, **(t or {}))
    gp = None if grad_pack is None else grad_pack[0]
    tm, ts, tq, tk = t["tm"], t["ts"], t["tq"], t["tk"]
    big = dict(tm=1024, tn=1024)
    saved = []
    xl, xl16 = x, x.astype(BF16)
    for l in range(DEPTH):
        n = f"l{l}_"
        wq, wr = w["w_in"][l][:, :D_QKV], w["w_in"][l][:, D_QKV:]
        qkv = _matmul(xl16, wq, name=n + "proj_qkv", tm=1024, tn=D_QKV, tk=1024, out_dtype=BF16)
        rest = _matmul(xl16, wr, name=n + "proj_rest", tk=1024, **big)
        attn, carry = _attn_fwd(qkv, name=n + "attn_fwd", tq=tq, tk=tk)
        cw8 = _rows8(w["conv_w"][l])
        pwbd = _block_diag(w["pool_w"][l]).astype(BF16)
        ps = w["pool_scale"][l].reshape(1, D_POOL)
        gain = w["mix_norm_g"][l].reshape(1, D_MODEL)
        mixn = _mixer_fwd(rest, attn, cw8, pwbd, ps, gain, name=n + "mixer_fwd", ts=ts)
        x1, x1_16, xh1, rs1 = _matmul_ln(mixn, w["w_o"][l], xl, w["ln1_g"][l], w["ln1_b"][l], name=n + "wo_ln",
                                         tm=tm, tk=1024)
        hpre, hid = _matmul(x1_16, w["w_up"][l], name=n + "ffn_up", tk=1024, relu2_out=True, **big)
        x2, x2_16, xh2, rs2 = _matmul_ln(hid, w["w_down"][l], x1, w["ln2_g"][l], w["ln2_b"][l],
                                         name=n + "ffn_down_ln", tm=tm // 2, tk=D_FF)
        saved.append(dict(xin16=xl16, wq=wq, wr=wr, qkv=qkv, rest=rest, attn=attn, carry=carry, cw8=cw8, pwbd=pwbd,
                          ps=ps, gain=gain, mixn=mixn, x1_16=x1_16, xh1=xh1, rs1=rs1, hpre=hpre, hid=hid, xh2=xh2,
                          rs2=rs2))
        xl, xl16 = x2, x2_16

    dy, lsum = _loss_grad(xl, tgt, name="loss_grad", tm=tm)
    grads = {k: [None] * DEPTH for k in
             ("w_in", "conv_w", "pool_w", "pool_scale", "mix_norm_g", "w_o", "ln1_g", "ln1_b", "w_up", "w_down",
              "ln2_g", "ln2_b")}
    dw = dict(tk=2048, ta=True, out_dtype=BF16, **big)
    for l in reversed(range(DEPTH)):
        n = f"l{l}_"
        s = saved[l]
        dr2, dr2_16, dg2, db2 = _ln_bwd(dy, s["xh2"], s["rs2"], w["ln2_g"][l], name=n + "ln2_bwd", tm=tm)
        dhpre = _matmul(dr2_16, w["w_down"][l], name=n + "ffn_down_dx", tk=1024, tb=True, out_dtype=BF16,
                        epi="drelu2", e=s["hpre"], **big)
        if gp is None:
            grads["w_down"][l] = _matmul(s["hid"], dr2_16, name=n + "ffn_down_dw", **dw)
        else:
            gp = _matmul(s["hid"], dr2_16, name=n + "ffn_down_dw", out_chips=_ChipWeight(gp, grad_pack[2] + l, "rows"),
                         **dw)
        dx1 = _matmul(dhpre, w["w_up"][l], name=n + "ffn_up_dx", tm=512, tn=1024, tk=D_FF, tb=True,
                      epi="add", e=dr2, e_scale=ALPHA)
        if gp is None:
            grads["w_up"][l] = _matmul(s["x1_16"], dhpre, name=n + "ffn_up_dw", **dw)
        else:
            gp = _matmul(s["x1_16"], dhpre, name=n + "ffn_up_dw", out_chips=_ChipWeight(gp, grad_pack[1] + l, "cols"),
                         **dw)
        dr1, dr1_16, dg1, db1 = _ln_bwd(dx1, s["xh1"], s["rs1"], w["ln1_g"][l], name=n + "ln1_bwd", tm=tm)
        dmixn = _matmul(dr1_16, w["w_o"][l], name=n + "wo_dx", tk=1024, tb=True, **big)
        grads["w_o"][l] = _matmul(s["mixn"], dr1_16, name=n + "wo_dw", **dw)
        d_attn, aux, dgain, dsc, dcw, dpw = _mixer_bwd1(dmixn, s["rest"], s["attn"], s["cw8"], s["pwbd"], s["ps"],
                                                        s["gain"], name=n + "mixer_bwd1", ts=ts)
        drest = _mixer_bwd2(aux, s["rest"], s["cw8"], name=n + "mixer_bwd2", ts=ts)
        dqkv = jnp.concatenate(_attn_bwd(s["qkv"], s["carry"], d_attn, name=n + "attn_bwd", tq=tq, tk=tk), axis=1)
        dxa = _matmul(dqkv, s["wq"], name=n + "proj_qkv_dx", tk=D_QKV, tb=True, epi="add", e=dr1, e_scale=ALPHA,
                      **big)
        dy = _matmul(drest, s["wr"], name=n + "proj_rest_dx", tk=1024, tb=True, epi="add", e=dxa, e_scale=1.0, **big)
        dwq = _matmul(s["xin16"], dqkv, name=n + "proj_qkv_dw", tm=1024, tn=D_QKV, tk=1024, ta=True, out_dtype=BF16)
        dwr = _matmul(s["xin16"], drest, name=n + "proj_rest_dw", **dw)
        grads["w_in"][l] = jnp.concatenate([dwq, dwr], axis=1)
        grads["ln2_g"][l] = dg2.sum(0)
        grads["ln2_b"][l] = db2.sum(0)
        grads["ln1_g"][l] = dg1.sum(0)
        grads["ln1_b"][l] = db1.sum(0)
        grads["mix_norm_g"][l] = dgain.sum(0)
        grads["pool_scale"][l] = dsc.sum(0)
        grads["conv_w"][l] = dcw.reshape(3, 8, D_CONV).sum(1)
        grads["pool_w"][l] = jnp.stack([dpw[64 * g:64 * g + 64, 64 * g:64 * g + 64] for g in range(4)])
    grads = {k: jnp.stack(v) for k, v in grads.items() if v[0] is not None}
    if gp is not None:
        grads["pack"] = gp
    return lsum, dy, grads


ANY = pl.BlockSpec(memory_space=pl.ANY)


def _place():
    x, y, c = lax.axis_index("x"), lax.axis_index("y"), lax.axis_index("c")
    chips = [(1 - x, y), (x, 1 - y), (1 - x, 1 - y)]
    return x, y, c, chips


def _remote(src, dst, send_sems, recv_sems, k, to):
    return pltpu.make_async_remote_copy(src_ref=src, dst_ref=dst, send_sem=send_sems.at[k], recv_sem=recv_sems.at[k],
                                        device_id=to, device_id_type=MESH)


DMA_CHUNK_BYTES = 1 << 20


def _n_chunks(rows, dtype, width=D_MODEL):
    tiles = rows // 16
    want = max(1, (rows * width * jnp.dtype(dtype).itemsize) // DMA_CHUNK_BYTES)
    best = 1
    for n in range(1, tiles + 1):
        if tiles % n == 0 and n <= want:
            best = n
    return best


class _Chunked:
    def __init__(self, src, dst, send_sems, recv_sems, k, to, n):
        self.args = (send_sems, recv_sems, k, to)
        self.whole = self._one(src, dst)
        rows = src.shape[-2]
        assert rows % n == 0
        step = rows // n
        leads = [()] if len(src.shape) == 2 else [(i,) for i in range(src.shape[0])]
        self.parts = [self._one(src.at[(*lead, pl.ds(t * step, step))], dst.at[(*lead, pl.ds(t * step, step))])
                      for lead in leads for t in range(n)]

    def _one(self, src, dst):
        send_sems, recv_sems, k, to = self.args
        if to is None:
            return pltpu.make_async_copy(src, dst, recv_sems)
        return _remote(src, dst, send_sems, recv_sems, k, to)

    def like(self, src, dst):
        return self._one(src, dst)

    def start(self):
        for p in self.parts:
            p.start()

    def wait(self):
        self.whole.wait()

    def wait_send(self):
        self.whole.wait_send()

    def wait_recv(self):
        self.whole.wait_recv()


def _allgather_chips(pack, *, name):
    R, C = pack.shape
    H = R // 2
    Q = H // 2
    assert R % 64 == 0
    A, B = 0, 1

    def body(p_ref, o_ref, send_sems, recv_sems):
        x, y, c, _ = _place()
        my, kx, ky, kd = 2 * x + y, 2 * (1 - x) + y, 2 * x + (1 - y), 2 * (1 - x) + (1 - y)
        xn, yn, sib = (1 - x, y, c), (x, 1 - y, c), (x, y, 1 - c)
        n = _n_chunks(Q, pack.dtype)

        def own(ab):
            return p_ref.at[pl.ds(c * H + ab * Q, Q), :]

        def quarter(k, hc, ab):
            return o_ref.at[k, pl.ds(hc * H + ab * Q, Q), :]

        def send(src, k, ab, sem, to):
            cp = _Chunked(src, quarter(k, c, ab), send_sems, recv_sems, sem, to, n)
            cp.start()
            return cp

        def landed(sent, k, hc, ab):
            sent.like(quarter(k, hc, ab), quarter(k, hc, ab)).wait_recv()

        a_x = send(own(A), my, A, 0, xn)
        b_y = send(own(B), my, B, 3, yn)
        b_x = send(own(B), my, B, 1, xn)
        a_y = send(own(A), my, A, 4, yn)
        landed(b_y, ky, c, B)
        fb = send(quarter(ky, c, B), ky, B, 2, xn)
        landed(a_x, kx, c, A)
        fa = send(quarter(kx, c, A), kx, A, 5, yn)
        arrivals = [(kx, A, None), (ky, B, None), (kx, B, b_x), (ky, A, a_y), (kd, B, fb), (kd, A, fa)]
        passed = []
        for j, (k, ab, sent) in enumerate(arrivals):
            if sent is not None:
                landed(sent, k, c, ab)
            passed.append(_Chunked(quarter(k, c, ab), quarter(k, c, ab), send_sems, recv_sems, 6 + j, sib, n))
            passed[-1].start()
        for j, (k, ab, _) in enumerate(arrivals):
            landed(passed[j], k, 1 - c, ab)
        for cp in [a_x, b_y, b_x, a_y, fb, fa] + passed:
            cp.wait_send()

    return pl.pallas_call(
        body, name=name, in_specs=[ANY], out_specs=ANY,
        out_shape=jax.ShapeDtypeStruct((N_CHIPS, R, C), pack.dtype),
        scratch_shapes=[pltpu.SemaphoreType.DMA((12,)), pltpu.SemaphoreType.DMA((12,))],
    )(pack)


def _swap_halves(gp, *, name):
    K, R, C = gp.shape
    H = R // 2

    def body(g_ref, theirs_ref, send_sems, recv_sems):
        x, y, c, _ = _place()
        n = _n_chunks(H, gp.dtype)
        cp = _Chunked(g_ref.at[:, pl.ds((1 - c) * H, H), :], theirs_ref, send_sems, recv_sems, 0, (x, y, 1 - c), n)
        cp.start()
        cp.wait()

    return pl.pallas_call(
        body, name=name, in_specs=[ANY], out_specs=ANY, out_shape=jax.ShapeDtypeStruct((K, H, C), gp.dtype),
        scratch_shapes=[pltpu.SemaphoreType.DMA((1,)), pltpu.SemaphoreType.DMA((1,))],
    )(gp)


def _scatter_chips(part, *, name):
    K, H, C = part.shape

    def body(p_ref, o_ref, send_sems, recv_sems):
        x, y, c, chips = _place()
        n = _n_chunks(H, part.dtype)
        copies = [_Chunked(p_ref.at[2 * cx + cy], o_ref.at[j], send_sems, recv_sems, j, (cx, cy, c), n)
                  for j, (cx, cy) in enumerate(chips)]
        for cp in copies:
            cp.start()
        for cp in copies:
            cp.wait()

    return pl.pallas_call(
        body, name=name, in_specs=[ANY], out_specs=ANY,
        out_shape=jax.ShapeDtypeStruct((3, H, C), part.dtype),
        scratch_shapes=[pltpu.SemaphoreType.DMA((3,)), pltpu.SemaphoreType.DMA((3,))],
    )(part)


def _join_halves(both, *, name):
    H, C = both.shape[0] // 2, both.shape[1]

    def body(in_ref, o_ref, send_sems, recv_sems):
        x, y, c, _ = _place()
        n = _n_chunks(H, both.dtype)
        mine = pl.ds(c * H, H)
        theirs = pl.ds((1 - c) * H, H)
        cp = _Chunked(in_ref.at[mine, :], o_ref.at[mine, :], send_sems, recv_sems, 0, (x, y, 1 - c), n)
        cp.start()
        cp.wait_send()
        cp.like(in_ref.at[theirs, :], o_ref.at[theirs, :]).wait_recv()

    return pl.pallas_call(
        body, name=name, in_specs=[ANY], out_specs=ANY, input_output_aliases={0: 0},
        out_shape=jax.ShapeDtypeStruct(both.shape, both.dtype),
        scratch_shapes=[pltpu.SemaphoreType.DMA((1,)), pltpu.SemaphoreType.DMA((1,))],
    )(both)


def _allreduce_small(v, *, name):
    R, C = v.shape
    n_dev = 8

    def body(v_ref, o_ref, gat, send_sems, recv_sems):
        x, y, c, chips = _place()
        sib = (x, y, 1 - c)

        def rows(px, py, pc):
            return gat.at[4 * px + 2 * py + pc]

        gat[4 * x + 2 * y + c] = v_ref[...]
        first = [_remote(v_ref, rows(x, y, c), send_sems, recv_sems, 0, sib)]
        first += [_remote(v_ref, rows(x, y, c), send_sems, recv_sems, 1 + j, (cx, cy, c))
                  for j, (cx, cy) in enumerate(chips)]
        for cp in first:
            cp.start()
        passed = []
        for j, (cx, cy) in enumerate(chips):
            _remote(v_ref, rows(cx, cy, c), send_sems, recv_sems, 1 + j, sib).wait_recv()
            fwd = _remote(rows(cx, cy, c), rows(cx, cy, c), send_sems, recv_sems, 4 + j, sib)
            fwd.start()
            passed.append(fwd)
        _remote(v_ref, rows(x, y, 1 - c), send_sems, recv_sems, 0, sib).wait_recv()
        for j, (cx, cy) in enumerate(chips):
            _remote(v_ref, rows(cx, cy, 1 - c), send_sems, recv_sems, 4 + j, sib).wait_recv()
        for cp in first + passed:
            cp.wait_send()
        acc = gat[0]
        for d in range(1, n_dev):
            acc = acc + gat[d]
        o_ref[...] = acc

    vm = pl.BlockSpec(memory_space=pltpu.VMEM)
    return pl.pallas_call(
        body, name=name, in_specs=[vm], out_specs=vm,
        out_shape=jax.ShapeDtypeStruct((R, C), F32),
        scratch_shapes=[pltpu.VMEM((n_dev, R, C), F32), pltpu.SemaphoreType.DMA((7,)), pltpu.SemaphoreType.DMA((7,))],
    )(v)


def _add_pairs(gp, theirs, place, *, name, tr):
    K, H, C = theirs.shape
    tr = min(tr, H)
    assert H % tr == 0
    nb = H // tr

    def body(place_ref, a_ref, b_ref, o_ref):
        o_ref[...] = (a_ref[...].astype(F32) + b_ref[...].astype(F32)).astype(BF16)

    blk = pl.BlockSpec((1, tr, C), lambda k, i, p: (k, i, 0))
    mine = pl.BlockSpec((1, tr, C), lambda k, i, p: (k, i + p[1] * nb, 0))
    return pl.pallas_call(
        body, name=name,
        grid_spec=pltpu.PrefetchScalarGridSpec(num_scalar_prefetch=1, grid=(K, nb), in_specs=[mine, blk],
                                               out_specs=blk),
        out_shape=jax.ShapeDtypeStruct((K, H, C), BF16),
        compiler_params=_params(("parallel", "parallel"), 3 * _nbytes((tr, C), BF16)),
    )(place, gp, theirs)


def _add_final(gp, theirs, others, place, *, name, tr):
    K, H, C = theirs.shape
    tr = min(tr, H)
    assert H % tr == 0
    nb = H // tr

    def body(place_ref, a_ref, b_ref, o_ref_in, out_ref):
        acc = a_ref[0].astype(F32) + b_ref[0].astype(F32)
        for j in range(3):
            acc = acc + o_ref_in[j].astype(F32)
        out_ref[...] = acc

    return pl.pallas_call(
        body, name=name,
        grid_spec=pltpu.PrefetchScalarGridSpec(
            num_scalar_prefetch=1, grid=(nb,),
            in_specs=[pl.BlockSpec((1, tr, C), lambda i, p: (p[0], i + p[1] * nb, 0)),
                      pl.BlockSpec((1, tr, C), lambda i, p: (p[0], i, 0)),
                      pl.BlockSpec((3, tr, C), lambda i, p: (0, i, 0))],
            out_specs=pl.BlockSpec((tr, C), lambda i, p: (i + p[1] * nb, 0))),
        out_shape=jax.ShapeDtypeStruct((2 * H, C), F32),
        compiler_params=_params(("parallel",), 6 * _nbytes((tr, C), F32)),
    )(place, gp, theirs, others)


def _adamw(w, g, m, v, *, name, tr, row0=0):
    R, C = w.shape
    tr = min(tr, R)
    assert R % tr == 0 and row0 % tr == 0
    off = row0 // tr

    def body(w_ref, g_ref, m_ref, v_ref, go_ref, d_ref, mo_ref, vo_ref):
        gv = g_ref[...]
        m2 = ADAM_B1 * m_ref[...] + (1.0 - ADAM_B1) * gv
        v2 = ADAM_B2 * v_ref[...] + (1.0 - ADAM_B2) * jnp.square(gv)
        m_hat = m2 / (1.0 - ADAM_B1 ** ADAM_STEP)
        v_hat = v2 / (1.0 - ADAM_B2 ** ADAM_STEP)
        d_ref[...] = -ADAM_LR * (m_hat / (jnp.sqrt(v_hat) + ADAM_EPS) + ADAM_WD * w_ref[...])
        go_ref[...] = gv
        mo_ref[...] = m2
        vo_ref[...] = v2

    blk = pl.BlockSpec((tr, C), lambda i: (i, 0))
    shape = jax.ShapeDtypeStruct((R, C), F32)
    return pl.pallas_call(
        body, name=name, grid=(R // tr,),
        in_specs=[blk, pl.BlockSpec((tr, C), lambda i: (i + off, 0)), blk, blk], out_specs=[blk] * 4,
        out_shape=[shape] * 4,
        compiler_params=_params(("parallel",), 8 * _nbytes((tr, C), F32)),
    )(w, g, m, v)


BIG = ("w_up", "w_down", "w_in", "w_o")
IN_PLACE = ("w_up", "w_down")
BIG_AXIS = dict(w_in=2, w_o=1, w_up=2, w_down=1)
SMALL = ("pool_w", "pool_scale", "mix_norm_g", "ln1_g", "ln1_b", "ln2_g", "ln2_b")
CONV_ROWS = 64
SMALL_ROWS = 48


def _big_rows(shards):
    sizes = [shards[n].size // D_MODEL for n in BIG]
    starts = [sum(sizes[:i]) for i in range(len(sizes))]
    return sizes, starts


def _to_chips(a, axis):
    shape = list(a.shape)
    shape[axis:axis + 1] = [N_CHIPS, shape[axis] // N_CHIPS]
    return jnp.moveaxis(a.reshape(shape), axis, 0)


def _from_chips(a, axis):
    a = jnp.moveaxis(a, 0, axis)
    shape = list(a.shape)
    shape[axis:axis + 2] = [shape[axis] * shape[axis + 1]]
    return a.reshape(shape)


def _pad_rows(flat, rows):
    return jnp.pad(flat, (0, rows * D_MODEL - flat.shape[0])).reshape(rows, D_MODEL)


def kernel(x, w_in, conv_w, pool_w, pool_scale, mix_norm_g, w_o, ln1_g, ln1_b, w_up, w_down, ln2_g, ln2_b, loss_target, m_w_in, m_conv_w, m_pool_w, m_pool_scale, m_mix_norm_g, m_w_o, m_ln1_g, m_ln1_b, m_w_up, m_w_down, m_ln2_g, m_ln2_b, v_w_in, v_conv_w, v_pool_w, v_pool_scale, v_mix_norm_g, v_w_o, v_ln1_g, v_ln1_b, v_w_up, v_w_down, v_ln2_g, v_ln2_b):
    wts = dict(w_in=w_in, conv_w=conv_w, pool_w=pool_w, pool_scale=pool_scale, mix_norm_g=mix_norm_g, w_o=w_o,
               ln1_g=ln1_g, ln1_b=ln1_b, w_up=w_up, w_down=w_down, ln2_g=ln2_g, ln2_b=ln2_b)
    mom = dict(w_in=m_w_in, conv_w=m_conv_w, pool_w=m_pool_w, pool_scale=m_pool_scale, mix_norm_g=m_mix_norm_g,
               w_o=m_w_o, ln1_g=m_ln1_g, ln1_b=m_ln1_b, w_up=m_w_up, w_down=m_w_down, ln2_g=m_ln2_g, ln2_b=m_ln2_b)
    var = dict(w_in=v_w_in, conv_w=v_conv_w, pool_w=v_pool_w, pool_scale=v_pool_scale, mix_norm_g=v_mix_norm_g,
               w_o=v_w_o, ln1_g=v_ln1_g, ln1_b=v_ln1_b, w_up=v_w_up, w_down=v_w_down, ln2_g=v_ln2_g, ln2_b=v_ln2_b)
    chip = 2 * lax.axis_index("x") + lax.axis_index("y")
    sizes, starts = _big_rows(wts)
    big_rows = sum(sizes)

    conv_bits = lax.bitcast_convert_type(conv_w.reshape(-1), BF16).reshape(-1)
    pack = jnp.concatenate([wts[n].reshape(-1, D_MODEL).astype(BF16) for n in BIG]
                           + [_pad_rows(conv_bits, CONV_ROWS)], axis=0)
    gathered = _allgather_chips(pack, name="gather_weights")
    gathered = lax.dynamic_update_index_in_dim(gathered, pack, chip, 0)
    full = {}
    first_block = {}
    for n, size, start in zip(BIG, sizes, starts):
        if n in IN_PLACE:
            assert start % CHIP_BLOCK == 0 and size == DEPTH * CHIP_BLOCK
            first_block[n] = start // CHIP_BLOCK
            along = "cols" if BIG_AXIS[n] == 2 else "rows"
            full[n] = [_ChipWeight(gathered, first_block[n] + l, along) for l in range(DEPTH)]
        else:
            full[n] = _from_chips(gathered[:, start:start + size].reshape((N_CHIPS,) + wts[n].shape), BIG_AXIS[n])
    conv_parts = [lax.bitcast_convert_type(gathered[k, big_rows:].reshape(-1)[:2 * conv_w.size].reshape(-1, 2), F32)
                  .reshape(conv_w.shape) for k in range(N_CHIPS)]
    full["conv_w"] = jnp.concatenate(conv_parts, axis=2)
    for n in SMALL:
        full[n] = wts[n]

    gpack = jnp.zeros((N_CHIPS, big_rows, D_MODEL), BF16)
    lsum, grad_x, grads = _local_step(x[0], loss_target[0], full,
                                      grad_pack=(gpack, first_block["w_up"], first_block["w_down"]))

    others = [n for n in BIG if n not in IN_PLACE]
    rest = jnp.concatenate([_to_chips(grads[n], BIG_AXIS[n]).reshape(N_CHIPS, -1, D_MODEL) for n in others], axis=1)
    gpack = lax.dynamic_update_slice_in_dim(grads["pack"], rest.astype(BF16), starts[len(IN_PLACE)], axis=1)
    place = jnp.stack([chip, lax.axis_index("c")]).astype(jnp.int32)
    theirs = _swap_halves(gpack, name="grad_swap_cores")
    chip_sum = _add_pairs(gpack, theirs, place, name="grad_add_cores", tr=736)
    from_chips = _scatter_chips(chip_sum, name="grad_scatter_chips")
    half_sum = _add_final(gpack, theirs, from_chips, place, name="grad_add_chips", tr=736)
    gsum = _join_halves(half_sum, name="grad_join_cores")

    small_flat = jnp.concatenate([grads[n].reshape(-1) for n in SMALL] + [grads["conv_w"].reshape(-1),
                                                                          lsum.sum().reshape(1)])
    small_sum = _allreduce_small(_pad_rows(small_flat, SMALL_ROWS), name="allreduce_small").reshape(-1)
    gsmall = {}
    pos = 0
    for n in SMALL:
        gsmall[n] = small_sum[pos:pos + wts[n].size].reshape(wts[n].shape)
        pos += wts[n].size
    conv_full = small_sum[pos:pos + 4 * conv_w.size].reshape(DEPTH, 3, D_CONV)
    pos += 4 * conv_w.size
    loss = small_sum[pos]
    gsmall["conv_w"] = lax.dynamic_slice_in_dim(conv_full, chip * conv_w.shape[2], conv_w.shape[2], axis=2)

    out_g, out_d, out_m, out_v = {}, {}, {}, {}
    for n, size, start in zip(BIG, sizes, starts):
        shp = wts[n].shape
        g, row0 = gsum, start
        if shp[-1] != D_MODEL:
            g, row0 = gsum[start:start + size].reshape(-1, shp[-1]), 0
        res = _adamw(wts[n].reshape(-1, shp[-1]), g, mom[n].reshape(-1, shp[-1]), var[n].reshape(-1, shp[-1]),
                     name="adamw_" + n, tr=256, row0=row0)
        out_g[n], out_d[n], out_m[n], out_v[n] = [r.reshape(shp) for r in res]
    small_names = SMALL + ("conv_w",)
    packs = [_pad_rows(jnp.concatenate([d[n].reshape(-1) for n in small_names]), SMALL_ROWS)
             for d in (wts, gsmall, mom, var)]
    res = _adamw(*packs, name="adamw_small", tr=SMALL_ROWS)
    pos = 0
    for n in small_names:
        shp = wts[n].shape
        out_g[n], out_d[n], out_m[n], out_v[n] = [r.reshape(-1)[pos:pos + wts[n].size].reshape(shp) for r in res]
        pos += wts[n].size

    order = ("w_in", "conv_w", "pool_w", "pool_scale", "mix_norm_g", "w_o", "ln1_g", "ln1_b", "w_up", "w_down",
             "ln2_g", "ln2_b")
    return (loss, grad_x[None], *[out_g[n] for n in order], *[out_d[n] for n in order],
            *[out_m[n] for n in order], *[out_v[n] for n in order])
```

```python
import math
from typing import NamedTuple

import jax
import jax.numpy as jnp
from jax import lax
from jax.experimental import pallas as pl
from jax.experimental.pallas import tpu as pltpu

F32 = jnp.float32
BF16 = jnp.bfloat16
MESH = pl.DeviceIdType.MESH

D_MODEL = 1024
DEPTH = 2
HEAD_DIM = 64
D_SB = 512
D_CONV = 256
D_POOL = 256
D_QKV = 3 * D_SB
D_REST = 3 * D_CONV + D_POOL
D_FF = 4 * D_MODEL
ALPHA = (2 * DEPTH) ** 0.25
LN_EPS = 1e-5
RMS_EPS = 1e-6
SCALE = HEAD_DIM ** -0.5
N_CHIPS = 4
HALO = 16

ADAM_LR = 0.001
ADAM_B1 = 0.9
ADAM_B2 = 0.999
ADAM_EPS = 1e-08
ADAM_WD = 0.01
ADAM_STEP = 10

VMEM_V7X_BYTES = 64 * 1024 * 1024
VMEM_CAP_BYTES = VMEM_V7X_BYTES - 8 * 1024 * 1024


def _params(sem, block_bytes):
    limit = min(VMEM_CAP_BYTES, max(32 * 1024 * 1024, 3 * block_bytes))
    return pltpu.CompilerParams(dimension_semantics=sem, vmem_limit_bytes=limit)


def _nbytes(shape, dtype):
    return math.prod(shape) * jnp.dtype(dtype).itemsize


def _dot(a, b, dims=(((1,), (0,)), ((), ()))):
    return lax.dot_general(a, b, dims, preferred_element_type=F32)


NT = (((1,), (1,)), ((), ()))
TN = (((0,), (0,)), ((), ()))


def _split(x):
    hi = x.astype(BF16)
    lo = (x - hi.astype(F32)).astype(BF16)
    return hi, lo


def _sum8(x):
    r, c = x.shape
    return x.reshape(r // 8, 8, c).sum(axis=0)


class _ChipWeight(NamedTuple):
    arr: jax.Array
    rb: int
    along: str


CHIP_BLOCK = 1024


def _matmul(a, b, *, name, tm, tn, tk, ta=False, tb=False, out_dtype=F32,
            epi=None, e=None, e_scale=1.0, relu2_out=False, out_chips=None):
    M, K = (a.shape[1], a.shape[0]) if ta else a.shape
    chips = isinstance(b, _ChipWeight)
    split_k = chips and ((b.along == "cols") == tb)
    if chips:
        N = CHIP_BLOCK if split_k else N_CHIPS * CHIP_BLOCK
        assert K == (N_CHIPS * CHIP_BLOCK if split_k else CHIP_BLOCK) and not ta, (name, K)
        tn, tk = CHIP_BLOCK, K
    else:
        N = b.shape[0] if tb else b.shape[1]
    tm, tn, tk = min(tm, M), min(tn, N), min(tk, K)
    assert M % tm == 0 and N % tn == 0 and K % tk == 0, (name, M, N, K)
    nk = K // tk
    dims = (((0 if ta else 1,), (1 if tb else 0,)), ((), ()))
    n_in = 2 + (epi is not None) + (out_chips is not None)

    def body(*refs):
        a_ref, b_ref = refs[0], refs[1]
        e_ref = refs[2] if epi is not None else None
        o_ref = refs[n_in]
        scr = refs[-1:]
        if not chips:
            p = _dot(a_ref[...].astype(BF16), b_ref[...].astype(BF16), dims)
        elif split_k:
            p = _dot(a_ref[:, 0:CHIP_BLOCK].astype(BF16), b_ref[0], dims)
            for c in range(1, N_CHIPS):
                p = p + _dot(a_ref[:, c * CHIP_BLOCK:(c + 1) * CHIP_BLOCK].astype(BF16), b_ref[c], dims)
        else:
            p = _dot(a_ref[...].astype(BF16), b_ref[0], dims)

        def finish(acc):
            if epi == "drelu2":
                acc = acc * (2.0 * jnp.maximum(e_ref[...], 0.0))
            elif epi == "add":
                acc = acc + e_scale * e_ref[...]
            if out_chips is None:
                o_ref[...] = acc.astype(out_dtype)
            else:
                o_ref[0] = acc.astype(out_dtype)
            if relu2_out:
                refs[n_in + 1][...] = jnp.square(jnp.maximum(acc, 0.0)).astype(BF16)

        if nk == 1:
            finish(p)
        else:
            acc_ref = scr[0]
            k = pl.program_id(2)

            @pl.when(k == 0)
            def _():
                acc_ref[...] = p

            @pl.when(k > 0)
            def _():
                acc_ref[...] += p

            @pl.when(k == nk - 1)
            def _():
                finish(acc_ref[...])

    a_spec = pl.BlockSpec((tk, tm), lambda i, j, k: (k, i)) if ta else pl.BlockSpec((tm, tk), lambda i, j, k: (i, k))
    if chips:
        b_arr, rb = b.arr, b.rb
        nblk = N_CHIPS if split_k else 1
        b_spec = pl.BlockSpec((nblk, CHIP_BLOCK, CHIP_BLOCK),
                              (lambda i, j, k: (0, rb, 0)) if split_k else (lambda i, j, k: (j, rb, 0)))
    else:
        b_arr = b
        b_spec = pl.BlockSpec((tn, tk), lambda i, j, k: (j, k)) if tb else pl.BlockSpec((tk, tn), lambda i, j, k: (k, j))
    o_spec = pl.BlockSpec((tm, tn), lambda i, j, k: (i, j))
    in_specs = [a_spec, b_spec]
    args = [a, b_arr]
    nbytes = _nbytes((tm, tk), a.dtype) + _nbytes((tk, tn), b_arr.dtype) + 2 * _nbytes((tm, tn), F32)
    if epi is not None:
        in_specs.append(o_spec)
        args.append(e)
        nbytes += _nbytes((tm, tn), e.dtype)
    scratch = [pltpu.VMEM((tm, tn), F32)] if nk > 1 else []
    out_shape = [jax.ShapeDtypeStruct((M, N), out_dtype)]
    out_specs = [o_spec]
    aliases = {}
    if out_chips is not None:
        assert tm == tn == CHIP_BLOCK and not relu2_out and out_chips.arr.dtype == out_dtype
        orb = out_chips.rb
        out_specs = [pl.BlockSpec((1, CHIP_BLOCK, CHIP_BLOCK),
                                  (lambda i, j, k: (j, orb, 0)) if out_chips.along == "cols" else
                                  (lambda i, j, k: (i, orb, 0)))]
        out_shape = [jax.ShapeDtypeStruct(out_chips.arr.shape, out_dtype)]
        in_specs.append(pl.BlockSpec(memory_space=pl.ANY))
        args.append(out_chips.arr)
        aliases = {len(args) - 1: 0}
    if relu2_out:
        out_shape.append(jax.ShapeDtypeStruct((M, N), BF16))
        out_specs.append(o_spec)
        nbytes += _nbytes((tm, tn), BF16)
    res = pl.pallas_call(
        body, name=name,
        grid=(M // tm, N // tn, nk),
        in_specs=in_specs, out_specs=out_specs,
        out_shape=out_shape,
        scratch_shapes=scratch,
        input_output_aliases=aliases,
        compiler_params=_params(("parallel", "parallel", "arbitrary"), nbytes),
    )(*args)
    return res if relu2_out else res[0]


def _matmul_ln(a, b, xres, g, bias, *, name, tm, tk):
    M, K = a.shape
    chips = isinstance(b, _ChipWeight)
    if chips:
        assert b.along == "rows" and K == N_CHIPS * CHIP_BLOCK
        N, tk = CHIP_BLOCK, K
    else:
        N = b.shape[1]
    tm, tk = min(tm, M), min(tk, K)
    assert M % tm == 0 and K % tk == 0 and N == D_MODEL
    nk = K // tk

    def body(a_ref, b_ref, x_ref, g_ref, bias_ref, y_ref, y16_ref, xh_ref, rs_ref, *scr):
        if chips:
            p = _dot(a_ref[:, 0:CHIP_BLOCK].astype(BF16), b_ref[0])
            for c in range(1, N_CHIPS):
                p = p + _dot(a_ref[:, c * CHIP_BLOCK:(c + 1) * CHIP_BLOCK].astype(BF16), b_ref[c])
        else:
            p = _dot(a_ref[...].astype(BF16), b_ref[...].astype(BF16))

        def finish(acc):
            r = ALPHA * x_ref[...] + acc
            mu = jnp.mean(r, axis=-1, keepdims=True)
            xc = r - mu
            var = jnp.mean(xc * xc, axis=-1, keepdims=True)
            rstd = lax.rsqrt(var + LN_EPS)
            xh = xc * rstd
            y = xh * g_ref[...] + bias_ref[...]
            y_ref[...] = y
            y16_ref[...] = y.astype(BF16)
            xh_ref[...] = xh
            rs_ref[...] = rstd

        if nk == 1:
            finish(p)
        else:
            acc_ref = scr[0]
            k = pl.program_id(1)

            @pl.when(k == 0)
            def _():
                acc_ref[...] = p

            @pl.when(k > 0)
            def _():
                acc_ref[...] += p

            @pl.when(k == nk - 1)
            def _():
                finish(acc_ref[...])

    row = pl.BlockSpec((tm, N), lambda i, k: (i, 0))
    vec = pl.BlockSpec((1, N), lambda i, k: (0, 0))
    if chips:
        b_arr, rb = b.arr, b.rb
        b_spec = pl.BlockSpec((N_CHIPS, CHIP_BLOCK, CHIP_BLOCK), lambda i, k: (0, rb, 0))
    else:
        b_arr = b
        b_spec = pl.BlockSpec((tk, N), lambda i, k: (k, 0))
    nbytes = _nbytes((tm, tk), a.dtype) + _nbytes((tk, N), b_arr.dtype) + 6 * _nbytes((tm, N), F32)
    scratch = [pltpu.VMEM((tm, N), F32)] if nk > 1 else []
    return pl.pallas_call(
        body, name=name,
        grid=(M // tm, nk),
        in_specs=[pl.BlockSpec((tm, tk), lambda i, k: (i, k)), b_spec, row, vec, vec],
        out_specs=[row, row, row, pl.BlockSpec((tm, 1), lambda i, k: (i, 0))],
        out_shape=[jax.ShapeDtypeStruct((M, N), F32), jax.ShapeDtypeStruct((M, N), BF16),
                   jax.ShapeDtypeStruct((M, N), F32), jax.ShapeDtypeStruct((M, 1), F32)],
        scratch_shapes=scratch,
        compiler_params=_params(("parallel", "arbitrary"), nbytes),
    )(a, b_arr, xres, g.reshape(1, N), bias.reshape(1, N))


def _ln_bwd(dy, xhat, rstd, g, *, name, tm):
    M, N = dy.shape
    tm = min(tm, M)

    def body(dy_ref, xh_ref, rs_ref, g_ref, dr_ref, dr16_ref, dg_ref, db_ref):
        i = pl.program_id(0)
        dyv = dy_ref[...]
        xh = xh_ref[...]
        dxh = dyv * g_ref[...]
        m1 = jnp.mean(dxh, axis=-1, keepdims=True)
        m2 = jnp.mean(dxh * xh, axis=-1, keepdims=True)
        dr = rs_ref[...] * (dxh - m1 - xh * m2)
        dr_ref[...] = dr
        dr16_ref[...] = dr.astype(BF16)
        pg = _sum8(dyv * xh)
        pb = _sum8(dyv)

        @pl.when(i == 0)
        def _():
            dg_ref[...] = pg
            db_ref[...] = pb

        @pl.when(i > 0)
        def _():
            dg_ref[...] += pg
            db_ref[...] += pb

    row = pl.BlockSpec((tm, N), lambda i: (i, 0))
    acc = pl.BlockSpec((8, N), lambda i: (0, 0))
    return pl.pallas_call(
        body, name=name, grid=(M // tm,),
        in_specs=[row, row, pl.BlockSpec((tm, 1), lambda i: (i, 0)), pl.BlockSpec((1, N), lambda i: (0, 0))],
        out_specs=[row, row, acc, acc],
        out_shape=[jax.ShapeDtypeStruct((M, N), F32), jax.ShapeDtypeStruct((M, N), BF16),
                   jax.ShapeDtypeStruct((8, N), F32), jax.ShapeDtypeStruct((8, N), F32)],
        compiler_params=_params(("arbitrary",), 5 * _nbytes((tm, N), F32)),
    )(dy, xhat, rstd, g.reshape(1, N))


def _loss_grad(y, tgt, *, name, tm):
    M, N = y.shape
    tm = min(tm, M)

    def body(y_ref, t_ref, dy_ref, l_ref):
        i = pl.program_id(0)
        d = y_ref[...] - t_ref[...]
        dy_ref[...] = d * (1.0 / N)
        pl_ = _sum8(d * d) * (0.5 / N)

        @pl.when(i == 0)
        def _():
            l_ref[...] = pl_

        @pl.when(i > 0)
        def _():
            l_ref[...] += pl_

    row = pl.BlockSpec((tm, N), lambda i: (i, 0))
    return pl.pallas_call(
        body, name=name, grid=(M // tm,),
        in_specs=[row, row], out_specs=[row, pl.BlockSpec((8, N), lambda i: (0, 0))],
        out_shape=[jax.ShapeDtypeStruct((M, N), F32), jax.ShapeDtypeStruct((8, N), F32)],
        compiler_params=_params(("arbitrary",), 3 * _nbytes((tm, N), F32)),
    )(y, tgt)


def _tri(n, kind):
    j = lax.broadcasted_iota(jnp.int32, (2 * n, n), 0) % n
    s = lax.broadcasted_iota(jnp.int32, (2 * n, n), 1)
    return ((j > s) if kind == "after" else (j < s)).astype(BF16)


LOG2E = 1.4426950408889634
DEAD = -104.0
NOT_VISITED = -1e30


def _log_terms(z):
    lse = jnp.log(1.0 + jnp.exp2(jnp.abs(z) * (-LOG2E)))
    logsig = jnp.minimum(z, 0.0) - lse
    return logsig, logsig - z


def _cumsum_mm(x, u2_ref):
    hi, lo = _split(x)
    return _dot(jnp.concatenate([hi, lo], axis=1), u2_ref[...])


def _head_rows(x2, scale):
    lane = lax.broadcasted_iota(jnp.int32, (1, 128), 1)
    zero = jnp.zeros_like(x2)
    both = jnp.concatenate([jnp.where(lane < HEAD_DIM, x2, zero), jnp.where(lane >= HEAD_DIM, x2, zero)], axis=0)
    return both * scale


def _causal_mask(i, ks, tq, tk):
    row = lax.broadcasted_iota(jnp.int32, (2 * tq, tk), 0)
    row = i * tq + jnp.where(row >= tq, row - tq, row)
    col = lax.broadcasted_iota(jnp.int32, (2 * tq, tk), 1)
    return (ks + col) < row


def _attn_fwd(qkv, *, name, tq, tk):
    S = qkv.shape[0]
    tq = tk = min(tq, tk, S)
    assert S % tq == 0 and S // tk <= 128
    tri = _tri(tk, "after")

    def body(q_ref, k_ref, v_ref, u_ref, o_ref, c_ref, qcat, oacc, cacc, call, ls_buf, tl_buf, l0_buf):
        i = pl.program_id(1)
        lane = lax.broadcasted_iota(jnp.int32, (1, 128), 1)
        qcat[...] = _head_rows(q_ref[...], SCALE)
        oacc[...] = jnp.zeros_like(oacc)
        cacc[...] = jnp.zeros_like(cacc)
        call[...] = jnp.full_like(call, NOT_VISITED)

        def scores(kb, masked, slot):
            ks = pl.multiple_of(jnp.maximum(kb, 0) * tk, tk)
            z = _dot(qcat[...], k_ref[pl.ds(ks, tk), :], NT)
            logsig, lom = _log_terms(z)
            if masked:
                msk = jnp.logical_and(_causal_mask(i, ks, tq, tk), kb >= 0)
                lom = jnp.where(msk, lom, 0.0)
                logsig = jnp.where(msk, logsig, -1e30)
            ls_buf[slot] = logsig
            tl_buf[slot] = _cumsum_mm(lom, u_ref)
            l0_buf[slot] = lom[:, 0:1]

        def weights(kb, slot):
            ks = pl.multiple_of(jnp.maximum(kb, 0) * tk, tk)
            tl = tl_buf[slot]
            c = cacc[...]
            call[...] = jnp.where(lane == kb, c, call[...])
            a = jnp.exp(ls_buf[slot] + tl + c).astype(BF16)
            oacc[...] += _dot(a, v_ref[pl.ds(ks, tk), :])
            cacc[...] = c + tl[:, 0:1] + l0_buf[slot]

        def pair(kb, masked, masked_next):
            scores(kb, masked, 0)
            scores(kb - 1, masked_next, 1)
            weights(kb, 0)
            weights(kb - 1, 1)

        @pl.when(i == 0)
        def _():
            pair(0, True, True)

        @pl.when(i > 0)
        def _():
            pair(i, True, False)

        def live(state):
            t, cmax = state
            return jnp.logical_and(t < (i - 1) // 2, cmax > DEAD)

        def trip(state):
            t, _ = state
            pair(i - 2 - 2 * t, False, False)
            return t + 1, jnp.max(cacc[...])

        t_end, cmax = lax.while_loop(live, trip, (0, jnp.max(cacc[...])))

        left_over = jnp.logical_and(i >= 2, i % 2 == 0)
        still_live = jnp.logical_and(t_end == (i - 1) // 2, cmax > DEAD)

        @pl.when(jnp.logical_and(left_over, still_live))
        def _():
            pair(0, False, True)

        o_ref[...] = jnp.where(lane < HEAD_DIM, oacc[0:tq], oacc[tq:2 * tq])
        c_ref[...] = jnp.concatenate([call[0:tq], call[tq:2 * tq]], axis=1)

    nbytes = (_nbytes((tq, 128), BF16) + 2 * _nbytes((S, 128), BF16) + _nbytes((2 * tk, tk), BF16)
              + 8 * _nbytes((tq, 128), F32) + 14 * _nbytes((2 * tq, tk), F32))
    return pl.pallas_call(
        body, name=name, grid=(4, S // tq),
        in_specs=[pl.BlockSpec((tq, 128), lambda j, i: (i, j)),
                  pl.BlockSpec((S, 128), lambda j, i: (0, 4 + j)),
                  pl.BlockSpec((S, 128), lambda j, i: (0, 8 + j)),
                  pl.BlockSpec((2 * tk, tk), lambda j, i: (0, 0))],
        out_specs=[pl.BlockSpec((tq, 128), lambda j, i: (i, j)),
                   pl.BlockSpec((tq, 256), lambda j, i: (i, j))],
        out_shape=[jax.ShapeDtypeStruct((S, D_SB), F32), jax.ShapeDtypeStruct((S, 1024), F32)],
        scratch_shapes=[pltpu.VMEM((2 * tq, 128), BF16), pltpu.VMEM((2 * tq, 128), F32),
                        pltpu.VMEM((2 * tq, 1), F32), pltpu.VMEM((2 * tq, 128), F32),
                        pltpu.VMEM((2, 2 * tq, tk), F32), pltpu.VMEM((2, 2 * tq, tk), F32),
                        pltpu.VMEM((2, 2 * tq, 1), F32)],
        compiler_params=_params(("parallel", "arbitrary"), nbytes),
    )(qkv, qkv, qkv, tri)


def _attn_bwd(qkv, carry, do, *, name, tq, tk):
    S = qkv.shape[0]
    tq = tk = min(tq, tk, S)
    assert S % tq == 0 and S // tk <= 128
    nkb = S // tk
    nq = S // tq
    tri_after = _tri(tk, "after")
    tri_before = _tri(tk, "before")

    def body(q_ref, k_ref, v_ref, c_ref, do_ref, ua_ref, ub_ref, dq_ref, dk_ref, dv_ref,
             qcat, docat, qcat_t, docat_t, ccat, dqacc, pacc, dkt, dvt, ls_buf, tl_buf, da_buf):
        i = pl.program_id(1)
        lane = lax.broadcasted_iota(jnp.int32, (1, 128), 1)
        sub = lax.broadcasted_iota(jnp.int32, (128, 1), 0)
        q2 = q_ref[...]
        do2 = do_ref[...]
        qcat[...] = _head_rows(q2, SCALE)
        docat[...] = _head_rows(do2, 1.0).astype(BF16)
        qt = q2.astype(F32).T * SCALE
        dot_ = do2.T
        qcat_t[...] = jnp.concatenate([jnp.where(sub < HEAD_DIM, qt, 0.0), jnp.where(sub >= HEAD_DIM, qt, 0.0)],
                                      axis=1).astype(BF16)
        docat_t[...] = jnp.concatenate([jnp.where(sub < HEAD_DIM, dot_, 0.0), jnp.where(sub >= HEAD_DIM, dot_, 0.0)],
                                       axis=1).astype(BF16)
        ccat[0:tq] = c_ref[:, 0:128]
        ccat[tq:2 * tq] = c_ref[:, 128:256]

        @pl.when(i == 0)
        def _():
            dkt[...] = jnp.zeros_like(dkt)
            dvt[...] = jnp.zeros_like(dvt)

        dqacc[...] = jnp.zeros_like(dqacc)
        pacc[...] = jnp.zeros_like(pacc)

        def scores(kb, masked, slot):
            ks = pl.multiple_of(jnp.maximum(kb, 0) * tk, tk)
            z = _dot(qcat[...], k_ref[pl.ds(ks, tk), :], NT)
            logsig, lom = _log_terms(z)
            if masked:
                msk = jnp.logical_and(_causal_mask(i, ks, tq, tk), kb >= 0)
                lom = jnp.where(msk, lom, 0.0)
                logsig = jnp.where(msk, logsig, -1e30)
            ls_buf[slot] = logsig
            tl_buf[slot] = _cumsum_mm(lom, ua_ref)
            da_buf[slot] = _dot(docat[...], v_ref[pl.ds(ks, tk), :], NT)

        def grads(kb, slot):
            kbc = jnp.maximum(kb, 0)
            ks = pl.multiple_of(kbc * tk, tk)
            logsig = ls_buf[slot]
            c = jnp.sum(jnp.where(lane == kb, ccat[...], 0.0), axis=1, keepdims=True)
            a = jnp.exp(logsig + tl_buf[slot] + c)
            g = a * da_buf[slot]
            before = _cumsum_mm(g, ub_ref)
            pc = pacc[...]
            dz = g - jnp.exp(logsig) * (g + before + pc)
            dzb = dz.astype(BF16)
            dqacc[...] += _dot(dzb, k_ref[pl.ds(ks, tk), :])
            dkt[kbc] += _dot(qcat_t[...], dzb)
            dvt[kbc] += _dot(docat_t[...], a.astype(BF16))
            pacc[...] = pc + before[:, tk - 1:tk] + g[:, tk - 1:tk]

        def pair(kb, masked, masked_next):
            scores(kb, masked, 0)
            scores(kb + 1, masked_next, 1)
            grads(kb, 0)
            grads(kb + 1, 1)

        reach = jnp.max(ccat[...], axis=0, keepdims=True)
        first = jnp.min(jnp.where(reach > DEAD, lane, 128).astype(F32)).astype(jnp.int32)
        first = jnp.minimum(first, i)
        start = first - (i - first + 1) % 2

        @pl.when(jnp.logical_and(start < 0, i >= 2))
        def _():
            pair(-1, True, False)

        k0 = jnp.where(start < 0, 1, start)

        def loop(t, carry_):
            pair(k0 + 2 * t, False, False)
            return carry_

        lax.fori_loop(0, jnp.maximum((i - 1 - k0) // 2, 0), loop, 0)

        @pl.when(i == 0)
        def _():
            pair(-1, True, True)

        @pl.when(i > 0)
        def _():
            pair(i - 1, False, True)
        dq_ref[...] = (jnp.where(lane < HEAD_DIM, dqacc[0:tq], dqacc[tq:2 * tq]) * SCALE).astype(BF16)

        @pl.when(i == nq - 1)
        def _():
            for kb in range(nkb):
                dk_ref[kb * tk:(kb + 1) * tk, :] = dkt[kb].T.astype(BF16)
                dv_ref[kb * tk:(kb + 1) * tk, :] = dvt[kb].T.astype(BF16)

    nbytes = (_nbytes((tq, 128), BF16) + 2 * _nbytes((S, 128), BF16) + 2 * _nbytes((2 * tk, tk), BF16)
              + 12 * _nbytes((tq, 128), F32) + 4 * _nbytes((S, 128), F32) + 14 * _nbytes((2 * tq, tk), F32))
    blk = pl.BlockSpec((tq, 128), lambda j, i: (i, j))
    full = pl.BlockSpec((S, 128), lambda j, i: (0, j))
    tri_spec = pl.BlockSpec((2 * tk, tk), lambda j, i: (0, 0))
    dq, dk, dv = pl.pallas_call(
        body, name=name, grid=(4, nq),
        in_specs=[blk,
                  pl.BlockSpec((S, 128), lambda j, i: (0, 4 + j)),
                  pl.BlockSpec((S, 128), lambda j, i: (0, 8 + j)),
                  pl.BlockSpec((tq, 256), lambda j, i: (i, j)),
                  blk, tri_spec, tri_spec],
        out_specs=[blk, full, full],
        out_shape=[jax.ShapeDtypeStruct((S, D_SB), BF16)] * 3,
        scratch_shapes=[pltpu.VMEM((2 * tq, 128), BF16), pltpu.VMEM((2 * tq, 128), BF16),
                        pltpu.VMEM((128, 2 * tq), BF16), pltpu.VMEM((128, 2 * tq), BF16),
                        pltpu.VMEM((2 * tq, 128), F32), pltpu.VMEM((2 * tq, 128), F32), pltpu.VMEM((2 * tq, 1), F32),
                        pltpu.VMEM((nkb, 128, tk), F32), pltpu.VMEM((nkb, 128, tk), F32),
                        pltpu.VMEM((2, 2 * tq, tk), F32), pltpu.VMEM((2, 2 * tq, tk), F32),
                        pltpu.VMEM((2, 2 * tq, tk), F32)],
        compiler_params=_params(("parallel", "arbitrary"), nbytes),
    )(qkv, qkv, qkv, carry, do, tri_after, tri_before)
    return dq, dk, dv


def _group_mats():
    lanes = jnp.arange(D_MODEL) // HEAD_DIM
    gs = (lanes[:, None] == jnp.arange(128)[None, :]).astype(BF16)
    return gs, gs.T


def _group_sum_bcast(x, gs, gb):
    hi, lo = _split(x)
    s = _dot(hi, gs) + _dot(lo, gs)
    return _bcast(s, gb)


def _bcast(s, gb):
    hi, lo = _split(s)
    return _dot(hi, gb) + _dot(lo, gb)


def _pool_lane_consts():
    lane = lax.broadcasted_iota(jnp.int32, (1, D_POOL), 1)
    grp = lane // (D_POOL // 4)
    win = jnp.where(grp == 0, 2, jnp.where(grp == 1, 4, jnp.where(grp == 2, 8, 16)))
    return grp, win


def _by_group(grp, s2, s4, s8, s16):
    return jnp.where(grp == 0, s2, jnp.where(grp == 1, s4, jnp.where(grp == 2, s8, s16)))


def _mixers(i, ts, prev_ref, cur_ref, cw_ref, pw_ref, ps_ref):
    cur = cur_ref[...]
    prev = jnp.where(i == 0, 0.0, prev_ref[...])
    ext = jnp.concatenate([prev, cur], axis=0)

    def back(a, k):
        return pltpu.roll(a, k, 0)

    u = ext[:, D_CONV:2 * D_CONV] * ext[:, 2 * D_CONV:3 * D_CONV]
    p = ext[:, 3 * D_CONV:]
    cv = (cw_ref[0:1, :] * back(u, 2) + cw_ref[1:2, :] * back(u, 1) + cw_ref[2:3, :] * u)[HALO:]
    s2 = p + back(p, 1)
    s4 = s2 + back(s2, 2)
    s8 = s4 + back(s4, 4)
    s16 = s8 + back(s8, 8)
    grp, win = _pool_lane_consts()
    t1 = i * ts + 1 + lax.broadcasted_iota(jnp.int32, (ts, 1), 0)
    cnt = jnp.minimum(t1, win).astype(F32)
    pooled = _by_group(grp, s2, s4, s8, s16)[HALO:] / cnt - p[HALO:]
    yp = _dot(pooled.astype(BF16), pw_ref[...])
    return dict(b=cur[:, 0:D_CONV], u=u, cv=cv, pooled=pooled, yp=yp, cnt=cnt,
                conv_out=cur[:, 0:D_CONV] * cv, pool_out=yp * ps_ref[...])


def _halo_specs(ts, S, width):
    nb = ts // HALO
    last = S // HALO - 1
    prev = pl.BlockSpec((HALO, width), lambda i: (jnp.maximum(i * nb - 1, 0), 0))
    nxt = pl.BlockSpec((HALO, width), lambda i: (jnp.minimum((i + 1) * nb, last), 0))
    return prev, nxt


def _mixer_fwd(rest, attn, cw8, pwbd, ps, gain, *, name, ts):
    S = rest.shape[0]
    ts = min(ts, S)
    gs, gb = _group_mats()

    def body(prev_ref, cur_ref, attn_ref, cw_ref, pw_ref, ps_ref, gain_ref, gs_ref, gb_ref, o_ref):
        i = pl.program_id(0)
        f = _mixers(i, ts, prev_ref, cur_ref, cw_ref, pw_ref, ps_ref)
        mix = jnp.concatenate([attn_ref[...], f["conv_out"], f["pool_out"]], axis=1)
        ss = _group_sum_bcast(mix * mix, gs_ref[...], gb_ref[...])
        rinv = lax.rsqrt(ss * (1.0 / HEAD_DIM) + RMS_EPS)
        o_ref[...] = (mix * rinv * gain_ref[...]).astype(BF16)

    prev, _ = _halo_specs(ts, S, D_REST)
    row = lambda w: pl.BlockSpec((ts, w), lambda i: (i, 0))
    const = lambda a: pl.BlockSpec(a.shape, lambda i: (0, 0))
    nbytes = 12 * _nbytes((ts + HALO, D_REST), F32)
    return pl.pallas_call(
        body, name=name, grid=(S // ts,),
        in_specs=[prev, row(D_REST), row(D_SB), const(cw8), const(pwbd), const(ps), const(gain), const(gs), const(gb)],
        out_specs=row(D_MODEL),
        out_shape=jax.ShapeDtypeStruct((S, D_MODEL), BF16),
        compiler_params=_params(("parallel",), nbytes),
    )(rest, rest, attn, cw8, pwbd, ps, gain, gs, gb)


def _mixer_bwd1(dmixn, rest, attn, cw8, pwbd, ps, gain, *, name, ts):
    S = rest.shape[0]
    ts = min(ts, S)
    gs, gb = _group_mats()

    def body(dm_ref, prev_ref, cur_ref, attn_ref, cw_ref, pw_ref, ps_ref, gain_ref, gs_ref, gb_ref,
             da_ref, aux_ref, dg_ref, dsc_ref, dcw_ref, dpw_ref):
        i = pl.program_id(0)
        f = _mixers(i, ts, prev_ref, cur_ref, cw_ref, pw_ref, ps_ref)
        mix = jnp.concatenate([attn_ref[...], f["conv_out"], f["pool_out"]], axis=1)
        gsm, gbm = gs_ref[...], gb_ref[...]
        ss = _group_sum_bcast(mix * mix, gsm, gbm)
        rinv = lax.rsqrt(ss * (1.0 / HEAD_DIM) + RMS_EPS)
        dm = dm_ref[...]
        xn = mix * rinv
        dyg = dm * gain_ref[...]
        gm = _group_sum_bcast(dyg * xn, gsm, gbm) * (1.0 / HEAD_DIM)
        dmix = rinv * (dyg - xn * gm)
        da_ref[...] = dmix[:, 0:D_SB]
        dco = dmix[:, D_SB:D_SB + D_CONV]
        dpo = dmix[:, D_SB + D_CONV:]
        dcv = dco * f["b"]
        dyp = dpo * ps_ref[...]
        dpooled = _dot(dyp.astype(BF16), pw_ref[...], NT)
        aux_ref[...] = jnp.concatenate([dco * f["cv"], dcv, dpooled / f["cnt"], dpooled], axis=1)
        u = f["u"]
        parts = [
            _sum8(dm * xn),
            _sum8(dpo * f["yp"]),
            jnp.concatenate([_sum8(dcv * pltpu.roll(u, 2, 0)[HALO:]), _sum8(dcv * pltpu.roll(u, 1, 0)[HALO:]),
                             _sum8(dcv * u[HALO:])], axis=0),
            _dot(f["pooled"].astype(BF16), dyp.astype(BF16), TN),
        ]
        outs = [dg_ref, dsc_ref, dcw_ref, dpw_ref]

        @pl.when(i == 0)
        def _():
            for o, v in zip(outs, parts):
                o[...] = v

        @pl.when(i > 0)
        def _():
            for o, v in zip(outs, parts):
                o[...] += v

    prev, _ = _halo_specs(ts, S, D_REST)
    row = lambda w: pl.BlockSpec((ts, w), lambda i: (i, 0))
    const = lambda a: pl.BlockSpec(a.shape, lambda i: (0, 0))
    acc = lambda r_, w: pl.BlockSpec((r_, w), lambda i: (0, 0))
    nbytes = 16 * _nbytes((ts + HALO, D_REST), F32)
    return pl.pallas_call(
        body, name=name, grid=(S // ts,),
        in_specs=[row(D_MODEL), prev, row(D_REST), row(D_SB), const(cw8), const(pwbd), const(ps), const(gain),
                  const(gs), const(gb)],
        out_specs=[row(D_SB), row(D_REST), acc(8, D_MODEL), acc(8, D_POOL), acc(24, D_CONV), acc(D_POOL, D_POOL)],
        out_shape=[jax.ShapeDtypeStruct((S, D_SB), F32), jax.ShapeDtypeStruct((S, D_REST), F32),
                   jax.ShapeDtypeStruct((8, D_MODEL), F32), jax.ShapeDtypeStruct((8, D_POOL), F32),
                   jax.ShapeDtypeStruct((24, D_CONV), F32), jax.ShapeDtypeStruct((D_POOL, D_POOL), F32)],
        compiler_params=_params(("arbitrary",), nbytes),
    )(dmixn, rest, rest, attn, cw8, pwbd, ps, gain, gs, gb)


def _mixer_bwd2(aux, rest, cw8, *, name, ts):
    S = rest.shape[0]
    ts = min(ts, S)
    nblk = S // ts

    def body(cur_ref, nxt_ref, rest_ref, cw_ref, o_ref):
        i = pl.program_id(0)
        cur = cur_ref[...]
        nxt = jnp.where(i == nblk - 1, 0.0, nxt_ref[...])
        ext = jnp.concatenate([cur, nxt], axis=0)
        n = ts + HALO

        def fwd(a, k):
            return pltpu.roll(a, n - k, 0)

        dcv = ext[:, D_CONV:2 * D_CONV]
        dps = ext[:, 2 * D_CONV:3 * D_CONV]
        du = (cw_ref[2:3, :] * dcv + cw_ref[1:2, :] * fwd(dcv, 1) + cw_ref[0:1, :] * fwd(dcv, 2))[0:ts]
        f2 = dps + fwd(dps, 1)
        f4 = f2 + fwd(f2, 2)
        f8 = f4 + fwd(f4, 4)
        f16 = f8 + fwd(f8, 8)
        grp, _ = _pool_lane_consts()
        dp = _by_group(grp, f2, f4, f8, f16)[0:ts] - cur[:, 3 * D_CONV:]
        rest_v = rest_ref[...]
        c_gate = rest_v[:, D_CONV:2 * D_CONV]
        h = rest_v[:, 2 * D_CONV:3 * D_CONV]
        o_ref[...] = jnp.concatenate([cur[:, 0:D_CONV], du * h, du * c_gate, dp], axis=1).astype(BF16)

    _, nxt = _halo_specs(ts, S, D_REST)
    row = pl.BlockSpec((ts, D_REST), lambda i: (i, 0))
    return pl.pallas_call(
        body, name=name, grid=(nblk,),
        in_specs=[row, nxt, row, pl.BlockSpec(cw8.shape, lambda i: (0, 0))],
        out_specs=row,
        out_shape=jax.ShapeDtypeStruct((S, D_REST), BF16),
        compiler_params=_params(("parallel",), 10 * _nbytes((ts + HALO, D_REST), F32)),
    )(aux, aux, rest, cw8)


def _block_diag(pw):
    wide = jnp.tile(pw.reshape(256, 64), (1, 4))
    grp = jnp.arange(256) // 64
    return jnp.where(grp[:, None] == grp[None, :], wide, 0.0)


def _rows8(v, rows=8):
    return jnp.pad(v, ((0, rows - v.shape[0]), (0, 0)))


TILES = dict(tm=512, ts=512, tq=256, tk=256)


def _local_step(x, tgt, w, t=None, grad_pack=None):
    t = dict(TILES, **(t or {}))
    gp = None if grad_pack is None else grad_pack[0]
    tm, ts, tq, tk = t["tm"], t["ts"], t["tq"], t["tk"]
    big = dict(tm=1024, tn=1024)
    saved = []
    xl, xl16 = x, x.astype(BF16)
    for l in range(DEPTH):
        n = f"l{l}_"
        wq, wr = w["w_in"][l][:, :D_QKV], w["w_in"][l][:, D_QKV:]
        qkv = _matmul(xl16, wq, name=n + "proj_qkv", tm=1024, tn=D_QKV, tk=1024, out_dtype=BF16)
        rest = _matmul(xl16, wr, name=n + "proj_rest", tk=1024, **big)
        attn, carry = _attn_fwd(qkv, name=n + "attn_fwd", tq=tq, tk=tk)
        cw8 = _rows8(w["conv_w"][l])
        pwbd = _block_diag(w["pool_w"][l]).astype(BF16)
        ps = w["pool_scale"][l].reshape(1, D_POOL)
        gain = w["mix_norm_g"][l].reshape(1, D_MODEL)
        mixn = _mixer_fwd(rest, attn, cw8, pwbd, ps, gain, name=n + "mixer_fwd", ts=ts)
        x1, x1_16, xh1, rs1 = _matmul_ln(mixn, w["w_o"][l], xl, w["ln1_g"][l], w["ln1_b"][l], name=n + "wo_ln",
                                         tm=tm, tk=1024)
        hpre, hid = _matmul(x1_16, w["w_up"][l], name=n + "ffn_up", tk=1024, relu2_out=True, **big)
        x2, x2_16, xh2, rs2 = _matmul_ln(hid, w["w_down"][l], x1, w["ln2_g"][l], w["ln2_b"][l],
                                         name=n + "ffn_down_ln", tm=tm // 2, tk=D_FF)
        saved.append(dict(xin16=xl16, wq=wq, wr=wr, qkv=qkv, rest=rest, attn=attn, carry=carry, cw8=cw8, pwbd=pwbd,
                          ps=ps, gain=gain, mixn=mixn, x1_16=x1_16, xh1=xh1, rs1=rs1, hpre=hpre, hid=hid, xh2=xh2,
                          rs2=rs2))
        xl, xl16 = x2, x2_16

    dy, lsum = _loss_grad(xl, tgt, name="loss_grad", tm=tm)
    grads = {k: [None] * DEPTH for k in
             ("w_in", "conv_w", "pool_w", "pool_scale", "mix_norm_g", "w_o", "ln1_g", "ln1_b", "w_up", "w_down",
              "ln2_g", "ln2_b")}
    dw = dict(tk=2048, ta=True, out_dtype=BF16, **big)
    for l in reversed(range(DEPTH)):
        n = f"l{l}_"
        s = saved[l]
        dr2, dr2_16, dg2, db2 = _ln_bwd(dy, s["xh2"], s["rs2"], w["ln2_g"][l], name=n + "ln2_bwd", tm=tm)
        dhpre = _matmul(dr2_16, w["w_down"][l], name=n + "ffn_down_dx", tk=1024, tb=True, out_dtype=BF16,
                        epi="drelu2", e=s["hpre"], **big)
        if gp is None:
            grads["w_down"][l] = _matmul(s["hid"], dr2_16, name=n + "ffn_down_dw", **dw)
        else:
            gp = _matmul(s["hid"], dr2_16, name=n + "ffn_down_dw", out_chips=_ChipWeight(gp, grad_pack[2] + l, "rows"),
                         **dw)
        dx1 = _matmul(dhpre, w["w_up"][l], name=n + "ffn_up_dx", tm=512, tn=1024, tk=D_FF, tb=True,
                      epi="add", e=dr2, e_scale=ALPHA)
        if gp is None:
            grads["w_up"][l] = _matmul(s["x1_16"], dhpre, name=n + "ffn_up_dw", **dw)
        else:
            gp = _matmul(s["x1_16"], dhpre, name=n + "ffn_up_dw", out_chips=_ChipWeight(gp, grad_pack[1] + l, "cols"),
                         **dw)
        dr1, dr1_16, dg1, db1 = _ln_bwd(dx1, s["xh1"], s["rs1"], w["ln1_g"][l], name=n + "ln1_bwd", tm=tm)
        dmixn = _matmul(dr1_16, w["w_o"][l], name=n + "wo_dx", tk=1024, tb=True, **big)
        grads["w_o"][l] = _matmul(s["mixn"], dr1_16, name=n + "wo_dw", **dw)
        d_attn, aux, dgain, dsc, dcw, dpw = _mixer_bwd1(dmixn, s["rest"], s["attn"], s["cw8"], s["pwbd"], s["ps"],
                                                        s["gain"], name=n + "mixer_bwd1", ts=ts)
        drest = _mixer_bwd2(aux, s["rest"], s["cw8"], name=n + "mixer_bwd2", ts=ts)
        dqkv = jnp.concatenate(_attn_bwd(s["qkv"], s["carry"], d_attn, name=n + "attn_bwd", tq=tq, tk=tk), axis=1)
        dxa = _matmul(dqkv, s["wq"], name=n + "proj_qkv_dx", tk=D_QKV, tb=True, epi="add", e=dr1, e_scale=ALPHA,
                      **big)
        dy = _matmul(drest, s["wr"], name=n + "proj_rest_dx", tk=1024, tb=True, epi="add", e=dxa, e_scale=1.0, **big)
        dwq = _matmul(s["xin16"], dqkv, name=n + "proj_qkv_dw", tm=1024, tn=D_QKV, tk=1024, ta=True, out_dtype=BF16)
        dwr = _matmul(s["xin16"], drest, name=n + "proj_rest_dw", **dw)
        grads["w_in"][l] = jnp.concatenate([dwq, dwr], axis=1)
        grads["ln2_g"][l] = dg2.sum(0)
        grads["ln2_b"][l] = db2.sum(0)
        grads["ln1_g"][l] = dg1.sum(0)
        grads["ln1_b"][l] = db1.sum(0)
        grads["mix_norm_g"][l] = dgain.sum(0)
        grads["pool_scale"][l] = dsc.sum(0)
        grads["conv_w"][l] = dcw.reshape(3, 8, D_CONV).sum(1)
        grads["pool_w"][l] = jnp.stack([dpw[64 * g:64 * g + 64, 64 * g:64 * g + 64] for g in range(4)])
    grads = {k: jnp.stack(v) for k, v in grads.items() if v[0] is not None}
    if gp is not None:
        grads["pack"] = gp
    return lsum, dy, grads


ANY = pl.BlockSpec(memory_space=pl.ANY)


def _place():
    x, y, c = lax.axis_index("x"), lax.axis_index("y"), lax.axis_index("c")
    chips = [(1 - x, y), (x, 1 - y), (1 - x, 1 - y)]
    return x, y, c, chips


def _remote(src, dst, send_sems, recv_sems, k, to):
    return pltpu.make_async_remote_copy(src_ref=src, dst_ref=dst, send_sem=send_sems.at[k], recv_sem=recv_sems.at[k],
                                        device_id=to, device_id_type=MESH)


class _Copy:
    def __init__(self, src, dst, send_sems, recv_sems, k, to):
        self.args = (send_sems, recv_sems, k, to)
        self.copy = _remote(src, dst, *self.args)

    def like(self, src, dst):
        return _remote(src, dst, *self.args)

    def start(self):
        self.copy.start()

    def wait(self):
        self.copy.wait()

    def wait_send(self):
        self.copy.wait_send()

    def wait_recv(self):
        self.copy.wait_recv()


def _allgather_chips(pack, *, name):
    R, C = pack.shape
    H = R // 2
    Q = H // 2
    assert R % 64 == 0
    A, B = 0, 1

    def body(p_ref, o_ref, send_sems, recv_sems):
        x, y, c, _ = _place()
        my, kx, ky, kd = 2 * x + y, 2 * (1 - x) + y, 2 * x + (1 - y), 2 * (1 - x) + (1 - y)
        xn, yn, sib = (1 - x, y, c), (x, 1 - y, c), (x, y, 1 - c)

        def own(ab):
            return p_ref.at[pl.ds(c * H + ab * Q, Q), :]

        def quarter(k, hc, ab):
            return o_ref.at[k, pl.ds(hc * H + ab * Q, Q), :]

        def send(src, k, ab, sem, to):
            cp = _Copy(src, quarter(k, c, ab), send_sems, recv_sems, sem, to)
            cp.start()
            return cp

        def landed(sent, k, hc, ab):
            sent.like(quarter(k, hc, ab), quarter(k, hc, ab)).wait_recv()

        a_x = send(own(A), my, A, 0, xn)
        b_y = send(own(B), my, B, 3, yn)
        b_x = send(own(B), my, B, 1, xn)
        a_y = send(own(A), my, A, 4, yn)
        landed(b_y, ky, c, B)
        fb = send(quarter(ky, c, B), ky, B, 2, xn)
        landed(a_x, kx, c, A)
        fa = send(quarter(kx, c, A), kx, A, 5, yn)
        arrivals = [(kx, A, None), (ky, B, None), (kx, B, b_x), (ky, A, a_y), (kd, B, fb), (kd, A, fa)]
        passed = []
        for j, (k, ab, sent) in enumerate(arrivals):
            if sent is not None:
                landed(sent, k, c, ab)
            passed.append(_Copy(quarter(k, c, ab), quarter(k, c, ab), send_sems, recv_sems, 6 + j, sib))
            passed[-1].start()
        for j, (k, ab, _) in enumerate(arrivals):
            landed(passed[j], k, 1 - c, ab)
        for cp in [a_x, b_y, b_x, a_y, fb, fa] + passed:
            cp.wait_send()

    return pl.pallas_call(
        body, name=name, in_specs=[ANY], out_specs=ANY,
        out_shape=jax.ShapeDtypeStruct((N_CHIPS, R, C), pack.dtype),
        scratch_shapes=[pltpu.SemaphoreType.DMA((12,)), pltpu.SemaphoreType.DMA((12,))],
    )(pack)


def _swap_halves(gp, *, name):
    K, R, C = gp.shape
    H = R // 2

    def body(g_ref, theirs_ref, send_sems, recv_sems):
        x, y, c, _ = _place()
        cp = _Copy(g_ref.at[:, pl.ds((1 - c) * H, H), :], theirs_ref, send_sems, recv_sems, 0, (x, y, 1 - c))
        cp.start()
        cp.wait()

    return pl.pallas_call(
        body, name=name, in_specs=[ANY], out_specs=ANY, out_shape=jax.ShapeDtypeStruct((K, H, C), gp.dtype),
        scratch_shapes=[pltpu.SemaphoreType.DMA((1,)), pltpu.SemaphoreType.DMA((1,))],
    )(gp)


def _scatter_chips(part, *, name):
    K, H, C = part.shape

    def body(p_ref, o_ref, send_sems, recv_sems):
        x, y, c, chips = _place()
        copies = [_Copy(p_ref.at[2 * cx + cy], o_ref.at[j], send_sems, recv_sems, j, (cx, cy, c))
                  for j, (cx, cy) in enumerate(chips)]
        for cp in copies:
            cp.start()
        for cp in copies:
            cp.wait()

    return pl.pallas_call(
        body, name=name, in_specs=[ANY], out_specs=ANY,
        out_shape=jax.ShapeDtypeStruct((3, H, C), part.dtype),
        scratch_shapes=[pltpu.SemaphoreType.DMA((3,)), pltpu.SemaphoreType.DMA((3,))],
    )(part)


def _join_halves(both, *, name):
    H, C = both.shape[0] // 2, both.shape[1]

    def body(in_ref, o_ref, send_sems, recv_sems):
        x, y, c, _ = _place()
        mine = pl.ds(c * H, H)
        theirs = pl.ds((1 - c) * H, H)
        cp = _Copy(in_ref.at[mine, :], o_ref.at[mine, :], send_sems, recv_sems, 0, (x, y, 1 - c))
        cp.start()
        cp.wait_send()
        cp.like(in_ref.at[theirs, :], o_ref.at[theirs, :]).wait_recv()

    return pl.pallas_call(
        body, name=name, in_specs=[ANY], out_specs=ANY, input_output_aliases={0: 0},
        out_shape=jax.ShapeDtypeStruct(both.shape, both.dtype),
        scratch_shapes=[pltpu.SemaphoreType.DMA((1,)), pltpu.SemaphoreType.DMA((1,))],
    )(both)


def _allreduce_small(v, *, name):
    R, C = v.shape
    n_dev = 8

    def body(v_ref, o_ref, gat, send_sems, recv_sems):
        x, y, c, chips = _place()
        sib = (x, y, 1 - c)

        def rows(px, py, pc):
            return gat.at[4 * px + 2 * py + pc]

        gat[4 * x + 2 * y + c] = v_ref[...]
        first = [_remote(v_ref, rows(x, y, c), send_sems, recv_sems, 0, sib)]
        first += [_remote(v_ref, rows(x, y, c), send_sems, recv_sems, 1 + j, (cx, cy, c))
                  for j, (cx, cy) in enumerate(chips)]
        for cp in first:
            cp.start()
        passed = []
        for j, (cx, cy) in enumerate(chips):
            _remote(v_ref, rows(cx, cy, c), send_sems, recv_sems, 1 + j, sib).wait_recv()
            fwd = _remote(rows(cx, cy, c), rows(cx, cy, c), send_sems, recv_sems, 4 + j, sib)
            fwd.start()
            passed.append(fwd)
        _remote(v_ref, rows(x, y, 1 - c), send_sems, recv_sems, 0, sib).wait_recv()
        for j, (cx, cy) in enumerate(chips):
            _remote(v_ref, rows(cx, cy, 1 - c), send_sems, recv_sems, 4 + j, sib).wait_recv()
        for cp in first + passed:
            cp.wait_send()
        acc = gat[0]
        for d in range(1, n_dev):
            acc = acc + gat[d]
        o_ref[...] = acc

    vm = pl.BlockSpec(memory_space=pltpu.VMEM)
    return pl.pallas_call(
        body, name=name, in_specs=[vm], out_specs=vm,
        out_shape=jax.ShapeDtypeStruct((R, C), F32),
        scratch_shapes=[pltpu.VMEM((n_dev, R, C), F32), pltpu.SemaphoreType.DMA((7,)), pltpu.SemaphoreType.DMA((7,))],
    )(v)


def _add_pairs(gp, theirs, place, *, name, tr):
    K, H, C = theirs.shape
    tr = min(tr, H)
    assert H % tr == 0
    nb = H // tr

    def body(place_ref, a_ref, b_ref, o_ref):
        o_ref[...] = (a_ref[...].astype(F32) + b_ref[...].astype(F32)).astype(BF16)

    blk = pl.BlockSpec((1, tr, C), lambda k, i, p: (k, i, 0))
    mine = pl.BlockSpec((1, tr, C), lambda k, i, p: (k, i + p[1] * nb, 0))
    return pl.pallas_call(
        body, name=name,
        grid_spec=pltpu.PrefetchScalarGridSpec(num_scalar_prefetch=1, grid=(K, nb), in_specs=[mine, blk],
                                               out_specs=blk),
        out_shape=jax.ShapeDtypeStruct((K, H, C), BF16),
        compiler_params=_params(("parallel", "parallel"), 3 * _nbytes((tr, C), BF16)),
    )(place, gp, theirs)


def _add_final(gp, theirs, others, place, *, name, tr):
    K, H, C = theirs.shape
    tr = min(tr, H)
    assert H % tr == 0
    nb = H // tr

    def body(place_ref, a_ref, b_ref, o_ref_in, out_ref):
        acc = a_ref[0].astype(F32) + b_ref[0].astype(F32)
        for j in range(3):
            acc = acc + o_ref_in[j].astype(F32)
        out_ref[...] = acc

    return pl.pallas_call(
        body, name=name,
        grid_spec=pltpu.PrefetchScalarGridSpec(
            num_scalar_prefetch=1, grid=(nb,),
            in_specs=[pl.BlockSpec((1, tr, C), lambda i, p: (p[0], i + p[1] * nb, 0)),
                      pl.BlockSpec((1, tr, C), lambda i, p: (p[0], i, 0)),
                      pl.BlockSpec((3, tr, C), lambda i, p: (0, i, 0))],
            out_specs=pl.BlockSpec((tr, C), lambda i, p: (i + p[1] * nb, 0))),
        out_shape=jax.ShapeDtypeStruct((2 * H, C), F32),
        compiler_params=_params(("parallel",), 6 * _nbytes((tr, C), F32)),
    )(place, gp, theirs, others)


def _adamw(w, g, m, v, *, name, tr, row0=0):
    R, C = w.shape
    tr = min(tr, R)
    assert R % tr == 0 and row0 % tr == 0
    off = row0 // tr

    def body(w_ref, g_ref, m_ref, v_ref, go_ref, d_ref, mo_ref, vo_ref):
        gv = g_ref[...]
        m2 = ADAM_B1 * m_ref[...] + (1.0 - ADAM_B1) * gv
        v2 = ADAM_B2 * v_ref[...] + (1.0 - ADAM_B2) * jnp.square(gv)
        m_hat = m2 / (1.0 - ADAM_B1 ** ADAM_STEP)
        v_hat = v2 / (1.0 - ADAM_B2 ** ADAM_STEP)
        d_ref[...] = -ADAM_LR * (m_hat / (jnp.sqrt(v_hat) + ADAM_EPS) + ADAM_WD * w_ref[...])
        go_ref[...] = gv
        mo_ref[...] = m2
        vo_ref[...] = v2

    blk = pl.BlockSpec((tr, C), lambda i: (i, 0))
    shape = jax.ShapeDtypeStruct((R, C), F32)
    return pl.pallas_call(
        body, name=name, grid=(R // tr,),
        in_specs=[blk, pl.BlockSpec((tr, C), lambda i: (i + off, 0)), blk, blk], out_specs=[blk] * 4,
        out_shape=[shape] * 4,
        compiler_params=_params(("parallel",), 8 * _nbytes((tr, C), F32)),
    )(w, g, m, v)


BIG = ("w_up", "w_down", "w_in", "w_o")
IN_PLACE = ("w_up", "w_down")
BIG_AXIS = dict(w_in=2, w_o=1, w_up=2, w_down=1)
SMALL = ("pool_w", "pool_scale", "mix_norm_g", "ln1_g", "ln1_b", "ln2_g", "ln2_b")
CONV_ROWS = 64
SMALL_ROWS = 48


def _big_rows(shards):
    sizes = [shards[n].size // D_MODEL for n in BIG]
    starts = [sum(sizes[:i]) for i in range(len(sizes))]
    return sizes, starts


def _to_chips(a, axis):
    shape = list(a.shape)
    shape[axis:axis + 1] = [N_CHIPS, shape[axis] // N_CHIPS]
    return jnp.moveaxis(a.reshape(shape), axis, 0)


def _from_chips(a, axis):
    a = jnp.moveaxis(a, 0, axis)
    shape = list(a.shape)
    shape[axis:axis + 2] = [shape[axis] * shape[axis + 1]]
    return a.reshape(shape)


def _pad_rows(flat, rows):
    return jnp.pad(flat, (0, rows * D_MODEL - flat.shape[0])).reshape(rows, D_MODEL)


def kernel(x, w_in, conv_w, pool_w, pool_scale, mix_norm_g, w_o, ln1_g, ln1_b, w_up, w_down, ln2_g, ln2_b, loss_target, m_w_in, m_conv_w, m_pool_w, m_pool_scale, m_mix_norm_g, m_w_o, m_ln1_g, m_ln1_b, m_w_up, m_w_down, m_ln2_g, m_ln2_b, v_w_in, v_conv_w, v_pool_w, v_pool_scale, v_mix_norm_g, v_w_o, v_ln1_g, v_ln1_b, v_w_up, v_w_down, v_ln2_g, v_ln2_b):
    wts = dict(w_in=w_in, conv_w=conv_w, pool_w=pool_w, pool_scale=pool_scale, mix_norm_g=mix_norm_g, w_o=w_o,
               ln1_g=ln1_g, ln1_b=ln1_b, w_up=w_up, w_down=w_down, ln2_g=ln2_g, ln2_b=ln2_b)
    mom = dict(w_in=m_w_in, conv_w=m_conv_w, pool_w=m_pool_w, pool_scale=m_pool_scale, mix_norm_g=m_mix_norm_g,
               w_o=m_w_o, ln1_g=m_ln1_g, ln1_b=m_ln1_b, w_up=m_w_up, w_down=m_w_down, ln2_g=m_ln2_g, ln2_b=m_ln2_b)
    var = dict(w_in=v_w_in, conv_w=v_conv_w, pool_w=v_pool_w, pool_scale=v_pool_scale, mix_norm_g=v_mix_norm_g,
               w_o=v_w_o, ln1_g=v_ln1_g, ln1_b=v_ln1_b, w_up=v_w_up, w_down=v_w_down, ln2_g=v_ln2_g, ln2_b=v_ln2_b)
    chip = 2 * lax.axis_index("x") + lax.axis_index("y")
    sizes, starts = _big_rows(wts)
    big_rows = sum(sizes)

    conv_bits = lax.bitcast_convert_type(conv_w.reshape(-1), BF16).reshape(-1)
    pack = jnp.concatenate([wts[n].reshape(-1, D_MODEL).astype(BF16) for n in BIG]
                           + [_pad_rows(conv_bits, CONV_ROWS)], axis=0)
    gathered = _allgather_chips(pack, name="gather_weights")
    gathered = lax.dynamic_update_index_in_dim(gathered, pack, chip, 0)
    full = {}
    first_block = {}
    for n, size, start in zip(BIG, sizes, starts):
        if n in IN_PLACE:
            assert start % CHIP_BLOCK == 0 and size == DEPTH * CHIP_BLOCK
            first_block[n] = start // CHIP_BLOCK
            along = "cols" if BIG_AXIS[n] == 2 else "rows"
            full[n] = [_ChipWeight(gathered, first_block[n] + l, along) for l in range(DEPTH)]
        else:
            full[n] = _from_chips(gathered[:, start:start + size].reshape((N_CHIPS,) + wts[n].shape), BIG_AXIS[n])
    conv_parts = [lax.bitcast_convert_type(gathered[k, big_rows:].reshape(-1)[:2 * conv_w.size].reshape(-1, 2), F32)
                  .reshape(conv_w.shape) for k in range(N_CHIPS)]
    full["conv_w"] = jnp.concatenate(conv_parts, axis=2)
    for n in SMALL:
        full[n] = wts[n]

    gpack = jnp.zeros((N_CHIPS, big_rows, D_MODEL), BF16)
    lsum, grad_x, grads = _local_step(x[0], loss_target[0], full,
                                      grad_pack=(gpack, first_block["w_up"], first_block["w_down"]))

    others = [n for n in BIG if n not in IN_PLACE]
    rest = jnp.concatenate([_to_chips(grads[n], BIG_AXIS[n]).reshape(N_CHIPS, -1, D_MODEL) for n in others], axis=1)
    gpack = lax.dynamic_update_slice_in_dim(grads["pack"], rest.astype(BF16), starts[len(IN_PLACE)], axis=1)
    place = jnp.stack([chip, lax.axis_index("c")]).astype(jnp.int32)
    theirs = _swap_halves(gpack, name="grad_swap_cores")
    add_rows = big_rows // 8
    chip_sum = _add_pairs(gpack, theirs, place, name="grad_add_cores", tr=add_rows)
    from_chips = _scatter_chips(chip_sum, name="grad_scatter_chips")
    half_sum = _add_final(gpack, theirs, from_chips, place, name="grad_add_chips", tr=add_rows)
    gsum = _join_halves(half_sum, name="grad_join_cores")

    small_flat = jnp.concatenate([grads[n].reshape(-1) for n in SMALL] + [grads["conv_w"].reshape(-1),
                                                                          lsum.sum().reshape(1)])
    small_sum = _allreduce_small(_pad_rows(small_flat, SMALL_ROWS), name="allreduce_small").reshape(-1)
    gsmall = {}
    pos = 0
    for n in SMALL:
        gsmall[n] = small_sum[pos:pos + wts[n].size].reshape(wts[n].shape)
        pos += wts[n].size
    conv_full = small_sum[pos:pos + 4 * conv_w.size].reshape(DEPTH, 3, D_CONV)
    pos += 4 * conv_w.size
    loss = small_sum[pos]
    gsmall["conv_w"] = lax.dynamic_slice_in_dim(conv_full, chip * conv_w.shape[2], conv_w.shape[2], axis=2)

    out_g, out_d, out_m, out_v = {}, {}, {}, {}
    for n, size, start in zip(BIG, sizes, starts):
        shp = wts[n].shape
        g, row0 = gsum, start
        if shp[-1] != D_MODEL:
            g, row0 = gsum[start:start + size].reshape(-1, shp[-1]), 0
        res = _adamw(wts[n].reshape(-1, shp[-1]), g, mom[n].reshape(-1, shp[-1]), var[n].reshape(-1, shp[-1]),
                     name="adamw_" + n, tr=256, row0=row0)
        out_g[n], out_d[n], out_m[n], out_v[n] = [r.reshape(shp) for r in res]
    small_names = SMALL + ("conv_w",)
    packs = [_pad_rows(jnp.concatenate([d[n].reshape(-1) for n in small_names]), SMALL_ROWS)
             for d in (wts, gsmall, mom, var)]
    res = _adamw(*packs, name="adamw_small", tr=SMALL_ROWS)
    pos = 0
    for n in small_names:
        shp = wts[n].shape
        out_g[n], out_d[n], out_m[n], out_v[n] = [r.reshape(-1)[pos:pos + wts[n].size].reshape(shp) for r in res]
        pos += wts[n].size

    order = ("w_in", "conv_w", "pool_w", "pool_scale", "mix_norm_g", "w_o", "ln1_g", "ln1_b", "w_up", "w_down",
             "ln2_g", "ln2_b")
    return (loss, grad_x[None], *[out_g[n] for n in order], *[out_d[n] for n in order],
            *[out_m[n] for n in order], *[out_v[n] for n in order])
```

```python
import math
from typing import NamedTuple

import jax
import jax.numpy as jnp
from jax import lax
from jax.experimental import pallas as pl
from jax.experimental.pallas import tpu as pltpu

F32 = jnp.float32
BF16 = jnp.bfloat16
MESH = pl.DeviceIdType.MESH

D_MODEL = 1024
DEPTH = 2
HEAD_DIM = 64
D_SB = 512
D_CONV = 256
D_POOL = 256
D_QKV = 3 * D_SB
D_REST = 3 * D_CONV + D_POOL
D_FF = 4 * D_MODEL
ALPHA = (2 * DEPTH) ** 0.25
LN_EPS = 1e-5
RMS_EPS = 1e-6
SCALE = HEAD_DIM ** -0.5
N_CHIPS = 4
HALO = 16

ADAM_LR = 0.001
ADAM_B1 = 0.9
ADAM_B2 = 0.999
ADAM_EPS = 1e-08
ADAM_WD = 0.01
ADAM_STEP = 10

VMEM_V7X_BYTES = 64 * 1024 * 1024
VMEM_CAP_BYTES = VMEM_V7X_BYTES - 8 * 1024 * 1024


def _params(sem, block_bytes):
    limit = min(VMEM_CAP_BYTES, max(32 * 1024 * 1024, 3 * block_bytes))
    return pltpu.CompilerParams(dimension_semantics=sem, vmem_limit_bytes=limit)


def _nbytes(shape, dtype):
    return math.prod(shape) * jnp.dtype(dtype).itemsize


def _dot(a, b, dims=(((1,), (0,)), ((), ()))):
    return lax.dot_general(a, b, dims, preferred_element_type=F32)


NT = (((1,), (1,)), ((), ()))
TN = (((0,), (0,)), ((), ()))


def _split(x):
    hi = x.astype(BF16)
    lo = (x - hi.astype(F32)).astype(BF16)
    return hi, lo


def _sum8(x):
    r, c = x.shape
    return x.reshape(r // 8, 8, c).sum(axis=0)


def _ln_bwd_rows(dy, xh_ref, rs_ref, g_ref, dr_ref, dr16_ref, dg_ref, db_ref, first):
    xh = xh_ref[...]
    dxh = dy * g_ref[...]
    m1 = jnp.mean(dxh, axis=-1, keepdims=True)
    m2 = jnp.mean(dxh * xh, axis=-1, keepdims=True)
    dr = rs_ref[...] * (dxh - m1 - xh * m2)
    dr_ref[...] = dr
    dr16_ref[...] = dr.astype(BF16)
    pg = _sum8(dy * xh)
    pb = _sum8(dy)

    @pl.when(first)
    def _():
        dg_ref[...] = pg
        db_ref[...] = pb

    @pl.when(jnp.logical_not(first))
    def _():
        dg_ref[...] += pg
        db_ref[...] += pb


class _ChipWeight(NamedTuple):
    arr: jax.Array
    rb: int
    along: str


CHIP_BLOCK = 1024


def _matmul(a, b, *, name, tm, tn, tk, ta=False, tb=False, out_dtype=F32,
            epi=None, e=None, e_scale=1.0, relu2_out=False, out_chips=None, ln_bwd=None):
    M, K = (a.shape[1], a.shape[0]) if ta else a.shape
    chips = isinstance(b, _ChipWeight)
    split_k = chips and ((b.along == "cols") == tb)
    if chips:
        N = CHIP_BLOCK if split_k else N_CHIPS * CHIP_BLOCK
        assert K == (N_CHIPS * CHIP_BLOCK if split_k else CHIP_BLOCK) and not ta, (name, K)
        tn, tk = CHIP_BLOCK, K
    else:
        N = b.shape[0] if tb else b.shape[1]
    tm, tn, tk = min(tm, M), min(tn, N), min(tk, K)
    assert M % tm == 0 and N % tn == 0 and K % tk == 0, (name, M, N, K)
    nk = K // tk
    dims = (((0 if ta else 1,), (1 if tb else 0,)), ((), ()))
    n_in = 2 + (epi is not None) + (out_chips is not None) + (3 if ln_bwd is not None else 0)
    assert ln_bwd is None or (tn == N and out_chips is None and not relu2_out)

    def body(*refs):
        a_ref, b_ref = refs[0], refs[1]
        e_ref = refs[2] if epi is not None else None
        o_ref = refs[n_in]
        scr = refs[-1:]
        if not chips:
            p = _dot(a_ref[...].astype(BF16), b_ref[...].astype(BF16), dims)
        elif split_k:
            p = _dot(a_ref[:, 0:CHIP_BLOCK].astype(BF16), b_ref[0], dims)
            for c in range(1, N_CHIPS):
                p = p + _dot(a_ref[:, c * CHIP_BLOCK:(c + 1) * CHIP_BLOCK].astype(BF16), b_ref[c], dims)
        else:
            p = _dot(a_ref[...].astype(BF16), b_ref[0], dims)

        def finish(acc):
            if epi == "drelu2":
                acc = acc * (2.0 * jnp.maximum(e_ref[...], 0.0))
            elif epi == "add":
                acc = acc + e_scale * e_ref[...]
            if ln_bwd is not None:
                _ln_bwd_rows(acc, *refs[n_in - 3:n_in + 4], pl.program_id(0) == 0)
            elif out_chips is None:
                o_ref[...] = acc.astype(out_dtype)
            else:
                o_ref[0] = acc.astype(out_dtype)
            if relu2_out:
                refs[n_in + 1][...] = jnp.square(jnp.maximum(acc, 0.0)).astype(BF16)

        if nk == 1:
            finish(p)
        else:
            acc_ref = scr[0]
            k = pl.program_id(2)

            @pl.when(k == 0)
            def _():
                acc_ref[...] = p

            @pl.when(k > 0)
            def _():
                acc_ref[...] += p

            @pl.when(k == nk - 1)
            def _():
                finish(acc_ref[...])

    a_spec = pl.BlockSpec((tk, tm), lambda i, j, k: (k, i)) if ta else pl.BlockSpec((tm, tk), lambda i, j, k: (i, k))
    if chips:
        b_arr, rb = b.arr, b.rb
        nblk = N_CHIPS if split_k else 1
        b_spec = pl.BlockSpec((nblk, CHIP_BLOCK, CHIP_BLOCK),
                              (lambda i, j, k: (0, rb, 0)) if split_k else (lambda i, j, k: (j, rb, 0)))
    else:
        b_arr = b
        b_spec = pl.BlockSpec((tn, tk), lambda i, j, k: (j, k)) if tb else pl.BlockSpec((tk, tn), lambda i, j, k: (k, j))
    o_spec = pl.BlockSpec((tm, tn), lambda i, j, k: (i, j))
    in_specs = [a_spec, b_spec]
    args = [a, b_arr]
    nbytes = _nbytes((tm, tk), a.dtype) + _nbytes((tk, tn), b_arr.dtype) + 2 * _nbytes((tm, tn), F32)
    if epi is not None:
        in_specs.append(o_spec)
        args.append(e)
        nbytes += _nbytes((tm, tn), e.dtype)
    scratch = [pltpu.VMEM((tm, tn), F32)] if nk > 1 else []
    out_shape = [jax.ShapeDtypeStruct((M, N), out_dtype)]
    out_specs = [o_spec]
    aliases = {}
    if out_chips is not None:
        assert tm == tn == CHIP_BLOCK and not relu2_out and out_chips.arr.dtype == out_dtype
        orb = out_chips.rb
        out_specs = [pl.BlockSpec((1, CHIP_BLOCK, CHIP_BLOCK),
                                  (lambda i, j, k: (j, orb, 0)) if out_chips.along == "cols" else
                                  (lambda i, j, k: (i, orb, 0)))]
        out_shape = [jax.ShapeDtypeStruct(out_chips.arr.shape, out_dtype)]
        in_specs.append(pl.BlockSpec(memory_space=pl.ANY))
        args.append(out_chips.arr)
        aliases = {len(args) - 1: 0}
    if relu2_out:
        out_shape.append(jax.ShapeDtypeStruct((M, N), BF16))
        out_specs.append(o_spec)
        nbytes += _nbytes((tm, tn), BF16)
    sem = ("parallel", "parallel", "arbitrary")
    if ln_bwd is not None:
        xhat, rstd, gain = ln_bwd
        in_specs += [o_spec, pl.BlockSpec((tm, 1), lambda i, j, k: (i, 0)), pl.BlockSpec((1, N), lambda i, j, k: (0, 0))]
        args += [xhat, rstd, gain.reshape(1, N)]
        acc_spec = pl.BlockSpec((8, N), lambda i, j, k: (0, 0))
        out_specs = [o_spec, o_spec, acc_spec, acc_spec]
        out_shape = [jax.ShapeDtypeStruct((M, N), F32), jax.ShapeDtypeStruct((M, N), BF16),
                     jax.ShapeDtypeStruct((8, N), F32), jax.ShapeDtypeStruct((8, N), F32)]
        nbytes += 3 * _nbytes((tm, tn), F32)
        sem = ("arbitrary", "arbitrary", "arbitrary")
    res = pl.pallas_call(
        body, name=name,
        grid=(M // tm, N // tn, nk),
        in_specs=in_specs, out_specs=out_specs,
        out_shape=out_shape,
        scratch_shapes=scratch,
        input_output_aliases=aliases,
        compiler_params=_params(sem, nbytes),
    )(*args)
    return res if (relu2_out or ln_bwd is not None) else res[0]


def _matmul_ln(a, b, xres, g, bias, *, name, tm, tk):
    M, K = a.shape
    chips = isinstance(b, _ChipWeight)
    if chips:
        assert b.along == "rows" and K == N_CHIPS * CHIP_BLOCK
        N, tk = CHIP_BLOCK, K
    else:
        N = b.shape[1]
    tm, tk = min(tm, M), min(tk, K)
    assert M % tm == 0 and K % tk == 0 and N == D_MODEL
    nk = K // tk

    def body(a_ref, b_ref, x_ref, g_ref, bias_ref, y_ref, y16_ref, xh_ref, rs_ref, *scr):
        if chips:
            p = _dot(a_ref[:, 0:CHIP_BLOCK].astype(BF16), b_ref[0])
            for c in range(1, N_CHIPS):
                p = p + _dot(a_ref[:, c * CHIP_BLOCK:(c + 1) * CHIP_BLOCK].astype(BF16), b_ref[c])
        else:
            p = _dot(a_ref[...].astype(BF16), b_ref[...].astype(BF16))

        def finish(acc):
            r = ALPHA * x_ref[...] + acc
            mu = jnp.mean(r, axis=-1, keepdims=True)
            xc = r - mu
            var = jnp.mean(xc * xc, axis=-1, keepdims=True)
            rstd = lax.rsqrt(var + LN_EPS)
            xh = xc * rstd
            y = xh * g_ref[...] + bias_ref[...]
            y_ref[...] = y
            y16_ref[...] = y.astype(BF16)
            xh_ref[...] = xh
            rs_ref[...] = rstd

        if nk == 1:
            finish(p)
        else:
            acc_ref = scr[0]
            k = pl.program_id(1)

            @pl.when(k == 0)
            def _():
                acc_ref[...] = p

            @pl.when(k > 0)
            def _():
                acc_ref[...] += p

            @pl.when(k == nk - 1)
            def _():
                finish(acc_ref[...])

    row = pl.BlockSpec((tm, N), lambda i, k: (i, 0))
    vec = pl.BlockSpec((1, N), lambda i, k: (0, 0))
    if chips:
        b_arr, rb = b.arr, b.rb
        b_spec = pl.BlockSpec((N_CHIPS, CHIP_BLOCK, CHIP_BLOCK), lambda i, k: (0, rb, 0))
    else:
        b_arr = b
        b_spec = pl.BlockSpec((tk, N), lambda i, k: (k, 0))
    nbytes = _nbytes((tm, tk), a.dtype) + _nbytes((tk, N), b_arr.dtype) + 6 * _nbytes((tm, N), F32)
    scratch = [pltpu.VMEM((tm, N), F32)] if nk > 1 else []
    return pl.pallas_call(
        body, name=name,
        grid=(M // tm, nk),
        in_specs=[pl.BlockSpec((tm, tk), lambda i, k: (i, k)), b_spec, row, vec, vec],
        out_specs=[row, row, row, pl.BlockSpec((tm, 1), lambda i, k: (i, 0))],
        out_shape=[jax.ShapeDtypeStruct((M, N), F32), jax.ShapeDtypeStruct((M, N), BF16),
                   jax.ShapeDtypeStruct((M, N), F32), jax.ShapeDtypeStruct((M, 1), F32)],
        scratch_shapes=scratch,
        compiler_params=_params(("parallel", "arbitrary"), nbytes),
    )(a, b_arr, xres, g.reshape(1, N), bias.reshape(1, N))


def _loss_ln_bwd(y, tgt, xhat, rstd, g, *, name, tm):
    M, N = y.shape
    tm = min(tm, M)

    def body(y_ref, t_ref, xh_ref, rs_ref, g_ref, dr_ref, dr16_ref, dg_ref, db_ref, l_ref):
        first = pl.program_id(0) == 0
        d = y_ref[...] - t_ref[...]
        part = _sum8(d * d) * (0.5 / N)

        @pl.when(first)
        def _():
            l_ref[...] = part

        @pl.when(jnp.logical_not(first))
        def _():
            l_ref[...] += part

        _ln_bwd_rows(d * (1.0 / N), xh_ref, rs_ref, g_ref, dr_ref, dr16_ref, dg_ref, db_ref, first)

    row = pl.BlockSpec((tm, N), lambda i: (i, 0))
    acc = pl.BlockSpec((8, N), lambda i: (0, 0))
    return pl.pallas_call(
        body, name=name, grid=(M // tm,),
        in_specs=[row, row, row, pl.BlockSpec((tm, 1), lambda i: (i, 0)), pl.BlockSpec((1, N), lambda i: (0, 0))],
        out_specs=[row, row, acc, acc, acc],
        out_shape=[jax.ShapeDtypeStruct((M, N), F32), jax.ShapeDtypeStruct((M, N), BF16)]
        + [jax.ShapeDtypeStruct((8, N), F32)] * 3,
        compiler_params=_params(("arbitrary",), 6 * _nbytes((tm, N), F32)),
    )(y, tgt, xhat, rstd, g.reshape(1, N))


def _tri(n, kind):
    j = lax.broadcasted_iota(jnp.int32, (2 * n, n), 0) % n
    s = lax.broadcasted_iota(jnp.int32, (2 * n, n), 1)
    return ((j > s) if kind == "after" else (j < s)).astype(BF16)


LOG2E = 1.4426950408889634
DEAD = -104.0
NOT_VISITED = -1e30


def _log_terms(z):
    lse = jnp.log(1.0 + jnp.exp2(jnp.abs(z) * (-LOG2E)))
    logsig = jnp.minimum(z, 0.0) - lse
    return logsig, logsig - z


def _cumsum_mm(x, u2_ref):
    hi, lo = _split(x)
    return _dot(jnp.concatenate([hi, lo], axis=1), u2_ref[...])


def _head_rows(x2, scale):
    lane = lax.broadcasted_iota(jnp.int32, (1, 128), 1)
    zero = jnp.zeros_like(x2)
    both = jnp.concatenate([jnp.where(lane < HEAD_DIM, x2, zero), jnp.where(lane >= HEAD_DIM, x2, zero)], axis=0)
    return both * scale


def _causal_mask(i, ks, tq, tk):
    row = lax.broadcasted_iota(jnp.int32, (2 * tq, tk), 0)
    row = i * tq + jnp.where(row >= tq, row - tq, row)
    col = lax.broadcasted_iota(jnp.int32, (2 * tq, tk), 1)
    return (ks + col) < row


def _attn_fwd(qkv, *, name, tq, tk):
    S = qkv.shape[0]
    tq = tk = min(tq, tk, S)
    assert S % tq == 0 and S // tk <= 128
    tri = _tri(tk, "after")

    def body(q_ref, k_ref, v_ref, u_ref, o_ref, c_ref, qcat, oacc, cacc, call, ls_buf, tl_buf, l0_buf):
        i = pl.program_id(1)
        lane = lax.broadcasted_iota(jnp.int32, (1, 128), 1)
        qcat[...] = _head_rows(q_ref[...], SCALE)
        oacc[...] = jnp.zeros_like(oacc)
        cacc[...] = jnp.zeros_like(cacc)
        call[...] = jnp.full_like(call, NOT_VISITED)

        def scores(kb, masked, slot):
            ks = pl.multiple_of(jnp.maximum(kb, 0) * tk, tk)
            z = _dot(qcat[...], k_ref[pl.ds(ks, tk), :], NT)
            logsig, lom = _log_terms(z)
            if masked:
                msk = jnp.logical_and(_causal_mask(i, ks, tq, tk), kb >= 0)
                lom = jnp.where(msk, lom, 0.0)
                logsig = jnp.where(msk, logsig, -1e30)
            ls_buf[slot] = logsig
            tl_buf[slot] = _cumsum_mm(lom, u_ref)
            l0_buf[slot] = lom[:, 0:1]

        def weights(kb, slot):
            ks = pl.multiple_of(jnp.maximum(kb, 0) * tk, tk)
            tl = tl_buf[slot]
            c = cacc[...]
            call[...] = jnp.where(lane == kb, c, call[...])
            a = jnp.exp(ls_buf[slot] + tl + c).astype(BF16)
            oacc[...] += _dot(a, v_ref[pl.ds(ks, tk), :])
            cacc[...] = c + tl[:, 0:1] + l0_buf[slot]

        def pair(kb, masked, masked_next):
            scores(kb, masked, 0)
            scores(kb - 1, masked_next, 1)
            weights(kb, 0)
            weights(kb - 1, 1)

        pair(i, True, True)

        def live(state):
            t, cmax = state
            return jnp.logical_and(t < (i - 1) // 2, cmax > DEAD)

        def trip(state):
            t, _ = state
            pair(i - 2 - 2 * t, False, False)
            return t + 1, jnp.max(cacc[...])

        t_end, cmax = lax.while_loop(live, trip, (0, jnp.max(cacc[...])))

        left_over = jnp.logical_and(i >= 2, i % 2 == 0)
        still_live = jnp.logical_and(t_end == (i - 1) // 2, cmax > DEAD)

        @pl.when(jnp.logical_and(left_over, still_live))
        def _():
            pair(0, False, True)

        o_ref[...] = jnp.where(lane < HEAD_DIM, oacc[0:tq], oacc[tq:2 * tq])
        c_ref[...] = jnp.concatenate([call[0:tq], call[tq:2 * tq]], axis=1)

    nbytes = (_nbytes((tq, 128), BF16) + 2 * _nbytes((S, 128), BF16) + _nbytes((2 * tk, tk), BF16)
              + 8 * _nbytes((tq, 128), F32) + 14 * _nbytes((2 * tq, tk), F32))
    return pl.pallas_call(
        body, name=name, grid=(4, S // tq),
        in_specs=[pl.BlockSpec((tq, 128), lambda j, i: (i, j)),
                  pl.BlockSpec((S, 128), lambda j, i: (0, 4 + j)),
                  pl.BlockSpec((S, 128), lambda j, i: (0, 8 + j)),
                  pl.BlockSpec((2 * tk, tk), lambda j, i: (0, 0))],
        out_specs=[pl.BlockSpec((tq, 128), lambda j, i: (i, j)),
                   pl.BlockSpec((tq, 256), lambda j, i: (i, j))],
        out_shape=[jax.ShapeDtypeStruct((S, D_SB), F32), jax.ShapeDtypeStruct((S, 1024), F32)],
        scratch_shapes=[pltpu.VMEM((2 * tq, 128), BF16), pltpu.VMEM((2 * tq, 128), F32),
                        pltpu.VMEM((2 * tq, 1), F32), pltpu.VMEM((2 * tq, 128), F32),
                        pltpu.VMEM((2, 2 * tq, tk), F32), pltpu.VMEM((2, 2 * tq, tk), F32),
                        pltpu.VMEM((2, 2 * tq, 1), F32)],
        compiler_params=_params(("parallel", "arbitrary"), nbytes),
    )(qkv, qkv, qkv, tri)


def _attn_bwd(qkv, carry, do, *, name, tq, tk):
    S = qkv.shape[0]
    tq = tk = min(tq, tk, S)
    assert S % tq == 0 and S // tk <= 128
    nkb = S // tk
    nq = S // tq
    tri_after = _tri(tk, "after")
    tri_before = _tri(tk, "before")

    def body(q_ref, k_ref, v_ref, c_ref, do_ref, ua_ref, ub_ref, dq_ref, dk_ref, dv_ref,
             qcat, docat, qcat_t, docat_t, ccat, dqacc, pacc, dkt, dvt, ls_buf, tl_buf, da_buf):
        i = pl.program_id(1)
        lane = lax.broadcasted_iota(jnp.int32, (1, 128), 1)
        sub = lax.broadcasted_iota(jnp.int32, (128, 1), 0)
        q2 = q_ref[...]
        do2 = do_ref[...]
        qcat[...] = _head_rows(q2, SCALE)
        docat[...] = _head_rows(do2, 1.0).astype(BF16)
        qt = q2.astype(F32).T * SCALE
        dot_ = do2.T
        qcat_t[...] = jnp.concatenate([jnp.where(sub < HEAD_DIM, qt, 0.0), jnp.where(sub >= HEAD_DIM, qt, 0.0)],
                                      axis=1).astype(BF16)
        docat_t[...] = jnp.concatenate([jnp.where(sub < HEAD_DIM, dot_, 0.0), jnp.where(sub >= HEAD_DIM, dot_, 0.0)],
                                       axis=1).astype(BF16)
        ccat[0:tq] = c_ref[:, 0:128]
        ccat[tq:2 * tq] = c_ref[:, 128:256]

        @pl.when(i == 0)
        def _():
            dkt[...] = jnp.zeros_like(dkt)
            dvt[...] = jnp.zeros_like(dvt)

        dqacc[...] = jnp.zeros_like(dqacc)
        pacc[...] = jnp.zeros_like(pacc)

        def scores(kb, masked, slot):
            ks = pl.multiple_of(jnp.maximum(kb, 0) * tk, tk)
            z = _dot(qcat[...], k_ref[pl.ds(ks, tk), :], NT)
            logsig, lom = _log_terms(z)
            if masked:
                msk = jnp.logical_and(_causal_mask(i, ks, tq, tk), kb >= 0)
                lom = jnp.where(msk, lom, 0.0)
                logsig = jnp.where(msk, logsig, -1e30)
            ls_buf[slot] = logsig
            tl_buf[slot] = _cumsum_mm(lom, ua_ref)
            da_buf[slot] = _dot(docat[...], v_ref[pl.ds(ks, tk), :], NT)

        def grads(kb, slot):
            kbc = jnp.maximum(kb, 0)
            ks = pl.multiple_of(kbc * tk, tk)
            logsig = ls_buf[slot]
            c = jnp.sum(jnp.where(lane == kb, ccat[...], 0.0), axis=1, keepdims=True)
            a = jnp.exp(logsig + tl_buf[slot] + c)
            g = a * da_buf[slot]
            before = _cumsum_mm(g, ub_ref)
            pc = pacc[...]
            dz = g - jnp.exp(logsig) * (g + before + pc)
            dzb = dz.astype(BF16)
            dqacc[...] += _dot(dzb, k_ref[pl.ds(ks, tk), :])
            dkt[kbc] += _dot(qcat_t[...], dzb)
            dvt[kbc] += _dot(docat_t[...], a.astype(BF16))
            pacc[...] = pc + before[:, tk - 1:tk] + g[:, tk - 1:tk]

        def pair(kb, masked, masked_next):
            scores(kb, masked, 0)
            scores(kb + 1, masked_next, 1)
            grads(kb, 0)
            grads(kb + 1, 1)

        reach = jnp.max(ccat[...], axis=0, keepdims=True)
        first = jnp.min(jnp.where(reach > DEAD, lane, 128).astype(F32)).astype(jnp.int32)
        first = jnp.minimum(first, i)
        start = first - (i - first + 1) % 2

        @pl.when(jnp.logical_and(start < 0, i >= 2))
        def _():
            pair(-1, True, False)

        k0 = jnp.where(start < 0, 1, start)

        def loop(t, carry_):
            pair(k0 + 2 * t, False, False)
            return carry_

        lax.fori_loop(0, jnp.maximum((i - 1 - k0) // 2, 0), loop, 0)

        @pl.when(i == 0)
        def _():
            pair(-1, True, True)

        @pl.when(i > 0)
        def _():
            pair(i - 1, False, True)
        dq_ref[...] = (jnp.where(lane < HEAD_DIM, dqacc[0:tq], dqacc[tq:2 * tq]) * SCALE).astype(BF16)

        @pl.when(i == nq - 1)
        def _():
            for kb in range(nkb):
                dk_ref[kb * tk:(kb + 1) * tk, :] = dkt[kb].T.astype(BF16)
                dv_ref[kb * tk:(kb + 1) * tk, :] = dvt[kb].T.astype(BF16)

    nbytes = (_nbytes((tq, 128), BF16) + 2 * _nbytes((S, 128), BF16) + 2 * _nbytes((2 * tk, tk), BF16)
              + 12 * _nbytes((tq, 128), F32) + 4 * _nbytes((S, 128), F32) + 14 * _nbytes((2 * tq, tk), F32))
    blk = pl.BlockSpec((tq, 128), lambda j, i: (i, j))
    full = pl.BlockSpec((S, 128), lambda j, i: (0, j))
    tri_spec = pl.BlockSpec((2 * tk, tk), lambda j, i: (0, 0))
    dq, dk, dv = pl.pallas_call(
        body, name=name, grid=(4, nq),
        in_specs=[blk,
                  pl.BlockSpec((S, 128), lambda j, i: (0, 4 + j)),
                  pl.BlockSpec((S, 128), lambda j, i: (0, 8 + j)),
                  pl.BlockSpec((tq, 256), lambda j, i: (i, j)),
                  blk, tri_spec, tri_spec],
        out_specs=[blk, full, full],
        out_shape=[jax.ShapeDtypeStruct((S, D_SB), BF16)] * 3,
        scratch_shapes=[pltpu.VMEM((2 * tq, 128), BF16), pltpu.VMEM((2 * tq, 128), BF16),
                        pltpu.VMEM((128, 2 * tq), BF16), pltpu.VMEM((128, 2 * tq), BF16),
                        pltpu.VMEM((2 * tq, 128), F32), pltpu.VMEM((2 * tq, 128), F32), pltpu.VMEM((2 * tq, 1), F32),
                        pltpu.VMEM((nkb, 128, tk), F32), pltpu.VMEM((nkb, 128, tk), F32),
                        pltpu.VMEM((2, 2 * tq, tk), F32), pltpu.VMEM((2, 2 * tq, tk), F32),
                        pltpu.VMEM((2, 2 * tq, tk), F32)],
        compiler_params=_params(("parallel", "arbitrary"), nbytes),
    )(qkv, qkv, qkv, carry, do, tri_after, tri_before)
    return dq, dk, dv


def _group_mats():
    lanes = jnp.arange(D_MODEL) // HEAD_DIM
    gs = (lanes[:, None] == jnp.arange(128)[None, :]).astype(BF16)
    return gs, gs.T


def _group_sum_bcast(x, gs, gb):
    hi, lo = _split(x)
    s = _dot(hi, gs) + _dot(lo, gs)
    return _bcast(s, gb)


def _bcast(s, gb):
    hi, lo = _split(s)
    return _dot(hi, gb) + _dot(lo, gb)


def _pool_lane_consts():
    lane = lax.broadcasted_iota(jnp.int32, (1, D_POOL), 1)
    grp = lane // (D_POOL // 4)
    win = jnp.where(grp == 0, 2, jnp.where(grp == 1, 4, jnp.where(grp == 2, 8, 16)))
    return grp, win


def _by_group(grp, s2, s4, s8, s16):
    return jnp.where(grp == 0, s2, jnp.where(grp == 1, s4, jnp.where(grp == 2, s8, s16)))


def _mixers(i, ts, prev_ref, cur_ref, cw_ref, pw_ref, ps_ref):
    cur = cur_ref[...]
    prev = jnp.where(i == 0, 0.0, prev_ref[...])
    ext = jnp.concatenate([prev, cur], axis=0)

    def back(a, k):
        return pltpu.roll(a, k, 0)

    u = ext[:, D_CONV:2 * D_CONV] * ext[:, 2 * D_CONV:3 * D_CONV]
    p = ext[:, 3 * D_CONV:]
    cv = (cw_ref[0:1, :] * back(u, 2) + cw_ref[1:2, :] * back(u, 1) + cw_ref[2:3, :] * u)[HALO:]
    s2 = p + back(p, 1)
    s4 = s2 + back(s2, 2)
    s8 = s4 + back(s4, 4)
    s16 = s8 + back(s8, 8)
    grp, win = _pool_lane_consts()
    t1 = i * ts + 1 + lax.broadcasted_iota(jnp.int32, (ts, 1), 0)
    cnt = jnp.minimum(t1, win).astype(F32)
    pooled = _by_group(grp, s2, s4, s8, s16)[HALO:] / cnt - p[HALO:]
    yp = _dot(pooled.astype(BF16), pw_ref[...])
    return dict(b=cur[:, 0:D_CONV], u=u, cv=cv, pooled=pooled, yp=yp, cnt=cnt,
                conv_out=cur[:, 0:D_CONV] * cv, pool_out=yp * ps_ref[...])


def _halo_specs(ts, S, width):
    nb = ts // HALO
    last = S // HALO - 1
    prev = pl.BlockSpec((HALO, width), lambda i: (jnp.maximum(i * nb - 1, 0), 0))
    nxt = pl.BlockSpec((HALO, width), lambda i: (jnp.minimum((i + 1) * nb, last), 0))
    return prev, nxt


def _mixer_fwd(rest, attn, cw8, pwbd, ps, gain, *, name, ts):
    S = rest.shape[0]
    ts = min(ts, S)
    gs, gb = _group_mats()

    def body(prev_ref, cur_ref, attn_ref, cw_ref, pw_ref, ps_ref, gain_ref, gs_ref, gb_ref, o_ref):
        i = pl.program_id(0)
        f = _mixers(i, ts, prev_ref, cur_ref, cw_ref, pw_ref, ps_ref)
        mix = jnp.concatenate([attn_ref[...], f["conv_out"], f["pool_out"]], axis=1)
        ss = _group_sum_bcast(mix * mix, gs_ref[...], gb_ref[...])
        rinv = lax.rsqrt(ss * (1.0 / HEAD_DIM) + RMS_EPS)
        o_ref[...] = (mix * rinv * gain_ref[...]).astype(BF16)

    prev, _ = _halo_specs(ts, S, D_REST)
    row = lambda w: pl.BlockSpec((ts, w), lambda i: (i, 0))
    const = lambda a: pl.BlockSpec(a.shape, lambda i: (0, 0))
    nbytes = 12 * _nbytes((ts + HALO, D_REST), F32)
    return pl.pallas_call(
        body, name=name, grid=(S // ts,),
        in_specs=[prev, row(D_REST), row(D_SB), const(cw8), const(pwbd), const(ps), const(gain), const(gs), const(gb)],
        out_specs=row(D_MODEL),
        out_shape=jax.ShapeDtypeStruct((S, D_MODEL), BF16),
        compiler_params=_params(("parallel",), nbytes),
    )(rest, rest, attn, cw8, pwbd, ps, gain, gs, gb)


def _mixer_bwd1(dmixn, rest, attn, cw8, pwbd, ps, gain, *, name, ts):
    S = rest.shape[0]
    ts = min(ts, S)
    gs, gb = _group_mats()

    def body(dm_ref, prev_ref, cur_ref, attn_ref, cw_ref, pw_ref, ps_ref, gain_ref, gs_ref, gb_ref,
             da_ref, aux_ref, dg_ref, dsc_ref, dcw_ref, dpw_ref):
        i = pl.program_id(0)
        f = _mixers(i, ts, prev_ref, cur_ref, cw_ref, pw_ref, ps_ref)
        mix = jnp.concatenate([attn_ref[...], f["conv_out"], f["pool_out"]], axis=1)
        gsm, gbm = gs_ref[...], gb_ref[...]
        ss = _group_sum_bcast(mix * mix, gsm, gbm)
        rinv = lax.rsqrt(ss * (1.0 / HEAD_DIM) + RMS_EPS)
        dm = dm_ref[...]
        xn = mix * rinv
        dyg = dm * gain_ref[...]
        gm = _group_sum_bcast(dyg * xn, gsm, gbm) * (1.0 / HEAD_DIM)
        dmix = rinv * (dyg - xn * gm)
        da_ref[...] = dmix[:, 0:D_SB]
        dco = dmix[:, D_SB:D_SB + D_CONV]
        dpo = dmix[:, D_SB + D_CONV:]
        dcv = dco * f["b"]
        dyp = dpo * ps_ref[...]
        dpooled = _dot(dyp.astype(BF16), pw_ref[...], NT)
        aux_ref[...] = jnp.concatenate([dco * f["cv"], dcv, dpooled / f["cnt"], dpooled], axis=1)
        u = f["u"]
        parts = [
            _sum8(dm * xn),
            _sum8(dpo * f["yp"]),
            jnp.concatenate([_sum8(dcv * pltpu.roll(u, 2, 0)[HALO:]), _sum8(dcv * pltpu.roll(u, 1, 0)[HALO:]),
                             _sum8(dcv * u[HALO:])], axis=0),
            _dot(f["pooled"].astype(BF16), dyp.astype(BF16), TN),
        ]
        outs = [dg_ref, dsc_ref, dcw_ref, dpw_ref]

        @pl.when(i == 0)
        def _():
            for o, v in zip(outs, parts):
                o[...] = v

        @pl.when(i > 0)
        def _():
            for o, v in zip(outs, parts):
                o[...] += v

    prev, _ = _halo_specs(ts, S, D_REST)
    row = lambda w: pl.BlockSpec((ts, w), lambda i: (i, 0))
    const = lambda a: pl.BlockSpec(a.shape, lambda i: (0, 0))
    acc = lambda r_, w: pl.BlockSpec((r_, w), lambda i: (0, 0))
    nbytes = 16 * _nbytes((ts + HALO, D_REST), F32)
    return pl.pallas_call(
        body, name=name, grid=(S // ts,),
        in_specs=[row(D_MODEL), prev, row(D_REST), row(D_SB), const(cw8), const(pwbd), const(ps), const(gain),
                  const(gs), const(gb)],
        out_specs=[row(D_SB), row(D_REST), acc(8, D_MODEL), acc(8, D_POOL), acc(24, D_CONV), acc(D_POOL, D_POOL)],
        out_shape=[jax.ShapeDtypeStruct((S, D_SB), F32), jax.ShapeDtypeStruct((S, D_REST), F32),
                   jax.ShapeDtypeStruct((8, D_MODEL), F32), jax.ShapeDtypeStruct((8, D_POOL), F32),
                   jax.ShapeDtypeStruct((24, D_CONV), F32), jax.ShapeDtypeStruct((D_POOL, D_POOL), F32)],
        compiler_params=_params(("arbitrary",), nbytes),
    )(dmixn, rest, rest, attn, cw8, pwbd, ps, gain, gs, gb)


def _mixer_bwd2(aux, rest, cw8, *, name, ts):
    S = rest.shape[0]
    ts = min(ts, S)
    nblk = S // ts

    def body(cur_ref, nxt_ref, rest_ref, cw_ref, o_ref):
        i = pl.program_id(0)
        cur = cur_ref[...]
        nxt = jnp.where(i == nblk - 1, 0.0, nxt_ref[...])
        ext = jnp.concatenate([cur, nxt], axis=0)
        n = ts + HALO

        def fwd(a, k):
            return pltpu.roll(a, n - k, 0)

        dcv = ext[:, D_CONV:2 * D_CONV]
        dps = ext[:, 2 * D_CONV:3 * D_CONV]
        du = (cw_ref[2:3, :] * dcv + cw_ref[1:2, :] * fwd(dcv, 1) + cw_ref[0:1, :] * fwd(dcv, 2))[0:ts]
        f2 = dps + fwd(dps, 1)
        f4 = f2 + fwd(f2, 2)
        f8 = f4 + fwd(f4, 4)
        f16 = f8 + fwd(f8, 8)
        grp, _ = _pool_lane_consts()
        dp = _by_group(grp, f2, f4, f8, f16)[0:ts] - cur[:, 3 * D_CONV:]
        rest_v = rest_ref[...]
        c_gate = rest_v[:, D_CONV:2 * D_CONV]
        h = rest_v[:, 2 * D_CONV:3 * D_CONV]
        o_ref[...] = jnp.concatenate([cur[:, 0:D_CONV], du * h, du * c_gate, dp], axis=1).astype(BF16)

    _, nxt = _halo_specs(ts, S, D_REST)
    row = pl.BlockSpec((ts, D_REST), lambda i: (i, 0))
    return pl.pallas_call(
        body, name=name, grid=(nblk,),
        in_specs=[row, nxt, row, pl.BlockSpec(cw8.shape, lambda i: (0, 0))],
        out_specs=row,
        out_shape=jax.ShapeDtypeStruct((S, D_REST), BF16),
        compiler_params=_params(("parallel",), 10 * _nbytes((ts + HALO, D_REST), F32)),
    )(aux, aux, rest, cw8)


def _block_diag(pw):
    wide = jnp.tile(pw.reshape(256, 64), (1, 4))
    grp = jnp.arange(256) // 64
    return jnp.where(grp[:, None] == grp[None, :], wide, 0.0)


def _rows8(v, rows=8):
    return jnp.pad(v, ((0, rows - v.shape[0]), (0, 0)))


TILES = dict(tm=512, ts=512, tq=256, tk=256)


def _local_step(x, tgt, w, t=None, grad_pack=None):
    t = dict(TILES, **(t or {}))
    gp = None if grad_pack is None else grad_pack[0]
    tm, ts, tq, tk = t["tm"], t["ts"], t["tq"], t["tk"]
    big = dict(tm=1024, tn=1024)
    saved = []
    xl, xl16 = x, x.astype(BF16)
    for l in range(DEPTH):
        n = f"l{l}_"
        wq, wr = w["w_in"][l][:, :D_QKV], w["w_in"][l][:, D_QKV:]
        qkv = _matmul(xl16, wq, name=n + "proj_qkv", tm=1024, tn=D_QKV, tk=1024, out_dtype=BF16)
        rest = _matmul(xl16, wr, name=n + "proj_rest", tk=1024, **big)
        attn, carry = _attn_fwd(qkv, name=n + "attn_fwd", tq=tq, tk=tk)
        cw8 = _rows8(w["conv_w"][l])
        pwbd = _block_diag(w["pool_w"][l]).astype(BF16)
        ps = w["pool_scale"][l].reshape(1, D_POOL)
        gain = w["mix_norm_g"][l].reshape(1, D_MODEL)
        mixn = _mixer_fwd(rest, attn, cw8, pwbd, ps, gain, name=n + "mixer_fwd", ts=ts)
        x1, x1_16, xh1, rs1 = _matmul_ln(mixn, w["w_o"][l], xl, w["ln1_g"][l], w["ln1_b"][l], name=n + "wo_ln",
                                         tm=tm, tk=1024)
        hpre, hid = _matmul(x1_16, w["w_up"][l], name=n + "ffn_up", tk=1024, relu2_out=True, **big)
        x2, x2_16, xh2, rs2 = _matmul_ln(hid, w["w_down"][l], x1, w["ln2_g"][l], w["ln2_b"][l],
                                         name=n + "ffn_down_ln", tm=tm // 2, tk=D_FF)
        saved.append(dict(xin16=xl16, wq=wq, wr=wr, qkv=qkv, rest=rest, attn=attn, carry=carry, cw8=cw8, pwbd=pwbd,
                          ps=ps, gain=gain, mixn=mixn, x1_16=x1_16, xh1=xh1, rs1=rs1, hpre=hpre, hid=hid, xh2=xh2,
                          rs2=rs2))
        xl, xl16 = x2, x2_16

    top = saved[-1]
    ln2_back = _loss_ln_bwd(xl, tgt, top["xh2"], top["rs2"], w["ln2_g"][DEPTH - 1], name="loss_ln2_bwd", tm=tm)
    lsum = ln2_back[4]
    grads = {k: [None] * DEPTH for k in
             ("w_in", "conv_w", "pool_w", "pool_scale", "mix_norm_g", "w_o", "ln1_g", "ln1_b", "w_up", "w_down",
              "ln2_g", "ln2_b")}
    dw = dict(tk=2048, ta=True, out_dtype=BF16, **big)
    for l in reversed(range(DEPTH)):
        n = f"l{l}_"
        s = saved[l]
        dr2, dr2_16, dg2, db2 = ln2_back[:4]
        dhpre = _matmul(dr2_16, w["w_down"][l], name=n + "ffn_down_dx", tk=1024, tb=True, out_dtype=BF16,
                        epi="drelu2", e=s["hpre"], **big)
        if gp is None:
            grads["w_down"][l] = _matmul(s["hid"], dr2_16, name=n + "ffn_down_dw", **dw)
        else:
            gp = _matmul(s["hid"], dr2_16, name=n + "ffn_down_dw", out_chips=_ChipWeight(gp, grad_pack[2] + l, "rows"),
                         **dw)
        dr1, dr1_16, dg1, db1 = _matmul(dhpre, w["w_up"][l], name=n + "ffn_up_dx_ln1_bwd", tm=512, tn=1024, tk=D_FF,
                                        tb=True, epi="add", e=dr2, e_scale=ALPHA,
                                        ln_bwd=(s["xh1"], s["rs1"], w["ln1_g"][l]))
        if gp is None:
            grads["w_up"][l] = _matmul(s["x1_16"], dhpre, name=n + "ffn_up_dw", **dw)
        else:
            gp = _matmul(s["x1_16"], dhpre, name=n + "ffn_up_dw", out_chips=_ChipWeight(gp, grad_pack[1] + l, "cols"),
                         **dw)
        dmixn = _matmul(dr1_16, w["w_o"][l], name=n + "wo_dx", tk=1024, tb=True, **big)
        grads["w_o"][l] = _matmul(s["mixn"], dr1_16, name=n + "wo_dw", **dw)
        d_attn, aux, dgain, dsc, dcw, dpw = _mixer_bwd1(dmixn, s["rest"], s["attn"], s["cw8"], s["pwbd"], s["ps"],
                                                        s["gain"], name=n + "mixer_bwd1", ts=ts)
        drest = _mixer_bwd2(aux, s["rest"], s["cw8"], name=n + "mixer_bwd2", ts=ts)
        dqkv = jnp.concatenate(_attn_bwd(s["qkv"], s["carry"], d_attn, name=n + "attn_bwd", tq=tq, tk=tk), axis=1)
        dxa = _matmul(dqkv, s["wq"], name=n + "proj_qkv_dx", tk=D_QKV, tb=True, epi="add", e=dr1, e_scale=ALPHA,
                      **big)
        below = None if l == 0 else (saved[l - 1]["xh2"], saved[l - 1]["rs2"], w["ln2_g"][l - 1])
        ln2_back = _matmul(drest, s["wr"], name=n + "proj_rest_dx", tm=512, tn=1024, tk=1024, tb=True, epi="add", e=dxa,
                           e_scale=1.0, ln_bwd=below)
        dy = ln2_back
        dwq = _matmul(s["xin16"], dqkv, name=n + "proj_qkv_dw", tm=1024, tn=D_QKV, tk=1024, ta=True, out_dtype=BF16)
        dwr = _matmul(s["xin16"], drest, name=n + "proj_rest_dw", **dw)
        grads["w_in"][l] = jnp.concatenate([dwq, dwr], axis=1)
        grads["ln2_g"][l] = dg2.sum(0)
        grads["ln2_b"][l] = db2.sum(0)
        grads["ln1_g"][l] = dg1.sum(0)
        grads["ln1_b"][l] = db1.sum(0)
        grads["mix_norm_g"][l] = dgain.sum(0)
        grads["pool_scale"][l] = dsc.sum(0)
        grads["conv_w"][l] = dcw.reshape(3, 8, D_CONV).sum(1)
        grads["pool_w"][l] = jnp.stack([dpw[64 * g:64 * g + 64, 64 * g:64 * g + 64] for g in range(4)])
    grads = {k: jnp.stack(v) for k, v in grads.items() if v[0] is not None}
    if gp is not None:
        grads["pack"] = gp
    return lsum, dy, grads


ANY = pl.BlockSpec(memory_space=pl.ANY)


def _place():
    x, y, c = lax.axis_index("x"), lax.axis_index("y"), lax.axis_index("c")
    chips = [(1 - x, y), (x, 1 - y), (1 - x, 1 - y)]
    return x, y, c, chips


def _remote(src, dst, send_sems, recv_sems, k, to):
    return pltpu.make_async_remote_copy(src_ref=src, dst_ref=dst, send_sem=send_sems.at[k], recv_sem=recv_sems.at[k],
                                        device_id=to, device_id_type=MESH)


class _Copy:
    def __init__(self, src, dst, send_sems, recv_sems, k, to):
        self.args = (send_sems, recv_sems, k, to)
        self.copy = _remote(src, dst, *self.args)

    def like(self, src, dst):
        return _remote(src, dst, *self.args)

    def start(self):
        self.copy.start()

    def wait(self):
        self.copy.wait()

    def wait_send(self):
        self.copy.wait_send()

    def wait_recv(self):
        self.copy.wait_recv()


def _allgather_chips(pack, *, name):
    R, C = pack.shape
    H = R // 2
    Q = H // 2
    assert R % 64 == 0
    A, B = 0, 1

    def body(p_ref, o_ref, send_sems, recv_sems):
        x, y, c, _ = _place()
        my, kx, ky, kd = 2 * x + y, 2 * (1 - x) + y, 2 * x + (1 - y), 2 * (1 - x) + (1 - y)
        xn, yn, sib = (1 - x, y, c), (x, 1 - y, c), (x, y, 1 - c)

        def own(ab):
            return p_ref.at[pl.ds(c * H + ab * Q, Q), :]

        def quarter(k, hc, ab):
            return o_ref.at[k, pl.ds(hc * H + ab * Q, Q), :]

        def send(src, k, ab, sem, to):
            cp = _Copy(src, quarter(k, c, ab), send_sems, recv_sems, sem, to)
            cp.start()
            return cp

        def landed(sent, k, hc, ab):
            sent.like(quarter(k, hc, ab), quarter(k, hc, ab)).wait_recv()

        a_x = send(own(A), my, A, 0, xn)
        b_y = send(own(B), my, B, 3, yn)
        b_x = send(own(B), my, B, 1, xn)
        a_y = send(own(A), my, A, 4, yn)
        landed(b_y, ky, c, B)
        fb = send(quarter(ky, c, B), ky, B, 2, xn)
        landed(a_x, kx, c, A)
        fa = send(quarter(kx, c, A), kx, A, 5, yn)
        arrivals = [(kx, A, None), (ky, B, None), (kx, B, b_x), (ky, A, a_y), (kd, B, fb), (kd, A, fa)]
        passed = []
        for j, (k, ab, sent) in enumerate(arrivals):
            if sent is not None:
                landed(sent, k, c, ab)
            passed.append(_Copy(quarter(k, c, ab), quarter(k, c, ab), send_sems, recv_sems, 6 + j, sib))
            passed[-1].start()
        for j, (k, ab, _) in enumerate(arrivals):
            landed(passed[j], k, 1 - c, ab)
        for cp in [a_x, b_y, b_x, a_y, fb, fa] + passed:
            cp.wait_send()

    return pl.pallas_call(
        body, name=name, in_specs=[ANY], out_specs=ANY,
        out_shape=jax.ShapeDtypeStruct((N_CHIPS, R, C), pack.dtype),
        scratch_shapes=[pltpu.SemaphoreType.DMA((12,)), pltpu.SemaphoreType.DMA((12,))],
    )(pack)


def _swap_halves(gp, *, name):
    K, R, C = gp.shape
    H = R // 2

    def body(g_ref, theirs_ref, send_sems, recv_sems):
        x, y, c, _ = _place()
        cp = _Copy(g_ref.at[:, pl.ds((1 - c) * H, H), :], theirs_ref, send_sems, recv_sems, 0, (x, y, 1 - c))
        cp.start()
        cp.wait()

    return pl.pallas_call(
        body, name=name, in_specs=[ANY], out_specs=ANY, out_shape=jax.ShapeDtypeStruct((K, H, C), gp.dtype),
        scratch_shapes=[pltpu.SemaphoreType.DMA((1,)), pltpu.SemaphoreType.DMA((1,))],
    )(gp)


def _scatter_chips(part, *, name):
    K, H, C = part.shape

    def body(p_ref, o_ref, send_sems, recv_sems):
        x, y, c, chips = _place()
        copies = [_Copy(p_ref.at[2 * cx + cy], o_ref.at[j], send_sems, recv_sems, j, (cx, cy, c))
                  for j, (cx, cy) in enumerate(chips)]
        for cp in copies:
            cp.start()
        for cp in copies:
            cp.wait()

    return pl.pallas_call(
        body, name=name, in_specs=[ANY], out_specs=ANY,
        out_shape=jax.ShapeDtypeStruct((3, H, C), part.dtype),
        scratch_shapes=[pltpu.SemaphoreType.DMA((3,)), pltpu.SemaphoreType.DMA((3,))],
    )(part)


def _join_halves(both, *, name):
    H, C = both.shape[0] // 2, both.shape[1]

    def body(in_ref, o_ref, send_sems, recv_sems):
        x, y, c, _ = _place()
        mine = pl.ds(c * H, H)
        theirs = pl.ds((1 - c) * H, H)
        cp = _Copy(in_ref.at[mine, :], o_ref.at[mine, :], send_sems, recv_sems, 0, (x, y, 1 - c))
        cp.start()
        cp.wait_send()
        cp.like(in_ref.at[theirs, :], o_ref.at[theirs, :]).wait_recv()

    return pl.pallas_call(
        body, name=name, in_specs=[ANY], out_specs=ANY, input_output_aliases={0: 0},
        out_shape=jax.ShapeDtypeStruct(both.shape, both.dtype),
        scratch_shapes=[pltpu.SemaphoreType.DMA((1,)), pltpu.SemaphoreType.DMA((1,))],
    )(both)


def _allreduce_small(v, *, name):
    R, C = v.shape
    n_dev = 8

    def body(v_ref, o_ref, gat, send_sems, recv_sems):
        x, y, c, chips = _place()
        sib = (x, y, 1 - c)

        def rows(px, py, pc):
            return gat.at[4 * px + 2 * py + pc]

        gat[4 * x + 2 * y + c] = v_ref[...]
        first = [_remote(v_ref, rows(x, y, c), send_sems, recv_sems, 0, sib)]
        first += [_remote(v_ref, rows(x, y, c), send_sems, recv_sems, 1 + j, (cx, cy, c))
                  for j, (cx, cy) in enumerate(chips)]
        for cp in first:
            cp.start()
        passed = []
        for j, (cx, cy) in enumerate(chips):
            _remote(v_ref, rows(cx, cy, c), send_sems, recv_sems, 1 + j, sib).wait_recv()
            fwd = _remote(rows(cx, cy, c), rows(cx, cy, c), send_sems, recv_sems, 4 + j, sib)
            fwd.start()
            passed.append(fwd)
        _remote(v_ref, rows(x, y, 1 - c), send_sems, recv_sems, 0, sib).wait_recv()
        for j, (cx, cy) in enumerate(chips):
            _remote(v_ref, rows(cx, cy, 1 - c), send_sems, recv_sems, 4 + j, sib).wait_recv()
        for cp in first + passed:
            cp.wait_send()
        acc = gat[0]
        for d in range(1, n_dev):
            acc = acc + gat[d]
        o_ref[...] = acc

    vm = pl.BlockSpec(memory_space=pltpu.VMEM)
    return pl.pallas_call(
        body, name=name, in_specs=[vm], out_specs=vm,
        out_shape=jax.ShapeDtypeStruct((R, C), F32),
        scratch_shapes=[pltpu.VMEM((n_dev, R, C), F32), pltpu.SemaphoreType.DMA((7,)), pltpu.SemaphoreType.DMA((7,))],
    )(v)


def _add_pairs(gp, theirs, place, *, name, tr):
    K, H, C = theirs.shape
    tr = min(tr, H)
    assert H % tr == 0
    nb = H // tr

    def body(place_ref, a_ref, b_ref, o_ref):
        o_ref[...] = (a_ref[...].astype(F32) + b_ref[...].astype(F32)).astype(BF16)

    blk = pl.BlockSpec((1, tr, C), lambda k, i, p: (k, i, 0))
    mine = pl.BlockSpec((1, tr, C), lambda k, i, p: (k, i + p[1] * nb, 0))
    return pl.pallas_call(
        body, name=name,
        grid_spec=pltpu.PrefetchScalarGridSpec(num_scalar_prefetch=1, grid=(K, nb), in_specs=[mine, blk],
                                               out_specs=blk),
        out_shape=jax.ShapeDtypeStruct((K, H, C), BF16),
        compiler_params=_params(("parallel", "parallel"), 3 * _nbytes((tr, C), BF16)),
    )(place, gp, theirs)


def _add_final(gp, theirs, others, place, *, name, tr):
    K, H, C = theirs.shape
    tr = min(tr, H)
    assert H % tr == 0
    nb = H // tr

    def body(place_ref, a_ref, b_ref, o_ref_in, out_ref):
        acc = a_ref[0].astype(F32) + b_ref[0].astype(F32)
        for j in range(3):
            acc = acc + o_ref_in[j].astype(F32)
        out_ref[...] = acc

    return pl.pallas_call(
        body, name=name,
        grid_spec=pltpu.PrefetchScalarGridSpec(
            num_scalar_prefetch=1, grid=(nb,),
            in_specs=[pl.BlockSpec((1, tr, C), lambda i, p: (p[0], i + p[1] * nb, 0)),
                      pl.BlockSpec((1, tr, C), lambda i, p: (p[0], i, 0)),
                      pl.BlockSpec((3, tr, C), lambda i, p: (0, i, 0))],
            out_specs=pl.BlockSpec((tr, C), lambda i, p: (i + p[1] * nb, 0))),
        out_shape=jax.ShapeDtypeStruct((2 * H, C), F32),
        compiler_params=_params(("parallel",), 6 * _nbytes((tr, C), F32)),
    )(place, gp, theirs, others)


def _adamw(w, g, m, v, *, name, tr, row0=0):
    R, C = w.shape
    tr = min(tr, R)
    assert R % tr == 0 and row0 % tr == 0
    off = row0 // tr

    def body(w_ref, g_ref, m_ref, v_ref, go_ref, d_ref, mo_ref, vo_ref):
        gv = g_ref[...]
        m2 = ADAM_B1 * m_ref[...] + (1.0 - ADAM_B1) * gv
        v2 = ADAM_B2 * v_ref[...] + (1.0 - ADAM_B2) * jnp.square(gv)
        m_hat = m2 / (1.0 - ADAM_B1 ** ADAM_STEP)
        v_hat = v2 / (1.0 - ADAM_B2 ** ADAM_STEP)
        d_ref[...] = -ADAM_LR * (m_hat / (jnp.sqrt(v_hat) + ADAM_EPS) + ADAM_WD * w_ref[...])
        go_ref[...] = gv
        mo_ref[...] = m2
        vo_ref[...] = v2

    blk = pl.BlockSpec((tr, C), lambda i: (i, 0))
    shape = jax.ShapeDtypeStruct((R, C), F32)
    return pl.pallas_call(
        body, name=name, grid=(R // tr,),
        in_specs=[blk, pl.BlockSpec((tr, C), lambda i: (i + off, 0)), blk, blk], out_specs=[blk] * 4,
        out_shape=[shape] * 4,
        compiler_params=_params(("parallel",), 8 * _nbytes((tr, C), F32)),
    )(w, g, m, v)


BIG = ("w_up", "w_down", "w_in", "w_o")
IN_PLACE = ("w_up", "w_down")
BIG_AXIS = dict(w_in=2, w_o=1, w_up=2, w_down=1)
SMALL = ("pool_w", "pool_scale", "mix_norm_g", "ln1_g", "ln1_b", "ln2_g", "ln2_b")
CONV_ROWS = 64
SMALL_ROWS = 48


def _big_rows(shards):
    sizes = [shards[n].size // D_MODEL for n in BIG]
    starts = [sum(sizes[:i]) for i in range(len(sizes))]
    return sizes, starts


def _to_chips(a, axis):
    shape = list(a.shape)
    shape[axis:axis + 1] = [N_CHIPS, shape[axis] // N_CHIPS]
    return jnp.moveaxis(a.reshape(shape), axis, 0)


def _from_chips(a, axis):
    a = jnp.moveaxis(a, 0, axis)
    shape = list(a.shape)
    shape[axis:axis + 2] = [shape[axis] * shape[axis + 1]]
    return a.reshape(shape)


def _pad_rows(flat, rows):
    return jnp.pad(flat, (0, rows * D_MODEL - flat.shape[0])).reshape(rows, D_MODEL)


def kernel(x, w_in, conv_w, pool_w, pool_scale, mix_norm_g, w_o, ln1_g, ln1_b, w_up, w_down, ln2_g, ln2_b, loss_target, m_w_in, m_conv_w, m_pool_w, m_pool_scale, m_mix_norm_g, m_w_o, m_ln1_g, m_ln1_b, m_w_up, m_w_down, m_ln2_g, m_ln2_b, v_w_in, v_conv_w, v_pool_w, v_pool_scale, v_mix_norm_g, v_w_o, v_ln1_g, v_ln1_b, v_w_up, v_w_down, v_ln2_g, v_ln2_b):
    wts = dict(w_in=w_in, conv_w=conv_w, pool_w=pool_w, pool_scale=pool_scale, mix_norm_g=mix_norm_g, w_o=w_o,
               ln1_g=ln1_g, ln1_b=ln1_b, w_up=w_up, w_down=w_down, ln2_g=ln2_g, ln2_b=ln2_b)
    mom = dict(w_in=m_w_in, conv_w=m_conv_w, pool_w=m_pool_w, pool_scale=m_pool_scale, mix_norm_g=m_mix_norm_g,
               w_o=m_w_o, ln1_g=m_ln1_g, ln1_b=m_ln1_b, w_up=m_w_up, w_down=m_w_down, ln2_g=m_ln2_g, ln2_b=m_ln2_b)
    var = dict(w_in=v_w_in, conv_w=v_conv_w, pool_w=v_pool_w, pool_scale=v_pool_scale, mix_norm_g=v_mix_norm_g,
               w_o=v_w_o, ln1_g=v_ln1_g, ln1_b=v_ln1_b, w_up=v_w_up, w_down=v_w_down, ln2_g=v_ln2_g, ln2_b=v_ln2_b)
    chip = 2 * lax.axis_index("x") + lax.axis_index("y")
    sizes, starts = _big_rows(wts)
    big_rows = sum(sizes)

    conv_bits = lax.bitcast_convert_type(conv_w.reshape(-1), BF16).reshape(-1)
    pack = jnp.concatenate([wts[n].reshape(-1, D_MODEL).astype(BF16) for n in BIG]
                           + [_pad_rows(conv_bits, CONV_ROWS)], axis=0)
    gathered = _allgather_chips(pack, name="gather_weights")
    gathered = lax.dynamic_update_index_in_dim(gathered, pack, chip, 0)
    full = {}
    first_block = {}
    for n, size, start in zip(BIG, sizes, starts):
        if n in IN_PLACE:
            assert start % CHIP_BLOCK == 0 and size == DEPTH * CHIP_BLOCK
            first_block[n] = start // CHIP_BLOCK
            along = "cols" if BIG_AXIS[n] == 2 else "rows"
            full[n] = [_ChipWeight(gathered, first_block[n] + l, along) for l in range(DEPTH)]
        else:
            full[n] = _from_chips(gathered[:, start:start + size].reshape((N_CHIPS,) + wts[n].shape), BIG_AXIS[n])
    conv_parts = [lax.bitcast_convert_type(gathered[k, big_rows:].reshape(-1)[:2 * conv_w.size].reshape(-1, 2), F32)
                  .reshape(conv_w.shape) for k in range(N_CHIPS)]
    full["conv_w"] = jnp.concatenate(conv_parts, axis=2)
    for n in SMALL:
        full[n] = wts[n]

    gpack = jnp.zeros((N_CHIPS, big_rows, D_MODEL), BF16)
    lsum, grad_x, grads = _local_step(x[0], loss_target[0], full,
                                      grad_pack=(gpack, first_block["w_up"], first_block["w_down"]))

    others = [n for n in BIG if n not in IN_PLACE]
    rest = jnp.concatenate([_to_chips(grads[n], BIG_AXIS[n]).reshape(N_CHIPS, -1, D_MODEL) for n in others], axis=1)
    gpack = lax.dynamic_update_slice_in_dim(grads["pack"], rest.astype(BF16), starts[len(IN_PLACE)], axis=1)
    place = jnp.stack([chip, lax.axis_index("c")]).astype(jnp.int32)
    theirs = _swap_halves(gpack, name="grad_swap_cores")
    add_rows = big_rows // 8
    chip_sum = _add_pairs(gpack, theirs, place, name="grad_add_cores", tr=add_rows)
    from_chips = _scatter_chips(chip_sum, name="grad_scatter_chips")
    half_sum = _add_final(gpack, theirs, from_chips, place, name="grad_add_chips", tr=add_rows)
    gsum = _join_halves(half_sum, name="grad_join_cores")

    small_flat = jnp.concatenate([grads[n].reshape(-1) for n in SMALL] + [grads["conv_w"].reshape(-1),
                                                                          lsum.sum().reshape(1)])
    small_sum = _allreduce_small(_pad_rows(small_flat, SMALL_ROWS), name="allreduce_small").reshape(-1)
    gsmall = {}
    pos = 0
    for n in SMALL:
        gsmall[n] = small_sum[pos:pos + wts[n].size].reshape(wts[n].shape)
        pos += wts[n].size
    conv_full = small_sum[pos:pos + 4 * conv_w.size].reshape(DEPTH, 3, D_CONV)
    pos += 4 * conv_w.size
    loss = small_sum[pos]
    gsmall["conv_w"] = lax.dynamic_slice_in_dim(conv_full, chip * conv_w.shape[2], conv_w.shape[2], axis=2)

    out_g, out_d, out_m, out_v = {}, {}, {}, {}
    for n, size, start in zip(BIG, sizes, starts):
        shp = wts[n].shape
        g, row0 = gsum, start
        if shp[-1] != D_MODEL:
            g, row0 = gsum[start:start + size].reshape(-1, shp[-1]), 0
        res = _adamw(wts[n].reshape(-1, shp[-1]), g, mom[n].reshape(-1, shp[-1]), var[n].reshape(-1, shp[-1]),
                     name="adamw_" + n, tr=256, row0=row0)
        out_g[n], out_d[n], out_m[n], out_v[n] = [r.reshape(shp) for r in res]
    small_names = SMALL + ("conv_w",)
    packs = [_pad_rows(jnp.concatenate([d[n].reshape(-1) for n in small_names]), SMALL_ROWS)
             for d in (wts, gsmall, mom, var)]
    res = _adamw(*packs, name="adamw_small", tr=SMALL_ROWS)
    pos = 0
    for n in small_names:
        shp = wts[n].shape
        out_g[n], out_d[n], out_m[n], out_v[n] = [r.reshape(-1)[pos:pos + wts[n].size].reshape(shp) for r in res]
        pos += wts[n].size

    order = ("w_in", "conv_w", "pool_w", "pool_scale", "mix_norm_g", "w_o", "ln1_g", "ln1_b", "w_up", "w_down",
             "ln2_g", "ln2_b")
    return (loss, grad_x[None], *[out_g[n] for n in order], *[out_d[n] for n in order],
            *[out_m[n] for n in order], *[out_v[n] for n in order])
```

```python
import math
from typing import NamedTuple

import jax
import jax.numpy as jnp
from jax import lax
from jax.experimental import pallas as pl
from jax.experimental.pallas import tpu as pltpu

F32 = jnp.float32
BF16 = jnp.bfloat16
MESH = pl.DeviceIdType.MESH

D_MODEL = 1024
DEPTH = 2
HEAD_DIM = 64
D_SB = 512
D_CONV = 256
D_POOL = 256
D_QKV = 3 * D_SB
D_REST = 3 * D_CONV + D_POOL
D_FF = 4 * D_MODEL
ALPHA = (2 * DEPTH) ** 0.25
LN_EPS = 1e-5
RMS_EPS = 1e-6
SCALE = HEAD_DIM ** -0.5
N_CHIPS = 4
HALO = 16

ADAM_LR = 0.001
ADAM_B1 = 0.9
ADAM_B2 = 0.999
ADAM_EPS = 1e-08
ADAM_WD = 0.01
ADAM_STEP = 10

VMEM_V7X_BYTES = 64 * 1024 * 1024
VMEM_CAP_BYTES = VMEM_V7X_BYTES - 8 * 1024 * 1024


def _params(sem, block_bytes):
    limit = min(VMEM_CAP_BYTES, max(32 * 1024 * 1024, 3 * block_bytes))
    return pltpu.CompilerParams(dimension_semantics=sem, vmem_limit_bytes=limit)


def _nbytes(shape, dtype):
    return math.prod(shape) * jnp.dtype(dtype).itemsize


def _dot(a, b, dims=(((1,), (0,)), ((), ()))):
    return lax.dot_general(a, b, dims, preferred_element_type=F32)


NT = (((1,), (1,)), ((), ()))
TN = (((0,), (0,)), ((), ()))


def _split(x):
    hi = x.astype(BF16)
    lo = (x - hi.astype(F32)).astype(BF16)
    return hi, lo


def _sum8(x):
    r, c = x.shape
    return x.reshape(r // 8, 8, c).sum(axis=0)


def _ln_bwd_rows(dy, xh_ref, rs_ref, g_ref, dr_ref, dr16_ref, dg_ref, db_ref, first):
    xh = xh_ref[...]
    dxh = dy * g_ref[...]
    m1 = jnp.mean(dxh, axis=-1, keepdims=True)
    m2 = jnp.mean(dxh * xh, axis=-1, keepdims=True)
    dr = rs_ref[...] * (dxh - m1 - xh * m2)
    dr_ref[...] = dr
    dr16_ref[...] = dr.astype(BF16)
    pg = _sum8(dy * xh)
    pb = _sum8(dy)

    @pl.when(first)
    def _():
        dg_ref[...] = pg
        db_ref[...] = pb

    @pl.when(jnp.logical_not(first))
    def _():
        dg_ref[...] += pg
        db_ref[...] += pb


class _ChipWeight(NamedTuple):
    arr: jax.Array
    rb: int
    along: str


CHIP_BLOCK = 1024


def _matmul(a, b, *, name, tm, tn, tk, ta=False, tb=False, out_dtype=F32,
            epi=None, e=None, e_scale=1.0, relu2_out=False, out_chips=None, ln_bwd=None):
    M, K = (a.shape[1], a.shape[0]) if ta else a.shape
    chips = isinstance(b, _ChipWeight)
    split_k = chips and ((b.along == "cols") == tb)
    if chips:
        N = CHIP_BLOCK if split_k else N_CHIPS * CHIP_BLOCK
        assert K == (N_CHIPS * CHIP_BLOCK if split_k else CHIP_BLOCK) and not ta, (name, K)
        tn, tk = CHIP_BLOCK, K
    else:
        N = b.shape[0] if tb else b.shape[1]
    tm, tn, tk = min(tm, M), min(tn, N), min(tk, K)
    assert M % tm == 0 and N % tn == 0 and K % tk == 0, (name, M, N, K)
    nk = K // tk
    dims = (((0 if ta else 1,), (1 if tb else 0,)), ((), ()))
    n_in = 2 + (epi is not None) + (out_chips is not None) + (3 if ln_bwd is not None else 0)
    assert ln_bwd is None or (tn == N and out_chips is None and not relu2_out)

    def body(*refs):
        a_ref, b_ref = refs[0], refs[1]
        e_ref = refs[2] if epi is not None else None
        o_ref = refs[n_in]
        scr = refs[-1:]
        if not chips:
            p = _dot(a_ref[...].astype(BF16), b_ref[...].astype(BF16), dims)
        elif split_k:
            p = _dot(a_ref[:, 0:CHIP_BLOCK].astype(BF16), b_ref[0], dims)
            for c in range(1, N_CHIPS):
                p = p + _dot(a_ref[:, c * CHIP_BLOCK:(c + 1) * CHIP_BLOCK].astype(BF16), b_ref[c], dims)
        else:
            p = _dot(a_ref[...].astype(BF16), b_ref[0], dims)

        def finish(acc):
            if epi == "drelu2":
                acc = acc * (2.0 * jnp.maximum(e_ref[...].astype(F32), 0.0))
            elif epi == "add":
                acc = acc + e_scale * e_ref[...]
            if ln_bwd is not None:
                _ln_bwd_rows(acc, *refs[n_in - 3:n_in + 4], pl.program_id(0) == 0)
            elif out_chips is None:
                o_ref[...] = acc.astype(out_dtype)
            else:
                o_ref[0] = acc.astype(out_dtype)
            if relu2_out:
                refs[n_in + 1][...] = jnp.square(jnp.maximum(acc, 0.0)).astype(BF16)

        if nk == 1:
            finish(p)
        else:
            acc_ref = scr[0]
            k = pl.program_id(2)

            @pl.when(k == 0)
            def _():
                acc_ref[...] = p

            @pl.when(k > 0)
            def _():
                acc_ref[...] += p

            @pl.when(k == nk - 1)
            def _():
                finish(acc_ref[...])

    a_spec = pl.BlockSpec((tk, tm), lambda i, j, k: (k, i)) if ta else pl.BlockSpec((tm, tk), lambda i, j, k: (i, k))
    if chips:
        b_arr, rb = b.arr, b.rb
        nblk = N_CHIPS if split_k else 1
        b_spec = pl.BlockSpec((nblk, CHIP_BLOCK, CHIP_BLOCK),
                              (lambda i, j, k: (0, rb, 0)) if split_k else (lambda i, j, k: (j, rb, 0)))
    else:
        b_arr = b
        b_spec = pl.BlockSpec((tn, tk), lambda i, j, k: (j, k)) if tb else pl.BlockSpec((tk, tn), lambda i, j, k: (k, j))
    o_spec = pl.BlockSpec((tm, tn), lambda i, j, k: (i, j))
    in_specs = [a_spec, b_spec]
    args = [a, b_arr]
    nbytes = _nbytes((tm, tk), a.dtype) + _nbytes((tk, tn), b_arr.dtype) + 2 * _nbytes((tm, tn), F32)
    if epi is not None:
        in_specs.append(o_spec)
        args.append(e)
        nbytes += _nbytes((tm, tn), e.dtype)
    scratch = [pltpu.VMEM((tm, tn), F32)] if nk > 1 else []
    out_shape = [jax.ShapeDtypeStruct((M, N), out_dtype)]
    out_specs = [o_spec]
    aliases = {}
    if out_chips is not None:
        assert tm == tn == CHIP_BLOCK and not relu2_out and out_chips.arr.dtype == out_dtype
        orb = out_chips.rb
        out_specs = [pl.BlockSpec((1, CHIP_BLOCK, CHIP_BLOCK),
                                  (lambda i, j, k: (j, orb, 0)) if out_chips.along == "cols" else
                                  (lambda i, j, k: (i, orb, 0)))]
        out_shape = [jax.ShapeDtypeStruct(out_chips.arr.shape, out_dtype)]
        in_specs.append(pl.BlockSpec(memory_space=pl.ANY))
        args.append(out_chips.arr)
        aliases = {len(args) - 1: 0}
    if relu2_out:
        out_shape.append(jax.ShapeDtypeStruct((M, N), BF16))
        out_specs.append(o_spec)
        nbytes += _nbytes((tm, tn), BF16)
    sem = ("parallel", "parallel", "arbitrary")
    if ln_bwd is not None:
        xhat, rstd, gain = ln_bwd
        in_specs += [o_spec, pl.BlockSpec((tm, 1), lambda i, j, k: (i, 0)), pl.BlockSpec((1, N), lambda i, j, k: (0, 0))]
        args += [xhat, rstd, gain.reshape(1, N)]
        acc_spec = pl.BlockSpec((8, N), lambda i, j, k: (0, 0))
        out_specs = [o_spec, o_spec, acc_spec, acc_spec]
        out_shape = [jax.ShapeDtypeStruct((M, N), F32), jax.ShapeDtypeStruct((M, N), BF16),
                     jax.ShapeDtypeStruct((8, N), F32), jax.ShapeDtypeStruct((8, N), F32)]
        nbytes += 3 * _nbytes((tm, tn), F32)
        sem = ("arbitrary", "arbitrary", "arbitrary")
    res = pl.pallas_call(
        body, name=name,
        grid=(M // tm, N // tn, nk),
        in_specs=in_specs, out_specs=out_specs,
        out_shape=out_shape,
        scratch_shapes=scratch,
        input_output_aliases=aliases,
        compiler_params=_params(sem, nbytes),
    )(*args)
    return res if (relu2_out or ln_bwd is not None) else res[0]


def _matmul_ln(a, b, xres, g, bias, *, name, tm, tk):
    M, K = a.shape
    chips = isinstance(b, _ChipWeight)
    if chips:
        assert b.along == "rows" and K == N_CHIPS * CHIP_BLOCK
        N, tk = CHIP_BLOCK, K
    else:
        N = b.shape[1]
    tm, tk = min(tm, M), min(tk, K)
    assert M % tm == 0 and K % tk == 0 and N == D_MODEL
    nk = K // tk

    def body(a_ref, b_ref, x_ref, g_ref, bias_ref, y_ref, y16_ref, xh_ref, rs_ref, *scr):
        if chips:
            p = _dot(a_ref[:, 0:CHIP_BLOCK].astype(BF16), b_ref[0])
            for c in range(1, N_CHIPS):
                p = p + _dot(a_ref[:, c * CHIP_BLOCK:(c + 1) * CHIP_BLOCK].astype(BF16), b_ref[c])
        else:
            p = _dot(a_ref[...].astype(BF16), b_ref[...].astype(BF16))

        def finish(acc):
            r = ALPHA * x_ref[...] + acc
            mu = jnp.mean(r, axis=-1, keepdims=True)
            xc = r - mu
            var = jnp.mean(xc * xc, axis=-1, keepdims=True)
            rstd = lax.rsqrt(var + LN_EPS)
            xh = xc * rstd
            y = xh * g_ref[...] + bias_ref[...]
            y_ref[...] = y
            y16_ref[...] = y.astype(BF16)
            xh_ref[...] = xh
            rs_ref[...] = rstd

        if nk == 1:
            finish(p)
        else:
            acc_ref = scr[0]
            k = pl.program_id(1)

            @pl.when(k == 0)
            def _():
                acc_ref[...] = p

            @pl.when(k > 0)
            def _():
                acc_ref[...] += p

            @pl.when(k == nk - 1)
            def _():
                finish(acc_ref[...])

    row = pl.BlockSpec((tm, N), lambda i, k: (i, 0))
    vec = pl.BlockSpec((1, N), lambda i, k: (0, 0))
    if chips:
        b_arr, rb = b.arr, b.rb
        b_spec = pl.BlockSpec((N_CHIPS, CHIP_BLOCK, CHIP_BLOCK), lambda i, k: (0, rb, 0))
    else:
        b_arr = b
        b_spec = pl.BlockSpec((tk, N), lambda i, k: (k, 0))
    nbytes = _nbytes((tm, tk), a.dtype) + _nbytes((tk, N), b_arr.dtype) + 6 * _nbytes((tm, N), F32)
    scratch = [pltpu.VMEM((tm, N), F32)] if nk > 1 else []
    return pl.pallas_call(
        body, name=name,
        grid=(M // tm, nk),
        in_specs=[pl.BlockSpec((tm, tk), lambda i, k: (i, k)), b_spec, row, vec, vec],
        out_specs=[row, row, row, pl.BlockSpec((tm, 1), lambda i, k: (i, 0))],
        out_shape=[jax.ShapeDtypeStruct((M, N), F32), jax.ShapeDtypeStruct((M, N), BF16),
                   jax.ShapeDtypeStruct((M, N), F32), jax.ShapeDtypeStruct((M, 1), F32)],
        scratch_shapes=scratch,
        compiler_params=_params(("parallel", "arbitrary"), nbytes),
    )(a, b_arr, xres, g.reshape(1, N), bias.reshape(1, N))


def _loss_ln_bwd(y, tgt, xhat, rstd, g, *, name, tm):
    M, N = y.shape
    tm = min(tm, M)

    def body(y_ref, t_ref, xh_ref, rs_ref, g_ref, dr_ref, dr16_ref, dg_ref, db_ref, l_ref):
        first = pl.program_id(0) == 0
        d = y_ref[...] - t_ref[...]
        part = _sum8(d * d) * (0.5 / N)

        @pl.when(first)
        def _():
            l_ref[...] = part

        @pl.when(jnp.logical_not(first))
        def _():
            l_ref[...] += part

        _ln_bwd_rows(d * (1.0 / N), xh_ref, rs_ref, g_ref, dr_ref, dr16_ref, dg_ref, db_ref, first)

    row = pl.BlockSpec((tm, N), lambda i: (i, 0))
    acc = pl.BlockSpec((8, N), lambda i: (0, 0))
    return pl.pallas_call(
        body, name=name, grid=(M // tm,),
        in_specs=[row, row, row, pl.BlockSpec((tm, 1), lambda i: (i, 0)), pl.BlockSpec((1, N), lambda i: (0, 0))],
        out_specs=[row, row, acc, acc, acc],
        out_shape=[jax.ShapeDtypeStruct((M, N), F32), jax.ShapeDtypeStruct((M, N), BF16)]
        + [jax.ShapeDtypeStruct((8, N), F32)] * 3,
        compiler_params=_params(("arbitrary",), 6 * _nbytes((tm, N), F32)),
    )(y, tgt, xhat, rstd, g.reshape(1, N))


def _tri(n, kind):
    j = lax.broadcasted_iota(jnp.int32, (2 * n, n), 0) % n
    s = lax.broadcasted_iota(jnp.int32, (2 * n, n), 1)
    return ((j > s) if kind == "after" else (j < s)).astype(BF16)


LOG2E = 1.4426950408889634
DEAD = -104.0
NOT_VISITED = -1e30


def _log_terms(z):
    lse = jnp.log(1.0 + jnp.exp2(jnp.abs(z) * (-LOG2E)))
    logsig = jnp.minimum(z, 0.0) - lse
    return logsig, logsig - z


def _cumsum_mm(x, u2_ref):
    hi, lo = _split(x)
    return _dot(jnp.concatenate([hi, lo], axis=1), u2_ref[...])


def _head_rows(x2, scale):
    lane = lax.broadcasted_iota(jnp.int32, (1, 128), 1)
    zero = jnp.zeros_like(x2)
    both = jnp.concatenate([jnp.where(lane < HEAD_DIM, x2, zero), jnp.where(lane >= HEAD_DIM, x2, zero)], axis=0)
    return both * scale


def _causal_mask(i, ks, tq, tk):
    row = lax.broadcasted_iota(jnp.int32, (2 * tq, tk), 0)
    row = i * tq + jnp.where(row >= tq, row - tq, row)
    col = lax.broadcasted_iota(jnp.int32, (2 * tq, tk), 1)
    return (ks + col) < row


def _attn_fwd(qkv, *, name, tq, tk):
    S = qkv.shape[0]
    tq = tk = min(tq, tk, S)
    assert S % tq == 0 and S // tk <= 128
    tri = _tri(tk, "after")

    def body(q_ref, k_ref, v_ref, u_ref, o_ref, c_ref, qcat, oacc, cacc, call, ls_buf, tl_buf, l0_buf):
        i = pl.program_id(1)
        lane = lax.broadcasted_iota(jnp.int32, (1, 128), 1)
        qcat[...] = _head_rows(q_ref[...], SCALE)
        oacc[...] = jnp.zeros_like(oacc)
        cacc[...] = jnp.zeros_like(cacc)
        call[...] = jnp.full_like(call, NOT_VISITED)

        def scores(kb, masked, slot):
            ks = pl.multiple_of(jnp.maximum(kb, 0) * tk, tk)
            z = _dot(qcat[...], k_ref[pl.ds(ks, tk), :], NT)
            logsig, lom = _log_terms(z)
            if masked:
                msk = jnp.logical_and(_causal_mask(i, ks, tq, tk), kb >= 0)
                lom = jnp.where(msk, lom, 0.0)
                logsig = jnp.where(msk, logsig, -1e30)
            ls_buf[slot] = logsig
            tl_buf[slot] = _cumsum_mm(lom, u_ref)
            l0_buf[slot] = lom[:, 0:1]

        def weights(kb, slot):
            ks = pl.multiple_of(jnp.maximum(kb, 0) * tk, tk)
            tl = tl_buf[slot]
            c = cacc[...]
            call[...] = jnp.where(lane == kb, c, call[...])
            a = jnp.exp(ls_buf[slot] + tl + c).astype(BF16)
            oacc[...] += _dot(a, v_ref[pl.ds(ks, tk), :])
            cacc[...] = c + tl[:, 0:1] + l0_buf[slot]

        def pair(kb, masked, masked_next):
            scores(kb, masked, 0)
            scores(kb - 1, masked_next, 1)
            weights(kb, 0)
            weights(kb - 1, 1)

        pair(i, True, True)

        def live(state):
            t, cmax = state
            return jnp.logical_and(t < (i - 1) // 2, cmax > DEAD)

        def trip(state):
            t, _ = state
            pair(i - 2 - 2 * t, False, False)
            return t + 1, jnp.max(cacc[...])

        t_end, cmax = lax.while_loop(live, trip, (0, jnp.max(cacc[...])))

        left_over = jnp.logical_and(i >= 2, i % 2 == 0)
        still_live = jnp.logical_and(t_end == (i - 1) // 2, cmax > DEAD)

        @pl.when(jnp.logical_and(left_over, still_live))
        def _():
            pair(0, False, True)

        o_ref[...] = jnp.where(lane < HEAD_DIM, oacc[0:tq], oacc[tq:2 * tq])
        c_ref[...] = jnp.concatenate([call[0:tq], call[tq:2 * tq]], axis=1)

    nbytes = (_nbytes((tq, 128), BF16) + 2 * _nbytes((S, 128), BF16) + _nbytes((2 * tk, tk), BF16)
              + 8 * _nbytes((tq, 128), F32) + 14 * _nbytes((2 * tq, tk), F32))
    return pl.pallas_call(
        body, name=name, grid=(4, S // tq),
        in_specs=[pl.BlockSpec((tq, 128), lambda j, i: (i, j)),
                  pl.BlockSpec((S, 128), lambda j, i: (0, 4 + j)),
                  pl.BlockSpec((S, 128), lambda j, i: (0, 8 + j)),
                  pl.BlockSpec((2 * tk, tk), lambda j, i: (0, 0))],
        out_specs=[pl.BlockSpec((tq, 128), lambda j, i: (i, j)),
                   pl.BlockSpec((tq, 256), lambda j, i: (i, j))],
        out_shape=[jax.ShapeDtypeStruct((S, D_SB), F32), jax.ShapeDtypeStruct((S, 1024), F32)],
        scratch_shapes=[pltpu.VMEM((2 * tq, 128), BF16), pltpu.VMEM((2 * tq, 128), F32),
                        pltpu.VMEM((2 * tq, 1), F32), pltpu.VMEM((2 * tq, 128), F32),
                        pltpu.VMEM((2, 2 * tq, tk), F32), pltpu.VMEM((2, 2 * tq, tk), F32),
                        pltpu.VMEM((2, 2 * tq, 1), F32)],
        compiler_params=_params(("parallel", "arbitrary"), nbytes),
    )(qkv, qkv, qkv, tri)


def _attn_bwd(qkv, carry, do, *, name, tq, tk):
    S = qkv.shape[0]
    tq = tk = min(tq, tk, S)
    assert S % tq == 0 and S // tk <= 128
    nkb = S // tk
    nq = S // tq
    tri_after = _tri(tk, "after")
    tri_before = _tri(tk, "before")

    def body(q_ref, k_ref, v_ref, c_ref, do_ref, ua_ref, ub_ref, dq_ref, dk_ref, dv_ref,
             qcat, docat, qcat_t, docat_t, ccat, dqacc, pacc, dkt, dvt, ls_buf, tl_buf, da_buf):
        i = pl.program_id(1)
        lane = lax.broadcasted_iota(jnp.int32, (1, 128), 1)
        sub = lax.broadcasted_iota(jnp.int32, (128, 1), 0)
        q2 = q_ref[...]
        do2 = do_ref[...]
        qcat[...] = _head_rows(q2, SCALE)
        docat[...] = _head_rows(do2, 1.0).astype(BF16)
        qt = q2.astype(F32).T * SCALE
        dot_ = do2.T
        qcat_t[...] = jnp.concatenate([jnp.where(sub < HEAD_DIM, qt, 0.0), jnp.where(sub >= HEAD_DIM, qt, 0.0)],
                                      axis=1).astype(BF16)
        docat_t[...] = jnp.concatenate([jnp.where(sub < HEAD_DIM, dot_, 0.0), jnp.where(sub >= HEAD_DIM, dot_, 0.0)],
                                       axis=1).astype(BF16)
        ccat[0:tq] = c_ref[:, 0:128]
        ccat[tq:2 * tq] = c_ref[:, 128:256]

        @pl.when(i == 0)
        def _():
            dkt[...] = jnp.zeros_like(dkt)
            dvt[...] = jnp.zeros_like(dvt)

        dqacc[...] = jnp.zeros_like(dqacc)
        pacc[...] = jnp.zeros_like(pacc)

        def scores(kb, masked, slot):
            ks = pl.multiple_of(jnp.maximum(kb, 0) * tk, tk)
            z = _dot(qcat[...], k_ref[pl.ds(ks, tk), :], NT)
            logsig, lom = _log_terms(z)
            if masked:
                msk = jnp.logical_and(_causal_mask(i, ks, tq, tk), kb >= 0)
                lom = jnp.where(msk, lom, 0.0)
                logsig = jnp.where(msk, logsig, -1e30)
            ls_buf[slot] = logsig
            tl_buf[slot] = _cumsum_mm(lom, ua_ref)
            da_buf[slot] = _dot(docat[...], v_ref[pl.ds(ks, tk), :], NT)

        def grads(kb, slot):
            kbc = jnp.maximum(kb, 0)
            ks = pl.multiple_of(kbc * tk, tk)
            logsig = ls_buf[slot]
            c = jnp.sum(jnp.where(lane == kb, ccat[...], 0.0), axis=1, keepdims=True)
            a = jnp.exp(logsig + tl_buf[slot] + c)
            g = a * da_buf[slot]
            before = _cumsum_mm(g, ub_ref)
            pc = pacc[...]
            dz = g - jnp.exp(logsig) * (g + before + pc)
            dzb = dz.astype(BF16)
            dqacc[...] += _dot(dzb, k_ref[pl.ds(ks, tk), :])
            dkt[kbc] += _dot(qcat_t[...], dzb)
            dvt[kbc] += _dot(docat_t[...], a.astype(BF16))
            pacc[...] = pc + before[:, tk - 1:tk] + g[:, tk - 1:tk]

        def pair(kb, masked, masked_next):
            scores(kb, masked, 0)
            scores(kb + 1, masked_next, 1)
            grads(kb, 0)
            grads(kb + 1, 1)

        reach = jnp.max(ccat[...], axis=0, keepdims=True)
        first = jnp.min(jnp.where(reach > DEAD, lane, 128).astype(F32)).astype(jnp.int32)
        first = jnp.minimum(first, i)
        start = first - (i - first + 1) % 2

        @pl.when(jnp.logical_and(start < 0, i >= 2))
        def _():
            pair(-1, True, False)

        k0 = jnp.where(start < 0, 1, start)

        def loop(t, carry_):
            pair(k0 + 2 * t, False, False)
            return carry_

        lax.fori_loop(0, jnp.maximum((i - 1 - k0) // 2, 0), loop, 0)

        @pl.when(i == 0)
        def _():
            pair(-1, True, True)

        @pl.when(i > 0)
        def _():
            pair(i - 1, False, True)
        dq_ref[...] = (jnp.where(lane < HEAD_DIM, dqacc[0:tq], dqacc[tq:2 * tq]) * SCALE).astype(BF16)

        @pl.when(i == nq - 1)
        def _():
            for kb in range(nkb):
                dk_ref[kb * tk:(kb + 1) * tk, :] = dkt[kb].T.astype(BF16)
                dv_ref[kb * tk:(kb + 1) * tk, :] = dvt[kb].T.astype(BF16)

    nbytes = (_nbytes((tq, 128), BF16) + 2 * _nbytes((S, 128), BF16) + 2 * _nbytes((2 * tk, tk), BF16)
              + 12 * _nbytes((tq, 128), F32) + 4 * _nbytes((S, 128), F32) + 14 * _nbytes((2 * tq, tk), F32))
    blk = pl.BlockSpec((tq, 128), lambda j, i: (i, j))
    full = pl.BlockSpec((S, 128), lambda j, i: (0, j))
    tri_spec = pl.BlockSpec((2 * tk, tk), lambda j, i: (0, 0))
    dq, dk, dv = pl.pallas_call(
        body, name=name, grid=(4, nq),
        in_specs=[blk,
                  pl.BlockSpec((S, 128), lambda j, i: (0, 4 + j)),
                  pl.BlockSpec((S, 128), lambda j, i: (0, 8 + j)),
                  pl.BlockSpec((tq, 256), lambda j, i: (i, j)),
                  blk, tri_spec, tri_spec],
        out_specs=[blk, full, full],
        out_shape=[jax.ShapeDtypeStruct((S, D_SB), BF16)] * 3,
        scratch_shapes=[pltpu.VMEM((2 * tq, 128), BF16), pltpu.VMEM((2 * tq, 128), BF16),
                        pltpu.VMEM((128, 2 * tq), BF16), pltpu.VMEM((128, 2 * tq), BF16),
                        pltpu.VMEM((2 * tq, 128), F32), pltpu.VMEM((2 * tq, 128), F32), pltpu.VMEM((2 * tq, 1), F32),
                        pltpu.VMEM((nkb, 128, tk), F32), pltpu.VMEM((nkb, 128, tk), F32),
                        pltpu.VMEM((2, 2 * tq, tk), F32), pltpu.VMEM((2, 2 * tq, tk), F32),
                        pltpu.VMEM((2, 2 * tq, tk), F32)],
        compiler_params=_params(("parallel", "arbitrary"), nbytes),
    )(qkv, qkv, qkv, carry, do, tri_after, tri_before)
    return dq, dk, dv


def _group_mats():
    lanes = jnp.arange(D_MODEL) // HEAD_DIM
    gs = (lanes[:, None] == jnp.arange(128)[None, :]).astype(BF16)
    return gs, gs.T


def _group_sum_bcast(x, gs, gb):
    hi, lo = _split(x)
    s = _dot(hi, gs) + _dot(lo, gs)
    return _bcast(s, gb)


def _bcast(s, gb):
    hi, lo = _split(s)
    return _dot(hi, gb) + _dot(lo, gb)


def _pool_lane_consts():
    lane = lax.broadcasted_iota(jnp.int32, (1, D_POOL), 1)
    grp = lane // (D_POOL // 4)
    win = jnp.where(grp == 0, 2, jnp.where(grp == 1, 4, jnp.where(grp == 2, 8, 16)))
    return grp, win


def _by_group(grp, s2, s4, s8, s16):
    return jnp.where(grp == 0, s2, jnp.where(grp == 1, s4, jnp.where(grp == 2, s8, s16)))


def _mixers(i, ts, prev_ref, cur_ref, cw_ref, pw_ref, ps_ref):
    cur = cur_ref[...]
    prev = jnp.where(i == 0, 0.0, prev_ref[...])
    ext = jnp.concatenate([prev, cur], axis=0)

    def back(a, k):
        return pltpu.roll(a, k, 0)

    u = ext[:, D_CONV:2 * D_CONV] * ext[:, 2 * D_CONV:3 * D_CONV]
    p = ext[:, 3 * D_CONV:]
    cv = (cw_ref[0:1, :] * back(u, 2) + cw_ref[1:2, :] * back(u, 1) + cw_ref[2:3, :] * u)[HALO:]
    s2 = p + back(p, 1)
    s4 = s2 + back(s2, 2)
    s8 = s4 + back(s4, 4)
    s16 = s8 + back(s8, 8)
    grp, win = _pool_lane_consts()
    t1 = i * ts + 1 + lax.broadcasted_iota(jnp.int32, (ts, 1), 0)
    cnt = jnp.minimum(t1, win).astype(F32)
    pooled = _by_group(grp, s2, s4, s8, s16)[HALO:] / cnt - p[HALO:]
    yp = _dot(pooled.astype(BF16), pw_ref[...])
    return dict(b=cur[:, 0:D_CONV], u=u, cv=cv, pooled=pooled, yp=yp, cnt=cnt,
                conv_out=cur[:, 0:D_CONV] * cv, pool_out=yp * ps_ref[...])


def _halo_specs(ts, S, width):
    nb = ts // HALO
    last = S // HALO - 1
    prev = pl.BlockSpec((HALO, width), lambda i: (jnp.maximum(i * nb - 1, 0), 0))
    nxt = pl.BlockSpec((HALO, width), lambda i: (jnp.minimum((i + 1) * nb, last), 0))
    return prev, nxt


def _mixer_fwd(rest, attn, cw8, pwbd, ps, gain, *, name, ts):
    S = rest.shape[0]
    ts = min(ts, S)
    gs, gb = _group_mats()

    def body(prev_ref, cur_ref, attn_ref, cw_ref, pw_ref, ps_ref, gain_ref, gs_ref, gb_ref, o_ref):
        i = pl.program_id(0)
        f = _mixers(i, ts, prev_ref, cur_ref, cw_ref, pw_ref, ps_ref)
        mix = jnp.concatenate([attn_ref[...], f["conv_out"], f["pool_out"]], axis=1)
        ss = _group_sum_bcast(mix * mix, gs_ref[...], gb_ref[...])
        rinv = lax.rsqrt(ss * (1.0 / HEAD_DIM) + RMS_EPS)
        o_ref[...] = (mix * rinv * gain_ref[...]).astype(BF16)

    prev, _ = _halo_specs(ts, S, D_REST)
    row = lambda w: pl.BlockSpec((ts, w), lambda i: (i, 0))
    const = lambda a: pl.BlockSpec(a.shape, lambda i: (0, 0))
    nbytes = 12 * _nbytes((ts + HALO, D_REST), F32)
    return pl.pallas_call(
        body, name=name, grid=(S // ts,),
        in_specs=[prev, row(D_REST), row(D_SB), const(cw8), const(pwbd), const(ps), const(gain), const(gs), const(gb)],
        out_specs=row(D_MODEL),
        out_shape=jax.ShapeDtypeStruct((S, D_MODEL), BF16),
        compiler_params=_params(("parallel",), nbytes),
    )(rest, rest, attn, cw8, pwbd, ps, gain, gs, gb)


def _mixer_bwd1(dmixn, rest, attn, cw8, pwbd, ps, gain, *, name, ts):
    S = rest.shape[0]
    ts = min(ts, S)
    gs, gb = _group_mats()

    def body(dm_ref, prev_ref, cur_ref, attn_ref, cw_ref, pw_ref, ps_ref, gain_ref, gs_ref, gb_ref,
             da_ref, aux_ref, dg_ref, dsc_ref, dcw_ref, dpw_ref):
        i = pl.program_id(0)
        f = _mixers(i, ts, prev_ref, cur_ref, cw_ref, pw_ref, ps_ref)
        mix = jnp.concatenate([attn_ref[...], f["conv_out"], f["pool_out"]], axis=1)
        gsm, gbm = gs_ref[...], gb_ref[...]
        ss = _group_sum_bcast(mix * mix, gsm, gbm)
        rinv = lax.rsqrt(ss * (1.0 / HEAD_DIM) + RMS_EPS)
        dm = dm_ref[...]
        xn = mix * rinv
        dyg = dm * gain_ref[...]
        gm = _group_sum_bcast(dyg * xn, gsm, gbm) * (1.0 / HEAD_DIM)
        dmix = rinv * (dyg - xn * gm)
        da_ref[...] = dmix[:, 0:D_SB]
        dco = dmix[:, D_SB:D_SB + D_CONV]
        dpo = dmix[:, D_SB + D_CONV:]
        dcv = dco * f["b"]
        dyp = dpo * ps_ref[...]
        dpooled = _dot(dyp.astype(BF16), pw_ref[...], NT)
        aux_ref[...] = jnp.concatenate([dco * f["cv"], dcv, dpooled / f["cnt"], dpooled], axis=1)
        u = f["u"]
        parts = [
            _sum8(dm * xn),
            _sum8(dpo * f["yp"]),
            jnp.concatenate([_sum8(dcv * pltpu.roll(u, 2, 0)[HALO:]), _sum8(dcv * pltpu.roll(u, 1, 0)[HALO:]),
                             _sum8(dcv * u[HALO:])], axis=0),
            _dot(f["pooled"].astype(BF16), dyp.astype(BF16), TN),
        ]
        outs = [dg_ref, dsc_ref, dcw_ref, dpw_ref]

        @pl.when(i == 0)
        def _():
            for o, v in zip(outs, parts):
                o[...] = v

        @pl.when(i > 0)
        def _():
            for o, v in zip(outs, parts):
                o[...] += v

    prev, _ = _halo_specs(ts, S, D_REST)
    row = lambda w: pl.BlockSpec((ts, w), lambda i: (i, 0))
    const = lambda a: pl.BlockSpec(a.shape, lambda i: (0, 0))
    acc = lambda r_, w: pl.BlockSpec((r_, w), lambda i: (0, 0))
    nbytes = 16 * _nbytes((ts + HALO, D_REST), F32)
    return pl.pallas_call(
        body, name=name, grid=(S // ts,),
        in_specs=[row(D_MODEL), prev, row(D_REST), row(D_SB), const(cw8), const(pwbd), const(ps), const(gain),
                  const(gs), const(gb)],
        out_specs=[row(D_SB), row(D_REST), acc(8, D_MODEL), acc(8, D_POOL), acc(24, D_CONV), acc(D_POOL, D_POOL)],
        out_shape=[jax.ShapeDtypeStruct((S, D_SB), F32), jax.ShapeDtypeStruct((S, D_REST), F32),
                   jax.ShapeDtypeStruct((8, D_MODEL), F32), jax.ShapeDtypeStruct((8, D_POOL), F32),
                   jax.ShapeDtypeStruct((24, D_CONV), F32), jax.ShapeDtypeStruct((D_POOL, D_POOL), F32)],
        compiler_params=_params(("arbitrary",), nbytes),
    )(dmixn, rest, rest, attn, cw8, pwbd, ps, gain, gs, gb)


def _mixer_bwd2(aux, rest, cw8, *, name, ts):
    S = rest.shape[0]
    ts = min(ts, S)
    nblk = S // ts

    def body(cur_ref, nxt_ref, rest_ref, cw_ref, o_ref):
        i = pl.program_id(0)
        cur = cur_ref[...]
        nxt = jnp.where(i == nblk - 1, 0.0, nxt_ref[...])
        ext = jnp.concatenate([cur, nxt], axis=0)
        n = ts + HALO

        def fwd(a, k):
            return pltpu.roll(a, n - k, 0)

        dcv = ext[:, D_CONV:2 * D_CONV]
        dps = ext[:, 2 * D_CONV:3 * D_CONV]
        du = (cw_ref[2:3, :] * dcv + cw_ref[1:2, :] * fwd(dcv, 1) + cw_ref[0:1, :] * fwd(dcv, 2))[0:ts]
        f2 = dps + fwd(dps, 1)
        f4 = f2 + fwd(f2, 2)
        f8 = f4 + fwd(f4, 4)
        f16 = f8 + fwd(f8, 8)
        grp, _ = _pool_lane_consts()
        dp = _by_group(grp, f2, f4, f8, f16)[0:ts] - cur[:, 3 * D_CONV:]
        rest_v = rest_ref[...]
        c_gate = rest_v[:, D_CONV:2 * D_CONV]
        h = rest_v[:, 2 * D_CONV:3 * D_CONV]
        o_ref[...] = jnp.concatenate([cur[:, 0:D_CONV], du * h, du * c_gate, dp], axis=1).astype(BF16)

    _, nxt = _halo_specs(ts, S, D_REST)
    row = pl.BlockSpec((ts, D_REST), lambda i: (i, 0))
    return pl.pallas_call(
        body, name=name, grid=(nblk,),
        in_specs=[row, nxt, row, pl.BlockSpec(cw8.shape, lambda i: (0, 0))],
        out_specs=row,
        out_shape=jax.ShapeDtypeStruct((S, D_REST), BF16),
        compiler_params=_params(("parallel",), 10 * _nbytes((ts + HALO, D_REST), F32)),
    )(aux, aux, rest, cw8)


def _block_diag(pw):
    wide = jnp.tile(pw.reshape(256, 64), (1, 4))
    grp = jnp.arange(256) // 64
    return jnp.where(grp[:, None] == grp[None, :], wide, 0.0)


def _rows8(v, rows=8):
    return jnp.pad(v, ((0, rows - v.shape[0]), (0, 0)))


TILES = dict(tm=512, ts=512, tq=256, tk=256)


def _local_step(x, tgt, w, t=None, grad_pack=None):
    t = dict(TILES, **(t or {}))
    gp = None if grad_pack is None else grad_pack[0]
    tm, ts, tq, tk = t["tm"], t["ts"], t["tq"], t["tk"]
    big = dict(tm=1024, tn=1024)
    saved = []
    xl, xl16 = x, x.astype(BF16)
    for l in range(DEPTH):
        n = f"l{l}_"
        wq, wr = w["w_in"][l][:, :D_QKV], w["w_in"][l][:, D_QKV:]
        qkv = _matmul(xl16, wq, name=n + "proj_qkv", tm=1024, tn=D_QKV, tk=1024, out_dtype=BF16)
        rest = _matmul(xl16, wr, name=n + "proj_rest", tk=1024, **big)
        attn, carry = _attn_fwd(qkv, name=n + "attn_fwd", tq=tq, tk=tk)
        cw8 = _rows8(w["conv_w"][l])
        pwbd = _block_diag(w["pool_w"][l]).astype(BF16)
        ps = w["pool_scale"][l].reshape(1, D_POOL)
        gain = w["mix_norm_g"][l].reshape(1, D_MODEL)
        mixn = _mixer_fwd(rest, attn, cw8, pwbd, ps, gain, name=n + "mixer_fwd", ts=ts)
        x1, x1_16, xh1, rs1 = _matmul_ln(mixn, w["w_o"][l], xl, w["ln1_g"][l], w["ln1_b"][l], name=n + "wo_ln",
                                         tm=tm, tk=1024)
        hpre, hid = _matmul(x1_16, w["w_up"][l], name=n + "ffn_up", tk=1024, relu2_out=True, out_dtype=BF16, **big)
        x2, x2_16, xh2, rs2 = _matmul_ln(hid, w["w_down"][l], x1, w["ln2_g"][l], w["ln2_b"][l],
                                         name=n + "ffn_down_ln", tm=tm // 2, tk=D_FF)
        saved.append(dict(xin16=xl16, wq=wq, wr=wr, qkv=qkv, rest=rest, attn=attn, carry=carry, cw8=cw8, pwbd=pwbd,
                          ps=ps, gain=gain, mixn=mixn, x1_16=x1_16, xh1=xh1, rs1=rs1, hpre=hpre, hid=hid, xh2=xh2,
                          rs2=rs2))
        xl, xl16 = x2, x2_16

    top = saved[-1]
    ln2_back = _loss_ln_bwd(xl, tgt, top["xh2"], top["rs2"], w["ln2_g"][DEPTH - 1], name="loss_ln2_bwd", tm=tm)
    lsum = ln2_back[4]
    grads = {k: [None] * DEPTH for k in
             ("w_in", "conv_w", "pool_w", "pool_scale", "mix_norm_g", "w_o", "ln1_g", "ln1_b", "w_up", "w_down",
              "ln2_g", "ln2_b")}
    dw = dict(tk=2048, ta=True, out_dtype=BF16, **big)
    for l in reversed(range(DEPTH)):
        n = f"l{l}_"
        s = saved[l]
        dr2, dr2_16, dg2, db2 = ln2_back[:4]
        dhpre = _matmul(dr2_16, w["w_down"][l], name=n + "ffn_down_dx", tk=1024, tb=True, out_dtype=BF16,
                        epi="drelu2", e=s["hpre"], **big)
        if gp is None:
            grads["w_down"][l] = _matmul(s["hid"], dr2_16, name=n + "ffn_down_dw", **dw)
        else:
            gp = _matmul(s["hid"], dr2_16, name=n + "ffn_down_dw", out_chips=_ChipWeight(gp, grad_pack[2] + l, "rows"),
                         **dw)
        dr1, dr1_16, dg1, db1 = _matmul(dhpre, w["w_up"][l], name=n + "ffn_up_dx_ln1_bwd", tm=512, tn=1024, tk=D_FF,
                                        tb=True, epi="add", e=dr2, e_scale=ALPHA,
                                        ln_bwd=(s["xh1"], s["rs1"], w["ln1_g"][l]))
        if gp is None:
            grads["w_up"][l] = _matmul(s["x1_16"], dhpre, name=n + "ffn_up_dw", **dw)
        else:
            gp = _matmul(s["x1_16"], dhpre, name=n + "ffn_up_dw", out_chips=_ChipWeight(gp, grad_pack[1] + l, "cols"),
                         **dw)
        dmixn = _matmul(dr1_16, w["w_o"][l], name=n + "wo_dx", tk=1024, tb=True, **big)
        grads["w_o"][l] = _matmul(s["mixn"], dr1_16, name=n + "wo_dw", **dw)
        d_attn, aux, dgain, dsc, dcw, dpw = _mixer_bwd1(dmixn, s["rest"], s["attn"], s["cw8"], s["pwbd"], s["ps"],
                                                        s["gain"], name=n + "mixer_bwd1", ts=ts)
        drest = _mixer_bwd2(aux, s["rest"], s["cw8"], name=n + "mixer_bwd2", ts=ts)
        dqkv = jnp.concatenate(_attn_bwd(s["qkv"], s["carry"], d_attn, name=n + "attn_bwd", tq=tq, tk=tk), axis=1)
        dxa = _matmul(dqkv, s["wq"], name=n + "proj_qkv_dx", tk=D_QKV, tb=True, epi="add", e=dr1, e_scale=ALPHA,
                      **big)
        below = None if l == 0 else (saved[l - 1]["xh2"], saved[l - 1]["rs2"], w["ln2_g"][l - 1])
        ln2_back = _matmul(drest, s["wr"], name=n + "proj_rest_dx", tm=512, tn=1024, tk=1024, tb=True, epi="add", e=dxa,
                           e_scale=1.0, ln_bwd=below)
        dy = ln2_back
        dwq = _matmul(s["xin16"], dqkv, name=n + "proj_qkv_dw", tm=1024, tn=D_QKV, tk=1024, ta=True, out_dtype=BF16)
        dwr = _matmul(s["xin16"], drest, name=n + "proj_rest_dw", **dw)
        grads["w_in"][l] = jnp.concatenate([dwq, dwr], axis=1)
        grads["ln2_g"][l] = dg2.sum(0)
        grads["ln2_b"][l] = db2.sum(0)
        grads["ln1_g"][l] = dg1.sum(0)
        grads["ln1_b"][l] = db1.sum(0)
        grads["mix_norm_g"][l] = dgain.sum(0)
        grads["pool_scale"][l] = dsc.sum(0)
        grads["conv_w"][l] = dcw.reshape(3, 8, D_CONV).sum(1)
        grads["pool_w"][l] = jnp.stack([dpw[64 * g:64 * g + 64, 64 * g:64 * g + 64] for g in range(4)])
    grads = {k: jnp.stack(v) for k, v in grads.items() if v[0] is not None}
    if gp is not None:
        grads["pack"] = gp
    return lsum, dy, grads


ANY = pl.BlockSpec(memory_space=pl.ANY)


def _place():
    x, y, c = lax.axis_index("x"), lax.axis_index("y"), lax.axis_index("c")
    chips = [(1 - x, y), (x, 1 - y), (1 - x, 1 - y)]
    return x, y, c, chips


def _remote(src, dst, send_sems, recv_sems, k, to):
    return pltpu.make_async_remote_copy(src_ref=src, dst_ref=dst, send_sem=send_sems.at[k], recv_sem=recv_sems.at[k],
                                        device_id=to, device_id_type=MESH)


class _Copy:
    def __init__(self, src, dst, send_sems, recv_sems, k, to):
        self.args = (send_sems, recv_sems, k, to)
        self.copy = _remote(src, dst, *self.args)

    def like(self, src, dst):
        return _remote(src, dst, *self.args)

    def start(self):
        self.copy.start()

    def wait(self):
        self.copy.wait()

    def wait_send(self):
        self.copy.wait_send()

    def wait_recv(self):
        self.copy.wait_recv()


def _allgather_chips(pack, *, name):
    R, C = pack.shape
    H = R // 2
    Q = H // 2
    assert R % 64 == 0
    A, B = 0, 1

    def body(p_ref, o_ref, send_sems, recv_sems):
        x, y, c, _ = _place()
        my, kx, ky, kd = 2 * x + y, 2 * (1 - x) + y, 2 * x + (1 - y), 2 * (1 - x) + (1 - y)
        xn, yn, sib = (1 - x, y, c), (x, 1 - y, c), (x, y, 1 - c)

        def own(ab):
            return p_ref.at[pl.ds(c * H + ab * Q, Q), :]

        def quarter(k, hc, ab):
            return o_ref.at[k, pl.ds(hc * H + ab * Q, Q), :]

        def send(src, k, ab, sem, to):
            cp = _Copy(src, quarter(k, c, ab), send_sems, recv_sems, sem, to)
            cp.start()
            return cp

        def landed(sent, k, hc, ab):
            sent.like(quarter(k, hc, ab), quarter(k, hc, ab)).wait_recv()

        a_x = send(own(A), my, A, 0, xn)
        b_y = send(own(B), my, B, 3, yn)
        b_x = send(own(B), my, B, 1, xn)
        a_y = send(own(A), my, A, 4, yn)
        landed(b_y, ky, c, B)
        fb = send(quarter(ky, c, B), ky, B, 2, xn)
        landed(a_x, kx, c, A)
        fa = send(quarter(kx, c, A), kx, A, 5, yn)
        arrivals = [(kx, A, None), (ky, B, None), (kx, B, b_x), (ky, A, a_y), (kd, B, fb), (kd, A, fa)]
        passed = []
        for j, (k, ab, sent) in enumerate(arrivals):
            if sent is not None:
                landed(sent, k, c, ab)
            passed.append(_Copy(quarter(k, c, ab), quarter(k, c, ab), send_sems, recv_sems, 6 + j, sib))
            passed[-1].start()
        for j, (k, ab, _) in enumerate(arrivals):
            landed(passed[j], k, 1 - c, ab)
        for cp in [a_x, b_y, b_x, a_y, fb, fa] + passed:
            cp.wait_send()

    return pl.pallas_call(
        body, name=name, in_specs=[ANY], out_specs=ANY,
        out_shape=jax.ShapeDtypeStruct((N_CHIPS, R, C), pack.dtype),
        scratch_shapes=[pltpu.SemaphoreType.DMA((12,)), pltpu.SemaphoreType.DMA((12,))],
    )(pack)


def _swap_halves(gp, *, name):
    K, R, C = gp.shape
    H = R // 2

    def body(g_ref, theirs_ref, send_sems, recv_sems):
        x, y, c, _ = _place()
        cp = _Copy(g_ref.at[:, pl.ds((1 - c) * H, H), :], theirs_ref, send_sems, recv_sems, 0, (x, y, 1 - c))
        cp.start()
        cp.wait()

    return pl.pallas_call(
        body, name=name, in_specs=[ANY], out_specs=ANY, out_shape=jax.ShapeDtypeStruct((K, H, C), gp.dtype),
        scratch_shapes=[pltpu.SemaphoreType.DMA((1,)), pltpu.SemaphoreType.DMA((1,))],
    )(gp)


def _scatter_chips(part, *, name):
    K, H, C = part.shape

    def body(p_ref, o_ref, send_sems, recv_sems):
        x, y, c, chips = _place()
        copies = [_Copy(p_ref.at[2 * cx + cy], o_ref.at[j], send_sems, recv_sems, j, (cx, cy, c))
                  for j, (cx, cy) in enumerate(chips)]
        for cp in copies:
            cp.start()
        for cp in copies:
            cp.wait()

    return pl.pallas_call(
        body, name=name, in_specs=[ANY], out_specs=ANY,
        out_shape=jax.ShapeDtypeStruct((3, H, C), part.dtype),
        scratch_shapes=[pltpu.SemaphoreType.DMA((3,)), pltpu.SemaphoreType.DMA((3,))],
    )(part)


def _join_halves(both, *, name):
    H, C = both.shape[0] // 2, both.shape[1]

    def body(in_ref, o_ref, send_sems, recv_sems):
        x, y, c, _ = _place()
        mine = pl.ds(c * H, H)
        theirs = pl.ds((1 - c) * H, H)
        cp = _Copy(in_ref.at[mine, :], o_ref.at[mine, :], send_sems, recv_sems, 0, (x, y, 1 - c))
        cp.start()
        cp.wait_send()
        cp.like(in_ref.at[theirs, :], o_ref.at[theirs, :]).wait_recv()

    return pl.pallas_call(
        body, name=name, in_specs=[ANY], out_specs=ANY, input_output_aliases={0: 0},
        out_shape=jax.ShapeDtypeStruct(both.shape, both.dtype),
        scratch_shapes=[pltpu.SemaphoreType.DMA((1,)), pltpu.SemaphoreType.DMA((1,))],
    )(both)


def _allreduce_small(v, *, name):
    R, C = v.shape
    n_dev = 8

    def body(v_ref, o_ref, gat, send_sems, recv_sems):
        x, y, c, chips = _place()
        sib = (x, y, 1 - c)

        def rows(px, py, pc):
            return gat.at[4 * px + 2 * py + pc]

        gat[4 * x + 2 * y + c] = v_ref[...]
        first = [_remote(v_ref, rows(x, y, c), send_sems, recv_sems, 0, sib)]
        first += [_remote(v_ref, rows(x, y, c), send_sems, recv_sems, 1 + j, (cx, cy, c))
                  for j, (cx, cy) in enumerate(chips)]
        for cp in first:
            cp.start()
        passed = []
        for j, (cx, cy) in enumerate(chips):
            _remote(v_ref, rows(cx, cy, c), send_sems, recv_sems, 1 + j, sib).wait_recv()
            fwd = _remote(rows(cx, cy, c), rows(cx, cy, c), send_sems, recv_sems, 4 + j, sib)
            fwd.start()
            passed.append(fwd)
        _remote(v_ref, rows(x, y, 1 - c), send_sems, recv_sems, 0, sib).wait_recv()
        for j, (cx, cy) in enumerate(chips):
            _remote(v_ref, rows(cx, cy, 1 - c), send_sems, recv_sems, 4 + j, sib).wait_recv()
        for cp in first + passed:
            cp.wait_send()
        acc = gat[0]
        for d in range(1, n_dev):
            acc = acc + gat[d]
        o_ref[...] = acc

    vm = pl.BlockSpec(memory_space=pltpu.VMEM)
    return pl.pallas_call(
        body, name=name, in_specs=[vm], out_specs=vm,
        out_shape=jax.ShapeDtypeStruct((R, C), F32),
        scratch_shapes=[pltpu.VMEM((n_dev, R, C), F32), pltpu.SemaphoreType.DMA((7,)), pltpu.SemaphoreType.DMA((7,))],
    )(v)


def _add_pairs(gp, theirs, place, *, name, tr):
    K, H, C = theirs.shape
    tr = min(tr, H)
    assert H % tr == 0
    nb = H // tr

    def body(place_ref, a_ref, b_ref, o_ref):
        o_ref[...] = (a_ref[...].astype(F32) + b_ref[...].astype(F32)).astype(BF16)

    blk = pl.BlockSpec((1, tr, C), lambda k, i, p: (k, i, 0))
    mine = pl.BlockSpec((1, tr, C), lambda k, i, p: (k, i + p[1] * nb, 0))
    return pl.pallas_call(
        body, name=name,
        grid_spec=pltpu.PrefetchScalarGridSpec(num_scalar_prefetch=1, grid=(K, nb), in_specs=[mine, blk],
                                               out_specs=blk),
        out_shape=jax.ShapeDtypeStruct((K, H, C), BF16),
        compiler_params=_params(("parallel", "parallel"), 3 * _nbytes((tr, C), BF16)),
    )(place, gp, theirs)


def _add_final(gp, theirs, others, place, *, name, tr):
    K, H, C = theirs.shape
    tr = min(tr, H)
    assert H % tr == 0
    nb = H // tr

    def body(place_ref, a_ref, b_ref, o_ref_in, out_ref):
        acc = a_ref[0].astype(F32) + b_ref[0].astype(F32)
        for j in range(3):
            acc = acc + o_ref_in[j].astype(F32)
        out_ref[...] = acc

    return pl.pallas_call(
        body, name=name,
        grid_spec=pltpu.PrefetchScalarGridSpec(
            num_scalar_prefetch=1, grid=(nb,),
            in_specs=[pl.BlockSpec((1, tr, C), lambda i, p: (p[0], i + p[1] * nb, 0)),
                      pl.BlockSpec((1, tr, C), lambda i, p: (p[0], i, 0)),
                      pl.BlockSpec((3, tr, C), lambda i, p: (0, i, 0))],
            out_specs=pl.BlockSpec((tr, C), lambda i, p: (i + p[1] * nb, 0))),
        out_shape=jax.ShapeDtypeStruct((2 * H, C), F32),
        compiler_params=_params(("parallel",), 6 * _nbytes((tr, C), F32)),
    )(place, gp, theirs, others)


def _adamw(w, g, m, v, *, name, tr, row0=0):
    R, C = w.shape
    tr = min(tr, R)
    assert R % tr == 0 and row0 % tr == 0
    off = row0 // tr

    def body(w_ref, g_ref, m_ref, v_ref, go_ref, d_ref, mo_ref, vo_ref):
        gv = g_ref[...]
        m2 = ADAM_B1 * m_ref[...] + (1.0 - ADAM_B1) * gv
        v2 = ADAM_B2 * v_ref[...] + (1.0 - ADAM_B2) * jnp.square(gv)
        m_hat = m2 / (1.0 - ADAM_B1 ** ADAM_STEP)
        v_hat = v2 / (1.0 - ADAM_B2 ** ADAM_STEP)
        d_ref[...] = -ADAM_LR * (m_hat / (jnp.sqrt(v_hat) + ADAM_EPS) + ADAM_WD * w_ref[...])
        go_ref[...] = gv
        mo_ref[...] = m2
        vo_ref[...] = v2

    blk = pl.BlockSpec((tr, C), lambda i: (i, 0))
    shape = jax.ShapeDtypeStruct((R, C), F32)
    return pl.pallas_call(
        body, name=name, grid=(R // tr,),
        in_specs=[blk, pl.BlockSpec((tr, C), lambda i: (i + off, 0)), blk, blk], out_specs=[blk] * 4,
        out_shape=[shape] * 4,
        compiler_params=_params(("parallel",), 8 * _nbytes((tr, C), F32)),
    )(w, g, m, v)


BIG = ("w_up", "w_down", "w_in", "w_o")
IN_PLACE = ("w_up", "w_down")
BIG_AXIS = dict(w_in=2, w_o=1, w_up=2, w_down=1)
SMALL = ("pool_w", "pool_scale", "mix_norm_g", "ln1_g", "ln1_b", "ln2_g", "ln2_b")
CONV_ROWS = 64
SMALL_ROWS = 48


def _big_rows(shards):
    sizes = [shards[n].size // D_MODEL for n in BIG]
    starts = [sum(sizes[:i]) for i in range(len(sizes))]
    return sizes, starts


def _to_chips(a, axis):
    shape = list(a.shape)
    shape[axis:axis + 1] = [N_CHIPS, shape[axis] // N_CHIPS]
    return jnp.moveaxis(a.reshape(shape), axis, 0)


def _from_chips(a, axis):
    a = jnp.moveaxis(a, 0, axis)
    shape = list(a.shape)
    shape[axis:axis + 2] = [shape[axis] * shape[axis + 1]]
    return a.reshape(shape)


def _pad_rows(flat, rows):
    return jnp.pad(flat, (0, rows * D_MODEL - flat.shape[0])).reshape(rows, D_MODEL)


def kernel(x, w_in, conv_w, pool_w, pool_scale, mix_norm_g, w_o, ln1_g, ln1_b, w_up, w_down, ln2_g, ln2_b, loss_target, m_w_in, m_conv_w, m_pool_w, m_pool_scale, m_mix_norm_g, m_w_o, m_ln1_g, m_ln1_b, m_w_up, m_w_down, m_ln2_g, m_ln2_b, v_w_in, v_conv_w, v_pool_w, v_pool_scale, v_mix_norm_g, v_w_o, v_ln1_g, v_ln1_b, v_w_up, v_w_down, v_ln2_g, v_ln2_b):
    wts = dict(w_in=w_in, conv_w=conv_w, pool_w=pool_w, pool_scale=pool_scale, mix_norm_g=mix_norm_g, w_o=w_o,
               ln1_g=ln1_g, ln1_b=ln1_b, w_up=w_up, w_down=w_down, ln2_g=ln2_g, ln2_b=ln2_b)
    mom = dict(w_in=m_w_in, conv_w=m_conv_w, pool_w=m_pool_w, pool_scale=m_pool_scale, mix_norm_g=m_mix_norm_g,
               w_o=m_w_o, ln1_g=m_ln1_g, ln1_b=m_ln1_b, w_up=m_w_up, w_down=m_w_down, ln2_g=m_ln2_g, ln2_b=m_ln2_b)
    var = dict(w_in=v_w_in, conv_w=v_conv_w, pool_w=v_pool_w, pool_scale=v_pool_scale, mix_norm_g=v_mix_norm_g,
               w_o=v_w_o, ln1_g=v_ln1_g, ln1_b=v_ln1_b, w_up=v_w_up, w_down=v_w_down, ln2_g=v_ln2_g, ln2_b=v_ln2_b)
    chip = 2 * lax.axis_index("x") + lax.axis_index("y")
    sizes, starts = _big_rows(wts)
    big_rows = sum(sizes)

    conv_bits = lax.bitcast_convert_type(conv_w.reshape(-1), BF16).reshape(-1)
    pack = jnp.concatenate([wts[n].reshape(-1, D_MODEL).astype(BF16) for n in BIG]
                           + [_pad_rows(conv_bits, CONV_ROWS)], axis=0)
    gathered = _allgather_chips(pack, name="gather_weights")
    gathered = lax.dynamic_update_index_in_dim(gathered, pack, chip, 0)
    full = {}
    first_block = {}
    for n, size, start in zip(BIG, sizes, starts):
        if n in IN_PLACE:
            assert start % CHIP_BLOCK == 0 and size == DEPTH * CHIP_BLOCK
            first_block[n] = start // CHIP_BLOCK
            along = "cols" if BIG_AXIS[n] == 2 else "rows"
            full[n] = [_ChipWeight(gathered, first_block[n] + l, along) for l in range(DEPTH)]
        else:
            full[n] = _from_chips(gathered[:, start:start + size].reshape((N_CHIPS,) + wts[n].shape), BIG_AXIS[n])
    conv_parts = [lax.bitcast_convert_type(gathered[k, big_rows:].reshape(-1)[:2 * conv_w.size].reshape(-1, 2), F32)
                  .reshape(conv_w.shape) for k in range(N_CHIPS)]
    full["conv_w"] = jnp.concatenate(conv_parts, axis=2)
    for n in SMALL:
        full[n] = wts[n]

    gpack = jnp.zeros((N_CHIPS, big_rows, D_MODEL), BF16)
    lsum, grad_x, grads = _local_step(x[0], loss_target[0], full,
                                      grad_pack=(gpack, first_block["w_up"], first_block["w_down"]))

    others = [n for n in BIG if n not in IN_PLACE]
    rest = jnp.concatenate([_to_chips(grads[n], BIG_AXIS[n]).reshape(N_CHIPS, -1, D_MODEL) for n in others], axis=1)
    gpack = lax.dynamic_update_slice_in_dim(grads["pack"], rest.astype(BF16), starts[len(IN_PLACE)], axis=1)
    place = jnp.stack([chip, lax.axis_index("c")]).astype(jnp.int32)
    theirs = _swap_halves(gpack, name="grad_swap_cores")
    add_rows = big_rows // 8
    chip_sum = _add_pairs(gpack, theirs, place, name="grad_add_cores", tr=add_rows)
    from_chips = _scatter_chips(chip_sum, name="grad_scatter_chips")
    half_sum = _add_final(gpack, theirs, from_chips, place, name="grad_add_chips", tr=add_rows)
    gsum = _join_halves(half_sum, name="grad_join_cores")

    small_flat = jnp.concatenate([grads[n].reshape(-1) for n in SMALL] + [grads["conv_w"].reshape(-1),
                                                                          lsum.sum().reshape(1)])
    small_sum = _allreduce_small(_pad_rows(small_flat, SMALL_ROWS), name="allreduce_small").reshape(-1)
    gsmall = {}
    pos = 0
    for n in SMALL:
        gsmall[n] = small_sum[pos:pos + wts[n].size].reshape(wts[n].shape)
        pos += wts[n].size
    conv_full = small_sum[pos:pos + 4 * conv_w.size].reshape(DEPTH, 3, D_CONV)
    pos += 4 * conv_w.size
    loss = small_sum[pos]
    gsmall["conv_w"] = lax.dynamic_slice_in_dim(conv_full, chip * conv_w.shape[2], conv_w.shape[2], axis=2)

    out_g, out_d, out_m, out_v = {}, {}, {}, {}
    for n, size, start in zip(BIG, sizes, starts):
        shp = wts[n].shape
        g, row0 = gsum, start
        if shp[-1] != D_MODEL:
            g, row0 = gsum[start:start + size].reshape(-1, shp[-1]), 0
        res = _adamw(wts[n].reshape(-1, shp[-1]), g, mom[n].reshape(-1, shp[-1]), var[n].reshape(-1, shp[-1]),
                     name="adamw_" + n, tr=256, row0=row0)
        out_g[n], out_d[n], out_m[n], out_v[n] = [r.reshape(shp) for r in res]
    small_names = SMALL + ("conv_w",)
    packs = [_pad_rows(jnp.concatenate([d[n].reshape(-1) for n in small_names]), SMALL_ROWS)
             for d in (wts, gsmall, mom, var)]
    res = _adamw(*packs, name="adamw_small", tr=SMALL_ROWS)
    pos = 0
    for n in small_names:
        shp = wts[n].shape
        out_g[n], out_d[n], out_m[n], out_v[n] = [r.reshape(-1)[pos:pos + wts[n].size].reshape(shp) for r in res]
        pos += wts[n].size

    order = ("w_in", "conv_w", "pool_w", "pool_scale", "mix_norm_g", "w_o", "ln1_g", "ln1_b", "w_up", "w_down",
             "ln2_g", "ln2_b")
    return (loss, grad_x[None], *[out_g[n] for n in order], *[out_d[n] for n in order],
            *[out_m[n] for n in order], *[out_v[n] for n in order])
```

```python
import math
from typing import NamedTuple

import jax
import jax.numpy as jnp
from jax import lax
from jax.experimental import pallas as pl
from jax.experimental.pallas import tpu as pltpu

F32 = jnp.float32
BF16 = jnp.bfloat16
MESH = pl.DeviceIdType.MESH

D_MODEL = 1024
DEPTH = 2
HEAD_DIM = 64
D_SB = 512
D_CONV = 256
D_POOL = 256
D_QKV = 3 * D_SB
D_REST = 3 * D_CONV + D_POOL
D_FF = 4 * D_MODEL
ALPHA = (2 * DEPTH) ** 0.25
LN_EPS = 1e-5
RMS_EPS = 1e-6
SCALE = HEAD_DIM ** -0.5
N_CHIPS = 4
HALO = 16

ADAM_LR = 0.001
ADAM_B1 = 0.9
ADAM_B2 = 0.999
ADAM_EPS = 1e-08
ADAM_WD = 0.01
ADAM_STEP = 10

VMEM_V7X_BYTES = 64 * 1024 * 1024
VMEM_CAP_BYTES = VMEM_V7X_BYTES - 8 * 1024 * 1024


def _params(sem, block_bytes):
    limit = min(VMEM_CAP_BYTES, max(32 * 1024 * 1024, 3 * block_bytes))
    return pltpu.CompilerParams(dimension_semantics=sem, vmem_limit_bytes=limit)


def _nbytes(shape, dtype):
    return math.prod(shape) * jnp.dtype(dtype).itemsize


def _dot(a, b, dims=(((1,), (0,)), ((), ()))):
    return lax.dot_general(a, b, dims, preferred_element_type=F32)


NT = (((1,), (1,)), ((), ()))
TN = (((0,), (0,)), ((), ()))


def _split(x):
    hi = x.astype(BF16)
    lo = (x - hi.astype(F32)).astype(BF16)
    return hi, lo


def _sum8(x):
    r, c = x.shape
    return x.reshape(r // 8, 8, c).sum(axis=0)


def _ln_bwd_rows(dy, xh_ref, rs_ref, g_ref, dr_ref, dr16_ref, dg_ref, db_ref, first):
    xh = xh_ref[...]
    dxh = dy * g_ref[...]
    m1 = jnp.mean(dxh, axis=-1, keepdims=True)
    m2 = jnp.mean(dxh * xh, axis=-1, keepdims=True)
    dr = rs_ref[...] * (dxh - m1 - xh * m2)
    dr_ref[...] = dr
    dr16_ref[...] = dr.astype(BF16)
    pg = _sum8(dy * xh)
    pb = _sum8(dy)

    @pl.when(first)
    def _():
        dg_ref[...] = pg
        db_ref[...] = pb

    @pl.when(jnp.logical_not(first))
    def _():
        dg_ref[...] += pg
        db_ref[...] += pb


class _ChipWeight(NamedTuple):
    arr: jax.Array
    rb: int
    along: str


CHIP_BLOCK = 1024


def _matmul(a, b, *, name, tm, tn, tk, ta=False, tb=False, out_dtype=F32,
            epi=None, e=None, e_scale=1.0, relu2_out=False, out_chips=None, ln_bwd=None):
    M, K = (a.shape[1], a.shape[0]) if ta else a.shape
    chips = isinstance(b, _ChipWeight)
    split_k = chips and ((b.along == "cols") == tb)
    if chips:
        N = CHIP_BLOCK if split_k else N_CHIPS * CHIP_BLOCK
        assert K == (N_CHIPS * CHIP_BLOCK if split_k else CHIP_BLOCK) and not ta, (name, K)
        tn, tk = CHIP_BLOCK, K
    else:
        N = b.shape[0] if tb else b.shape[1]
    tm, tn, tk = min(tm, M), min(tn, N), min(tk, K)
    assert M % tm == 0 and N % tn == 0 and K % tk == 0, (name, M, N, K)
    nk = K // tk
    dims = (((0 if ta else 1,), (1 if tb else 0,)), ((), ()))
    n_in = 2 + (epi is not None) + (out_chips is not None) + (3 if ln_bwd is not None else 0)
    assert ln_bwd is None or (tn == N and out_chips is None and not relu2_out)

    def body(*refs):
        a_ref, b_ref = refs[0], refs[1]
        e_ref = refs[2] if epi is not None else None
        o_ref = refs[n_in]
        scr = refs[-1:]
        if not chips:
            p = _dot(a_ref[...].astype(BF16), b_ref[...].astype(BF16), dims)
        elif split_k:
            p = _dot(a_ref[:, 0:CHIP_BLOCK].astype(BF16), b_ref[0], dims)
            for c in range(1, N_CHIPS):
                p = p + _dot(a_ref[:, c * CHIP_BLOCK:(c + 1) * CHIP_BLOCK].astype(BF16), b_ref[c], dims)
        else:
            p = _dot(a_ref[...].astype(BF16), b_ref[0], dims)

        def finish(acc):
            if epi == "drelu2":
                acc = acc * (2.0 * jnp.maximum(e_ref[...].astype(F32), 0.0))
            elif epi == "add":
                acc = acc + e_scale * e_ref[...]
            if ln_bwd is not None:
                _ln_bwd_rows(acc, *refs[n_in - 3:n_in + 4], pl.program_id(0) == 0)
            elif out_chips is None:
                o_ref[...] = acc.astype(out_dtype)
            else:
                o_ref[0] = acc.astype(out_dtype)
            if relu2_out:
                refs[n_in + 1][...] = jnp.square(jnp.maximum(acc, 0.0)).astype(BF16)

        if nk == 1:
            finish(p)
        else:
            acc_ref = scr[0]
            k = pl.program_id(2)

            @pl.when(k == 0)
            def _():
                acc_ref[...] = p

            @pl.when(k > 0)
            def _():
                acc_ref[...] += p

            @pl.when(k == nk - 1)
            def _():
                finish(acc_ref[...])

    a_spec = pl.BlockSpec((tk, tm), lambda i, j, k: (k, i)) if ta else pl.BlockSpec((tm, tk), lambda i, j, k: (i, k))
    if chips:
        b_arr, rb = b.arr, b.rb
        nblk = N_CHIPS if split_k else 1
        b_spec = pl.BlockSpec((nblk, CHIP_BLOCK, CHIP_BLOCK),
                              (lambda i, j, k: (0, rb, 0)) if split_k else (lambda i, j, k: (j, rb, 0)))
    else:
        b_arr = b
        b_spec = pl.BlockSpec((tn, tk), lambda i, j, k: (j, k)) if tb else pl.BlockSpec((tk, tn), lambda i, j, k: (k, j))
    o_spec = pl.BlockSpec((tm, tn), lambda i, j, k: (i, j))
    in_specs = [a_spec, b_spec]
    args = [a, b_arr]
    nbytes = _nbytes((tm, tk), a.dtype) + _nbytes((tk, tn), b_arr.dtype) + 2 * _nbytes((tm, tn), F32)
    if epi is not None:
        in_specs.append(o_spec)
        args.append(e)
        nbytes += _nbytes((tm, tn), e.dtype)
    scratch = [pltpu.VMEM((tm, tn), F32)] if nk > 1 else []
    out_shape = [jax.ShapeDtypeStruct((M, N), out_dtype)]
    out_specs = [o_spec]
    aliases = {}
    if out_chips is not None:
        assert tm == tn == CHIP_BLOCK and not relu2_out and out_chips.arr.dtype == out_dtype
        orb = out_chips.rb
        out_specs = [pl.BlockSpec((1, CHIP_BLOCK, CHIP_BLOCK),
                                  (lambda i, j, k: (j, orb, 0)) if out_chips.along == "cols" else
                                  (lambda i, j, k: (i, orb, 0)))]
        out_shape = [jax.ShapeDtypeStruct(out_chips.arr.shape, out_dtype)]
        in_specs.append(pl.BlockSpec(memory_space=pl.ANY))
        args.append(out_chips.arr)
        aliases = {len(args) - 1: 0}
    if relu2_out:
        out_shape.append(jax.ShapeDtypeStruct((M, N), BF16))
        out_specs.append(o_spec)
        nbytes += _nbytes((tm, tn), BF16)
    sem = ("parallel", "parallel", "arbitrary")
    if ln_bwd is not None:
        xhat, rstd, gain = ln_bwd
        in_specs += [o_spec, pl.BlockSpec((tm, 1), lambda i, j, k: (i, 0)), pl.BlockSpec((1, N), lambda i, j, k: (0, 0))]
        args += [xhat, rstd, gain.reshape(1, N)]
        acc_spec = pl.BlockSpec((8, N), lambda i, j, k: (0, 0))
        out_specs = [o_spec, o_spec, acc_spec, acc_spec]
        out_shape = [jax.ShapeDtypeStruct((M, N), F32), jax.ShapeDtypeStruct((M, N), BF16),
                     jax.ShapeDtypeStruct((8, N), F32), jax.ShapeDtypeStruct((8, N), F32)]
        nbytes += 3 * _nbytes((tm, tn), F32)
        sem = ("arbitrary", "arbitrary", "arbitrary")
    res = pl.pallas_call(
        body, name=name,
        grid=(M // tm, N // tn, nk),
        in_specs=in_specs, out_specs=out_specs,
        out_shape=out_shape,
        scratch_shapes=scratch,
        input_output_aliases=aliases,
        compiler_params=_params(sem, nbytes),
    )(*args)
    return res if (relu2_out or ln_bwd is not None) else res[0]


def _matmul_ln(a, b, xres, g, bias, *, name, tm, tk, res_affine=None):
    M, K = a.shape
    chips = isinstance(b, _ChipWeight)
    if chips:
        assert b.along == "rows" and K == N_CHIPS * CHIP_BLOCK
        N, tk = CHIP_BLOCK, K
    else:
        N = b.shape[1]
    tm, tk = min(tm, M), min(tk, K)
    assert M % tm == 0 and K % tk == 0 and N == D_MODEL
    nk = K // tk

    n_vec = 2 if res_affine is None else 4

    def body(*refs):
        a_ref, b_ref, x_ref, g_ref, bias_ref = refs[:5]
        y16_ref, xh_ref, rs_ref = refs[3 + n_vec:6 + n_vec]
        scr = refs[6 + n_vec:]
        if chips:
            p = _dot(a_ref[:, 0:CHIP_BLOCK].astype(BF16), b_ref[0])
            for c in range(1, N_CHIPS):
                p = p + _dot(a_ref[:, c * CHIP_BLOCK:(c + 1) * CHIP_BLOCK].astype(BF16), b_ref[c])
        else:
            p = _dot(a_ref[...].astype(BF16), b_ref[...].astype(BF16))

        def finish(acc):
            xv = x_ref[...]
            if res_affine is not None:
                xv = xv * refs[5][...] + refs[6][...]
            r = ALPHA * xv + acc
            mu = jnp.mean(r, axis=-1, keepdims=True)
            xc = r - mu
            var = jnp.mean(xc * xc, axis=-1, keepdims=True)
            rstd = lax.rsqrt(var + LN_EPS)
            xh = xc * rstd
            y16_ref[...] = (xh * g_ref[...] + bias_ref[...]).astype(BF16)
            xh_ref[...] = xh
            rs_ref[...] = rstd

        if nk == 1:
            finish(p)
        else:
            acc_ref = scr[0]
            k = pl.program_id(1)

            @pl.when(k == 0)
            def _():
                acc_ref[...] = p

            @pl.when(k > 0)
            def _():
                acc_ref[...] += p

            @pl.when(k == nk - 1)
            def _():
                finish(acc_ref[...])

    row = pl.BlockSpec((tm, N), lambda i, k: (i, 0))
    vec = pl.BlockSpec((1, N), lambda i, k: (0, 0))
    if chips:
        b_arr, rb = b.arr, b.rb
        b_spec = pl.BlockSpec((N_CHIPS, CHIP_BLOCK, CHIP_BLOCK), lambda i, k: (0, rb, 0))
    else:
        b_arr = b
        b_spec = pl.BlockSpec((tk, N), lambda i, k: (k, 0))
    nbytes = _nbytes((tm, tk), a.dtype) + _nbytes((tk, N), b_arr.dtype) + 6 * _nbytes((tm, N), F32)
    scratch = [pltpu.VMEM((tm, N), F32)] if nk > 1 else []
    return pl.pallas_call(
        body, name=name,
        grid=(M // tm, nk),
        in_specs=[pl.BlockSpec((tm, tk), lambda i, k: (i, k)), b_spec, row] + [vec] * n_vec,
        out_specs=[row, row, pl.BlockSpec((tm, 1), lambda i, k: (i, 0))],
        out_shape=[jax.ShapeDtypeStruct((M, N), BF16), jax.ShapeDtypeStruct((M, N), F32),
                   jax.ShapeDtypeStruct((M, 1), F32)],
        scratch_shapes=scratch,
        compiler_params=_params(("parallel", "arbitrary"), nbytes),
    )(a, b_arr, xres, g.reshape(1, N), bias.reshape(1, N), *[v.reshape(1, N) for v in (res_affine or ())])


def _loss_ln_bwd(tgt, xhat, rstd, g, bias, *, name, tm):
    M, N = tgt.shape
    tm = min(tm, M)

    def body(t_ref, xh_ref, rs_ref, g_ref, bias_ref, dr_ref, dr16_ref, dg_ref, db_ref, l_ref):
        first = pl.program_id(0) == 0
        d = (xh_ref[...] * g_ref[...] + bias_ref[...]) - t_ref[...]
        part = _sum8(d * d) * (0.5 / N)

        @pl.when(first)
        def _():
            l_ref[...] = part

        @pl.when(jnp.logical_not(first))
        def _():
            l_ref[...] += part

        _ln_bwd_rows(d * (1.0 / N), xh_ref, rs_ref, g_ref, dr_ref, dr16_ref, dg_ref, db_ref, first)

    row = pl.BlockSpec((tm, N), lambda i: (i, 0))
    acc = pl.BlockSpec((8, N), lambda i: (0, 0))
    return pl.pallas_call(
        body, name=name, grid=(M // tm,),
        in_specs=[row, row, pl.BlockSpec((tm, 1), lambda i: (i, 0))] + [pl.BlockSpec((1, N), lambda i: (0, 0))] * 2,
        out_specs=[row, row, acc, acc, acc],
        out_shape=[jax.ShapeDtypeStruct((M, N), F32), jax.ShapeDtypeStruct((M, N), BF16)]
        + [jax.ShapeDtypeStruct((8, N), F32)] * 3,
        compiler_params=_params(("arbitrary",), 6 * _nbytes((tm, N), F32)),
    )(tgt, xhat, rstd, g.reshape(1, N), bias.reshape(1, N))


def _tri(n, kind):
    j = lax.broadcasted_iota(jnp.int32, (2 * n, n), 0) % n
    s = lax.broadcasted_iota(jnp.int32, (2 * n, n), 1)
    return ((j > s) if kind == "after" else (j < s)).astype(BF16)


LOG2E = 1.4426950408889634
DEAD = -104.0
NOT_VISITED = -1e30


def _log_terms(z):
    lse = jnp.log(1.0 + jnp.exp2(jnp.abs(z) * (-LOG2E)))
    logsig = jnp.minimum(z, 0.0) - lse
    return logsig, logsig - z


def _cumsum_mm(x, u2_ref):
    hi, lo = _split(x)
    return _dot(jnp.concatenate([hi, lo], axis=1), u2_ref[...])


def _head_rows(x2, scale):
    lane = lax.broadcasted_iota(jnp.int32, (1, 128), 1)
    zero = jnp.zeros_like(x2)
    both = jnp.concatenate([jnp.where(lane < HEAD_DIM, x2, zero), jnp.where(lane >= HEAD_DIM, x2, zero)], axis=0)
    return both * scale


def _causal_mask(i, ks, tq, tk):
    row = lax.broadcasted_iota(jnp.int32, (2 * tq, tk), 0)
    row = i * tq + jnp.where(row >= tq, row - tq, row)
    col = lax.broadcasted_iota(jnp.int32, (2 * tq, tk), 1)
    return (ks + col) < row


def _attn_fwd(qkv, *, name, tq, tk):
    S = qkv.shape[0]
    tq = tk = min(tq, tk, S)
    assert S % tq == 0 and S // tk <= 128
    tri = _tri(tk, "after")

    def body(q_ref, k_ref, v_ref, u_ref, o_ref, c_ref, qcat, oacc, cacc, call, ls_buf, tl_buf, l0_buf):
        i = pl.program_id(1)
        lane = lax.broadcasted_iota(jnp.int32, (1, 128), 1)
        qcat[...] = _head_rows(q_ref[...], SCALE)
        oacc[...] = jnp.zeros_like(oacc)
        cacc[...] = jnp.zeros_like(cacc)
        call[...] = jnp.full_like(call, NOT_VISITED)

        def scores(kb, masked, slot):
            ks = pl.multiple_of(jnp.maximum(kb, 0) * tk, tk)
            z = _dot(qcat[...], k_ref[pl.ds(ks, tk), :], NT)
            logsig, lom = _log_terms(z)
            if masked:
                msk = jnp.logical_and(_causal_mask(i, ks, tq, tk), kb >= 0)
                lom = jnp.where(msk, lom, 0.0)
                logsig = jnp.where(msk, logsig, -1e30)
            ls_buf[slot] = logsig
            tl_buf[slot] = _cumsum_mm(lom, u_ref)
            l0_buf[slot] = lom[:, 0:1]

        def weights(kb, slot):
            ks = pl.multiple_of(jnp.maximum(kb, 0) * tk, tk)
            tl = tl_buf[slot]
            c = cacc[...]
            call[...] = jnp.where(lane == kb, c, call[...])
            a = jnp.exp(ls_buf[slot] + tl + c).astype(BF16)
            oacc[...] += _dot(a, v_ref[pl.ds(ks, tk), :])
            cacc[...] = c + tl[:, 0:1] + l0_buf[slot]

        def pair(kb, masked, masked_next):
            scores(kb, masked, 0)
            scores(kb - 1, masked_next, 1)
            weights(kb, 0)
            weights(kb - 1, 1)

        pair(i, True, True)

        def live(state):
            t, cmax = state
            return jnp.logical_and(t < (i - 1) // 2, cmax > DEAD)

        def trip(state):
            t, _ = state
            pair(i - 2 - 2 * t, False, False)
            return t + 1, jnp.max(cacc[...])

        t_end, cmax = lax.while_loop(live, trip, (0, jnp.max(cacc[...])))

        left_over = jnp.logical_and(i >= 2, i % 2 == 0)
        still_live = jnp.logical_and(t_end == (i - 1) // 2, cmax > DEAD)

        @pl.when(jnp.logical_and(left_over, still_live))
        def _():
            pair(0, False, True)

        o_ref[...] = jnp.where(lane < HEAD_DIM, oacc[0:tq], oacc[tq:2 * tq])
        c_ref[...] = jnp.concatenate([call[0:tq], call[tq:2 * tq]], axis=1)

    nbytes = (_nbytes((tq, 128), BF16) + 2 * _nbytes((S, 128), BF16) + _nbytes((2 * tk, tk), BF16)
              + 8 * _nbytes((tq, 128), F32) + 14 * _nbytes((2 * tq, tk), F32))
    return pl.pallas_call(
        body, name=name, grid=(4, S // tq),
        in_specs=[pl.BlockSpec((tq, 128), lambda j, i: (i, j)),
                  pl.BlockSpec((S, 128), lambda j, i: (0, 4 + j)),
                  pl.BlockSpec((S, 128), lambda j, i: (0, 8 + j)),
                  pl.BlockSpec((2 * tk, tk), lambda j, i: (0, 0))],
        out_specs=[pl.BlockSpec((tq, 128), lambda j, i: (i, j)),
                   pl.BlockSpec((tq, 256), lambda j, i: (i, j))],
        out_shape=[jax.ShapeDtypeStruct((S, D_SB), F32), jax.ShapeDtypeStruct((S, 1024), F32)],
        scratch_shapes=[pltpu.VMEM((2 * tq, 128), BF16), pltpu.VMEM((2 * tq, 128), F32),
                        pltpu.VMEM((2 * tq, 1), F32), pltpu.VMEM((2 * tq, 128), F32),
                        pltpu.VMEM((2, 2 * tq, tk), F32), pltpu.VMEM((2, 2 * tq, tk), F32),
                        pltpu.VMEM((2, 2 * tq, 1), F32)],
        compiler_params=_params(("parallel", "arbitrary"), nbytes),
    )(qkv, qkv, qkv, tri)


def _attn_bwd(qkv, carry, do, *, name, tq, tk):
    S = qkv.shape[0]
    tq = tk = min(tq, tk, S)
    assert S % tq == 0 and S // tk <= 128
    nkb = S // tk
    nq = S // tq
    tri_after = _tri(tk, "after")
    tri_before = _tri(tk, "before")

    def body(q_ref, k_ref, v_ref, c_ref, do_ref, ua_ref, ub_ref, dq_ref, dk_ref, dv_ref,
             qcat, docat, qcat_t, docat_t, ccat, dqacc, pacc, dkt, dvt, ls_buf, tl_buf, da_buf):
        i = pl.program_id(1)
        lane = lax.broadcasted_iota(jnp.int32, (1, 128), 1)
        sub = lax.broadcasted_iota(jnp.int32, (128, 1), 0)
        q2 = q_ref[...]
        do2 = do_ref[...]
        qcat[...] = _head_rows(q2, SCALE)
        docat[...] = _head_rows(do2, 1.0).astype(BF16)
        qt = q2.astype(F32).T * SCALE
        dot_ = do2.T
        qcat_t[...] = jnp.concatenate([jnp.where(sub < HEAD_DIM, qt, 0.0), jnp.where(sub >= HEAD_DIM, qt, 0.0)],
                                      axis=1).astype(BF16)
        docat_t[...] = jnp.concatenate([jnp.where(sub < HEAD_DIM, dot_, 0.0), jnp.where(sub >= HEAD_DIM, dot_, 0.0)],
                                       axis=1).astype(BF16)
        ccat[0:tq] = c_ref[:, 0:128]
        ccat[tq:2 * tq] = c_ref[:, 128:256]

        @pl.when(i == 0)
        def _():
            dkt[...] = jnp.zeros_like(dkt)
            dvt[...] = jnp.zeros_like(dvt)

        dqacc[...] = jnp.zeros_like(dqacc)
        pacc[...] = jnp.zeros_like(pacc)

        def scores(kb, masked, slot):
            ks = pl.multiple_of(jnp.maximum(kb, 0) * tk, tk)
            z = _dot(qcat[...], k_ref[pl.ds(ks, tk), :], NT)
            logsig, lom = _log_terms(z)
            if masked:
                msk = jnp.logical_and(_causal_mask(i, ks, tq, tk), kb >= 0)
                lom = jnp.where(msk, lom, 0.0)
                logsig = jnp.where(msk, logsig, -1e30)
            ls_buf[slot] = logsig
            tl_buf[slot] = _cumsum_mm(lom, ua_ref)
            da_buf[slot] = _dot(docat[...], v_ref[pl.ds(ks, tk), :], NT)

        def grads(kb, slot):
            kbc = jnp.maximum(kb, 0)
            ks = pl.multiple_of(kbc * tk, tk)
            logsig = ls_buf[slot]
            c = jnp.sum(jnp.where(lane == kb, ccat[...], 0.0), axis=1, keepdims=True)
            a = jnp.exp(logsig + tl_buf[slot] + c)
            g = a * da_buf[slot]
            before = _cumsum_mm(g, ub_ref)
            pc = pacc[...]
            dz = g - jnp.exp(logsig) * (g + before + pc)
            dzb = dz.astype(BF16)
            dqacc[...] += _dot(dzb, k_ref[pl.ds(ks, tk), :])
            dkt[kbc] += _dot(qcat_t[...], dzb)
            dvt[kbc] += _dot(docat_t[...], a.astype(BF16))
            pacc[...] = pc + before[:, tk - 1:tk] + g[:, tk - 1:tk]

        def pair(kb, masked, masked_next):
            scores(kb, masked, 0)
            scores(kb + 1, masked_next, 1)
            grads(kb, 0)
            grads(kb + 1, 1)

        reach = jnp.max(ccat[...], axis=0, keepdims=True)
        first = jnp.min(jnp.where(reach > DEAD, lane, 128).astype(F32)).astype(jnp.int32)
        first = jnp.minimum(first, i)
        start = first - (i - first + 1) % 2

        @pl.when(jnp.logical_and(start < 0, i >= 2))
        def _():
            pair(-1, True, False)

        k0 = jnp.where(start < 0, 1, start)

        def loop(t, carry_):
            pair(k0 + 2 * t, False, False)
            return carry_

        lax.fori_loop(0, jnp.maximum((i - 1 - k0) // 2, 0), loop, 0)

        @pl.when(i == 0)
        def _():
            pair(-1, True, True)

        @pl.when(i > 0)
        def _():
            pair(i - 1, False, True)
        dq_ref[...] = (jnp.where(lane < HEAD_DIM, dqacc[0:tq], dqacc[tq:2 * tq]) * SCALE).astype(BF16)

        @pl.when(i == nq - 1)
        def _():
            for kb in range(nkb):
                dk_ref[kb * tk:(kb + 1) * tk, :] = dkt[kb].T.astype(BF16)
                dv_ref[kb * tk:(kb + 1) * tk, :] = dvt[kb].T.astype(BF16)

    nbytes = (_nbytes((tq, 128), BF16) + 2 * _nbytes((S, 128), BF16) + 2 * _nbytes((2 * tk, tk), BF16)
              + 12 * _nbytes((tq, 128), F32) + 4 * _nbytes((S, 128), F32) + 14 * _nbytes((2 * tq, tk), F32))
    blk = pl.BlockSpec((tq, 128), lambda j, i: (i, j))
    full = pl.BlockSpec((S, 128), lambda j, i: (0, j))
    tri_spec = pl.BlockSpec((2 * tk, tk), lambda j, i: (0, 0))
    dq, dk, dv = pl.pallas_call(
        body, name=name, grid=(4, nq),
        in_specs=[blk,
                  pl.BlockSpec((S, 128), lambda j, i: (0, 4 + j)),
                  pl.BlockSpec((S, 128), lambda j, i: (0, 8 + j)),
                  pl.BlockSpec((tq, 256), lambda j, i: (i, j)),
                  blk, tri_spec, tri_spec],
        out_specs=[blk, full, full],
        out_shape=[jax.ShapeDtypeStruct((S, D_SB), BF16)] * 3,
        scratch_shapes=[pltpu.VMEM((2 * tq, 128), BF16), pltpu.VMEM((2 * tq, 128), BF16),
                        pltpu.VMEM((128, 2 * tq), BF16), pltpu.VMEM((128, 2 * tq), BF16),
                        pltpu.VMEM((2 * tq, 128), F32), pltpu.VMEM((2 * tq, 128), F32), pltpu.VMEM((2 * tq, 1), F32),
                        pltpu.VMEM((nkb, 128, tk), F32), pltpu.VMEM((nkb, 128, tk), F32),
                        pltpu.VMEM((2, 2 * tq, tk), F32), pltpu.VMEM((2, 2 * tq, tk), F32),
                        pltpu.VMEM((2, 2 * tq, tk), F32)],
        compiler_params=_params(("parallel", "arbitrary"), nbytes),
    )(qkv, qkv, qkv, carry, do, tri_after, tri_before)
    return dq, dk, dv


def _group_mats():
    lanes = jnp.arange(D_MODEL) // HEAD_DIM
    gs = (lanes[:, None] == jnp.arange(128)[None, :]).astype(BF16)
    return gs, gs.T


def _group_sum_bcast(x, gs, gb):
    hi, lo = _split(x)
    s = _dot(hi, gs) + _dot(lo, gs)
    return _bcast(s, gb)


def _bcast(s, gb):
    hi, lo = _split(s)
    return _dot(hi, gb) + _dot(lo, gb)


def _pool_lane_consts():
    lane = lax.broadcasted_iota(jnp.int32, (1, D_POOL), 1)
    grp = lane // (D_POOL // 4)
    win = jnp.where(grp == 0, 2, jnp.where(grp == 1, 4, jnp.where(grp == 2, 8, 16)))
    return grp, win


def _by_group(grp, s2, s4, s8, s16):
    return jnp.where(grp == 0, s2, jnp.where(grp == 1, s4, jnp.where(grp == 2, s8, s16)))


def _mixers(i, ts, prev_ref, cur_ref, cw_ref, pw_ref, ps_ref):
    cur = cur_ref[...]
    prev = jnp.where(i == 0, 0.0, prev_ref[...])
    ext = jnp.concatenate([prev, cur], axis=0)

    def back(a, k):
        return pltpu.roll(a, k, 0)

    u = ext[:, D_CONV:2 * D_CONV] * ext[:, 2 * D_CONV:3 * D_CONV]
    p = ext[:, 3 * D_CONV:]
    cv = (cw_ref[0:1, :] * back(u, 2) + cw_ref[1:2, :] * back(u, 1) + cw_ref[2:3, :] * u)[HALO:]
    s2 = p + back(p, 1)
    s4 = s2 + back(s2, 2)
    s8 = s4 + back(s4, 4)
    s16 = s8 + back(s8, 8)
    grp, win = _pool_lane_consts()
    t1 = i * ts + 1 + lax.broadcasted_iota(jnp.int32, (ts, 1), 0)
    cnt = jnp.minimum(t1, win).astype(F32)
    pooled = _by_group(grp, s2, s4, s8, s16)[HALO:] / cnt - p[HALO:]
    yp = _dot(pooled.astype(BF16), pw_ref[...])
    return dict(b=cur[:, 0:D_CONV], u=u, cv=cv, pooled=pooled, yp=yp, cnt=cnt,
                conv_out=cur[:, 0:D_CONV] * cv, pool_out=yp * ps_ref[...])


def _halo_specs(ts, S, width):
    nb = ts // HALO
    last = S // HALO - 1
    prev = pl.BlockSpec((HALO, width), lambda i: (jnp.maximum(i * nb - 1, 0), 0))
    nxt = pl.BlockSpec((HALO, width), lambda i: (jnp.minimum((i + 1) * nb, last), 0))
    return prev, nxt


def _mixer_fwd(rest, attn, cw8, pwbd, ps, gain, *, name, ts):
    S = rest.shape[0]
    ts = min(ts, S)
    gs, gb = _group_mats()

    def body(prev_ref, cur_ref, attn_ref, cw_ref, pw_ref, ps_ref, gain_ref, gs_ref, gb_ref, o_ref):
        i = pl.program_id(0)
        f = _mixers(i, ts, prev_ref, cur_ref, cw_ref, pw_ref, ps_ref)
        mix = jnp.concatenate([attn_ref[...], f["conv_out"], f["pool_out"]], axis=1)
        ss = _group_sum_bcast(mix * mix, gs_ref[...], gb_ref[...])
        rinv = lax.rsqrt(ss * (1.0 / HEAD_DIM) + RMS_EPS)
        o_ref[...] = (mix * rinv * gain_ref[...]).astype(BF16)

    prev, _ = _halo_specs(ts, S, D_REST)
    row = lambda w: pl.BlockSpec((ts, w), lambda i: (i, 0))
    const = lambda a: pl.BlockSpec(a.shape, lambda i: (0, 0))
    nbytes = 12 * _nbytes((ts + HALO, D_REST), F32)
    return pl.pallas_call(
        body, name=name, grid=(S // ts,),
        in_specs=[prev, row(D_REST), row(D_SB), const(cw8), const(pwbd), const(ps), const(gain), const(gs), const(gb)],
        out_specs=row(D_MODEL),
        out_shape=jax.ShapeDtypeStruct((S, D_MODEL), BF16),
        compiler_params=_params(("parallel",), nbytes),
    )(rest, rest, attn, cw8, pwbd, ps, gain, gs, gb)


def _mixer_bwd1(dmixn, rest, attn, cw8, pwbd, ps, gain, *, name, ts):
    S = rest.shape[0]
    ts = min(ts, S)
    gs, gb = _group_mats()

    def body(dm_ref, prev_ref, cur_ref, attn_ref, cw_ref, pw_ref, ps_ref, gain_ref, gs_ref, gb_ref,
             da_ref, aux_ref, dg_ref, dsc_ref, dcw_ref, dpw_ref):
        i = pl.program_id(0)
        f = _mixers(i, ts, prev_ref, cur_ref, cw_ref, pw_ref, ps_ref)
        mix = jnp.concatenate([attn_ref[...], f["conv_out"], f["pool_out"]], axis=1)
        gsm, gbm = gs_ref[...], gb_ref[...]
        ss = _group_sum_bcast(mix * mix, gsm, gbm)
        rinv = lax.rsqrt(ss * (1.0 / HEAD_DIM) + RMS_EPS)
        dm = dm_ref[...]
        xn = mix * rinv
        dyg = dm * gain_ref[...]
        gm = _group_sum_bcast(dyg * xn, gsm, gbm) * (1.0 / HEAD_DIM)
        dmix = rinv * (dyg - xn * gm)
        da_ref[...] = dmix[:, 0:D_SB]
        dco = dmix[:, D_SB:D_SB + D_CONV]
        dpo = dmix[:, D_SB + D_CONV:]
        dcv = dco * f["b"]
        dyp = dpo * ps_ref[...]
        dpooled = _dot(dyp.astype(BF16), pw_ref[...], NT)
        aux_ref[...] = jnp.concatenate([dco * f["cv"], dcv, dpooled / f["cnt"], dpooled], axis=1)
        u = f["u"]
        parts = [
            _sum8(dm * xn),
            _sum8(dpo * f["yp"]),
            jnp.concatenate([_sum8(dcv * pltpu.roll(u, 2, 0)[HALO:]), _sum8(dcv * pltpu.roll(u, 1, 0)[HALO:]),
                             _sum8(dcv * u[HALO:])], axis=0),
            _dot(f["pooled"].astype(BF16), dyp.astype(BF16), TN),
        ]
        outs = [dg_ref, dsc_ref, dcw_ref, dpw_ref]

        @pl.when(i == 0)
        def _():
            for o, v in zip(outs, parts):
                o[...] = v

        @pl.when(i > 0)
        def _():
            for o, v in zip(outs, parts):
                o[...] += v

    prev, _ = _halo_specs(ts, S, D_REST)
    row = lambda w: pl.BlockSpec((ts, w), lambda i: (i, 0))
    const = lambda a: pl.BlockSpec(a.shape, lambda i: (0, 0))
    acc = lambda r_, w: pl.BlockSpec((r_, w), lambda i: (0, 0))
    nbytes = 16 * _nbytes((ts + HALO, D_REST), F32)
    return pl.pallas_call(
        body, name=name, grid=(S // ts,),
        in_specs=[row(D_MODEL), prev, row(D_REST), row(D_SB), const(cw8), const(pwbd), const(ps), const(gain),
                  const(gs), const(gb)],
        out_specs=[row(D_SB), row(D_REST), acc(8, D_MODEL), acc(8, D_POOL), acc(24, D_CONV), acc(D_POOL, D_POOL)],
        out_shape=[jax.ShapeDtypeStruct((S, D_SB), F32), jax.ShapeDtypeStruct((S, D_REST), F32),
                   jax.ShapeDtypeStruct((8, D_MODEL), F32), jax.ShapeDtypeStruct((8, D_POOL), F32),
                   jax.ShapeDtypeStruct((24, D_CONV), F32), jax.ShapeDtypeStruct((D_POOL, D_POOL), F32)],
        compiler_params=_params(("arbitrary",), nbytes),
    )(dmixn, rest, rest, attn, cw8, pwbd, ps, gain, gs, gb)


def _mixer_bwd2(aux, rest, cw8, *, name, ts):
    S = rest.shape[0]
    ts = min(ts, S)
    nblk = S // ts

    def body(cur_ref, nxt_ref, rest_ref, cw_ref, o_ref):
        i = pl.program_id(0)
        cur = cur_ref[...]
        nxt = jnp.where(i == nblk - 1, 0.0, nxt_ref[...])
        ext = jnp.concatenate([cur, nxt], axis=0)
        n = ts + HALO

        def fwd(a, k):
            return pltpu.roll(a, n - k, 0)

        dcv = ext[:, D_CONV:2 * D_CONV]
        dps = ext[:, 2 * D_CONV:3 * D_CONV]
        du = (cw_ref[2:3, :] * dcv + cw_ref[1:2, :] * fwd(dcv, 1) + cw_ref[0:1, :] * fwd(dcv, 2))[0:ts]
        f2 = dps + fwd(dps, 1)
        f4 = f2 + fwd(f2, 2)
        f8 = f4 + fwd(f4, 4)
        f16 = f8 + fwd(f8, 8)
        grp, _ = _pool_lane_consts()
        dp = _by_group(grp, f2, f4, f8, f16)[0:ts] - cur[:, 3 * D_CONV:]
        rest_v = rest_ref[...]
        c_gate = rest_v[:, D_CONV:2 * D_CONV]
        h = rest_v[:, 2 * D_CONV:3 * D_CONV]
        o_ref[...] = jnp.concatenate([cur[:, 0:D_CONV], du * h, du * c_gate, dp], axis=1).astype(BF16)

    _, nxt = _halo_specs(ts, S, D_REST)
    row = pl.BlockSpec((ts, D_REST), lambda i: (i, 0))
    return pl.pallas_call(
        body, name=name, grid=(nblk,),
        in_specs=[row, nxt, row, pl.BlockSpec(cw8.shape, lambda i: (0, 0))],
        out_specs=row,
        out_shape=jax.ShapeDtypeStruct((S, D_REST), BF16),
        compiler_params=_params(("parallel",), 10 * _nbytes((ts + HALO, D_REST), F32)),
    )(aux, aux, rest, cw8)


def _block_diag(pw):
    wide = jnp.tile(pw.reshape(256, 64), (1, 4))
    grp = jnp.arange(256) // 64
    return jnp.where(grp[:, None] == grp[None, :], wide, 0.0)


def _rows8(v, rows=8):
    return jnp.pad(v, ((0, rows - v.shape[0]), (0, 0)))


TILES = dict(tm=512, ts=512, tq=256, tk=256)


def _local_step(x, tgt, w, t=None, grad_pack=None):
    t = dict(TILES, **(t or {}))
    gp = None if grad_pack is None else grad_pack[0]
    tm, ts, tq, tk = t["tm"], t["ts"], t["tq"], t["tk"]
    big = dict(tm=1024, tn=1024)
    saved = []
    xl, xl_affine, xl16 = x, None, x.astype(BF16)
    for l in range(DEPTH):
        n = f"l{l}_"
        wq, wr = w["w_in"][l][:, :D_QKV], w["w_in"][l][:, D_QKV:]
        qkv = _matmul(xl16, wq, name=n + "proj_qkv", tm=1024, tn=D_QKV, tk=1024, out_dtype=BF16)
        rest = _matmul(xl16, wr, name=n + "proj_rest", tk=1024, **big)
        attn, carry = _attn_fwd(qkv, name=n + "attn_fwd", tq=tq, tk=tk)
        cw8 = _rows8(w["conv_w"][l])
        pwbd = _block_diag(w["pool_w"][l]).astype(BF16)
        ps = w["pool_scale"][l].reshape(1, D_POOL)
        gain = w["mix_norm_g"][l].reshape(1, D_MODEL)
        mixn = _mixer_fwd(rest, attn, cw8, pwbd, ps, gain, name=n + "mixer_fwd", ts=ts)
        x1_16, xh1, rs1 = _matmul_ln(mixn, w["w_o"][l], xl, w["ln1_g"][l], w["ln1_b"][l], name=n + "wo_ln",
                                     tm=tm, tk=1024, res_affine=xl_affine)
        hpre, hid = _matmul(x1_16, w["w_up"][l], name=n + "ffn_up", tk=1024, relu2_out=True, out_dtype=BF16, **big)
        x2_16, xh2, rs2 = _matmul_ln(hid, w["w_down"][l], xh1, w["ln2_g"][l], w["ln2_b"][l], name=n + "ffn_down_ln",
                                     tm=tm // 2, tk=D_FF, res_affine=(w["ln1_g"][l], w["ln1_b"][l]))
        saved.append(dict(xin16=xl16, wq=wq, wr=wr, qkv=qkv, rest=rest, attn=attn, carry=carry, cw8=cw8, pwbd=pwbd,
                          ps=ps, gain=gain, mixn=mixn, x1_16=x1_16, xh1=xh1, rs1=rs1, hpre=hpre, hid=hid, xh2=xh2,
                          rs2=rs2))
        xl, xl_affine, xl16 = xh2, (w["ln2_g"][l], w["ln2_b"][l]), x2_16

    top = saved[-1]
    ln2_back = _loss_ln_bwd(tgt, top["xh2"], top["rs2"], w["ln2_g"][DEPTH - 1], w["ln2_b"][DEPTH - 1],
                            name="loss_ln2_bwd", tm=tm)
    lsum = ln2_back[4]
    grads = {k: [None] * DEPTH for k in
             ("w_in", "conv_w", "pool_w", "pool_scale", "mix_norm_g", "w_o", "ln1_g", "ln1_b", "w_up", "w_down",
              "ln2_g", "ln2_b")}
    dw = dict(tk=2048, ta=True, out_dtype=BF16, **big)
    for l in reversed(range(DEPTH)):
        n = f"l{l}_"
        s = saved[l]
        dr2, dr2_16, dg2, db2 = ln2_back[:4]
        dhpre = _matmul(dr2_16, w["w_down"][l], name=n + "ffn_down_dx", tk=1024, tb=True, out_dtype=BF16,
                        epi="drelu2", e=s["hpre"], **big)
        if gp is None:
            grads["w_down"][l] = _matmul(s["hid"], dr2_16, name=n + "ffn_down_dw", **dw)
        else:
            gp = _matmul(s["hid"], dr2_16, name=n + "ffn_down_dw", out_chips=_ChipWeight(gp, grad_pack[2] + l, "rows"),
                         **dw)
        dr1, dr1_16, dg1, db1 = _matmul(dhpre, w["w_up"][l], name=n + "ffn_up_dx_ln1_bwd", tm=512, tn=1024, tk=D_FF,
                                        tb=True, epi="add", e=dr2, e_scale=ALPHA,
                                        ln_bwd=(s["xh1"], s["rs1"], w["ln1_g"][l]))
        if gp is None:
            grads["w_up"][l] = _matmul(s["x1_16"], dhpre, name=n + "ffn_up_dw", **dw)
        else:
            gp = _matmul(s["x1_16"], dhpre, name=n + "ffn_up_dw", out_chips=_ChipWeight(gp, grad_pack[1] + l, "cols"),
                         **dw)
        dmixn = _matmul(dr1_16, w["w_o"][l], name=n + "wo_dx", tk=1024, tb=True, **big)
        grads["w_o"][l] = _matmul(s["mixn"], dr1_16, name=n + "wo_dw", **dw)
        d_attn, aux, dgain, dsc, dcw, dpw = _mixer_bwd1(dmixn, s["rest"], s["attn"], s["cw8"], s["pwbd"], s["ps"],
                                                        s["gain"], name=n + "mixer_bwd1", ts=ts)
        drest = _mixer_bwd2(aux, s["rest"], s["cw8"], name=n + "mixer_bwd2", ts=ts)
        dqkv = jnp.concatenate(_attn_bwd(s["qkv"], s["carry"], d_attn, name=n + "attn_bwd", tq=tq, tk=tk), axis=1)
        dxa = _matmul(dqkv, s["wq"], name=n + "proj_qkv_dx", tk=D_QKV, tb=True, epi="add", e=dr1, e_scale=ALPHA,
                      **big)
        below = None if l == 0 else (saved[l - 1]["xh2"], saved[l - 1]["rs2"], w["ln2_g"][l - 1])
        ln2_back = _matmul(drest, s["wr"], name=n + "proj_rest_dx", tm=512, tn=1024, tk=1024, tb=True, epi="add", e=dxa,
                           e_scale=1.0, ln_bwd=below)
        dy = ln2_back
        dwq = _matmul(s["xin16"], dqkv, name=n + "proj_qkv_dw", tm=1024, tn=D_QKV, tk=1024, ta=True, out_dtype=BF16)
        dwr = _matmul(s["xin16"], drest, name=n + "proj_rest_dw", **dw)
        grads["w_in"][l] = jnp.concatenate([dwq, dwr], axis=1)
        grads["ln2_g"][l] = dg2.sum(0)
        grads["ln2_b"][l] = db2.sum(0)
        grads["ln1_g"][l] = dg1.sum(0)
        grads["ln1_b"][l] = db1.sum(0)
        grads["mix_norm_g"][l] = dgain.sum(0)
        grads["pool_scale"][l] = dsc.sum(0)
        grads["conv_w"][l] = dcw.reshape(3, 8, D_CONV).sum(1)
        grads["pool_w"][l] = jnp.stack([dpw[64 * g:64 * g + 64, 64 * g:64 * g + 64] for g in range(4)])
    grads = {k: jnp.stack(v) for k, v in grads.items() if v[0] is not None}
    if gp is not None:
        grads["pack"] = gp
    return lsum, dy, grads


ANY = pl.BlockSpec(memory_space=pl.ANY)


def _place():
    x, y, c = lax.axis_index("x"), lax.axis_index("y"), lax.axis_index("c")
    chips = [(1 - x, y), (x, 1 - y), (1 - x, 1 - y)]
    return x, y, c, chips


def _remote(src, dst, send_sems, recv_sems, k, to):
    return pltpu.make_async_remote_copy(src_ref=src, dst_ref=dst, send_sem=send_sems.at[k], recv_sem=recv_sems.at[k],
                                        device_id=to, device_id_type=MESH)


class _Copy:
    def __init__(self, src, dst, send_sems, recv_sems, k, to):
        self.args = (send_sems, recv_sems, k, to)
        self.copy = _remote(src, dst, *self.args)

    def like(self, src, dst):
        return _remote(src, dst, *self.args)

    def start(self):
        self.copy.start()

    def wait(self):
        self.copy.wait()

    def wait_send(self):
        self.copy.wait_send()

    def wait_recv(self):
        self.copy.wait_recv()


def _allgather_chips(pack, *, name):
    R, C = pack.shape
    H = R // 2
    Q = H // 2
    assert R % 64 == 0
    A, B = 0, 1

    def body(p_ref, o_ref, send_sems, recv_sems):
        x, y, c, _ = _place()
        my, kx, ky, kd = 2 * x + y, 2 * (1 - x) + y, 2 * x + (1 - y), 2 * (1 - x) + (1 - y)
        xn, yn, sib = (1 - x, y, c), (x, 1 - y, c), (x, y, 1 - c)

        def own(ab):
            return p_ref.at[pl.ds(c * H + ab * Q, Q), :]

        def quarter(k, hc, ab):
            return o_ref.at[k, pl.ds(hc * H + ab * Q, Q), :]

        def send(src, k, ab, sem, to):
            cp = _Copy(src, quarter(k, c, ab), send_sems, recv_sems, sem, to)
            cp.start()
            return cp

        def landed(sent, k, hc, ab):
            sent.like(quarter(k, hc, ab), quarter(k, hc, ab)).wait_recv()

        a_x = send(own(A), my, A, 0, xn)
        b_y = send(own(B), my, B, 3, yn)
        b_x = send(own(B), my, B, 1, xn)
        a_y = send(own(A), my, A, 4, yn)
        landed(b_y, ky, c, B)
        fb = send(quarter(ky, c, B), ky, B, 2, xn)
        landed(a_x, kx, c, A)
        fa = send(quarter(kx, c, A), kx, A, 5, yn)
        arrivals = [(kx, A, None), (ky, B, None), (kx, B, b_x), (ky, A, a_y), (kd, B, fb), (kd, A, fa)]
        passed = []
        for j, (k, ab, sent) in enumerate(arrivals):
            if sent is not None:
                landed(sent, k, c, ab)
            passed.append(_Copy(quarter(k, c, ab), quarter(k, c, ab), send_sems, recv_sems, 6 + j, sib))
            passed[-1].start()
        for j, (k, ab, _) in enumerate(arrivals):
            landed(passed[j], k, 1 - c, ab)
        for cp in [a_x, b_y, b_x, a_y, fb, fa] + passed:
            cp.wait_send()

    return pl.pallas_call(
        body, name=name, in_specs=[ANY], out_specs=ANY,
        out_shape=jax.ShapeDtypeStruct((N_CHIPS, R, C), pack.dtype),
        scratch_shapes=[pltpu.SemaphoreType.DMA((12,)), pltpu.SemaphoreType.DMA((12,))],
    )(pack)


def _swap_halves(gp, *, name):
    K, R, C = gp.shape
    H = R // 2

    def body(g_ref, theirs_ref, send_sems, recv_sems):
        x, y, c, _ = _place()
        cp = _Copy(g_ref.at[:, pl.ds((1 - c) * H, H), :], theirs_ref, send_sems, recv_sems, 0, (x, y, 1 - c))
        cp.start()
        cp.wait()

    return pl.pallas_call(
        body, name=name, in_specs=[ANY], out_specs=ANY, out_shape=jax.ShapeDtypeStruct((K, H, C), gp.dtype),
        scratch_shapes=[pltpu.SemaphoreType.DMA((1,)), pltpu.SemaphoreType.DMA((1,))],
    )(gp)


def _scatter_chips(part, *, name):
    K, H, C = part.shape

    def body(p_ref, o_ref, send_sems, recv_sems):
        x, y, c, chips = _place()
        copies = [_Copy(p_ref.at[2 * cx + cy], o_ref.at[j], send_sems, recv_sems, j, (cx, cy, c))
                  for j, (cx, cy) in enumerate(chips)]
        for cp in copies:
            cp.start()
        for cp in copies:
            cp.wait()

    return pl.pallas_call(
        body, name=name, in_specs=[ANY], out_specs=ANY,
        out_shape=jax.ShapeDtypeStruct((3, H, C), part.dtype),
        scratch_shapes=[pltpu.SemaphoreType.DMA((3,)), pltpu.SemaphoreType.DMA((3,))],
    )(part)


def _join_halves(both, *, name):
    H, C = both.shape[0] // 2, both.shape[1]

    def body(in_ref, o_ref, send_sems, recv_sems):
        x, y, c, _ = _place()
        mine = pl.ds(c * H, H)
        theirs = pl.ds((1 - c) * H, H)
        cp = _Copy(in_ref.at[mine, :], o_ref.at[mine, :], send_sems, recv_sems, 0, (x, y, 1 - c))
        cp.start()
        cp.wait_send()
        cp.like(in_ref.at[theirs, :], o_ref.at[theirs, :]).wait_recv()

    return pl.pallas_call(
        body, name=name, in_specs=[ANY], out_specs=ANY, input_output_aliases={0: 0},
        out_shape=jax.ShapeDtypeStruct(both.shape, both.dtype),
        scratch_shapes=[pltpu.SemaphoreType.DMA((1,)), pltpu.SemaphoreType.DMA((1,))],
    )(both)


def _allreduce_small(v, *, name):
    R, C = v.shape
    n_dev = 8

    def body(v_ref, o_ref, gat, send_sems, recv_sems):
        x, y, c, chips = _place()
        sib = (x, y, 1 - c)

        def rows(px, py, pc):
            return gat.at[4 * px + 2 * py + pc]

        gat[4 * x + 2 * y + c] = v_ref[...]
        first = [_remote(v_ref, rows(x, y, c), send_sems, recv_sems, 0, sib)]
        first += [_remote(v_ref, rows(x, y, c), send_sems, recv_sems, 1 + j, (cx, cy, c))
                  for j, (cx, cy) in enumerate(chips)]
        for cp in first:
            cp.start()
        passed = []
        for j, (cx, cy) in enumerate(chips):
            _remote(v_ref, rows(cx, cy, c), send_sems, recv_sems, 1 + j, sib).wait_recv()
            fwd = _remote(rows(cx, cy, c), rows(cx, cy, c), send_sems, recv_sems, 4 + j, sib)
            fwd.start()
            passed.append(fwd)
        _remote(v_ref, rows(x, y, 1 - c), send_sems, recv_sems, 0, sib).wait_recv()
        for j, (cx, cy) in enumerate(chips):
            _remote(v_ref, rows(cx, cy, 1 - c), send_sems, recv_sems, 4 + j, sib).wait_recv()
        for cp in first + passed:
            cp.wait_send()
        acc = gat[0]
        for d in range(1, n_dev):
            acc = acc + gat[d]
        o_ref[...] = acc

    vm = pl.BlockSpec(memory_space=pltpu.VMEM)
    return pl.pallas_call(
        body, name=name, in_specs=[vm], out_specs=vm,
        out_shape=jax.ShapeDtypeStruct((R, C), F32),
        scratch_shapes=[pltpu.VMEM((n_dev, R, C), F32), pltpu.SemaphoreType.DMA((7,)), pltpu.SemaphoreType.DMA((7,))],
    )(v)


def _add_pairs(gp, theirs, place, *, name, tr):
    K, H, C = theirs.shape
    tr = min(tr, H)
    assert H % tr == 0
    nb = H // tr

    def body(place_ref, a_ref, b_ref, o_ref):
        o_ref[...] = (a_ref[...].astype(F32) + b_ref[...].astype(F32)).astype(BF16)

    blk = pl.BlockSpec((1, tr, C), lambda k, i, p: (k, i, 0))
    mine = pl.BlockSpec((1, tr, C), lambda k, i, p: (k, i + p[1] * nb, 0))
    return pl.pallas_call(
        body, name=name,
        grid_spec=pltpu.PrefetchScalarGridSpec(num_scalar_prefetch=1, grid=(K, nb), in_specs=[mine, blk],
                                               out_specs=blk),
        out_shape=jax.ShapeDtypeStruct((K, H, C), BF16),
        compiler_params=_params(("parallel", "parallel"), 3 * _nbytes((tr, C), BF16)),
    )(place, gp, theirs)


def _add_final(gp, theirs, others, place, *, name, tr):
    K, H, C = theirs.shape
    tr = min(tr, H)
    assert H % tr == 0
    nb = H // tr

    def body(place_ref, a_ref, b_ref, o_ref_in, out_ref):
        acc = a_ref[0].astype(F32) + b_ref[0].astype(F32)
        for j in range(3):
            acc = acc + o_ref_in[j].astype(F32)
        out_ref[...] = acc

    return pl.pallas_call(
        body, name=name,
        grid_spec=pltpu.PrefetchScalarGridSpec(
            num_scalar_prefetch=1, grid=(nb,),
            in_specs=[pl.BlockSpec((1, tr, C), lambda i, p: (p[0], i + p[1] * nb, 0)),
                      pl.BlockSpec((1, tr, C), lambda i, p: (p[0], i, 0)),
                      pl.BlockSpec((3, tr, C), lambda i, p: (0, i, 0))],
            out_specs=pl.BlockSpec((tr, C), lambda i, p: (i + p[1] * nb, 0))),
        out_shape=jax.ShapeDtypeStruct((2 * H, C), F32),
        compiler_params=_params(("parallel",), 6 * _nbytes((tr, C), F32)),
    )(place, gp, theirs, others)


def _adamw(w, g, m, v, *, name, tr, row0=0):
    R, C = w.shape
    tr = min(tr, R)
    assert R % tr == 0 and row0 % tr == 0
    off = row0 // tr

    def body(w_ref, g_ref, m_ref, v_ref, go_ref, d_ref, mo_ref, vo_ref):
        gv = g_ref[...]
        m2 = ADAM_B1 * m_ref[...] + (1.0 - ADAM_B1) * gv
        v2 = ADAM_B2 * v_ref[...] + (1.0 - ADAM_B2) * jnp.square(gv)
        m_hat = m2 / (1.0 - ADAM_B1 ** ADAM_STEP)
        v_hat = v2 / (1.0 - ADAM_B2 ** ADAM_STEP)
        d_ref[...] = -ADAM_LR * (m_hat / (jnp.sqrt(v_hat) + ADAM_EPS) + ADAM_WD * w_ref[...])
        go_ref[...] = gv
        mo_ref[...] = m2
        vo_ref[...] = v2

    blk = pl.BlockSpec((tr, C), lambda i: (i, 0))
    shape = jax.ShapeDtypeStruct((R, C), F32)
    return pl.pallas_call(
        body, name=name, grid=(R // tr,),
        in_specs=[blk, pl.BlockSpec((tr, C), lambda i: (i + off, 0)), blk, blk], out_specs=[blk] * 4,
        out_shape=[shape] * 4,
        compiler_params=_params(("parallel",), 8 * _nbytes((tr, C), F32)),
    )(w, g, m, v)


BIG = ("w_up", "w_down", "w_in", "w_o")
IN_PLACE = ("w_up", "w_down")
BIG_AXIS = dict(w_in=2, w_o=1, w_up=2, w_down=1)
SMALL = ("pool_w", "pool_scale", "mix_norm_g", "ln1_g", "ln1_b", "ln2_g", "ln2_b")
CONV_ROWS = 64
SMALL_ROWS = 48


def _big_rows(shards):
    sizes = [shards[n].size // D_MODEL for n in BIG]
    starts = [sum(sizes[:i]) for i in range(len(sizes))]
    return sizes, starts


def _to_chips(a, axis):
    shape = list(a.shape)
    shape[axis:axis + 1] = [N_CHIPS, shape[axis] // N_CHIPS]
    return jnp.moveaxis(a.reshape(shape), axis, 0)


def _from_chips(a, axis):
    a = jnp.moveaxis(a, 0, axis)
    shape = list(a.shape)
    shape[axis:axis + 2] = [shape[axis] * shape[axis + 1]]
    return a.reshape(shape)


def _pad_rows(flat, rows):
    return jnp.pad(flat, (0, rows * D_MODEL - flat.shape[0])).reshape(rows, D_MODEL)


def kernel(x, w_in, conv_w, pool_w, pool_scale, mix_norm_g, w_o, ln1_g, ln1_b, w_up, w_down, ln2_g, ln2_b, loss_target, m_w_in, m_conv_w, m_pool_w, m_pool_scale, m_mix_norm_g, m_w_o, m_ln1_g, m_ln1_b, m_w_up, m_w_down, m_ln2_g, m_ln2_b, v_w_in, v_conv_w, v_pool_w, v_pool_scale, v_mix_norm_g, v_w_o, v_ln1_g, v_ln1_b, v_w_up, v_w_down, v_ln2_g, v_ln2_b):
    wts = dict(w_in=w_in, conv_w=conv_w, pool_w=pool_w, pool_scale=pool_scale, mix_norm_g=mix_norm_g, w_o=w_o,
               ln1_g=ln1_g, ln1_b=ln1_b, w_up=w_up, w_down=w_down, ln2_g=ln2_g, ln2_b=ln2_b)
    mom = dict(w_in=m_w_in, conv_w=m_conv_w, pool_w=m_pool_w, pool_scale=m_pool_scale, mix_norm_g=m_mix_norm_g,
               w_o=m_w_o, ln1_g=m_ln1_g, ln1_b=m_ln1_b, w_up=m_w_up, w_down=m_w_down, ln2_g=m_ln2_g, ln2_b=m_ln2_b)
    var = dict(w_in=v_w_in, conv_w=v_conv_w, pool_w=v_pool_w, pool_scale=v_pool_scale, mix_norm_g=v_mix_norm_g,
               w_o=v_w_o, ln1_g=v_ln1_g, ln1_b=v_ln1_b, w_up=v_w_up, w_down=v_w_down, ln2_g=v_ln2_g, ln2_b=v_ln2_b)
    chip = 2 * lax.axis_index("x") + lax.axis_index("y")
    sizes, starts = _big_rows(wts)
    big_rows = sum(sizes)

    conv_bits = lax.bitcast_convert_type(conv_w.reshape(-1), BF16).reshape(-1)
    pack = jnp.concatenate([wts[n].reshape(-1, D_MODEL).astype(BF16) for n in BIG]
                           + [_pad_rows(conv_bits, CONV_ROWS)], axis=0)
    gathered = _allgather_chips(pack, name="gather_weights")
    gathered = lax.dynamic_update_index_in_dim(gathered, pack, chip, 0)
    full = {}
    first_block = {}
    for n, size, start in zip(BIG, sizes, starts):
        if n in IN_PLACE:
            assert start % CHIP_BLOCK == 0 and size == DEPTH * CHIP_BLOCK
            first_block[n] = start // CHIP_BLOCK
            along = "cols" if BIG_AXIS[n] == 2 else "rows"
            full[n] = [_ChipWeight(gathered, first_block[n] + l, along) for l in range(DEPTH)]
        else:
            full[n] = _from_chips(gathered[:, start:start + size].reshape((N_CHIPS,) + wts[n].shape), BIG_AXIS[n])
    conv_parts = [lax.bitcast_convert_type(gathered[k, big_rows:].reshape(-1)[:2 * conv_w.size].reshape(-1, 2), F32)
                  .reshape(conv_w.shape) for k in range(N_CHIPS)]
    full["conv_w"] = jnp.concatenate(conv_parts, axis=2)
    for n in SMALL:
        full[n] = wts[n]

    gpack = jnp.zeros((N_CHIPS, big_rows, D_MODEL), BF16)
    lsum, grad_x, grads = _local_step(x[0], loss_target[0], full,
                                      grad_pack=(gpack, first_block["w_up"], first_block["w_down"]))

    others = [n for n in BIG if n not in IN_PLACE]
    rest = jnp.concatenate([_to_chips(grads[n], BIG_AXIS[n]).reshape(N_CHIPS, -1, D_MODEL) for n in others], axis=1)
    gpack = lax.dynamic_update_slice_in_dim(grads["pack"], rest.astype(BF16), starts[len(IN_PLACE)], axis=1)
    place = jnp.stack([chip, lax.axis_index("c")]).astype(jnp.int32)
    theirs = _swap_halves(gpack, name="grad_swap_cores")
    add_rows = big_rows // 8
    chip_sum = _add_pairs(gpack, theirs, place, name="grad_add_cores", tr=add_rows)
    from_chips = _scatter_chips(chip_sum, name="grad_scatter_chips")
    half_sum = _add_final(gpack, theirs, from_chips, place, name="grad_add_chips", tr=add_rows)
    gsum = _join_halves(half_sum, name="grad_join_cores")

    small_flat = jnp.concatenate([grads[n].reshape(-1) for n in SMALL] + [grads["conv_w"].reshape(-1),
                                                                          lsum.sum().reshape(1)])
    small_sum = _allreduce_small(_pad_rows(small_flat, SMALL_ROWS), name="allreduce_small").reshape(-1)
    gsmall = {}
    pos = 0
    for n in SMALL:
        gsmall[n] = small_sum[pos:pos + wts[n].size].reshape(wts[n].shape)
        pos += wts[n].size
    conv_full = small_sum[pos:pos + 4 * conv_w.size].reshape(DEPTH, 3, D_CONV)
    pos += 4 * conv_w.size
    loss = small_sum[pos]
    gsmall["conv_w"] = lax.dynamic_slice_in_dim(conv_full, chip * conv_w.shape[2], conv_w.shape[2], axis=2)

    out_g, out_d, out_m, out_v = {}, {}, {}, {}
    for n, size, start in zip(BIG, sizes, starts):
        shp = wts[n].shape
        g, row0 = gsum, start
        if shp[-1] != D_MODEL:
            g, row0 = gsum[start:start + size].reshape(-1, shp[-1]), 0
        res = _adamw(wts[n].reshape(-1, shp[-1]), g, mom[n].reshape(-1, shp[-1]), var[n].reshape(-1, shp[-1]),
                     name="adamw_" + n, tr=256, row0=row0)
        out_g[n], out_d[n], out_m[n], out_v[n] = [r.reshape(shp) for r in res]
    small_names = SMALL + ("conv_w",)
    packs = [_pad_rows(jnp.concatenate([d[n].reshape(-1) for n in small_names]), SMALL_ROWS)
             for d in (wts, gsmall, mom, var)]
    res = _adamw(*packs, name="adamw_small", tr=SMALL_ROWS)
    pos = 0
    for n in small_names:
        shp = wts[n].shape
        out_g[n], out_d[n], out_m[n], out_v[n] = [r.reshape(-1)[pos:pos + wts[n].size].reshape(shp) for r in res]
        pos += wts[n].size

    order = ("w_in", "conv_w", "pool_w", "pool_scale", "mix_norm_g", "w_o", "ln1_g", "ln1_b", "w_up", "w_down",
             "ln2_g", "ln2_b")
    return (loss, grad_x[None], *[out_g[n] for n in order], *[out_d[n] for n in order],
            *[out_m[n] for n in order], *[out_v[n] for n in order])
```

```python
import math
from typing import NamedTuple

import jax
import jax.numpy as jnp
from jax import lax
from jax.experimental import pallas as pl
from jax.experimental.pallas import tpu as pltpu

F32 = jnp.float32
BF16 = jnp.bfloat16
MESH = pl.DeviceIdType.MESH

D_MODEL = 1024
DEPTH = 2
HEAD_DIM = 64
D_SB = 512
D_CONV = 256
D_POOL = 256
D_QKV = 3 * D_SB
D_REST = 3 * D_CONV + D_POOL
D_FF = 4 * D_MODEL
ALPHA = (2 * DEPTH) ** 0.25
LN_EPS = 1e-5
RMS_EPS = 1e-6
SCALE = HEAD_DIM ** -0.5
N_CHIPS = 4
HALO = 16

ADAM_LR = 0.001
ADAM_B1 = 0.9
ADAM_B2 = 0.999
ADAM_EPS = 1e-08
ADAM_WD = 0.01
ADAM_STEP = 10

VMEM_V7X_BYTES = 64 * 1024 * 1024
VMEM_CAP_BYTES = VMEM_V7X_BYTES - 8 * 1024 * 1024


def _params(sem, block_bytes):
    limit = min(VMEM_CAP_BYTES, max(32 * 1024 * 1024, 3 * block_bytes))
    return pltpu.CompilerParams(dimension_semantics=sem, vmem_limit_bytes=limit)


def _nbytes(shape, dtype):
    return math.prod(shape) * jnp.dtype(dtype).itemsize


def _dot(a, b, dims=(((1,), (0,)), ((), ()))):
    return lax.dot_general(a, b, dims, preferred_element_type=F32)


NT = (((1,), (1,)), ((), ()))
TN = (((0,), (0,)), ((), ()))


def _split(x):
    hi = x.astype(BF16)
    lo = (x - hi.astype(F32)).astype(BF16)
    return hi, lo


def _sum8(x):
    r, c = x.shape
    return x.reshape(r // 8, 8, c).sum(axis=0)


def _ln_bwd_rows(dy, xh_ref, rs_ref, g_ref, dr_ref, dr16_ref, dg_ref, db_ref, first):
    xh = xh_ref[...]
    dxh = dy * g_ref[...]
    m1 = jnp.mean(dxh, axis=-1, keepdims=True)
    m2 = jnp.mean(dxh * xh, axis=-1, keepdims=True)
    dr = rs_ref[...] * (dxh - m1 - xh * m2)
    dr_ref[...] = dr
    dr16_ref[...] = dr.astype(BF16)
    pg = _sum8(dy * xh)
    pb = _sum8(dy)

    @pl.when(first)
    def _():
        dg_ref[...] = pg
        db_ref[...] = pb

    @pl.when(jnp.logical_not(first))
    def _():
        dg_ref[...] += pg
        db_ref[...] += pb


class _ChipWeight(NamedTuple):
    arr: jax.Array
    rb: int
    along: str


CHIP_BLOCK = 1024


def _matmul(a, b, *, name, tm, tn, tk, ta=False, tb=False, out_dtype=F32,
            epi=None, e=None, e_scale=1.0, relu2_out=False, out_chips=None, ln_bwd=None, also=None):
    M, K = (a.shape[1], a.shape[0]) if ta else a.shape
    chips = isinstance(b, _ChipWeight)
    split_k = chips and ((b.along == "cols") == tb)
    if chips:
        N = CHIP_BLOCK if split_k else N_CHIPS * CHIP_BLOCK
        assert K == (N_CHIPS * CHIP_BLOCK if split_k else CHIP_BLOCK) and not ta, (name, K)
        tn, tk = CHIP_BLOCK, K
    else:
        N = b.shape[0] if tb else b.shape[1]
    tm, tn, tk = min(tm, M), min(tn, N), min(tk, K)
    assert M % tm == 0 and N % tn == 0 and K % tk == 0, (name, M, N, K)
    nk = K // tk
    dims = (((0 if ta else 1,), (1 if tb else 0,)), ((), ()))
    i_also = 2 + (epi is not None) + (out_chips is not None)
    n_in = i_also + (2 if also is not None else 0) + (3 if ln_bwd is not None else 0)
    assert also is None or (nk == 1 and not ta)
    assert ln_bwd is None or (tn == N and out_chips is None and not relu2_out)

    def body(*refs):
        a_ref, b_ref = refs[0], refs[1]
        e_ref = refs[2] if epi is not None else None
        o_ref = refs[n_in]
        scr = refs[-1:]
        if not chips:
            p = _dot(a_ref[...].astype(BF16), b_ref[...].astype(BF16), dims)
        elif split_k:
            p = _dot(a_ref[:, 0:CHIP_BLOCK].astype(BF16), b_ref[0], dims)
            for c in range(1, N_CHIPS):
                p = p + _dot(a_ref[:, c * CHIP_BLOCK:(c + 1) * CHIP_BLOCK].astype(BF16), b_ref[c], dims)
        else:
            p = _dot(a_ref[...].astype(BF16), b_ref[0], dims)
        if also is not None:
            p = p + _dot(refs[i_also][...].astype(BF16), refs[i_also + 1][...].astype(BF16), dims)

        def finish(acc):
            if epi == "drelu2":
                acc = acc * (2.0 * jnp.maximum(e_ref[...].astype(F32), 0.0))
            elif epi == "add":
                acc = acc + e_scale * e_ref[...]
            if ln_bwd is not None:
                _ln_bwd_rows(acc, *refs[n_in - 3:n_in + 4], pl.program_id(0) == 0)
            elif out_chips is None:
                o_ref[...] = acc.astype(out_dtype)
            else:
                o_ref[0] = acc.astype(out_dtype)
            if relu2_out:
                refs[n_in + 1][...] = jnp.square(jnp.maximum(acc, 0.0)).astype(BF16)

        if nk == 1:
            finish(p)
        else:
            acc_ref = scr[0]
            k = pl.program_id(2)

            @pl.when(k == 0)
            def _():
                acc_ref[...] = p

            @pl.when(k > 0)
            def _():
                acc_ref[...] += p

            @pl.when(k == nk - 1)
            def _():
                finish(acc_ref[...])

    a_spec = pl.BlockSpec((tk, tm), lambda i, j, k: (k, i)) if ta else pl.BlockSpec((tm, tk), lambda i, j, k: (i, k))
    if chips:
        b_arr, rb = b.arr, b.rb
        nblk = N_CHIPS if split_k else 1
        b_spec = pl.BlockSpec((nblk, CHIP_BLOCK, CHIP_BLOCK),
                              (lambda i, j, k: (0, rb, 0)) if split_k else (lambda i, j, k: (j, rb, 0)))
    else:
        b_arr = b
        b_spec = pl.BlockSpec((tn, tk), lambda i, j, k: (j, k)) if tb else pl.BlockSpec((tk, tn), lambda i, j, k: (k, j))
    o_spec = pl.BlockSpec((tm, tn), lambda i, j, k: (i, j))
    in_specs = [a_spec, b_spec]
    args = [a, b_arr]
    nbytes = _nbytes((tm, tk), a.dtype) + _nbytes((tk, tn), b_arr.dtype) + 2 * _nbytes((tm, tn), F32)
    if epi is not None:
        in_specs.append(o_spec)
        args.append(e)
        nbytes += _nbytes((tm, tn), e.dtype)
    scratch = [pltpu.VMEM((tm, tn), F32)] if nk > 1 else []
    out_shape = [jax.ShapeDtypeStruct((M, N), out_dtype)]
    out_specs = [o_spec]
    aliases = {}
    if out_chips is not None:
        assert tm == tn == CHIP_BLOCK and not relu2_out and out_chips.arr.dtype == out_dtype
        orb = out_chips.rb
        out_specs = [pl.BlockSpec((1, CHIP_BLOCK, CHIP_BLOCK),
                                  (lambda i, j, k: (j, orb, 0)) if out_chips.along == "cols" else
                                  (lambda i, j, k: (i, orb, 0)))]
        out_shape = [jax.ShapeDtypeStruct(out_chips.arr.shape, out_dtype)]
        in_specs.append(pl.BlockSpec(memory_space=pl.ANY))
        args.append(out_chips.arr)
        aliases = {len(args) - 1: 0}
    if relu2_out:
        out_shape.append(jax.ShapeDtypeStruct((M, N), BF16))
        out_specs.append(o_spec)
        nbytes += _nbytes((tm, tn), BF16)
    if also is not None:
        a2, b2 = also
        k2 = a2.shape[1]
        in_specs += [pl.BlockSpec((tm, k2), lambda i, j, k: (i, 0)),
                     pl.BlockSpec((tn, k2), lambda i, j, k: (j, 0)) if tb else pl.BlockSpec((k2, tn), lambda i, j, k: (0, j))]
        args += [a2, b2]
        nbytes += _nbytes((tm, k2), a2.dtype) + _nbytes((k2, tn), b2.dtype)
    sem = ("parallel", "parallel", "arbitrary")
    if ln_bwd is not None:
        xhat, rstd, gain = ln_bwd
        in_specs += [o_spec, pl.BlockSpec((tm, 1), lambda i, j, k: (i, 0)), pl.BlockSpec((1, N), lambda i, j, k: (0, 0))]
        args += [xhat, rstd, gain.reshape(1, N)]
        acc_spec = pl.BlockSpec((8, N), lambda i, j, k: (0, 0))
        out_specs = [o_spec, o_spec, acc_spec, acc_spec]
        out_shape = [jax.ShapeDtypeStruct((M, N), F32), jax.ShapeDtypeStruct((M, N), BF16),
                     jax.ShapeDtypeStruct((8, N), F32), jax.ShapeDtypeStruct((8, N), F32)]
        nbytes += 3 * _nbytes((tm, tn), F32)
        sem = ("arbitrary", "arbitrary", "arbitrary")
    res = pl.pallas_call(
        body, name=name,
        grid=(M // tm, N // tn, nk),
        in_specs=in_specs, out_specs=out_specs,
        out_shape=out_shape,
        scratch_shapes=scratch,
        input_output_aliases=aliases,
        compiler_params=_params(sem, nbytes),
    )(*args)
    return res if (relu2_out or ln_bwd is not None) else res[0]


def _matmul_ln(a, b, xres, g, bias, *, name, tm, tk, res_affine=None):
    M, K = a.shape
    chips = isinstance(b, _ChipWeight)
    if chips:
        assert b.along == "rows" and K == N_CHIPS * CHIP_BLOCK
        N, tk = CHIP_BLOCK, K
    else:
        N = b.shape[1]
    tm, tk = min(tm, M), min(tk, K)
    assert M % tm == 0 and K % tk == 0 and N == D_MODEL
    nk = K // tk

    n_vec = 2 if res_affine is None else 4

    def body(*refs):
        a_ref, b_ref, x_ref, g_ref, bias_ref = refs[:5]
        y16_ref, xh_ref, rs_ref = refs[3 + n_vec:6 + n_vec]
        scr = refs[6 + n_vec:]
        if chips:
            p = _dot(a_ref[:, 0:CHIP_BLOCK].astype(BF16), b_ref[0])
            for c in range(1, N_CHIPS):
                p = p + _dot(a_ref[:, c * CHIP_BLOCK:(c + 1) * CHIP_BLOCK].astype(BF16), b_ref[c])
        else:
            p = _dot(a_ref[...].astype(BF16), b_ref[...].astype(BF16))

        def finish(acc):
            xv = x_ref[...]
            if res_affine is not None:
                xv = xv * refs[5][...] + refs[6][...]
            r = ALPHA * xv + acc
            mu = jnp.mean(r, axis=-1, keepdims=True)
            xc = r - mu
            var = jnp.mean(xc * xc, axis=-1, keepdims=True)
            rstd = lax.rsqrt(var + LN_EPS)
            xh = xc * rstd
            y16_ref[...] = (xh * g_ref[...] + bias_ref[...]).astype(BF16)
            xh_ref[...] = xh
            rs_ref[...] = rstd

        if nk == 1:
            finish(p)
        else:
            acc_ref = scr[0]
            k = pl.program_id(1)

            @pl.when(k == 0)
            def _():
                acc_ref[...] = p

            @pl.when(k > 0)
            def _():
                acc_ref[...] += p

            @pl.when(k == nk - 1)
            def _():
                finish(acc_ref[...])

    row = pl.BlockSpec((tm, N), lambda i, k: (i, 0))
    vec = pl.BlockSpec((1, N), lambda i, k: (0, 0))
    if chips:
        b_arr, rb = b.arr, b.rb
        b_spec = pl.BlockSpec((N_CHIPS, CHIP_BLOCK, CHIP_BLOCK), lambda i, k: (0, rb, 0))
    else:
        b_arr = b
        b_spec = pl.BlockSpec((tk, N), lambda i, k: (k, 0))
    nbytes = _nbytes((tm, tk), a.dtype) + _nbytes((tk, N), b_arr.dtype) + 6 * _nbytes((tm, N), F32)
    scratch = [pltpu.VMEM((tm, N), F32)] if nk > 1 else []
    return pl.pallas_call(
        body, name=name,
        grid=(M // tm, nk),
        in_specs=[pl.BlockSpec((tm, tk), lambda i, k: (i, k)), b_spec, row] + [vec] * n_vec,
        out_specs=[row, row, pl.BlockSpec((tm, 1), lambda i, k: (i, 0))],
        out_shape=[jax.ShapeDtypeStruct((M, N), BF16), jax.ShapeDtypeStruct((M, N), F32),
                   jax.ShapeDtypeStruct((M, 1), F32)],
        scratch_shapes=scratch,
        compiler_params=_params(("parallel", "arbitrary"), nbytes),
    )(a, b_arr, xres, g.reshape(1, N), bias.reshape(1, N), *[v.reshape(1, N) for v in (res_affine or ())])


def _loss_ln_bwd(tgt, xhat, rstd, g, bias, *, name, tm):
    M, N = tgt.shape
    tm = min(tm, M)

    def body(t_ref, xh_ref, rs_ref, g_ref, bias_ref, dr_ref, dr16_ref, dg_ref, db_ref, l_ref):
        first = pl.program_id(0) == 0
        d = (xh_ref[...] * g_ref[...] + bias_ref[...]) - t_ref[...]
        part = _sum8(d * d) * (0.5 / N)

        @pl.when(first)
        def _():
            l_ref[...] = part

        @pl.when(jnp.logical_not(first))
        def _():
            l_ref[...] += part

        _ln_bwd_rows(d * (1.0 / N), xh_ref, rs_ref, g_ref, dr_ref, dr16_ref, dg_ref, db_ref, first)

    row = pl.BlockSpec((tm, N), lambda i: (i, 0))
    acc = pl.BlockSpec((8, N), lambda i: (0, 0))
    return pl.pallas_call(
        body, name=name, grid=(M // tm,),
        in_specs=[row, row, pl.BlockSpec((tm, 1), lambda i: (i, 0))] + [pl.BlockSpec((1, N), lambda i: (0, 0))] * 2,
        out_specs=[row, row, acc, acc, acc],
        out_shape=[jax.ShapeDtypeStruct((M, N), F32), jax.ShapeDtypeStruct((M, N), BF16)]
        + [jax.ShapeDtypeStruct((8, N), F32)] * 3,
        compiler_params=_params(("arbitrary",), 6 * _nbytes((tm, N), F32)),
    )(tgt, xhat, rstd, g.reshape(1, N), bias.reshape(1, N))


def _tri(n, kind):
    j = lax.broadcasted_iota(jnp.int32, (2 * n, n), 0) % n
    s = lax.broadcasted_iota(jnp.int32, (2 * n, n), 1)
    return ((j > s) if kind == "after" else (j < s)).astype(BF16)


LOG2E = 1.4426950408889634
DEAD = -104.0
NOT_VISITED = -1e30


def _log_terms(z):
    lse = jnp.log(1.0 + jnp.exp2(jnp.abs(z) * (-LOG2E)))
    logsig = jnp.minimum(z, 0.0) - lse
    return logsig, logsig - z


def _cumsum_mm(x, u2_ref):
    hi, lo = _split(x)
    return _dot(jnp.concatenate([hi, lo], axis=1), u2_ref[...])


def _head_rows(x2, scale):
    lane = lax.broadcasted_iota(jnp.int32, (1, 128), 1)
    zero = jnp.zeros_like(x2)
    both = jnp.concatenate([jnp.where(lane < HEAD_DIM, x2, zero), jnp.where(lane >= HEAD_DIM, x2, zero)], axis=0)
    return both * scale


def _causal_mask(i, ks, tq, tk):
    row = lax.broadcasted_iota(jnp.int32, (2 * tq, tk), 0)
    row = i * tq + jnp.where(row >= tq, row - tq, row)
    col = lax.broadcasted_iota(jnp.int32, (2 * tq, tk), 1)
    return (ks + col) < row


def _attn_fwd(qkv, *, name, tq, tk):
    S = qkv.shape[0]
    tq = tk = min(tq, tk, S)
    assert S % tq == 0 and S // tk <= 128
    tri = _tri(tk, "after")

    def body(q_ref, k_ref, v_ref, u_ref, o_ref, c_ref, qcat, oacc, cacc, call, ls_buf, tl_buf, l0_buf):
        i = pl.program_id(1)
        lane = lax.broadcasted_iota(jnp.int32, (1, 128), 1)
        qcat[...] = _head_rows(q_ref[...], SCALE)
        oacc[...] = jnp.zeros_like(oacc)
        cacc[...] = jnp.zeros_like(cacc)
        call[...] = jnp.full_like(call, NOT_VISITED)

        def scores(kb, masked, slot):
            ks = pl.multiple_of(jnp.maximum(kb, 0) * tk, tk)
            z = _dot(qcat[...], k_ref[pl.ds(ks, tk), :], NT)
            logsig, lom = _log_terms(z)
            if masked:
                msk = jnp.logical_and(_causal_mask(i, ks, tq, tk), kb >= 0)
                lom = jnp.where(msk, lom, 0.0)
                logsig = jnp.where(msk, logsig, -1e30)
            ls_buf[slot] = logsig
            tl_buf[slot] = _cumsum_mm(lom, u_ref)
            l0_buf[slot] = lom[:, 0:1]

        def weights(kb, slot):
            ks = pl.multiple_of(jnp.maximum(kb, 0) * tk, tk)
            tl = tl_buf[slot]
            c = cacc[...]
            call[...] = jnp.where(lane == kb, c, call[...])
            a = jnp.exp(ls_buf[slot] + tl + c).astype(BF16)
            oacc[...] += _dot(a, v_ref[pl.ds(ks, tk), :])
            cacc[...] = c + tl[:, 0:1] + l0_buf[slot]

        def pair(kb, masked, masked_next):
            scores(kb, masked, 0)
            scores(kb - 1, masked_next, 1)
            weights(kb, 0)
            weights(kb - 1, 1)

        pair(i, True, True)

        def live(state):
            t, cmax = state
            return jnp.logical_and(t < (i - 1) // 2, cmax > DEAD)

        def trip(state):
            t, _ = state
            pair(i - 2 - 2 * t, False, False)
            return t + 1, jnp.max(cacc[...])

        t_end, cmax = lax.while_loop(live, trip, (0, jnp.max(cacc[...])))

        left_over = jnp.logical_and(i >= 2, i % 2 == 0)
        still_live = jnp.logical_and(t_end == (i - 1) // 2, cmax > DEAD)

        @pl.when(jnp.logical_and(left_over, still_live))
        def _():
            pair(0, False, True)

        o_ref[...] = jnp.where(lane < HEAD_DIM, oacc[0:tq], oacc[tq:2 * tq])
        c_ref[...] = jnp.concatenate([call[0:tq], call[tq:2 * tq]], axis=1)

    nbytes = (_nbytes((tq, 128), BF16) + 2 * _nbytes((S, 128), BF16) + _nbytes((2 * tk, tk), BF16)
              + 8 * _nbytes((tq, 128), F32) + 14 * _nbytes((2 * tq, tk), F32))
    return pl.pallas_call(
        body, name=name, grid=(4, S // tq),
        in_specs=[pl.BlockSpec((tq, 128), lambda j, i: (i, j)),
                  pl.BlockSpec((S, 128), lambda j, i: (0, 4 + j)),
                  pl.BlockSpec((S, 128), lambda j, i: (0, 8 + j)),
                  pl.BlockSpec((2 * tk, tk), lambda j, i: (0, 0))],
        out_specs=[pl.BlockSpec((tq, 128), lambda j, i: (i, j)),
                   pl.BlockSpec((tq, 256), lambda j, i: (i, j))],
        out_shape=[jax.ShapeDtypeStruct((S, D_SB), F32), jax.ShapeDtypeStruct((S, 1024), F32)],
        scratch_shapes=[pltpu.VMEM((2 * tq, 128), BF16), pltpu.VMEM((2 * tq, 128), F32),
                        pltpu.VMEM((2 * tq, 1), F32), pltpu.VMEM((2 * tq, 128), F32),
                        pltpu.VMEM((2, 2 * tq, tk), F32), pltpu.VMEM((2, 2 * tq, tk), F32),
                        pltpu.VMEM((2, 2 * tq, 1), F32)],
        compiler_params=_params(("parallel", "arbitrary"), nbytes),
    )(qkv, qkv, qkv, tri)


def _attn_bwd(qkv, carry, do, *, name, tq, tk):
    S = qkv.shape[0]
    tq = tk = min(tq, tk, S)
    assert S % tq == 0 and S // tk <= 128
    nkb = S // tk
    nq = S // tq
    tri_after = _tri(tk, "after")
    tri_before = _tri(tk, "before")

    def body(q_ref, k_ref, v_ref, c_ref, do_ref, ua_ref, ub_ref, dq_ref, dk_ref, dv_ref,
             qcat, docat, qcat_t, docat_t, ccat, dqacc, pacc, dkt, dvt, ls_buf, tl_buf, da_buf):
        i = pl.program_id(1)
        lane = lax.broadcasted_iota(jnp.int32, (1, 128), 1)
        sub = lax.broadcasted_iota(jnp.int32, (128, 1), 0)
        q2 = q_ref[...]
        do2 = do_ref[...]
        qcat[...] = _head_rows(q2, SCALE)
        docat[...] = _head_rows(do2, 1.0).astype(BF16)
        qt = q2.astype(F32).T * SCALE
        dot_ = do2.T
        qcat_t[...] = jnp.concatenate([jnp.where(sub < HEAD_DIM, qt, 0.0), jnp.where(sub >= HEAD_DIM, qt, 0.0)],
                                      axis=1).astype(BF16)
        docat_t[...] = jnp.concatenate([jnp.where(sub < HEAD_DIM, dot_, 0.0), jnp.where(sub >= HEAD_DIM, dot_, 0.0)],
                                       axis=1).astype(BF16)
        ccat[0:tq] = c_ref[:, 0:128]
        ccat[tq:2 * tq] = c_ref[:, 128:256]

        @pl.when(i == 0)
        def _():
            dkt[...] = jnp.zeros_like(dkt)
            dvt[...] = jnp.zeros_like(dvt)

        dqacc[...] = jnp.zeros_like(dqacc)
        pacc[...] = jnp.zeros_like(pacc)

        def scores(kb, masked, slot):
            ks = pl.multiple_of(jnp.maximum(kb, 0) * tk, tk)
            z = _dot(qcat[...], k_ref[pl.ds(ks, tk), :], NT)
            logsig, lom = _log_terms(z)
            if masked:
                msk = jnp.logical_and(_causal_mask(i, ks, tq, tk), kb >= 0)
                lom = jnp.where(msk, lom, 0.0)
                logsig = jnp.where(msk, logsig, -1e30)
            ls_buf[slot] = logsig
            tl_buf[slot] = _cumsum_mm(lom, ua_ref)
            da_buf[slot] = _dot(docat[...], v_ref[pl.ds(ks, tk), :], NT)

        def grads(kb, slot):
            kbc = jnp.maximum(kb, 0)
            ks = pl.multiple_of(kbc * tk, tk)
            logsig = ls_buf[slot]
            c = jnp.sum(jnp.where(lane == kb, ccat[...], 0.0), axis=1, keepdims=True)
            a = jnp.exp(logsig + tl_buf[slot] + c)
            g = a * da_buf[slot]
            before = _cumsum_mm(g, ub_ref)
            pc = pacc[...]
            dz = g - jnp.exp(logsig) * (g + before + pc)
            dzb = dz.astype(BF16)
            dqacc[...] += _dot(dzb, k_ref[pl.ds(ks, tk), :])
            dkt[kbc] += _dot(qcat_t[...], dzb)
            dvt[kbc] += _dot(docat_t[...], a.astype(BF16))
            pacc[...] = pc + before[:, tk - 1:tk] + g[:, tk - 1:tk]

        def pair(kb, masked, masked_next):
            scores(kb, masked, 0)
            scores(kb + 1, masked_next, 1)
            grads(kb, 0)
            grads(kb + 1, 1)

        reach = jnp.max(ccat[...], axis=0, keepdims=True)
        first = jnp.min(jnp.where(reach > DEAD, lane, 128).astype(F32)).astype(jnp.int32)
        first = jnp.minimum(first, i)
        start = first - (i - first + 1) % 2

        @pl.when(jnp.logical_and(start < 0, i >= 2))
        def _():
            pair(-1, True, False)

        k0 = jnp.where(start < 0, 1, start)

        def loop(t, carry_):
            pair(k0 + 2 * t, False, False)
            return carry_

        lax.fori_loop(0, jnp.maximum((i - 1 - k0) // 2, 0), loop, 0)

        @pl.when(i == 0)
        def _():
            pair(-1, True, True)

        @pl.when(i > 0)
        def _():
            pair(i - 1, False, True)
        dq_ref[...] = (jnp.where(lane < HEAD_DIM, dqacc[0:tq], dqacc[tq:2 * tq]) * SCALE).astype(BF16)

        @pl.when(i == nq - 1)
        def _():
            for kb in range(nkb):
                dk_ref[kb * tk:(kb + 1) * tk, :] = dkt[kb].T.astype(BF16)
                dv_ref[kb * tk:(kb + 1) * tk, :] = dvt[kb].T.astype(BF16)

    nbytes = (_nbytes((tq, 128), BF16) + 2 * _nbytes((S, 128), BF16) + 2 * _nbytes((2 * tk, tk), BF16)
              + 12 * _nbytes((tq, 128), F32) + 4 * _nbytes((S, 128), F32) + 14 * _nbytes((2 * tq, tk), F32))
    blk = pl.BlockSpec((tq, 128), lambda j, i: (i, j))
    full = pl.BlockSpec((S, 128), lambda j, i: (0, j))
    tri_spec = pl.BlockSpec((2 * tk, tk), lambda j, i: (0, 0))
    dq, dk, dv = pl.pallas_call(
        body, name=name, grid=(4, nq),
        in_specs=[blk,
                  pl.BlockSpec((S, 128), lambda j, i: (0, 4 + j)),
                  pl.BlockSpec((S, 128), lambda j, i: (0, 8 + j)),
                  pl.BlockSpec((tq, 256), lambda j, i: (i, j)),
                  blk, tri_spec, tri_spec],
        out_specs=[blk, full, full],
        out_shape=[jax.ShapeDtypeStruct((S, D_SB), BF16)] * 3,
        scratch_shapes=[pltpu.VMEM((2 * tq, 128), BF16), pltpu.VMEM((2 * tq, 128), BF16),
                        pltpu.VMEM((128, 2 * tq), BF16), pltpu.VMEM((128, 2 * tq), BF16),
                        pltpu.VMEM((2 * tq, 128), F32), pltpu.VMEM((2 * tq, 128), F32), pltpu.VMEM((2 * tq, 1), F32),
                        pltpu.VMEM((nkb, 128, tk), F32), pltpu.VMEM((nkb, 128, tk), F32),
                        pltpu.VMEM((2, 2 * tq, tk), F32), pltpu.VMEM((2, 2 * tq, tk), F32),
                        pltpu.VMEM((2, 2 * tq, tk), F32)],
        compiler_params=_params(("parallel", "arbitrary"), nbytes),
    )(qkv, qkv, qkv, carry, do, tri_after, tri_before)
    return dq, dk, dv


def _group_mats():
    lanes = jnp.arange(D_MODEL) // HEAD_DIM
    gs = (lanes[:, None] == jnp.arange(128)[None, :]).astype(BF16)
    return gs, gs.T


def _group_sum_bcast(x, gs, gb):
    hi, lo = _split(x)
    s = _dot(hi, gs) + _dot(lo, gs)
    return _bcast(s, gb)


def _bcast(s, gb):
    hi, lo = _split(s)
    return _dot(hi, gb) + _dot(lo, gb)


def _pool_lane_consts():
    lane = lax.broadcasted_iota(jnp.int32, (1, D_POOL), 1)
    grp = lane // (D_POOL // 4)
    win = jnp.where(grp == 0, 2, jnp.where(grp == 1, 4, jnp.where(grp == 2, 8, 16)))
    return grp, win


def _by_group(grp, s2, s4, s8, s16):
    return jnp.where(grp == 0, s2, jnp.where(grp == 1, s4, jnp.where(grp == 2, s8, s16)))


def _mixers(i, ts, prev_ref, cur_ref, cw_ref, pw_ref, ps_ref):
    cur = cur_ref[...]
    prev = jnp.where(i == 0, 0.0, prev_ref[...])
    ext = jnp.concatenate([prev, cur], axis=0)

    def back(a, k):
        return pltpu.roll(a, k, 0)

    u = ext[:, D_CONV:2 * D_CONV] * ext[:, 2 * D_CONV:3 * D_CONV]
    p = ext[:, 3 * D_CONV:]
    cv = (cw_ref[0:1, :] * back(u, 2) + cw_ref[1:2, :] * back(u, 1) + cw_ref[2:3, :] * u)[HALO:]
    s2 = p + back(p, 1)
    s4 = s2 + back(s2, 2)
    s8 = s4 + back(s4, 4)
    s16 = s8 + back(s8, 8)
    grp, win = _pool_lane_consts()
    t1 = i * ts + 1 + lax.broadcasted_iota(jnp.int32, (ts, 1), 0)
    cnt = jnp.minimum(t1, win).astype(F32)
    pooled = _by_group(grp, s2, s4, s8, s16)[HALO:] / cnt - p[HALO:]
    yp = _dot(pooled.astype(BF16), pw_ref[...])
    return dict(b=cur[:, 0:D_CONV], u=u, cv=cv, pooled=pooled, yp=yp, cnt=cnt,
                conv_out=cur[:, 0:D_CONV] * cv, pool_out=yp * ps_ref[...])


def _halo_specs(ts, S, width):
    nb = ts // HALO
    last = S // HALO - 1
    prev = pl.BlockSpec((HALO, width), lambda i: (jnp.maximum(i * nb - 1, 0), 0))
    nxt = pl.BlockSpec((HALO, width), lambda i: (jnp.minimum((i + 1) * nb, last), 0))
    return prev, nxt


def _mixer_fwd(rest, attn, cw8, pwbd, ps, gain, *, name, ts):
    S = rest.shape[0]
    ts = min(ts, S)
    gs, gb = _group_mats()

    def body(prev_ref, cur_ref, attn_ref, cw_ref, pw_ref, ps_ref, gain_ref, gs_ref, gb_ref, o_ref):
        i = pl.program_id(0)
        f = _mixers(i, ts, prev_ref, cur_ref, cw_ref, pw_ref, ps_ref)
        mix = jnp.concatenate([attn_ref[...], f["conv_out"], f["pool_out"]], axis=1)
        ss = _group_sum_bcast(mix * mix, gs_ref[...], gb_ref[...])
        rinv = lax.rsqrt(ss * (1.0 / HEAD_DIM) + RMS_EPS)
        o_ref[...] = (mix * rinv * gain_ref[...]).astype(BF16)

    prev, _ = _halo_specs(ts, S, D_REST)
    row = lambda w: pl.BlockSpec((ts, w), lambda i: (i, 0))
    const = lambda a: pl.BlockSpec(a.shape, lambda i: (0, 0))
    nbytes = 12 * _nbytes((ts + HALO, D_REST), F32)
    return pl.pallas_call(
        body, name=name, grid=(S // ts,),
        in_specs=[prev, row(D_REST), row(D_SB), const(cw8), const(pwbd), const(ps), const(gain), const(gs), const(gb)],
        out_specs=row(D_MODEL),
        out_shape=jax.ShapeDtypeStruct((S, D_MODEL), BF16),
        compiler_params=_params(("parallel",), nbytes),
    )(rest, rest, attn, cw8, pwbd, ps, gain, gs, gb)


def _mixer_bwd1(dmixn, rest, attn, cw8, pwbd, ps, gain, *, name, ts):
    S = rest.shape[0]
    ts = min(ts, S)
    gs, gb = _group_mats()

    def body(dm_ref, prev_ref, cur_ref, attn_ref, cw_ref, pw_ref, ps_ref, gain_ref, gs_ref, gb_ref,
             da_ref, aux_ref, dg_ref, dsc_ref, dcw_ref, dpw_ref):
        i = pl.program_id(0)
        f = _mixers(i, ts, prev_ref, cur_ref, cw_ref, pw_ref, ps_ref)
        mix = jnp.concatenate([attn_ref[...], f["conv_out"], f["pool_out"]], axis=1)
        gsm, gbm = gs_ref[...], gb_ref[...]
        ss = _group_sum_bcast(mix * mix, gsm, gbm)
        rinv = lax.rsqrt(ss * (1.0 / HEAD_DIM) + RMS_EPS)
        dm = dm_ref[...]
        xn = mix * rinv
        dyg = dm * gain_ref[...]
        gm = _group_sum_bcast(dyg * xn, gsm, gbm) * (1.0 / HEAD_DIM)
        dmix = rinv * (dyg - xn * gm)
        da_ref[...] = dmix[:, 0:D_SB]
        dco = dmix[:, D_SB:D_SB + D_CONV]
        dpo = dmix[:, D_SB + D_CONV:]
        dcv = dco * f["b"]
        dyp = dpo * ps_ref[...]
        dpooled = _dot(dyp.astype(BF16), pw_ref[...], NT)
        aux_ref[...] = jnp.concatenate([dco * f["cv"], dcv, dpooled / f["cnt"], dpooled], axis=1)
        u = f["u"]
        parts = [
            _sum8(dm * xn),
            _sum8(dpo * f["yp"]),
            jnp.concatenate([_sum8(dcv * pltpu.roll(u, 2, 0)[HALO:]), _sum8(dcv * pltpu.roll(u, 1, 0)[HALO:]),
                             _sum8(dcv * u[HALO:])], axis=0),
            _dot(f["pooled"].astype(BF16), dyp.astype(BF16), TN),
        ]
        outs = [dg_ref, dsc_ref, dcw_ref, dpw_ref]

        @pl.when(i == 0)
        def _():
            for o, v in zip(outs, parts):
                o[...] = v

        @pl.when(i > 0)
        def _():
            for o, v in zip(outs, parts):
                o[...] += v

    prev, _ = _halo_specs(ts, S, D_REST)
    row = lambda w: pl.BlockSpec((ts, w), lambda i: (i, 0))
    const = lambda a: pl.BlockSpec(a.shape, lambda i: (0, 0))
    acc = lambda r_, w: pl.BlockSpec((r_, w), lambda i: (0, 0))
    nbytes = 16 * _nbytes((ts + HALO, D_REST), F32)
    return pl.pallas_call(
        body, name=name, grid=(S // ts,),
        in_specs=[row(D_MODEL), prev, row(D_REST), row(D_SB), const(cw8), const(pwbd), const(ps), const(gain),
                  const(gs), const(gb)],
        out_specs=[row(D_SB), row(D_REST), acc(8, D_MODEL), acc(8, D_POOL), acc(24, D_CONV), acc(D_POOL, D_POOL)],
        out_shape=[jax.ShapeDtypeStruct((S, D_SB), F32), jax.ShapeDtypeStruct((S, D_REST), F32),
                   jax.ShapeDtypeStruct((8, D_MODEL), F32), jax.ShapeDtypeStruct((8, D_POOL), F32),
                   jax.ShapeDtypeStruct((24, D_CONV), F32), jax.ShapeDtypeStruct((D_POOL, D_POOL), F32)],
        compiler_params=_params(("arbitrary",), nbytes),
    )(dmixn, rest, rest, attn, cw8, pwbd, ps, gain, gs, gb)


def _mixer_bwd2(aux, rest, cw8, *, name, ts):
    S = rest.shape[0]
    ts = min(ts, S)
    nblk = S // ts

    def body(cur_ref, nxt_ref, rest_ref, cw_ref, o_ref):
        i = pl.program_id(0)
        cur = cur_ref[...]
        nxt = jnp.where(i == nblk - 1, 0.0, nxt_ref[...])
        ext = jnp.concatenate([cur, nxt], axis=0)
        n = ts + HALO

        def fwd(a, k):
            return pltpu.roll(a, n - k, 0)

        dcv = ext[:, D_CONV:2 * D_CONV]
        dps = ext[:, 2 * D_CONV:3 * D_CONV]
        du = (cw_ref[2:3, :] * dcv + cw_ref[1:2, :] * fwd(dcv, 1) + cw_ref[0:1, :] * fwd(dcv, 2))[0:ts]
        f2 = dps + fwd(dps, 1)
        f4 = f2 + fwd(f2, 2)
        f8 = f4 + fwd(f4, 4)
        f16 = f8 + fwd(f8, 8)
        grp, _ = _pool_lane_consts()
        dp = _by_group(grp, f2, f4, f8, f16)[0:ts] - cur[:, 3 * D_CONV:]
        rest_v = rest_ref[...]
        c_gate = rest_v[:, D_CONV:2 * D_CONV]
        h = rest_v[:, 2 * D_CONV:3 * D_CONV]
        o_ref[...] = jnp.concatenate([cur[:, 0:D_CONV], du * h, du * c_gate, dp], axis=1).astype(BF16)

    _, nxt = _halo_specs(ts, S, D_REST)
    row = pl.BlockSpec((ts, D_REST), lambda i: (i, 0))
    return pl.pallas_call(
        body, name=name, grid=(nblk,),
        in_specs=[row, nxt, row, pl.BlockSpec(cw8.shape, lambda i: (0, 0))],
        out_specs=row,
        out_shape=jax.ShapeDtypeStruct((S, D_REST), BF16),
        compiler_params=_params(("parallel",), 10 * _nbytes((ts + HALO, D_REST), F32)),
    )(aux, aux, rest, cw8)


def _block_diag(pw):
    wide = jnp.tile(pw.reshape(256, 64), (1, 4))
    grp = jnp.arange(256) // 64
    return jnp.where(grp[:, None] == grp[None, :], wide, 0.0)


def _rows8(v, rows=8):
    return jnp.pad(v, ((0, rows - v.shape[0]), (0, 0)))


TILES = dict(tm=512, ts=512, tq=256, tk=256)


def _local_step(x, tgt, w, t=None, grad_pack=None):
    t = dict(TILES, **(t or {}))
    gp = None if grad_pack is None else grad_pack[0]
    tm, ts, tq, tk = t["tm"], t["ts"], t["tq"], t["tk"]
    big = dict(tm=1024, tn=1024)
    saved = []
    xl, xl_affine, xl16 = x, None, x.astype(BF16)
    for l in range(DEPTH):
        n = f"l{l}_"
        wq, wr = w["w_in"][l][:, :D_QKV], w["w_in"][l][:, D_QKV:]
        qkv = _matmul(xl16, wq, name=n + "proj_qkv", tm=1024, tn=D_QKV, tk=1024, out_dtype=BF16)
        rest = _matmul(xl16, wr, name=n + "proj_rest", tk=1024, **big)
        attn, carry = _attn_fwd(qkv, name=n + "attn_fwd", tq=tq, tk=tk)
        cw8 = _rows8(w["conv_w"][l])
        pwbd = _block_diag(w["pool_w"][l]).astype(BF16)
        ps = w["pool_scale"][l].reshape(1, D_POOL)
        gain = w["mix_norm_g"][l].reshape(1, D_MODEL)
        mixn = _mixer_fwd(rest, attn, cw8, pwbd, ps, gain, name=n + "mixer_fwd", ts=ts)
        x1_16, xh1, rs1 = _matmul_ln(mixn, w["w_o"][l], xl, w["ln1_g"][l], w["ln1_b"][l], name=n + "wo_ln",
                                     tm=tm, tk=1024, res_affine=xl_affine)
        hpre, hid = _matmul(x1_16, w["w_up"][l], name=n + "ffn_up", tk=1024, relu2_out=True, out_dtype=BF16, **big)
        x2_16, xh2, rs2 = _matmul_ln(hid, w["w_down"][l], xh1, w["ln2_g"][l], w["ln2_b"][l], name=n + "ffn_down_ln",
                                     tm=tm // 2, tk=D_FF, res_affine=(w["ln1_g"][l], w["ln1_b"][l]))
        saved.append(dict(xin16=xl16, wq=wq, wr=wr, qkv=qkv, rest=rest, attn=attn, carry=carry, cw8=cw8, pwbd=pwbd,
                          ps=ps, gain=gain, mixn=mixn, x1_16=x1_16, xh1=xh1, rs1=rs1, hpre=hpre, hid=hid, xh2=xh2,
                          rs2=rs2))
        xl, xl_affine, xl16 = xh2, (w["ln2_g"][l], w["ln2_b"][l]), x2_16

    top = saved[-1]
    ln2_back = _loss_ln_bwd(tgt, top["xh2"], top["rs2"], w["ln2_g"][DEPTH - 1], w["ln2_b"][DEPTH - 1],
                            name="loss_ln2_bwd", tm=tm)
    lsum = ln2_back[4]
    grads = {k: [None] * DEPTH for k in
             ("w_in", "conv_w", "pool_w", "pool_scale", "mix_norm_g", "w_o", "ln1_g", "ln1_b", "w_up", "w_down",
              "ln2_g", "ln2_b")}
    dw = dict(tk=2048, ta=True, out_dtype=BF16, **big)
    for l in reversed(range(DEPTH)):
        n = f"l{l}_"
        s = saved[l]
        dr2, dr2_16, dg2, db2 = ln2_back[:4]
        dhpre = _matmul(dr2_16, w["w_down"][l], name=n + "ffn_down_dx", tk=1024, tb=True, out_dtype=BF16,
                        epi="drelu2", e=s["hpre"], **big)
        if gp is None:
            grads["w_down"][l] = _matmul(s["hid"], dr2_16, name=n + "ffn_down_dw", **dw)
        else:
            gp = _matmul(s["hid"], dr2_16, name=n + "ffn_down_dw", out_chips=_ChipWeight(gp, grad_pack[2] + l, "rows"),
                         **dw)
        dr1, dr1_16, dg1, db1 = _matmul(dhpre, w["w_up"][l], name=n + "ffn_up_dx_ln1_bwd", tm=512, tn=1024, tk=D_FF,
                                        tb=True, epi="add", e=dr2, e_scale=ALPHA,
                                        ln_bwd=(s["xh1"], s["rs1"], w["ln1_g"][l]))
        if gp is None:
            grads["w_up"][l] = _matmul(s["x1_16"], dhpre, name=n + "ffn_up_dw", **dw)
        else:
            gp = _matmul(s["x1_16"], dhpre, name=n + "ffn_up_dw", out_chips=_ChipWeight(gp, grad_pack[1] + l, "cols"),
                         **dw)
        dmixn = _matmul(dr1_16, w["w_o"][l], name=n + "wo_dx", tk=1024, tb=True, **big)
        grads["w_o"][l] = _matmul(s["mixn"], dr1_16, name=n + "wo_dw", **dw)
        d_attn, aux, dgain, dsc, dcw, dpw = _mixer_bwd1(dmixn, s["rest"], s["attn"], s["cw8"], s["pwbd"], s["ps"],
                                                        s["gain"], name=n + "mixer_bwd1", ts=ts)
        drest = _mixer_bwd2(aux, s["rest"], s["cw8"], name=n + "mixer_bwd2", ts=ts)
        dqkv = jnp.concatenate(_attn_bwd(s["qkv"], s["carry"], d_attn, name=n + "attn_bwd", tq=tq, tk=tk), axis=1)
        below = None if l == 0 else (saved[l - 1]["xh2"], saved[l - 1]["rs2"], w["ln2_g"][l - 1])
        ln2_back = _matmul(dqkv, s["wq"], name=n + "proj_dx", tm=512, tn=1024, tk=D_QKV, tb=True, also=(drest, s["wr"]),
                           epi="add", e=dr1, e_scale=ALPHA, ln_bwd=below)
        dy = ln2_back
        dwq = _matmul(s["xin16"], dqkv, name=n + "proj_qkv_dw", tm=1024, tn=D_QKV, tk=1024, ta=True, out_dtype=BF16)
        dwr = _matmul(s["xin16"], drest, name=n + "proj_rest_dw", **dw)
        grads["w_in"][l] = jnp.concatenate([dwq, dwr], axis=1)
        grads["ln2_g"][l] = dg2.sum(0)
        grads["ln2_b"][l] = db2.sum(0)
        grads["ln1_g"][l] = dg1.sum(0)
        grads["ln1_b"][l] = db1.sum(0)
        grads["mix_norm_g"][l] = dgain.sum(0)
        grads["pool_scale"][l] = dsc.sum(0)
        grads["conv_w"][l] = dcw.reshape(3, 8, D_CONV).sum(1)
        grads["pool_w"][l] = jnp.stack([dpw[64 * g:64 * g + 64, 64 * g:64 * g + 64] for g in range(4)])
    grads = {k: jnp.stack(v) for k, v in grads.items() if v[0] is not None}
    if gp is not None:
        grads["pack"] = gp
    return lsum, dy, grads


ANY = pl.BlockSpec(memory_space=pl.ANY)


def _place():
    x, y, c = lax.axis_index("x"), lax.axis_index("y"), lax.axis_index("c")
    chips = [(1 - x, y), (x, 1 - y), (1 - x, 1 - y)]
    return x, y, c, chips


def _remote(src, dst, send_sems, recv_sems, k, to):
    return pltpu.make_async_remote_copy(src_ref=src, dst_ref=dst, send_sem=send_sems.at[k], recv_sem=recv_sems.at[k],
                                        device_id=to, device_id_type=MESH)


class _Copy:
    def __init__(self, src, dst, send_sems, recv_sems, k, to):
        self.args = (send_sems, recv_sems, k, to)
        self.copy = _remote(src, dst, *self.args)

    def like(self, src, dst):
        return _remote(src, dst, *self.args)

    def start(self):
        self.copy.start()

    def wait(self):
        self.copy.wait()

    def wait_send(self):
        self.copy.wait_send()

    def wait_recv(self):
        self.copy.wait_recv()


def _allgather_chips(pack, *, name):
    R, C = pack.shape
    H = R // 2
    Q = H // 2
    assert R % 64 == 0
    A, B = 0, 1

    def body(p_ref, o_ref, send_sems, recv_sems):
        x, y, c, _ = _place()
        my, kx, ky, kd = 2 * x + y, 2 * (1 - x) + y, 2 * x + (1 - y), 2 * (1 - x) + (1 - y)
        xn, yn, sib = (1 - x, y, c), (x, 1 - y, c), (x, y, 1 - c)

        def own(ab):
            return p_ref.at[pl.ds(c * H + ab * Q, Q), :]

        def quarter(k, hc, ab):
            return o_ref.at[k, pl.ds(hc * H + ab * Q, Q), :]

        def send(src, k, ab, sem, to):
            cp = _Copy(src, quarter(k, c, ab), send_sems, recv_sems, sem, to)
            cp.start()
            return cp

        def landed(sent, k, hc, ab):
            sent.like(quarter(k, hc, ab), quarter(k, hc, ab)).wait_recv()

        a_x = send(own(A), my, A, 0, xn)
        b_y = send(own(B), my, B, 3, yn)
        b_x = send(own(B), my, B, 1, xn)
        a_y = send(own(A), my, A, 4, yn)
        landed(b_y, ky, c, B)
        fb = send(quarter(ky, c, B), ky, B, 2, xn)
        landed(a_x, kx, c, A)
        fa = send(quarter(kx, c, A), kx, A, 5, yn)
        arrivals = [(kx, A, None), (ky, B, None), (kx, B, b_x), (ky, A, a_y), (kd, B, fb), (kd, A, fa)]
        passed = []
        for j, (k, ab, sent) in enumerate(arrivals):
            if sent is not None:
                landed(sent, k, c, ab)
            passed.append(_Copy(quarter(k, c, ab), quarter(k, c, ab), send_sems, recv_sems, 6 + j, sib))
            passed[-1].start()
        for j, (k, ab, _) in enumerate(arrivals):
            landed(passed[j], k, 1 - c, ab)
        for cp in [a_x, b_y, b_x, a_y, fb, fa] + passed:
            cp.wait_send()

    return pl.pallas_call(
        body, name=name, in_specs=[ANY], out_specs=ANY,
        out_shape=jax.ShapeDtypeStruct((N_CHIPS, R, C), pack.dtype),
        scratch_shapes=[pltpu.SemaphoreType.DMA((12,)), pltpu.SemaphoreType.DMA((12,))],
    )(pack)


def _swap_halves(gp, *, name):
    K, R, C = gp.shape
    H = R // 2

    def body(g_ref, theirs_ref, send_sems, recv_sems):
        x, y, c, _ = _place()
        cp = _Copy(g_ref.at[:, pl.ds((1 - c) * H, H), :], theirs_ref, send_sems, recv_sems, 0, (x, y, 1 - c))
        cp.start()
        cp.wait()

    return pl.pallas_call(
        body, name=name, in_specs=[ANY], out_specs=ANY, out_shape=jax.ShapeDtypeStruct((K, H, C), gp.dtype),
        scratch_shapes=[pltpu.SemaphoreType.DMA((1,)), pltpu.SemaphoreType.DMA((1,))],
    )(gp)


def _scatter_chips(part, *, name):
    K, H, C = part.shape

    def body(p_ref, o_ref, send_sems, recv_sems):
        x, y, c, chips = _place()
        copies = [_Copy(p_ref.at[2 * cx + cy], o_ref.at[j], send_sems, recv_sems, j, (cx, cy, c))
                  for j, (cx, cy) in enumerate(chips)]
        for cp in copies:
            cp.start()
        for cp in copies:
            cp.wait()

    return pl.pallas_call(
        body, name=name, in_specs=[ANY], out_specs=ANY,
        out_shape=jax.ShapeDtypeStruct((3, H, C), part.dtype),
        scratch_shapes=[pltpu.SemaphoreType.DMA((3,)), pltpu.SemaphoreType.DMA((3,))],
    )(part)


def _join_halves(both, *, name):
    H, C = both.shape[0] // 2, both.shape[1]

    def body(in_ref, o_ref, send_sems, recv_sems):
        x, y, c, _ = _place()
        mine = pl.ds(c * H, H)
        theirs = pl.ds((1 - c) * H, H)
        cp = _Copy(in_ref.at[mine, :], o_ref.at[mine, :], send_sems, recv_sems, 0, (x, y, 1 - c))
        cp.start()
        cp.wait_send()
        cp.like(in_ref.at[theirs, :], o_ref.at[theirs, :]).wait_recv()

    return pl.pallas_call(
        body, name=name, in_specs=[ANY], out_specs=ANY, input_output_aliases={0: 0},
        out_shape=jax.ShapeDtypeStruct(both.shape, both.dtype),
        scratch_shapes=[pltpu.SemaphoreType.DMA((1,)), pltpu.SemaphoreType.DMA((1,))],
    )(both)


def _allreduce_small(v, *, name):
    R, C = v.shape
    n_dev = 8

    def body(v_ref, o_ref, gat, send_sems, recv_sems):
        x, y, c, chips = _place()
        sib = (x, y, 1 - c)

        def rows(px, py, pc):
            return gat.at[4 * px + 2 * py + pc]

        gat[4 * x + 2 * y + c] = v_ref[...]
        first = [_remote(v_ref, rows(x, y, c), send_sems, recv_sems, 0, sib)]
        first += [_remote(v_ref, rows(x, y, c), send_sems, recv_sems, 1 + j, (cx, cy, c))
                  for j, (cx, cy) in enumerate(chips)]
        for cp in first:
            cp.start()
        passed = []
        for j, (cx, cy) in enumerate(chips):
            _remote(v_ref, rows(cx, cy, c), send_sems, recv_sems, 1 + j, sib).wait_recv()
            fwd = _remote(rows(cx, cy, c), rows(cx, cy, c), send_sems, recv_sems, 4 + j, sib)
            fwd.start()
            passed.append(fwd)
        _remote(v_ref, rows(x, y, 1 - c), send_sems, recv_sems, 0, sib).wait_recv()
        for j, (cx, cy) in enumerate(chips):
            _remote(v_ref, rows(cx, cy, 1 - c), send_sems, recv_sems, 4 + j, sib).wait_recv()
        for cp in first + passed:
            cp.wait_send()
        acc = gat[0]
        for d in range(1, n_dev):
            acc = acc + gat[d]
        o_ref[...] = acc

    vm = pl.BlockSpec(memory_space=pltpu.VMEM)
    return pl.pallas_call(
        body, name=name, in_specs=[vm], out_specs=vm,
        out_shape=jax.ShapeDtypeStruct((R, C), F32),
        scratch_shapes=[pltpu.VMEM((n_dev, R, C), F32), pltpu.SemaphoreType.DMA((7,)), pltpu.SemaphoreType.DMA((7,))],
    )(v)


def _add_pairs(gp, theirs, place, *, name, tr):
    K, H, C = theirs.shape
    tr = min(tr, H)
    assert H % tr == 0
    nb = H // tr

    def body(place_ref, a_ref, b_ref, o_ref):
        o_ref[...] = (a_ref[...].astype(F32) + b_ref[...].astype(F32)).astype(BF16)

    blk = pl.BlockSpec((1, tr, C), lambda k, i, p: (k, i, 0))
    mine = pl.BlockSpec((1, tr, C), lambda k, i, p: (k, i + p[1] * nb, 0))
    return pl.pallas_call(
        body, name=name,
        grid_spec=pltpu.PrefetchScalarGridSpec(num_scalar_prefetch=1, grid=(K, nb), in_specs=[mine, blk],
                                               out_specs=blk),
        out_shape=jax.ShapeDtypeStruct((K, H, C), BF16),
        compiler_params=_params(("parallel", "parallel"), 3 * _nbytes((tr, C), BF16)),
    )(place, gp, theirs)


def _add_final(gp, theirs, others, place, *, name, tr):
    K, H, C = theirs.shape
    tr = min(tr, H)
    assert H % tr == 0
    nb = H // tr

    def body(place_ref, a_ref, b_ref, o_ref_in, out_ref):
        acc = a_ref[0].astype(F32) + b_ref[0].astype(F32)
        for j in range(3):
            acc = acc + o_ref_in[j].astype(F32)
        out_ref[...] = acc

    return pl.pallas_call(
        body, name=name,
        grid_spec=pltpu.PrefetchScalarGridSpec(
            num_scalar_prefetch=1, grid=(nb,),
            in_specs=[pl.BlockSpec((1, tr, C), lambda i, p: (p[0], i + p[1] * nb, 0)),
                      pl.BlockSpec((1, tr, C), lambda i, p: (p[0], i, 0)),
                      pl.BlockSpec((3, tr, C), lambda i, p: (0, i, 0))],
            out_specs=pl.BlockSpec((tr, C), lambda i, p: (i + p[1] * nb, 0))),
        out_shape=jax.ShapeDtypeStruct((2 * H, C), F32),
        compiler_params=_params(("parallel",), 6 * _nbytes((tr, C), F32)),
    )(place, gp, theirs, others)


def _adamw(w, g, m, v, *, name, tr, row0=0):
    R, C = w.shape
    tr = min(tr, R)
    assert R % tr == 0 and row0 % tr == 0
    off = row0 // tr

    def body(w_ref, g_ref, m_ref, v_ref, go_ref, d_ref, mo_ref, vo_ref):
        gv = g_ref[...]
        m2 = ADAM_B1 * m_ref[...] + (1.0 - ADAM_B1) * gv
        v2 = ADAM_B2 * v_ref[...] + (1.0 - ADAM_B2) * jnp.square(gv)
        m_hat = m2 / (1.0 - ADAM_B1 ** ADAM_STEP)
        v_hat = v2 / (1.0 - ADAM_B2 ** ADAM_STEP)
        d_ref[...] = -ADAM_LR * (m_hat / (jnp.sqrt(v_hat) + ADAM_EPS) + ADAM_WD * w_ref[...])
        go_ref[...] = gv
        mo_ref[...] = m2
        vo_ref[...] = v2

    blk = pl.BlockSpec((tr, C), lambda i: (i, 0))
    shape = jax.ShapeDtypeStruct((R, C), F32)
    return pl.pallas_call(
        body, name=name, grid=(R // tr,),
        in_specs=[blk, pl.BlockSpec((tr, C), lambda i: (i + off, 0)), blk, blk], out_specs=[blk] * 4,
        out_shape=[shape] * 4,
        compiler_params=_params(("parallel",), 8 * _nbytes((tr, C), F32)),
    )(w, g, m, v)


BIG = ("w_up", "w_down", "w_in", "w_o")
IN_PLACE = ("w_up", "w_down")
BIG_AXIS = dict(w_in=2, w_o=1, w_up=2, w_down=1)
SMALL = ("pool_w", "pool_scale", "mix_norm_g", "ln1_g", "ln1_b", "ln2_g", "ln2_b")
CONV_ROWS = 64
SMALL_ROWS = 48


def _big_rows(shards):
    sizes = [shards[n].size // D_MODEL for n in BIG]
    starts = [sum(sizes[:i]) for i in range(len(sizes))]
    return sizes, starts


def _to_chips(a, axis):
    shape = list(a.shape)
    shape[axis:axis + 1] = [N_CHIPS, shape[axis] // N_CHIPS]
    return jnp.moveaxis(a.reshape(shape), axis, 0)


def _from_chips(a, axis):
    a = jnp.moveaxis(a, 0, axis)
    shape = list(a.shape)
    shape[axis:axis + 2] = [shape[axis] * shape[axis + 1]]
    return a.reshape(shape)


def _pad_rows(flat, rows):
    return jnp.pad(flat, (0, rows * D_MODEL - flat.shape[0])).reshape(rows, D_MODEL)


def kernel(x, w_in, conv_w, pool_w, pool_scale, mix_norm_g, w_o, ln1_g, ln1_b, w_up, w_down, ln2_g, ln2_b, loss_target, m_w_in, m_conv_w, m_pool_w, m_pool_scale, m_mix_norm_g, m_w_o, m_ln1_g, m_ln1_b, m_w_up, m_w_down, m_ln2_g, m_ln2_b, v_w_in, v_conv_w, v_pool_w, v_pool_scale, v_mix_norm_g, v_w_o, v_ln1_g, v_ln1_b, v_w_up, v_w_down, v_ln2_g, v_ln2_b):
    wts = dict(w_in=w_in, conv_w=conv_w, pool_w=pool_w, pool_scale=pool_scale, mix_norm_g=mix_norm_g, w_o=w_o,
               ln1_g=ln1_g, ln1_b=ln1_b, w_up=w_up, w_down=w_down, ln2_g=ln2_g, ln2_b=ln2_b)
    mom = dict(w_in=m_w_in, conv_w=m_conv_w, pool_w=m_pool_w, pool_scale=m_pool_scale, mix_norm_g=m_mix_norm_g,
               w_o=m_w_o, ln1_g=m_ln1_g, ln1_b=m_ln1_b, w_up=m_w_up, w_down=m_w_down, ln2_g=m_ln2_g, ln2_b=m_ln2_b)
    var = dict(w_in=v_w_in, conv_w=v_conv_w, pool_w=v_pool_w, pool_scale=v_pool_scale, mix_norm_g=v_mix_norm_g,
               w_o=v_w_o, ln1_g=v_ln1_g, ln1_b=v_ln1_b, w_up=v_w_up, w_down=v_w_down, ln2_g=v_ln2_g, ln2_b=v_ln2_b)
    chip = 2 * lax.axis_index("x") + lax.axis_index("y")
    sizes, starts = _big_rows(wts)
    big_rows = sum(sizes)

    conv_bits = lax.bitcast_convert_type(conv_w.reshape(-1), BF16).reshape(-1)
    pack = jnp.concatenate([wts[n].reshape(-1, D_MODEL).astype(BF16) for n in BIG]
                           + [_pad_rows(conv_bits, CONV_ROWS)], axis=0)
    gathered = _allgather_chips(pack, name="gather_weights")
    gathered = lax.dynamic_update_index_in_dim(gathered, pack, chip, 0)
    full = {}
    first_block = {}
    for n, size, start in zip(BIG, sizes, starts):
        if n in IN_PLACE:
            assert start % CHIP_BLOCK == 0 and size == DEPTH * CHIP_BLOCK
            first_block[n] = start // CHIP_BLOCK
            along = "cols" if BIG_AXIS[n] == 2 else "rows"
            full[n] = [_ChipWeight(gathered, first_block[n] + l, along) for l in range(DEPTH)]
        else:
            full[n] = _from_chips(gathered[:, start:start + size].reshape((N_CHIPS,) + wts[n].shape), BIG_AXIS[n])
    conv_parts = [lax.bitcast_convert_type(gathered[k, big_rows:].reshape(-1)[:2 * conv_w.size].reshape(-1, 2), F32)
                  .reshape(conv_w.shape) for k in range(N_CHIPS)]
    full["conv_w"] = jnp.concatenate(conv_parts, axis=2)
    for n in SMALL:
        full[n] = wts[n]

    gpack = jnp.zeros((N_CHIPS, big_rows, D_MODEL), BF16)
    lsum, grad_x, grads = _local_step(x[0], loss_target[0], full,
                                      grad_pack=(gpack, first_block["w_up"], first_block["w_down"]))

    others = [n for n in BIG if n not in IN_PLACE]
    rest = jnp.concatenate([_to_chips(grads[n], BIG_AXIS[n]).reshape(N_CHIPS, -1, D_MODEL) for n in others], axis=1)
    gpack = lax.dynamic_update_slice_in_dim(grads["pack"], rest.astype(BF16), starts[len(IN_PLACE)], axis=1)
    place = jnp.stack([chip, lax.axis_index("c")]).astype(jnp.int32)
    theirs = _swap_halves(gpack, name="grad_swap_cores")
    add_rows = big_rows // 8
    chip_sum = _add_pairs(gpack, theirs, place, name="grad_add_cores", tr=add_rows)
    from_chips = _scatter_chips(chip_sum, name="grad_scatter_chips")
    half_sum = _add_final(gpack, theirs, from_chips, place, name="grad_add_chips", tr=add_rows)
    gsum = _join_halves(half_sum, name="grad_join_cores")

    small_flat = jnp.concatenate([grads[n].reshape(-1) for n in SMALL] + [grads["conv_w"].reshape(-1),
                                                                          lsum.sum().reshape(1)])
    small_sum = _allreduce_small(_pad_rows(small_flat, SMALL_ROWS), name="allreduce_small").reshape(-1)
    gsmall = {}
    pos = 0
    for n in SMALL:
        gsmall[n] = small_sum[pos:pos + wts[n].size].reshape(wts[n].shape)
        pos += wts[n].size
    conv_full = small_sum[pos:pos + 4 * conv_w.size].reshape(DEPTH, 3, D_CONV)
    pos += 4 * conv_w.size
    loss = small_sum[pos]
    gsmall["conv_w"] = lax.dynamic_slice_in_dim(conv_full, chip * conv_w.shape[2], conv_w.shape[2], axis=2)

    out_g, out_d, out_m, out_v = {}, {}, {}, {}
    for n, size, start in zip(BIG, sizes, starts):
        shp = wts[n].shape
        g, row0 = gsum, start
        if shp[-1] != D_MODEL:
            g, row0 = gsum[start:start + size].reshape(-1, shp[-1]), 0
        res = _adamw(wts[n].reshape(-1, shp[-1]), g, mom[n].reshape(-1, shp[-1]), var[n].reshape(-1, shp[-1]),
                     name="adamw_" + n, tr=256, row0=row0)
        out_g[n], out_d[n], out_m[n], out_v[n] = [r.reshape(shp) for r in res]
    small_names = SMALL + ("conv_w",)
    packs = [_pad_rows(jnp.concatenate([d[n].reshape(-1) for n in small_names]), SMALL_ROWS)
             for d in (wts, gsmall, mom, var)]
    res = _adamw(*packs, name="adamw_small", tr=SMALL_ROWS)
    pos = 0
    for n in small_names:
        shp = wts[n].shape
        out_g[n], out_d[n], out_m[n], out_v[n] = [r.reshape(-1)[pos:pos + wts[n].size].reshape(shp) for r in res]
        pos += wts[n].size

    order = ("w_in", "conv_w", "pool_w", "pool_scale", "mix_norm_g", "w_o", "ln1_g", "ln1_b", "w_up", "w_down",
             "ln2_g", "ln2_b")
    return (loss, grad_x[None], *[out_g[n] for n in order], *[out_d[n] for n in order],
            *[out_m[n] for n in order], *[out_v[n] for n in order])
```

```python
import math
from typing import NamedTuple

import jax
import jax.numpy as jnp
from jax import lax
from jax.experimental import pallas as pl
from jax.experimental.pallas import tpu as pltpu

F32 = jnp.float32
BF16 = jnp.bfloat16
MESH = pl.DeviceIdType.MESH

D_MODEL = 1024
DEPTH = 2
HEAD_DIM = 64
D_SB = 512
D_CONV = 256
D_POOL = 256
D_QKV = 3 * D_SB
D_REST = 3 * D_CONV + D_POOL
D_FF = 4 * D_MODEL
ALPHA = (2 * DEPTH) ** 0.25
LN_EPS = 1e-5
RMS_EPS = 1e-6
SCALE = HEAD_DIM ** -0.5
N_CHIPS = 4
HALO = 16

ADAM_LR = 0.001
ADAM_B1 = 0.9
ADAM_B2 = 0.999
ADAM_EPS = 1e-08
ADAM_WD = 0.01
ADAM_STEP = 10

VMEM_V7X_BYTES = 64 * 1024 * 1024
VMEM_CAP_BYTES = VMEM_V7X_BYTES - 8 * 1024 * 1024


def _params(sem, block_bytes):
    limit = min(VMEM_CAP_BYTES, max(32 * 1024 * 1024, 3 * block_bytes))
    return pltpu.CompilerParams(dimension_semantics=sem, vmem_limit_bytes=limit)


def _nbytes(shape, dtype):
    return math.prod(shape) * jnp.dtype(dtype).itemsize


def _dot(a, b, dims=(((1,), (0,)), ((), ()))):
    return lax.dot_general(a, b, dims, preferred_element_type=F32)


NT = (((1,), (1,)), ((), ()))
TN = (((0,), (0,)), ((), ()))


def _split(x):
    hi = x.astype(BF16)
    lo = (x - hi.astype(F32)).astype(BF16)
    return hi, lo


def _sum8(x):
    r, c = x.shape
    return x.reshape(r // 8, 8, c).sum(axis=0)


def _ln_bwd_rows(dy, xh_ref, rs_ref, g_ref, dr_ref, dr16_ref, dg_ref, db_ref, first):
    xh = xh_ref[...]
    dxh = dy * g_ref[...]
    m1 = jnp.mean(dxh, axis=-1, keepdims=True)
    m2 = jnp.mean(dxh * xh, axis=-1, keepdims=True)
    dr = rs_ref[...] * (dxh - m1 - xh * m2)
    dr_ref[...] = dr
    dr16_ref[...] = dr.astype(BF16)
    pg = _sum8(dy * xh)
    pb = _sum8(dy)

    @pl.when(first)
    def _():
        dg_ref[...] = pg
        db_ref[...] = pb

    @pl.when(jnp.logical_not(first))
    def _():
        dg_ref[...] += pg
        db_ref[...] += pb


class _ChipWeight(NamedTuple):
    arr: jax.Array
    rb: int
    along: str


CHIP_BLOCK = 1024


def _matmul(a, b, *, name, tm, tn, tk, ta=False, tb=False, out_dtype=F32,
            epi=None, e=None, e_scale=1.0, relu2_out=False, out_chips=None, ln_bwd=None, also=None):
    M, K = (a.shape[1], a.shape[0]) if ta else a.shape
    chips = isinstance(b, _ChipWeight)
    split_k = chips and ((b.along == "cols") == tb)
    if chips:
        N = CHIP_BLOCK if split_k else N_CHIPS * CHIP_BLOCK
        assert K == (N_CHIPS * CHIP_BLOCK if split_k else CHIP_BLOCK) and not ta, (name, K)
        tn, tk = CHIP_BLOCK, K
    else:
        N = b.shape[0] if tb else b.shape[1]
    tm, tn, tk = min(tm, M), min(tn, N), min(tk, K)
    assert M % tm == 0 and N % tn == 0 and K % tk == 0, (name, M, N, K)
    nk = K // tk
    dims = (((0 if ta else 1,), (1 if tb else 0,)), ((), ()))
    i_also = 2 + (epi is not None) + (out_chips is not None)
    n_in = i_also + (2 if also is not None else 0) + (3 if ln_bwd is not None else 0)
    assert also is None or (nk == 1 and not ta)
    assert ln_bwd is None or (tn == N and out_chips is None and not relu2_out)

    def body(*refs):
        a_ref, b_ref = refs[0], refs[1]
        e_ref = refs[2] if epi is not None else None
        o_ref = refs[n_in]
        scr = refs[-1:]
        if not chips:
            p = _dot(a_ref[...].astype(BF16), b_ref[...].astype(BF16), dims)
        elif split_k:
            p = _dot(a_ref[:, 0:CHIP_BLOCK].astype(BF16), b_ref[0], dims)
            for c in range(1, N_CHIPS):
                p = p + _dot(a_ref[:, c * CHIP_BLOCK:(c + 1) * CHIP_BLOCK].astype(BF16), b_ref[c], dims)
        else:
            p = _dot(a_ref[...].astype(BF16), b_ref[0], dims)
        if also is not None:
            p = p + _dot(refs[i_also][...].astype(BF16), refs[i_also + 1][...].astype(BF16), dims)

        def finish(acc):
            if epi == "drelu2":
                acc = acc * (2.0 * jnp.maximum(e_ref[...].astype(F32), 0.0))
            elif epi == "add":
                acc = acc + e_scale * e_ref[...]
            if ln_bwd is not None:
                _ln_bwd_rows(acc, *refs[n_in - 3:n_in + 4], pl.program_id(0) == 0)
            elif out_chips is None:
                o_ref[...] = acc.astype(out_dtype)
            else:
                o_ref[0] = acc.astype(out_dtype)
            if relu2_out:
                refs[n_in + 1][...] = jnp.square(jnp.maximum(acc, 0.0)).astype(BF16)

        if nk == 1:
            finish(p)
        else:
            acc_ref = scr[0]
            k = pl.program_id(2)

            @pl.when(k == 0)
            def _():
                acc_ref[...] = p

            @pl.when(k > 0)
            def _():
                acc_ref[...] += p

            @pl.when(k == nk - 1)
            def _():
                finish(acc_ref[...])

    a_spec = pl.BlockSpec((tk, tm), lambda i, j, k: (k, i)) if ta else pl.BlockSpec((tm, tk), lambda i, j, k: (i, k))
    if chips:
        b_arr, rb = b.arr, b.rb
        nblk = N_CHIPS if split_k else 1
        b_spec = pl.BlockSpec((nblk, CHIP_BLOCK, CHIP_BLOCK),
                              (lambda i, j, k: (0, rb, 0)) if split_k else (lambda i, j, k: (j, rb, 0)))
    else:
        b_arr = b
        b_spec = pl.BlockSpec((tn, tk), lambda i, j, k: (j, k)) if tb else pl.BlockSpec((tk, tn), lambda i, j, k: (k, j))
    o_spec = pl.BlockSpec((tm, tn), lambda i, j, k: (i, j))
    in_specs = [a_spec, b_spec]
    args = [a, b_arr]
    nbytes = _nbytes((tm, tk), a.dtype) + _nbytes((tk, tn), b_arr.dtype) + 2 * _nbytes((tm, tn), F32)
    if epi is not None:
        in_specs.append(o_spec)
        args.append(e)
        nbytes += _nbytes((tm, tn), e.dtype)
    scratch = [pltpu.VMEM((tm, tn), F32)] if nk > 1 else []
    out_shape = [jax.ShapeDtypeStruct((M, N), out_dtype)]
    out_specs = [o_spec]
    aliases = {}
    if out_chips is not None:
        assert tm == tn == CHIP_BLOCK and not relu2_out and out_chips.arr.dtype == out_dtype
        orb = out_chips.rb
        out_specs = [pl.BlockSpec((1, CHIP_BLOCK, CHIP_BLOCK),
                                  (lambda i, j, k: (j, orb, 0)) if out_chips.along == "cols" else
                                  (lambda i, j, k: (i, orb, 0)))]
        out_shape = [jax.ShapeDtypeStruct(out_chips.arr.shape, out_dtype)]
        in_specs.append(pl.BlockSpec(memory_space=pl.ANY))
        args.append(out_chips.arr)
        aliases = {len(args) - 1: 0}
    if relu2_out:
        out_shape.append(jax.ShapeDtypeStruct((M, N), BF16))
        out_specs.append(o_spec)
        nbytes += _nbytes((tm, tn), BF16)
    if also is not None:
        a2, b2 = also
        k2 = a2.shape[1]
        in_specs += [pl.BlockSpec((tm, k2), lambda i, j, k: (i, 0)),
                     pl.BlockSpec((tn, k2), lambda i, j, k: (j, 0)) if tb else pl.BlockSpec((k2, tn), lambda i, j, k: (0, j))]
        args += [a2, b2]
        nbytes += _nbytes((tm, k2), a2.dtype) + _nbytes((k2, tn), b2.dtype)
    sem = ("parallel", "parallel", "arbitrary")
    if ln_bwd is not None:
        xhat, rstd, gain = ln_bwd
        in_specs += [o_spec, pl.BlockSpec((tm, 1), lambda i, j, k: (i, 0)), pl.BlockSpec((1, N), lambda i, j, k: (0, 0))]
        args += [xhat, rstd, gain.reshape(1, N)]
        acc_spec = pl.BlockSpec((8, N), lambda i, j, k: (0, 0))
        out_specs = [o_spec, o_spec, acc_spec, acc_spec]
        out_shape = [jax.ShapeDtypeStruct((M, N), F32), jax.ShapeDtypeStruct((M, N), BF16),
                     jax.ShapeDtypeStruct((8, N), F32), jax.ShapeDtypeStruct((8, N), F32)]
        nbytes += 3 * _nbytes((tm, tn), F32)
        sem = ("arbitrary", "arbitrary", "arbitrary")
    res = pl.pallas_call(
        body, name=name,
        grid=(M // tm, N // tn, nk),
        in_specs=in_specs, out_specs=out_specs,
        out_shape=out_shape,
        scratch_shapes=scratch,
        input_output_aliases=aliases,
        compiler_params=_params(sem, nbytes),
    )(*args)
    return res if (relu2_out or ln_bwd is not None) else res[0]


def _matmul_ln(a, b, xres, g, bias, *, name, tm, tk, res_affine=None):
    M, K = a.shape
    chips = isinstance(b, _ChipWeight)
    if chips:
        assert b.along == "rows" and K == N_CHIPS * CHIP_BLOCK
        N, tk = CHIP_BLOCK, K
    else:
        N = b.shape[1]
    tm, tk = min(tm, M), min(tk, K)
    assert M % tm == 0 and K % tk == 0 and N == D_MODEL
    nk = K // tk

    n_vec = 2 if res_affine is None else 4

    def body(*refs):
        a_ref, b_ref, x_ref, g_ref, bias_ref = refs[:5]
        y16_ref, xh_ref, rs_ref = refs[3 + n_vec:6 + n_vec]
        scr = refs[6 + n_vec:]
        if chips:
            p = _dot(a_ref[:, 0:CHIP_BLOCK].astype(BF16), b_ref[0])
            for c in range(1, N_CHIPS):
                p = p + _dot(a_ref[:, c * CHIP_BLOCK:(c + 1) * CHIP_BLOCK].astype(BF16), b_ref[c])
        else:
            p = _dot(a_ref[...].astype(BF16), b_ref[...].astype(BF16))

        def finish(acc):
            xv = x_ref[...]
            if res_affine is not None:
                xv = xv * refs[5][...] + refs[6][...]
            r = ALPHA * xv + acc
            mu = jnp.mean(r, axis=-1, keepdims=True)
            xc = r - mu
            var = jnp.mean(xc * xc, axis=-1, keepdims=True)
            rstd = lax.rsqrt(var + LN_EPS)
            xh = xc * rstd
            y16_ref[...] = (xh * g_ref[...] + bias_ref[...]).astype(BF16)
            xh_ref[...] = xh
            rs_ref[...] = rstd

        if nk == 1:
            finish(p)
        else:
            acc_ref = scr[0]
            k = pl.program_id(1)

            @pl.when(k == 0)
            def _():
                acc_ref[...] = p

            @pl.when(k > 0)
            def _():
                acc_ref[...] += p

            @pl.when(k == nk - 1)
            def _():
                finish(acc_ref[...])

    row = pl.BlockSpec((tm, N), lambda i, k: (i, 0))
    vec = pl.BlockSpec((1, N), lambda i, k: (0, 0))
    if chips:
        b_arr, rb = b.arr, b.rb
        b_spec = pl.BlockSpec((N_CHIPS, CHIP_BLOCK, CHIP_BLOCK), lambda i, k: (0, rb, 0))
    else:
        b_arr = b
        b_spec = pl.BlockSpec((tk, N), lambda i, k: (k, 0))
    nbytes = _nbytes((tm, tk), a.dtype) + _nbytes((tk, N), b_arr.dtype) + 6 * _nbytes((tm, N), F32)
    scratch = [pltpu.VMEM((tm, N), F32)] if nk > 1 else []
    return pl.pallas_call(
        body, name=name,
        grid=(M // tm, nk),
        in_specs=[pl.BlockSpec((tm, tk), lambda i, k: (i, k)), b_spec, row] + [vec] * n_vec,
        out_specs=[row, row, pl.BlockSpec((tm, 1), lambda i, k: (i, 0))],
        out_shape=[jax.ShapeDtypeStruct((M, N), BF16), jax.ShapeDtypeStruct((M, N), F32),
                   jax.ShapeDtypeStruct((M, 1), F32)],
        scratch_shapes=scratch,
        compiler_params=_params(("parallel", "arbitrary"), nbytes),
    )(a, b_arr, xres, g.reshape(1, N), bias.reshape(1, N), *[v.reshape(1, N) for v in (res_affine or ())])


def _loss_ln_bwd(tgt, xhat, rstd, g, bias, *, name, tm):
    M, N = tgt.shape
    tm = min(tm, M)

    def body(t_ref, xh_ref, rs_ref, g_ref, bias_ref, dr_ref, dr16_ref, dg_ref, db_ref, l_ref):
        first = pl.program_id(0) == 0
        d = (xh_ref[...] * g_ref[...] + bias_ref[...]) - t_ref[...]
        part = _sum8(d * d) * (0.5 / N)

        @pl.when(first)
        def _():
            l_ref[...] = part

        @pl.when(jnp.logical_not(first))
        def _():
            l_ref[...] += part

        _ln_bwd_rows(d * (1.0 / N), xh_ref, rs_ref, g_ref, dr_ref, dr16_ref, dg_ref, db_ref, first)

    row = pl.BlockSpec((tm, N), lambda i: (i, 0))
    acc = pl.BlockSpec((8, N), lambda i: (0, 0))
    return pl.pallas_call(
        body, name=name, grid=(M // tm,),
        in_specs=[row, row, pl.BlockSpec((tm, 1), lambda i: (i, 0))] + [pl.BlockSpec((1, N), lambda i: (0, 0))] * 2,
        out_specs=[row, row, acc, acc, acc],
        out_shape=[jax.ShapeDtypeStruct((M, N), F32), jax.ShapeDtypeStruct((M, N), BF16)]
        + [jax.ShapeDtypeStruct((8, N), F32)] * 3,
        compiler_params=_params(("arbitrary",), 6 * _nbytes((tm, N), F32)),
    )(tgt, xhat, rstd, g.reshape(1, N), bias.reshape(1, N))


def _tri(n, kind):
    j = lax.broadcasted_iota(jnp.int32, (2 * n, n), 0) % n
    s = lax.broadcasted_iota(jnp.int32, (2 * n, n), 1)
    return ((j > s) if kind == "after" else (j < s)).astype(BF16)


LOG2E = 1.4426950408889634
DEAD = -104.0
NOT_VISITED = -1e30


def _log_terms(z):
    lse = jnp.log(1.0 + jnp.exp2(jnp.abs(z) * (-LOG2E)))
    logsig = jnp.minimum(z, 0.0) - lse
    return logsig, logsig - z


def _cumsum_mm(x, u2_ref):
    hi, lo = _split(x)
    return _dot(jnp.concatenate([hi, lo], axis=1), u2_ref[...])


def _head_rows(x2, scale):
    lane = lax.broadcasted_iota(jnp.int32, (1, 128), 1)
    zero = jnp.zeros_like(x2)
    both = jnp.concatenate([jnp.where(lane < HEAD_DIM, x2, zero), jnp.where(lane >= HEAD_DIM, x2, zero)], axis=0)
    return both * scale


def _causal_mask(i, ks, tq, tk):
    row = lax.broadcasted_iota(jnp.int32, (2 * tq, tk), 0)
    row = i * tq + jnp.where(row >= tq, row - tq, row)
    col = lax.broadcasted_iota(jnp.int32, (2 * tq, tk), 1)
    return (ks + col) < row


def _attn_fwd(qkv, *, name, tq, tk):
    S = qkv.shape[0]
    tq = tk = min(tq, tk, S)
    assert S % tq == 0 and S // tk <= 128
    tri = _tri(tk, "after")

    def body(q_ref, k_ref, v_ref, u_ref, o_ref, c_ref, qcat, oacc, cacc, call, ls_buf, tl_buf, l0_buf):
        i = pl.program_id(1)
        lane = lax.broadcasted_iota(jnp.int32, (1, 128), 1)
        qcat[...] = _head_rows(q_ref[...], SCALE)
        oacc[...] = jnp.zeros_like(oacc)
        cacc[...] = jnp.zeros_like(cacc)
        call[...] = jnp.full_like(call, NOT_VISITED)

        def scores(kb, masked, slot):
            ks = pl.multiple_of(jnp.maximum(kb, 0) * tk, tk)
            z = _dot(qcat[...], k_ref[pl.ds(ks, tk), :], NT)
            logsig, lom = _log_terms(z)
            if masked:
                msk = jnp.logical_and(_causal_mask(i, ks, tq, tk), kb >= 0)
                lom = jnp.where(msk, lom, 0.0)
                logsig = jnp.where(msk, logsig, -1e30)
            ls_buf[slot] = logsig
            tl_buf[slot] = _cumsum_mm(lom, u_ref)
            l0_buf[slot] = lom[:, 0:1]

        def weights(kb, slot):
            ks = pl.multiple_of(jnp.maximum(kb, 0) * tk, tk)
            tl = tl_buf[slot]
            c = cacc[...]
            call[...] = jnp.where(lane == kb, c, call[...])
            a = jnp.exp(ls_buf[slot] + tl + c).astype(BF16)
            oacc[...] += _dot(a, v_ref[pl.ds(ks, tk), :])
            cacc[...] = c + tl[:, 0:1] + l0_buf[slot]

        def pair(kb, masked, masked_next):
            scores(kb, masked, 0)
            scores(kb - 1, masked_next, 1)
            weights(kb, 0)
            weights(kb - 1, 1)

        pair(i, True, True)

        def live(state):
            t, cmax = state
            return jnp.logical_and(t < (i - 1) // 2, cmax > DEAD)

        def trip(state):
            t, _ = state
            pair(i - 2 - 2 * t, False, False)
            return t + 1, jnp.max(cacc[...])

        t_end, cmax = lax.while_loop(live, trip, (0, jnp.max(cacc[...])))

        left_over = jnp.logical_and(i >= 2, i % 2 == 0)
        still_live = jnp.logical_and(t_end == (i - 1) // 2, cmax > DEAD)

        @pl.when(jnp.logical_and(left_over, still_live))
        def _():
            pair(0, False, True)

        o_ref[...] = jnp.where(lane < HEAD_DIM, oacc[0:tq], oacc[tq:2 * tq])
        c_ref[...] = jnp.concatenate([call[0:tq], call[tq:2 * tq]], axis=1)

    nbytes = (_nbytes((tq, 128), BF16) + 2 * _nbytes((S, 128), BF16) + _nbytes((2 * tk, tk), BF16)
              + 8 * _nbytes((tq, 128), F32) + 14 * _nbytes((2 * tq, tk), F32))
    return pl.pallas_call(
        body, name=name, grid=(4, S // tq),
        in_specs=[pl.BlockSpec((tq, 128), lambda j, i: (i, j)),
                  pl.BlockSpec((S, 128), lambda j, i: (0, 4 + j)),
                  pl.BlockSpec((S, 128), lambda j, i: (0, 8 + j)),
                  pl.BlockSpec((2 * tk, tk), lambda j, i: (0, 0))],
        out_specs=[pl.BlockSpec((tq, 128), lambda j, i: (i, j)),
                   pl.BlockSpec((tq, 256), lambda j, i: (i, j))],
        out_shape=[jax.ShapeDtypeStruct((S, D_SB), F32), jax.ShapeDtypeStruct((S, 1024), F32)],
        scratch_shapes=[pltpu.VMEM((2 * tq, 128), BF16), pltpu.VMEM((2 * tq, 128), F32),
                        pltpu.VMEM((2 * tq, 1), F32), pltpu.VMEM((2 * tq, 128), F32),
                        pltpu.VMEM((2, 2 * tq, tk), F32), pltpu.VMEM((2, 2 * tq, tk), F32),
                        pltpu.VMEM((2, 2 * tq, 1), F32)],
        compiler_params=_params(("parallel", "arbitrary"), nbytes),
    )(qkv, qkv, qkv, tri)


def _attn_bwd(qkv, carry, do, *, name, tq, tk):
    S = qkv.shape[0]
    tq = tk = min(tq, tk, S)
    assert S % tq == 0 and S // tk <= 128
    nkb = S // tk
    nq = S // tq
    tri_after = _tri(tk, "after")
    tri_before = _tri(tk, "before")

    def body(q_ref, k_ref, v_ref, c_ref, do_ref, ua_ref, ub_ref, dq_ref, dk_ref, dv_ref,
             qcat, docat, qcat_t, docat_t, ccat, dqacc, pacc, dkt, dvt, ls_buf, tl_buf, da_buf):
        i = pl.program_id(1)
        lane = lax.broadcasted_iota(jnp.int32, (1, 128), 1)
        sub = lax.broadcasted_iota(jnp.int32, (128, 1), 0)
        q2 = q_ref[...]
        do2 = do_ref[...]
        qcat[...] = _head_rows(q2, SCALE)
        docat[...] = _head_rows(do2, 1.0).astype(BF16)
        qt = q2.astype(F32).T * SCALE
        dot_ = do2.T
        qcat_t[...] = jnp.concatenate([jnp.where(sub < HEAD_DIM, qt, 0.0), jnp.where(sub >= HEAD_DIM, qt, 0.0)],
                                      axis=1).astype(BF16)
        docat_t[...] = jnp.concatenate([jnp.where(sub < HEAD_DIM, dot_, 0.0), jnp.where(sub >= HEAD_DIM, dot_, 0.0)],
                                       axis=1).astype(BF16)
        ccat[0:tq] = c_ref[:, 0:128]
        ccat[tq:2 * tq] = c_ref[:, 128:256]

        @pl.when(i == 0)
        def _():
            dkt[...] = jnp.zeros_like(dkt)
            dvt[...] = jnp.zeros_like(dvt)

        dqacc[...] = jnp.zeros_like(dqacc)
        pacc[...] = jnp.zeros_like(pacc)

        def scores(kb, masked, slot):
            ks = pl.multiple_of(jnp.maximum(kb, 0) * tk, tk)
            z = _dot(qcat[...], k_ref[pl.ds(ks, tk), :], NT)
            logsig, lom = _log_terms(z)
            if masked:
                msk = jnp.logical_and(_causal_mask(i, ks, tq, tk), kb >= 0)
                lom = jnp.where(msk, lom, 0.0)
                logsig = jnp.where(msk, logsig, -1e30)
            ls_buf[slot] = logsig
            tl_buf[slot] = _cumsum_mm(lom, ua_ref)
            da_buf[slot] = _dot(docat[...], v_ref[pl.ds(ks, tk), :], NT)

        def grads(kb, slot):
            kbc = jnp.maximum(kb, 0)
            ks = pl.multiple_of(kbc * tk, tk)
            logsig = ls_buf[slot]
            c = jnp.sum(jnp.where(lane == kb, ccat[...], 0.0), axis=1, keepdims=True)
            a = jnp.exp(logsig + tl_buf[slot] + c)
            g = a * da_buf[slot]
            before = _cumsum_mm(g, ub_ref)
            pc = pacc[...]
            dz = g - jnp.exp(logsig) * (g + before + pc)
            dzb = dz.astype(BF16)
            dqacc[...] += _dot(dzb, k_ref[pl.ds(ks, tk), :])
            dkt[kbc] += _dot(qcat_t[...], dzb)
            dvt[kbc] += _dot(docat_t[...], a.astype(BF16))
            pacc[...] = pc + before[:, tk - 1:tk] + g[:, tk - 1:tk]

        def pair(kb, masked, masked_next):
            scores(kb, masked, 0)
            scores(kb + 1, masked_next, 1)
            grads(kb, 0)
            grads(kb + 1, 1)

        reach = jnp.max(ccat[...], axis=0, keepdims=True)
        first = jnp.min(jnp.where(reach > DEAD, lane, 128).astype(F32)).astype(jnp.int32)
        first = jnp.minimum(first, i)
        start = first - (i - first + 1) % 2

        @pl.when(jnp.logical_and(start < 0, i >= 2))
        def _():
            pair(-1, True, False)

        k0 = jnp.where(start < 0, 1, start)

        def loop(t, carry_):
            pair(k0 + 2 * t, False, False)
            return carry_

        lax.fori_loop(0, jnp.maximum((i - 1 - k0) // 2, 0), loop, 0)

        @pl.when(i == 0)
        def _():
            pair(-1, True, True)

        @pl.when(i > 0)
        def _():
            pair(i - 1, False, True)
        dq_ref[...] = (jnp.where(lane < HEAD_DIM, dqacc[0:tq], dqacc[tq:2 * tq]) * SCALE).astype(BF16)

        @pl.when(i == nq - 1)
        def _():
            for kb in range(nkb):
                dk_ref[kb * tk:(kb + 1) * tk, :] = dkt[kb].T.astype(BF16)
                dv_ref[kb * tk:(kb + 1) * tk, :] = dvt[kb].T.astype(BF16)

    nbytes = (_nbytes((tq, 128), BF16) + 2 * _nbytes((S, 128), BF16) + 2 * _nbytes((2 * tk, tk), BF16)
              + 12 * _nbytes((tq, 128), F32) + 4 * _nbytes((S, 128), F32) + 14 * _nbytes((2 * tq, tk), F32))
    blk = pl.BlockSpec((tq, 128), lambda j, i: (i, j))
    full = pl.BlockSpec((S, 128), lambda j, i: (0, j))
    tri_spec = pl.BlockSpec((2 * tk, tk), lambda j, i: (0, 0))
    dq, dk, dv = pl.pallas_call(
        body, name=name, grid=(4, nq),
        in_specs=[blk,
                  pl.BlockSpec((S, 128), lambda j, i: (0, 4 + j)),
                  pl.BlockSpec((S, 128), lambda j, i: (0, 8 + j)),
                  pl.BlockSpec((tq, 256), lambda j, i: (i, j)),
                  blk, tri_spec, tri_spec],
        out_specs=[blk, full, full],
        out_shape=[jax.ShapeDtypeStruct((S, D_SB), BF16)] * 3,
        scratch_shapes=[pltpu.VMEM((2 * tq, 128), BF16), pltpu.VMEM((2 * tq, 128), BF16),
                        pltpu.VMEM((128, 2 * tq), BF16), pltpu.VMEM((128, 2 * tq), BF16),
                        pltpu.VMEM((2 * tq, 128), F32), pltpu.VMEM((2 * tq, 128), F32), pltpu.VMEM((2 * tq, 1), F32),
                        pltpu.VMEM((nkb, 128, tk), F32), pltpu.VMEM((nkb, 128, tk), F32),
                        pltpu.VMEM((2, 2 * tq, tk), F32), pltpu.VMEM((2, 2 * tq, tk), F32),
                        pltpu.VMEM((2, 2 * tq, tk), F32)],
        compiler_params=_params(("parallel", "arbitrary"), nbytes),
    )(qkv, qkv, qkv, carry, do, tri_after, tri_before)
    return dq, dk, dv


def _group_mats():
    lanes = jnp.arange(D_MODEL) // HEAD_DIM
    gs = (lanes[:, None] == jnp.arange(128)[None, :]).astype(BF16)
    return gs, gs.T


def _group_sum_bcast(x, gs, gb):
    hi, lo = _split(x)
    s = _dot(hi, gs) + _dot(lo, gs)
    return _bcast(s, gb)


def _bcast(s, gb):
    hi, lo = _split(s)
    return _dot(hi, gb) + _dot(lo, gb)


def _pool_lane_consts():
    lane = lax.broadcasted_iota(jnp.int32, (1, D_POOL), 1)
    grp = lane // (D_POOL // 4)
    win = jnp.where(grp == 0, 2, jnp.where(grp == 1, 4, jnp.where(grp == 2, 8, 16)))
    return grp, win


def _by_group(grp, s2, s4, s8, s16):
    return jnp.where(grp == 0, s2, jnp.where(grp == 1, s4, jnp.where(grp == 2, s8, s16)))


def _mixers(i, ts, prev_ref, cur_ref, cw_ref, pw_ref, ps_ref):
    cur = cur_ref[...]
    prev = jnp.where(i == 0, 0.0, prev_ref[...])
    ext = jnp.concatenate([prev, cur], axis=0)

    def back(a, k):
        return pltpu.roll(a, k, 0)

    u = ext[:, D_CONV:2 * D_CONV] * ext[:, 2 * D_CONV:3 * D_CONV]
    p = ext[:, 3 * D_CONV:]
    cv = (cw_ref[0:1, :] * back(u, 2) + cw_ref[1:2, :] * back(u, 1) + cw_ref[2:3, :] * u)[HALO:]
    s2 = p + back(p, 1)
    s4 = s2 + back(s2, 2)
    s8 = s4 + back(s4, 4)
    s16 = s8 + back(s8, 8)
    grp, win = _pool_lane_consts()
    t1 = i * ts + 1 + lax.broadcasted_iota(jnp.int32, (ts, 1), 0)
    cnt = jnp.minimum(t1, win).astype(F32)
    pooled = _by_group(grp, s2, s4, s8, s16)[HALO:] / cnt - p[HALO:]
    yp = _dot(pooled.astype(BF16), pw_ref[...])
    return dict(b=cur[:, 0:D_CONV], u=u, cv=cv, pooled=pooled, yp=yp, cnt=cnt,
                conv_out=cur[:, 0:D_CONV] * cv, pool_out=yp * ps_ref[...])


def _halo_specs(ts, S, width):
    nb = ts // HALO
    last = S // HALO - 1
    prev = pl.BlockSpec((HALO, width), lambda i: (jnp.maximum(i * nb - 1, 0), 0))
    nxt = pl.BlockSpec((HALO, width), lambda i: (jnp.minimum((i + 1) * nb, last), 0))
    return prev, nxt


def _mixer_fwd(rest, attn, cw8, pwbd, ps, gain, *, name, ts):
    S = rest.shape[0]
    ts = min(ts, S)
    gs, gb = _group_mats()

    def body(prev_ref, cur_ref, attn_ref, cw_ref, pw_ref, ps_ref, gain_ref, gs_ref, gb_ref, o_ref):
        i = pl.program_id(0)
        f = _mixers(i, ts, prev_ref, cur_ref, cw_ref, pw_ref, ps_ref)
        mix = jnp.concatenate([attn_ref[...], f["conv_out"], f["pool_out"]], axis=1)
        ss = _group_sum_bcast(mix * mix, gs_ref[...], gb_ref[...])
        rinv = lax.rsqrt(ss * (1.0 / HEAD_DIM) + RMS_EPS)
        o_ref[...] = (mix * rinv * gain_ref[...]).astype(BF16)

    prev, _ = _halo_specs(ts, S, D_REST)
    row = lambda w: pl.BlockSpec((ts, w), lambda i: (i, 0))
    const = lambda a: pl.BlockSpec(a.shape, lambda i: (0, 0))
    nbytes = 12 * _nbytes((ts + HALO, D_REST), F32)
    return pl.pallas_call(
        body, name=name, grid=(S // ts,),
        in_specs=[prev, row(D_REST), row(D_SB), const(cw8), const(pwbd), const(ps), const(gain), const(gs), const(gb)],
        out_specs=row(D_MODEL),
        out_shape=jax.ShapeDtypeStruct((S, D_MODEL), BF16),
        compiler_params=_params(("parallel",), nbytes),
    )(rest, rest, attn, cw8, pwbd, ps, gain, gs, gb)


def _mixer_bwd1(dmixn, rest, attn, cw8, pwbd, ps, gain, *, name, ts):
    S = rest.shape[0]
    ts = min(ts, S)
    gs, gb = _group_mats()

    def body(dm_ref, prev_ref, cur_ref, attn_ref, cw_ref, pw_ref, ps_ref, gain_ref, gs_ref, gb_ref,
             da_ref, aux_ref, dg_ref, dsc_ref, dcw_ref, dpw_ref):
        i = pl.program_id(0)
        f = _mixers(i, ts, prev_ref, cur_ref, cw_ref, pw_ref, ps_ref)
        mix = jnp.concatenate([attn_ref[...], f["conv_out"], f["pool_out"]], axis=1)
        gsm, gbm = gs_ref[...], gb_ref[...]
        ss = _group_sum_bcast(mix * mix, gsm, gbm)
        rinv = lax.rsqrt(ss * (1.0 / HEAD_DIM) + RMS_EPS)
        dm = dm_ref[...]
        xn = mix * rinv
        dyg = dm * gain_ref[...]
        gm = _group_sum_bcast(dyg * xn, gsm, gbm) * (1.0 / HEAD_DIM)
        dmix = rinv * (dyg - xn * gm)
        da_ref[...] = dmix[:, 0:D_SB]
        dco = dmix[:, D_SB:D_SB + D_CONV]
        dpo = dmix[:, D_SB + D_CONV:]
        dcv = dco * f["b"]
        dyp = dpo * ps_ref[...]
        dpooled = _dot(dyp.astype(BF16), pw_ref[...], NT)
        aux_ref[...] = jnp.concatenate([dco * f["cv"], dcv, dpooled / f["cnt"], dpooled], axis=1)
        u = f["u"]
        parts = [
            _sum8(dm * xn),
            _sum8(dpo * f["yp"]),
            jnp.concatenate([_sum8(dcv * pltpu.roll(u, 2, 0)[HALO:]), _sum8(dcv * pltpu.roll(u, 1, 0)[HALO:]),
                             _sum8(dcv * u[HALO:])], axis=0),
            _dot(f["pooled"].astype(BF16), dyp.astype(BF16), TN),
        ]
        outs = [dg_ref, dsc_ref, dcw_ref, dpw_ref]

        @pl.when(i == 0)
        def _():
            for o, v in zip(outs, parts):
                o[...] = v

        @pl.when(i > 0)
        def _():
            for o, v in zip(outs, parts):
                o[...] += v

    prev, _ = _halo_specs(ts, S, D_REST)
    row = lambda w: pl.BlockSpec((ts, w), lambda i: (i, 0))
    const = lambda a: pl.BlockSpec(a.shape, lambda i: (0, 0))
    acc = lambda r_, w: pl.BlockSpec((r_, w), lambda i: (0, 0))
    nbytes = 16 * _nbytes((ts + HALO, D_REST), F32)
    return pl.pallas_call(
        body, name=name, grid=(S // ts,),
        in_specs=[row(D_MODEL), prev, row(D_REST), row(D_SB), const(cw8), const(pwbd), const(ps), const(gain),
                  const(gs), const(gb)],
        out_specs=[row(D_SB), row(D_REST), acc(8, D_MODEL), acc(8, D_POOL), acc(24, D_CONV), acc(D_POOL, D_POOL)],
        out_shape=[jax.ShapeDtypeStruct((S, D_SB), F32), jax.ShapeDtypeStruct((S, D_REST), F32),
                   jax.ShapeDtypeStruct((8, D_MODEL), F32), jax.ShapeDtypeStruct((8, D_POOL), F32),
                   jax.ShapeDtypeStruct((24, D_CONV), F32), jax.ShapeDtypeStruct((D_POOL, D_POOL), F32)],
        compiler_params=_params(("arbitrary",), nbytes),
    )(dmixn, rest, rest, attn, cw8, pwbd, ps, gain, gs, gb)


def _mixer_bwd2(aux, rest, cw8, *, name, ts):
    S = rest.shape[0]
    ts = min(ts, S)
    nblk = S // ts

    def body(cur_ref, nxt_ref, rest_ref, cw_ref, o_ref):
        i = pl.program_id(0)
        cur = cur_ref[...]
        nxt = jnp.where(i == nblk - 1, 0.0, nxt_ref[...])
        ext = jnp.concatenate([cur, nxt], axis=0)
        n = ts + HALO

        def fwd(a, k):
            return pltpu.roll(a, n - k, 0)

        dcv = ext[:, D_CONV:2 * D_CONV]
        dps = ext[:, 2 * D_CONV:3 * D_CONV]
        du = (cw_ref[2:3, :] * dcv + cw_ref[1:2, :] * fwd(dcv, 1) + cw_ref[0:1, :] * fwd(dcv, 2))[0:ts]
        f2 = dps + fwd(dps, 1)
        f4 = f2 + fwd(f2, 2)
        f8 = f4 + fwd(f4, 4)
        f16 = f8 + fwd(f8, 8)
        grp, _ = _pool_lane_consts()
        dp = _by_group(grp, f2, f4, f8, f16)[0:ts] - cur[:, 3 * D_CONV:]
        rest_v = rest_ref[...]
        c_gate = rest_v[:, D_CONV:2 * D_CONV]
        h = rest_v[:, 2 * D_CONV:3 * D_CONV]
        o_ref[...] = jnp.concatenate([cur[:, 0:D_CONV], du * h, du * c_gate, dp], axis=1).astype(BF16)

    _, nxt = _halo_specs(ts, S, D_REST)
    row = pl.BlockSpec((ts, D_REST), lambda i: (i, 0))
    return pl.pallas_call(
        body, name=name, grid=(nblk,),
        in_specs=[row, nxt, row, pl.BlockSpec(cw8.shape, lambda i: (0, 0))],
        out_specs=row,
        out_shape=jax.ShapeDtypeStruct((S, D_REST), BF16),
        compiler_params=_params(("parallel",), 10 * _nbytes((ts + HALO, D_REST), F32)),
    )(aux, aux, rest, cw8)


def _block_diag(pw):
    wide = jnp.tile(pw.reshape(256, 64), (1, 4))
    grp = jnp.arange(256) // 64
    return jnp.where(grp[:, None] == grp[None, :], wide, 0.0)


def _rows8(v, rows=8):
    return jnp.pad(v, ((0, rows - v.shape[0]), (0, 0)))


TILES = dict(tm=512, ts=512, tq=256, tk=256)


def _local_step(x, tgt, w, t=None, grad_pack=None):
    t = dict(TILES, **(t or {}))
    gp = None if grad_pack is None else grad_pack[0]
    tm, ts, tq, tk = t["tm"], t["ts"], t["tq"], t["tk"]
    big = dict(tm=1024, tn=1024)
    saved = []
    xl, xl_affine, xl16 = x, None, x.astype(BF16)
    for l in range(DEPTH):
        n = f"l{l}_"
        wq, wr = w["w_in"][l][:, :D_QKV], w["w_in"][l][:, D_QKV:]
        qkv = _matmul(xl16, wq, name=n + "proj_qkv", tm=1024, tn=D_QKV, tk=1024, out_dtype=BF16)
        rest = _matmul(xl16, wr, name=n + "proj_rest", tk=1024, **big)
        attn, carry = _attn_fwd(qkv, name=n + "attn_fwd", tq=tq, tk=tk)
        cw8 = _rows8(w["conv_w"][l])
        pwbd = _block_diag(w["pool_w"][l]).astype(BF16)
        ps = w["pool_scale"][l].reshape(1, D_POOL)
        gain = w["mix_norm_g"][l].reshape(1, D_MODEL)
        mixn = _mixer_fwd(rest, attn, cw8, pwbd, ps, gain, name=n + "mixer_fwd", ts=ts)
        x1_16, xh1, rs1 = _matmul_ln(mixn, w["w_o"][l], xl, w["ln1_g"][l], w["ln1_b"][l], name=n + "wo_ln",
                                     tm=2 * tm, tk=1024, res_affine=xl_affine)
        hpre, hid = _matmul(x1_16, w["w_up"][l], name=n + "ffn_up", tk=1024, relu2_out=True, out_dtype=BF16, **big)
        x2_16, xh2, rs2 = _matmul_ln(hid, w["w_down"][l], xh1, w["ln2_g"][l], w["ln2_b"][l], name=n + "ffn_down_ln",
                                     tm=tm, tk=D_FF, res_affine=(w["ln1_g"][l], w["ln1_b"][l]))
        saved.append(dict(xin16=xl16, wq=wq, wr=wr, qkv=qkv, rest=rest, attn=attn, carry=carry, cw8=cw8, pwbd=pwbd,
                          ps=ps, gain=gain, mixn=mixn, x1_16=x1_16, xh1=xh1, rs1=rs1, hpre=hpre, hid=hid, xh2=xh2,
                          rs2=rs2))
        xl, xl_affine, xl16 = xh2, (w["ln2_g"][l], w["ln2_b"][l]), x2_16

    top = saved[-1]
    ln2_back = _loss_ln_bwd(tgt, top["xh2"], top["rs2"], w["ln2_g"][DEPTH - 1], w["ln2_b"][DEPTH - 1],
                            name="loss_ln2_bwd", tm=tm)
    lsum = ln2_back[4]
    grads = {k: [None] * DEPTH for k in
             ("w_in", "conv_w", "pool_w", "pool_scale", "mix_norm_g", "w_o", "ln1_g", "ln1_b", "w_up", "w_down",
              "ln2_g", "ln2_b")}
    dw = dict(tk=2048, ta=True, out_dtype=BF16, **big)
    for l in reversed(range(DEPTH)):
        n = f"l{l}_"
        s = saved[l]
        dr2, dr2_16, dg2, db2 = ln2_back[:4]
        dhpre = _matmul(dr2_16, w["w_down"][l], name=n + "ffn_down_dx", tk=1024, tb=True, out_dtype=BF16,
                        epi="drelu2", e=s["hpre"], **big)
        if gp is None:
            grads["w_down"][l] = _matmul(s["hid"], dr2_16, name=n + "ffn_down_dw", **dw)
        else:
            gp = _matmul(s["hid"], dr2_16, name=n + "ffn_down_dw", out_chips=_ChipWeight(gp, grad_pack[2] + l, "rows"),
                         **dw)
        dr1, dr1_16, dg1, db1 = _matmul(dhpre, w["w_up"][l], name=n + "ffn_up_dx_ln1_bwd", tm=512, tn=1024, tk=D_FF,
                                        tb=True, epi="add", e=dr2, e_scale=ALPHA,
                                        ln_bwd=(s["xh1"], s["rs1"], w["ln1_g"][l]))
        if gp is None:
            grads["w_up"][l] = _matmul(s["x1_16"], dhpre, name=n + "ffn_up_dw", **dw)
        else:
            gp = _matmul(s["x1_16"], dhpre, name=n + "ffn_up_dw", out_chips=_ChipWeight(gp, grad_pack[1] + l, "cols"),
                         **dw)
        dmixn = _matmul(dr1_16, w["w_o"][l], name=n + "wo_dx", tk=1024, tb=True, **big)
        grads["w_o"][l] = _matmul(s["mixn"], dr1_16, name=n + "wo_dw", **dw)
        d_attn, aux, dgain, dsc, dcw, dpw = _mixer_bwd1(dmixn, s["rest"], s["attn"], s["cw8"], s["pwbd"], s["ps"],
                                                        s["gain"], name=n + "mixer_bwd1", ts=ts)
        drest = _mixer_bwd2(aux, s["rest"], s["cw8"], name=n + "mixer_bwd2", ts=ts)
        dqkv = jnp.concatenate(_attn_bwd(s["qkv"], s["carry"], d_attn, name=n + "attn_bwd", tq=tq, tk=tk), axis=1)
        below = None if l == 0 else (saved[l - 1]["xh2"], saved[l - 1]["rs2"], w["ln2_g"][l - 1])
        ln2_back = _matmul(dqkv, s["wq"], name=n + "proj_dx", tm=512, tn=1024, tk=D_QKV, tb=True, also=(drest, s["wr"]),
                           epi="add", e=dr1, e_scale=ALPHA, ln_bwd=below)
        dy = ln2_back
        dwq = _matmul(s["xin16"], dqkv, name=n + "proj_qkv_dw", tm=1024, tn=D_QKV, tk=1024, ta=True, out_dtype=BF16)
        dwr = _matmul(s["xin16"], drest, name=n + "proj_rest_dw", **dw)
        grads["w_in"][l] = jnp.concatenate([dwq, dwr], axis=1)
        grads["ln2_g"][l] = dg2.sum(0)
        grads["ln2_b"][l] = db2.sum(0)
        grads["ln1_g"][l] = dg1.sum(0)
        grads["ln1_b"][l] = db1.sum(0)
        grads["mix_norm_g"][l] = dgain.sum(0)
        grads["pool_scale"][l] = dsc.sum(0)
        grads["conv_w"][l] = dcw.reshape(3, 8, D_CONV).sum(1)
        grads["pool_w"][l] = jnp.stack([dpw[64 * g:64 * g + 64, 64 * g:64 * g + 64] for g in range(4)])
    grads = {k: jnp.stack(v) for k, v in grads.items() if v[0] is not None}
    if gp is not None:
        grads["pack"] = gp
    return lsum, dy, grads


ANY = pl.BlockSpec(memory_space=pl.ANY)


def _place():
    x, y, c = lax.axis_index("x"), lax.axis_index("y"), lax.axis_index("c")
    chips = [(1 - x, y), (x, 1 - y), (1 - x, 1 - y)]
    return x, y, c, chips


def _remote(src, dst, send_sems, recv_sems, k, to):
    return pltpu.make_async_remote_copy(src_ref=src, dst_ref=dst, send_sem=send_sems.at[k], recv_sem=recv_sems.at[k],
                                        device_id=to, device_id_type=MESH)


class _Copy:
    def __init__(self, src, dst, send_sems, recv_sems, k, to):
        self.args = (send_sems, recv_sems, k, to)
        self.copy = _remote(src, dst, *self.args)

    def like(self, src, dst):
        return _remote(src, dst, *self.args)

    def start(self):
        self.copy.start()

    def wait(self):
        self.copy.wait()

    def wait_send(self):
        self.copy.wait_send()

    def wait_recv(self):
        self.copy.wait_recv()


def _allgather_chips(pack, *, name):
    R, C = pack.shape
    H = R // 2
    Q = H // 2
    assert R % 64 == 0
    A, B = 0, 1

    def body(p_ref, o_ref, send_sems, recv_sems):
        x, y, c, _ = _place()
        my, kx, ky, kd = 2 * x + y, 2 * (1 - x) + y, 2 * x + (1 - y), 2 * (1 - x) + (1 - y)
        xn, yn, sib = (1 - x, y, c), (x, 1 - y, c), (x, y, 1 - c)

        def own(ab):
            return p_ref.at[pl.ds(c * H + ab * Q, Q), :]

        def quarter(k, hc, ab):
            return o_ref.at[k, pl.ds(hc * H + ab * Q, Q), :]

        def send(src, k, ab, sem, to):
            cp = _Copy(src, quarter(k, c, ab), send_sems, recv_sems, sem, to)
            cp.start()
            return cp

        def landed(sent, k, hc, ab):
            sent.like(quarter(k, hc, ab), quarter(k, hc, ab)).wait_recv()

        a_x = send(own(A), my, A, 0, xn)
        b_y = send(own(B), my, B, 3, yn)
        b_x = send(own(B), my, B, 1, xn)
        a_y = send(own(A), my, A, 4, yn)
        landed(b_y, ky, c, B)
        fb = send(quarter(ky, c, B), ky, B, 2, xn)
        landed(a_x, kx, c, A)
        fa = send(quarter(kx, c, A), kx, A, 5, yn)
        arrivals = [(kx, A, None), (ky, B, None), (kx, B, b_x), (ky, A, a_y), (kd, B, fb), (kd, A, fa)]
        passed = []
        for j, (k, ab, sent) in enumerate(arrivals):
            if sent is not None:
                landed(sent, k, c, ab)
            passed.append(_Copy(quarter(k, c, ab), quarter(k, c, ab), send_sems, recv_sems, 6 + j, sib))
            passed[-1].start()
        for j, (k, ab, _) in enumerate(arrivals):
            landed(passed[j], k, 1 - c, ab)
        for cp in [a_x, b_y, b_x, a_y, fb, fa] + passed:
            cp.wait_send()

    return pl.pallas_call(
        body, name=name, in_specs=[ANY], out_specs=ANY,
        out_shape=jax.ShapeDtypeStruct((N_CHIPS, R, C), pack.dtype),
        scratch_shapes=[pltpu.SemaphoreType.DMA((12,)), pltpu.SemaphoreType.DMA((12,))],
    )(pack)


def _swap_halves(gp, *, name):
    K, R, C = gp.shape
    H = R // 2

    def body(g_ref, theirs_ref, send_sems, recv_sems):
        x, y, c, _ = _place()
        cp = _Copy(g_ref.at[:, pl.ds((1 - c) * H, H), :], theirs_ref, send_sems, recv_sems, 0, (x, y, 1 - c))
        cp.start()
        cp.wait()

    return pl.pallas_call(
        body, name=name, in_specs=[ANY], out_specs=ANY, out_shape=jax.ShapeDtypeStruct((K, H, C), gp.dtype),
        scratch_shapes=[pltpu.SemaphoreType.DMA((1,)), pltpu.SemaphoreType.DMA((1,))],
    )(gp)


def _scatter_chips(part, *, name):
    K, H, C = part.shape

    def body(p_ref, o_ref, send_sems, recv_sems):
        x, y, c, chips = _place()
        copies = [_Copy(p_ref.at[2 * cx + cy], o_ref.at[j], send_sems, recv_sems, j, (cx, cy, c))
                  for j, (cx, cy) in enumerate(chips)]
        for cp in copies:
            cp.start()
        for cp in copies:
            cp.wait()

    return pl.pallas_call(
        body, name=name, in_specs=[ANY], out_specs=ANY,
        out_shape=jax.ShapeDtypeStruct((3, H, C), part.dtype),
        scratch_shapes=[pltpu.SemaphoreType.DMA((3,)), pltpu.SemaphoreType.DMA((3,))],
    )(part)


def _join_halves(both, *, name):
    H, C = both.shape[0] // 2, both.shape[1]

    def body(in_ref, o_ref, send_sems, recv_sems):
        x, y, c, _ = _place()
        mine = pl.ds(c * H, H)
        theirs = pl.ds((1 - c) * H, H)
        cp = _Copy(in_ref.at[mine, :], o_ref.at[mine, :], send_sems, recv_sems, 0, (x, y, 1 - c))
        cp.start()
        cp.wait_send()
        cp.like(in_ref.at[theirs, :], o_ref.at[theirs, :]).wait_recv()

    return pl.pallas_call(
        body, name=name, in_specs=[ANY], out_specs=ANY, input_output_aliases={0: 0},
        out_shape=jax.ShapeDtypeStruct(both.shape, both.dtype),
        scratch_shapes=[pltpu.SemaphoreType.DMA((1,)), pltpu.SemaphoreType.DMA((1,))],
    )(both)


def _allreduce_small(v, *, name):
    R, C = v.shape
    n_dev = 8

    def body(v_ref, o_ref, gat, send_sems, recv_sems):
        x, y, c, chips = _place()
        sib = (x, y, 1 - c)

        def rows(px, py, pc):
            return gat.at[4 * px + 2 * py + pc]

        gat[4 * x + 2 * y + c] = v_ref[...]
        first = [_remote(v_ref, rows(x, y, c), send_sems, recv_sems, 0, sib)]
        first += [_remote(v_ref, rows(x, y, c), send_sems, recv_sems, 1 + j, (cx, cy, c))
                  for j, (cx, cy) in enumerate(chips)]
        for cp in first:
            cp.start()
        passed = []
        for j, (cx, cy) in enumerate(chips):
            _remote(v_ref, rows(cx, cy, c), send_sems, recv_sems, 1 + j, sib).wait_recv()
            fwd = _remote(rows(cx, cy, c), rows(cx, cy, c), send_sems, recv_sems, 4 + j, sib)
            fwd.start()
            passed.append(fwd)
        _remote(v_ref, rows(x, y, 1 - c), send_sems, recv_sems, 0, sib).wait_recv()
        for j, (cx, cy) in enumerate(chips):
            _remote(v_ref, rows(cx, cy, 1 - c), send_sems, recv_sems, 4 + j, sib).wait_recv()
        for cp in first + passed:
            cp.wait_send()
        acc = gat[0]
        for d in range(1, n_dev):
            acc = acc + gat[d]
        o_ref[...] = acc

    vm = pl.BlockSpec(memory_space=pltpu.VMEM)
    return pl.pallas_call(
        body, name=name, in_specs=[vm], out_specs=vm,
        out_shape=jax.ShapeDtypeStruct((R, C), F32),
        scratch_shapes=[pltpu.VMEM((n_dev, R, C), F32), pltpu.SemaphoreType.DMA((7,)), pltpu.SemaphoreType.DMA((7,))],
    )(v)


def _add_pairs(gp, theirs, place, *, name, tr):
    K, H, C = theirs.shape
    tr = min(tr, H)
    assert H % tr == 0
    nb = H // tr

    def body(place_ref, a_ref, b_ref, o_ref):
        o_ref[...] = (a_ref[...].astype(F32) + b_ref[...].astype(F32)).astype(BF16)

    blk = pl.BlockSpec((1, tr, C), lambda k, i, p: (k, i, 0))
    mine = pl.BlockSpec((1, tr, C), lambda k, i, p: (k, i + p[1] * nb, 0))
    return pl.pallas_call(
        body, name=name,
        grid_spec=pltpu.PrefetchScalarGridSpec(num_scalar_prefetch=1, grid=(K, nb), in_specs=[mine, blk],
                                               out_specs=blk),
        out_shape=jax.ShapeDtypeStruct((K, H, C), BF16),
        compiler_params=_params(("parallel", "parallel"), 3 * _nbytes((tr, C), BF16)),
    )(place, gp, theirs)


def _add_final(gp, theirs, others, place, *, name, tr):
    K, H, C = theirs.shape
    tr = min(tr, H)
    assert H % tr == 0
    nb = H // tr

    def body(place_ref, a_ref, b_ref, o_ref_in, out_ref):
        acc = a_ref[0].astype(F32) + b_ref[0].astype(F32)
        for j in range(3):
            acc = acc + o_ref_in[j].astype(F32)
        out_ref[...] = acc

    return pl.pallas_call(
        body, name=name,
        grid_spec=pltpu.PrefetchScalarGridSpec(
            num_scalar_prefetch=1, grid=(nb,),
            in_specs=[pl.BlockSpec((1, tr, C), lambda i, p: (p[0], i + p[1] * nb, 0)),
                      pl.BlockSpec((1, tr, C), lambda i, p: (p[0], i, 0)),
                      pl.BlockSpec((3, tr, C), lambda i, p: (0, i, 0))],
            out_specs=pl.BlockSpec((tr, C), lambda i, p: (i + p[1] * nb, 0))),
        out_shape=jax.ShapeDtypeStruct((2 * H, C), F32),
        compiler_params=_params(("parallel",), 6 * _nbytes((tr, C), F32)),
    )(place, gp, theirs, others)


def _adamw(w, g, m, v, *, name, tr, row0=0):
    R, C = w.shape
    tr = min(tr, R)
    assert R % tr == 0 and row0 % tr == 0
    off = row0 // tr

    def body(w_ref, g_ref, m_ref, v_ref, go_ref, d_ref, mo_ref, vo_ref):
        gv = g_ref[...]
        m2 = ADAM_B1 * m_ref[...] + (1.0 - ADAM_B1) * gv
        v2 = ADAM_B2 * v_ref[...] + (1.0 - ADAM_B2) * jnp.square(gv)
        m_hat = m2 / (1.0 - ADAM_B1 ** ADAM_STEP)
        v_hat = v2 / (1.0 - ADAM_B2 ** ADAM_STEP)
        d_ref[...] = -ADAM_LR * (m_hat / (jnp.sqrt(v_hat) + ADAM_EPS) + ADAM_WD * w_ref[...])
        go_ref[...] = gv
        mo_ref[...] = m2
        vo_ref[...] = v2

    blk = pl.BlockSpec((tr, C), lambda i: (i, 0))
    shape = jax.ShapeDtypeStruct((R, C), F32)
    return pl.pallas_call(
        body, name=name, grid=(R // tr,),
        in_specs=[blk, pl.BlockSpec((tr, C), lambda i: (i + off, 0)), blk, blk], out_specs=[blk] * 4,
        out_shape=[shape] * 4,
        compiler_params=_params(("parallel",), 8 * _nbytes((tr, C), F32)),
    )(w, g, m, v)


BIG = ("w_up", "w_down", "w_in", "w_o")
IN_PLACE = ("w_up", "w_down")
BIG_AXIS = dict(w_in=2, w_o=1, w_up=2, w_down=1)
SMALL = ("pool_w", "pool_scale", "mix_norm_g", "ln1_g", "ln1_b", "ln2_g", "ln2_b")
CONV_ROWS = 64
SMALL_ROWS = 48


def _big_rows(shards):
    sizes = [shards[n].size // D_MODEL for n in BIG]
    starts = [sum(sizes[:i]) for i in range(len(sizes))]
    return sizes, starts


def _to_chips(a, axis):
    shape = list(a.shape)
    shape[axis:axis + 1] = [N_CHIPS, shape[axis] // N_CHIPS]
    return jnp.moveaxis(a.reshape(shape), axis, 0)


def _from_chips(a, axis):
    a = jnp.moveaxis(a, 0, axis)
    shape = list(a.shape)
    shape[axis:axis + 2] = [shape[axis] * shape[axis + 1]]
    return a.reshape(shape)


def _pad_rows(flat, rows):
    return jnp.pad(flat, (0, rows * D_MODEL - flat.shape[0])).reshape(rows, D_MODEL)


def kernel(x, w_in, conv_w, pool_w, pool_scale, mix_norm_g, w_o, ln1_g, ln1_b, w_up, w_down, ln2_g, ln2_b, loss_target, m_w_in, m_conv_w, m_pool_w, m_pool_scale, m_mix_norm_g, m_w_o, m_ln1_g, m_ln1_b, m_w_up, m_w_down, m_ln2_g, m_ln2_b, v_w_in, v_conv_w, v_pool_w, v_pool_scale, v_mix_norm_g, v_w_o, v_ln1_g, v_ln1_b, v_w_up, v_w_down, v_ln2_g, v_ln2_b):
    wts = dict(w_in=w_in, conv_w=conv_w, pool_w=pool_w, pool_scale=pool_scale, mix_norm_g=mix_norm_g, w_o=w_o,
               ln1_g=ln1_g, ln1_b=ln1_b, w_up=w_up, w_down=w_down, ln2_g=ln2_g, ln2_b=ln2_b)
    mom = dict(w_in=m_w_in, conv_w=m_conv_w, pool_w=m_pool_w, pool_scale=m_pool_scale, mix_norm_g=m_mix_norm_g,
               w_o=m_w_o, ln1_g=m_ln1_g, ln1_b=m_ln1_b, w_up=m_w_up, w_down=m_w_down, ln2_g=m_ln2_g, ln2_b=m_ln2_b)
    var = dict(w_in=v_w_in, conv_w=v_conv_w, pool_w=v_pool_w, pool_scale=v_pool_scale, mix_norm_g=v_mix_norm_g,
               w_o=v_w_o, ln1_g=v_ln1_g, ln1_b=v_ln1_b, w_up=v_w_up, w_down=v_w_down, ln2_g=v_ln2_g, ln2_b=v_ln2_b)
    chip = 2 * lax.axis_index("x") + lax.axis_index("y")
    sizes, starts = _big_rows(wts)
    big_rows = sum(sizes)

    conv_bits = lax.bitcast_convert_type(conv_w.reshape(-1), BF16).reshape(-1)
    pack = jnp.concatenate([wts[n].reshape(-1, D_MODEL).astype(BF16) for n in BIG]
                           + [_pad_rows(conv_bits, CONV_ROWS)], axis=0)
    gathered = _allgather_chips(pack, name="gather_weights")
    gathered = lax.dynamic_update_index_in_dim(gathered, pack, chip, 0)
    full = {}
    first_block = {}
    for n, size, start in zip(BIG, sizes, starts):
        if n in IN_PLACE:
            assert start % CHIP_BLOCK == 0 and size == DEPTH * CHIP_BLOCK
            first_block[n] = start // CHIP_BLOCK
            along = "cols" if BIG_AXIS[n] == 2 else "rows"
            full[n] = [_ChipWeight(gathered, first_block[n] + l, along) for l in range(DEPTH)]
        else:
            full[n] = _from_chips(gathered[:, start:start + size].reshape((N_CHIPS,) + wts[n].shape), BIG_AXIS[n])
    conv_parts = [lax.bitcast_convert_type(gathered[k, big_rows:].reshape(-1)[:2 * conv_w.size].reshape(-1, 2), F32)
                  .reshape(conv_w.shape) for k in range(N_CHIPS)]
    full["conv_w"] = jnp.concatenate(conv_parts, axis=2)
    for n in SMALL:
        full[n] = wts[n]

    gpack = jnp.zeros((N_CHIPS, big_rows, D_MODEL), BF16)
    lsum, grad_x, grads = _local_step(x[0], loss_target[0], full,
                                      grad_pack=(gpack, first_block["w_up"], first_block["w_down"]))

    others = [n for n in BIG if n not in IN_PLACE]
    rest = jnp.concatenate([_to_chips(grads[n], BIG_AXIS[n]).reshape(N_CHIPS, -1, D_MODEL) for n in others], axis=1)
    gpack = lax.dynamic_update_slice_in_dim(grads["pack"], rest.astype(BF16), starts[len(IN_PLACE)], axis=1)
    place = jnp.stack([chip, lax.axis_index("c")]).astype(jnp.int32)
    theirs = _swap_halves(gpack, name="grad_swap_cores")
    add_rows = big_rows // 8
    chip_sum = _add_pairs(gpack, theirs, place, name="grad_add_cores", tr=add_rows)
    from_chips = _scatter_chips(chip_sum, name="grad_scatter_chips")
    half_sum = _add_final(gpack, theirs, from_chips, place, name="grad_add_chips", tr=add_rows)
    gsum = _join_halves(half_sum, name="grad_join_cores")

    small_flat = jnp.concatenate([grads[n].reshape(-1) for n in SMALL] + [grads["conv_w"].reshape(-1),
                                                                          lsum.sum().reshape(1)])
    small_sum = _allreduce_small(_pad_rows(small_flat, SMALL_ROWS), name="allreduce_small").reshape(-1)
    gsmall = {}
    pos = 0
    for n in SMALL:
        gsmall[n] = small_sum[pos:pos + wts[n].size].reshape(wts[n].shape)
        pos += wts[n].size
    conv_full = small_sum[pos:pos + 4 * conv_w.size].reshape(DEPTH, 3, D_CONV)
    pos += 4 * conv_w.size
    loss = small_sum[pos]
    gsmall["conv_w"] = lax.dynamic_slice_in_dim(conv_full, chip * conv_w.shape[2], conv_w.shape[2], axis=2)

    out_g, out_d, out_m, out_v = {}, {}, {}, {}
    for n, size, start in zip(BIG, sizes, starts):
        shp = wts[n].shape
        g, row0 = gsum, start
        if shp[-1] != D_MODEL:
            g, row0 = gsum[start:start + size].reshape(-1, shp[-1]), 0
        res = _adamw(wts[n].reshape(-1, shp[-1]), g, mom[n].reshape(-1, shp[-1]), var[n].reshape(-1, shp[-1]),
                     name="adamw_" + n, tr=256, row0=row0)
        out_g[n], out_d[n], out_m[n], out_v[n] = [r.reshape(shp) for r in res]
    small_names = SMALL + ("conv_w",)
    packs = [_pad_rows(jnp.concatenate([d[n].reshape(-1) for n in small_names]), SMALL_ROWS)
             for d in (wts, gsmall, mom, var)]
    res = _adamw(*packs, name="adamw_small", tr=SMALL_ROWS)
    pos = 0
    for n in small_names:
        shp = wts[n].shape
        out_g[n], out_d[n], out_m[n], out_v[n] = [r.reshape(-1)[pos:pos + wts[n].size].reshape(shp) for r in res]
        pos += wts[n].size

    order = ("w_in", "conv_w", "pool_w", "pool_scale", "mix_norm_g", "w_o", "ln1_g", "ln1_b", "w_up", "w_down",
             "ln2_g", "ln2_b")
    return (loss, grad_x[None], *[out_g[n] for n in order], *[out_d[n] for n in order],
            *[out_m[n] for n in order], *[out_v[n] for n in order])
```

```python
import math
from typing import NamedTuple

import jax
import jax.numpy as jnp
from jax import lax
from jax.experimental import pallas as pl
from jax.experimental.pallas import tpu as pltpu

F32 = jnp.float32
BF16 = jnp.bfloat16
MESH = pl.DeviceIdType.MESH

D_MODEL = 1024
DEPTH = 2
HEAD_DIM = 64
D_SB = 512
D_CONV = 256
D_POOL = 256
D_QKV = 3 * D_SB
D_REST = 3 * D_CONV + D_POOL
D_FF = 4 * D_MODEL
ALPHA = (2 * DEPTH) ** 0.25
LN_EPS = 1e-5
RMS_EPS = 1e-6
SCALE = HEAD_DIM ** -0.5
N_CHIPS = 4
HALO = 16

ADAM_LR = 0.001
ADAM_B1 = 0.9
ADAM_B2 = 0.999
ADAM_EPS = 1e-08
ADAM_WD = 0.01
ADAM_STEP = 10

VMEM_V7X_BYTES = 64 * 1024 * 1024
VMEM_CAP_BYTES = VMEM_V7X_BYTES - 8 * 1024 * 1024


def _params(sem, block_bytes):
    limit = min(VMEM_CAP_BYTES, max(32 * 1024 * 1024, 3 * block_bytes))
    return pltpu.CompilerParams(dimension_semantics=sem, vmem_limit_bytes=limit)


def _nbytes(shape, dtype):
    return math.prod(shape) * jnp.dtype(dtype).itemsize


def _dot(a, b, dims=(((1,), (0,)), ((), ()))):
    return lax.dot_general(a, b, dims, preferred_element_type=F32)


NT = (((1,), (1,)), ((), ()))
TN = (((0,), (0,)), ((), ()))


def _split(x):
    hi = x.astype(BF16)
    lo = (x - hi.astype(F32)).astype(BF16)
    return hi, lo


def _sum8(x):
    r, c = x.shape
    return x.reshape(r // 8, 8, c).sum(axis=0)


def _ln_bwd_rows(dy, xh_ref, rs_ref, g_ref, dr_ref, dr16_ref, dg_ref, db_ref, first):
    xh = xh_ref[...]
    dxh = dy * g_ref[...]
    m1 = jnp.mean(dxh, axis=-1, keepdims=True)
    m2 = jnp.mean(dxh * xh, axis=-1, keepdims=True)
    dr = rs_ref[...] * (dxh - m1 - xh * m2)
    dr_ref[...] = dr
    dr16_ref[...] = dr.astype(BF16)
    pg = _sum8(dy * xh)
    pb = _sum8(dy)

    @pl.when(first)
    def _():
        dg_ref[...] = pg
        db_ref[...] = pb

    @pl.when(jnp.logical_not(first))
    def _():
        dg_ref[...] += pg
        db_ref[...] += pb


class _ChipWeight(NamedTuple):
    arr: jax.Array
    rb: int
    along: str


CHIP_BLOCK = 1024


def _matmul(a, b, *, name, tm, tn, tk, ta=False, tb=False, out_dtype=F32,
            epi=None, e=None, e_scale=1.0, relu2_out=False, out_chips=None, ln_bwd=None, also=None):
    M, K = (a.shape[1], a.shape[0]) if ta else a.shape
    chips = isinstance(b, _ChipWeight)
    split_k = chips and ((b.along == "cols") == tb)
    if chips:
        N = CHIP_BLOCK if split_k else N_CHIPS * CHIP_BLOCK
        assert K == (N_CHIPS * CHIP_BLOCK if split_k else CHIP_BLOCK) and not ta, (name, K)
        tn, tk = CHIP_BLOCK, K
    else:
        N = b.shape[0] if tb else b.shape[1]
    tm, tn, tk = min(tm, M), min(tn, N), min(tk, K)
    assert M % tm == 0 and N % tn == 0 and K % tk == 0, (name, M, N, K)
    nk = K // tk
    dims = (((0 if ta else 1,), (1 if tb else 0,)), ((), ()))
    i_also = 2 + (epi is not None) + (out_chips is not None)
    n_in = i_also + (2 if also is not None else 0) + (3 if ln_bwd is not None else 0)
    assert also is None or (nk == 1 and not ta)
    assert ln_bwd is None or (tn == N and out_chips is None and not relu2_out)

    def body(*refs):
        a_ref, b_ref = refs[0], refs[1]
        e_ref = refs[2] if epi is not None else None
        o_ref = refs[n_in]
        scr = refs[-1:]
        if not chips:
            p = _dot(a_ref[...].astype(BF16), b_ref[...].astype(BF16), dims)
        elif split_k:
            p = _dot(a_ref[:, 0:CHIP_BLOCK].astype(BF16), b_ref[0], dims)
            for c in range(1, N_CHIPS):
                p = p + _dot(a_ref[:, c * CHIP_BLOCK:(c + 1) * CHIP_BLOCK].astype(BF16), b_ref[c], dims)
        else:
            p = _dot(a_ref[...].astype(BF16), b_ref[0], dims)
        if also is not None:
            p = p + _dot(refs[i_also][...].astype(BF16), refs[i_also + 1][...].astype(BF16), dims)

        def finish(acc):
            if epi == "drelu2":
                acc = acc * (2.0 * jnp.maximum(e_ref[...].astype(F32), 0.0))
            elif epi == "add":
                acc = acc + e_scale * e_ref[...]
            if ln_bwd is not None:
                _ln_bwd_rows(acc, *refs[n_in - 3:n_in + 4], pl.program_id(0) == 0)
            elif out_chips is None:
                o_ref[...] = acc.astype(out_dtype)
            else:
                o_ref[0] = acc.astype(out_dtype)
            if relu2_out:
                refs[n_in + 1][...] = jnp.square(jnp.maximum(acc, 0.0)).astype(BF16)

        if nk == 1:
            finish(p)
        else:
            acc_ref = scr[0]
            k = pl.program_id(2)

            @pl.when(k == 0)
            def _():
                acc_ref[...] = p

            @pl.when(k > 0)
            def _():
                acc_ref[...] += p

            @pl.when(k == nk - 1)
            def _():
                finish(acc_ref[...])

    a_spec = pl.BlockSpec((tk, tm), lambda i, j, k: (k, i)) if ta else pl.BlockSpec((tm, tk), lambda i, j, k: (i, k))
    if chips:
        b_arr, rb = b.arr, b.rb
        nblk = N_CHIPS if split_k else 1
        b_spec = pl.BlockSpec((nblk, CHIP_BLOCK, CHIP_BLOCK),
                              (lambda i, j, k: (0, rb, 0)) if split_k else (lambda i, j, k: (j, rb, 0)))
    else:
        b_arr = b
        b_spec = pl.BlockSpec((tn, tk), lambda i, j, k: (j, k)) if tb else pl.BlockSpec((tk, tn), lambda i, j, k: (k, j))
    o_spec = pl.BlockSpec((tm, tn), lambda i, j, k: (i, j))
    in_specs = [a_spec, b_spec]
    args = [a, b_arr]
    nbytes = _nbytes((tm, tk), a.dtype) + _nbytes((tk, tn), b_arr.dtype) + 2 * _nbytes((tm, tn), F32)
    if epi is not None:
        in_specs.append(o_spec)
        args.append(e)
        nbytes += _nbytes((tm, tn), e.dtype)
    scratch = [pltpu.VMEM((tm, tn), F32)] if nk > 1 else []
    out_shape = [jax.ShapeDtypeStruct((M, N), out_dtype)]
    out_specs = [o_spec]
    aliases = {}
    if out_chips is not None:
        assert tm == tn == CHIP_BLOCK and not relu2_out and out_chips.arr.dtype == out_dtype
        orb = out_chips.rb
        out_specs = [pl.BlockSpec((1, CHIP_BLOCK, CHIP_BLOCK),
                                  (lambda i, j, k: (j, orb, 0)) if out_chips.along == "cols" else
                                  (lambda i, j, k: (i, orb, 0)))]
        out_shape = [jax.ShapeDtypeStruct(out_chips.arr.shape, out_dtype)]
        in_specs.append(pl.BlockSpec(memory_space=pl.ANY))
        args.append(out_chips.arr)
        aliases = {len(args) - 1: 0}
    if relu2_out:
        out_shape.append(jax.ShapeDtypeStruct((M, N), BF16))
        out_specs.append(o_spec)
        nbytes += _nbytes((tm, tn), BF16)
    if also is not None:
        a2, b2 = also
        k2 = a2.shape[1]
        in_specs += [pl.BlockSpec((tm, k2), lambda i, j, k: (i, 0)),
                     pl.BlockSpec((tn, k2), lambda i, j, k: (j, 0)) if tb else pl.BlockSpec((k2, tn), lambda i, j, k: (0, j))]
        args += [a2, b2]
        nbytes += _nbytes((tm, k2), a2.dtype) + _nbytes((k2, tn), b2.dtype)
    sem = ("parallel", "parallel", "arbitrary")
    if ln_bwd is not None:
        xhat, rstd, gain = ln_bwd
        in_specs += [o_spec, pl.BlockSpec((tm, 1), lambda i, j, k: (i, 0)), pl.BlockSpec((1, N), lambda i, j, k: (0, 0))]
        args += [xhat, rstd, gain.reshape(1, N)]
        acc_spec = pl.BlockSpec((8, N), lambda i, j, k: (0, 0))
        out_specs = [o_spec, o_spec, acc_spec, acc_spec]
        out_shape = [jax.ShapeDtypeStruct((M, N), F32), jax.ShapeDtypeStruct((M, N), BF16),
                     jax.ShapeDtypeStruct((8, N), F32), jax.ShapeDtypeStruct((8, N), F32)]
        nbytes += 3 * _nbytes((tm, tn), F32)
        sem = ("arbitrary", "arbitrary", "arbitrary")
    res = pl.pallas_call(
        body, name=name,
        grid=(M // tm, N // tn, nk),
        in_specs=in_specs, out_specs=out_specs,
        out_shape=out_shape,
        scratch_shapes=scratch,
        input_output_aliases=aliases,
        compiler_params=_params(sem, nbytes),
    )(*args)
    return res if (relu2_out or ln_bwd is not None) else res[0]


def _matmul_ln(a, b, xres, g, bias, *, name, tm, tk, res_affine=None):
    M, K = a.shape
    chips = isinstance(b, _ChipWeight)
    if chips:
        assert b.along == "rows" and K == N_CHIPS * CHIP_BLOCK
        N, tk = CHIP_BLOCK, K
    else:
        N = b.shape[1]
    tm, tk = min(tm, M), min(tk, K)
    assert M % tm == 0 and K % tk == 0 and N == D_MODEL
    nk = K // tk

    n_vec = 2 if res_affine is None else 4

    def body(*refs):
        a_ref, b_ref, x_ref, g_ref, bias_ref = refs[:5]
        y16_ref, xh_ref, rs_ref = refs[3 + n_vec:6 + n_vec]
        scr = refs[6 + n_vec:]
        if chips:
            p = _dot(a_ref[:, 0:CHIP_BLOCK].astype(BF16), b_ref[0])
            for c in range(1, N_CHIPS):
                p = p + _dot(a_ref[:, c * CHIP_BLOCK:(c + 1) * CHIP_BLOCK].astype(BF16), b_ref[c])
        else:
            p = _dot(a_ref[...].astype(BF16), b_ref[...].astype(BF16))

        def finish(acc):
            xv = x_ref[...]
            if res_affine is not None:
                xv = xv * refs[5][...] + refs[6][...]
            r = ALPHA * xv + acc
            mu = jnp.mean(r, axis=-1, keepdims=True)
            xc = r - mu
            var = jnp.mean(xc * xc, axis=-1, keepdims=True)
            rstd = lax.rsqrt(var + LN_EPS)
            xh = xc * rstd
            y16_ref[...] = (xh * g_ref[...] + bias_ref[...]).astype(BF16)
            xh_ref[...] = xh
            rs_ref[...] = rstd

        if nk == 1:
            finish(p)
        else:
            acc_ref = scr[0]
            k = pl.program_id(1)

            @pl.when(k == 0)
            def _():
                acc_ref[...] = p

            @pl.when(k > 0)
            def _():
                acc_ref[...] += p

            @pl.when(k == nk - 1)
            def _():
                finish(acc_ref[...])

    row = pl.BlockSpec((tm, N), lambda i, k: (i, 0))
    vec = pl.BlockSpec((1, N), lambda i, k: (0, 0))
    if chips:
        b_arr, rb = b.arr, b.rb
        b_spec = pl.BlockSpec((N_CHIPS, CHIP_BLOCK, CHIP_BLOCK), lambda i, k: (0, rb, 0))
    else:
        b_arr = b
        b_spec = pl.BlockSpec((tk, N), lambda i, k: (k, 0))
    nbytes = _nbytes((tm, tk), a.dtype) + _nbytes((tk, N), b_arr.dtype) + 6 * _nbytes((tm, N), F32)
    scratch = [pltpu.VMEM((tm, N), F32)] if nk > 1 else []
    return pl.pallas_call(
        body, name=name,
        grid=(M // tm, nk),
        in_specs=[pl.BlockSpec((tm, tk), lambda i, k: (i, k)), b_spec, row] + [vec] * n_vec,
        out_specs=[row, row, pl.BlockSpec((tm, 1), lambda i, k: (i, 0))],
        out_shape=[jax.ShapeDtypeStruct((M, N), BF16), jax.ShapeDtypeStruct((M, N), F32),
                   jax.ShapeDtypeStruct((M, 1), F32)],
        scratch_shapes=scratch,
        compiler_params=_params(("parallel", "arbitrary"), nbytes),
    )(a, b_arr, xres, g.reshape(1, N), bias.reshape(1, N), *[v.reshape(1, N) for v in (res_affine or ())])


def _loss_ln_bwd(tgt, xhat, rstd, g, bias, *, name, tm):
    M, N = tgt.shape
    tm = min(tm, M)

    def body(t_ref, xh_ref, rs_ref, g_ref, bias_ref, dr_ref, dr16_ref, dg_ref, db_ref, l_ref):
        first = pl.program_id(0) == 0
        d = (xh_ref[...] * g_ref[...] + bias_ref[...]) - t_ref[...]
        part = _sum8(d * d) * (0.5 / N)

        @pl.when(first)
        def _():
            l_ref[...] = part

        @pl.when(jnp.logical_not(first))
        def _():
            l_ref[...] += part

        _ln_bwd_rows(d * (1.0 / N), xh_ref, rs_ref, g_ref, dr_ref, dr16_ref, dg_ref, db_ref, first)

    row = pl.BlockSpec((tm, N), lambda i: (i, 0))
    acc = pl.BlockSpec((8, N), lambda i: (0, 0))
    return pl.pallas_call(
        body, name=name, grid=(M // tm,),
        in_specs=[row, row, pl.BlockSpec((tm, 1), lambda i: (i, 0))] + [pl.BlockSpec((1, N), lambda i: (0, 0))] * 2,
        out_specs=[row, row, acc, acc, acc],
        out_shape=[jax.ShapeDtypeStruct((M, N), F32), jax.ShapeDtypeStruct((M, N), BF16)]
        + [jax.ShapeDtypeStruct((8, N), F32)] * 3,
        compiler_params=_params(("arbitrary",), 6 * _nbytes((tm, N), F32)),
    )(tgt, xhat, rstd, g.reshape(1, N), bias.reshape(1, N))


def _tri(n, kind):
    j = lax.broadcasted_iota(jnp.int32, (2 * n, n), 0) % n
    s = lax.broadcasted_iota(jnp.int32, (2 * n, n), 1)
    return ((j > s) if kind == "after" else (j < s)).astype(BF16)


LOG2E = 1.4426950408889634
DEAD = -104.0
NOT_VISITED = -1e30


def _log_terms(z):
    lse = jnp.log(1.0 + jnp.exp2(jnp.abs(z) * (-LOG2E)))
    logsig = jnp.minimum(z, 0.0) - lse
    return logsig, logsig - z


def _cumsum_mm(x, u2_ref):
    hi, lo = _split(x)
    return _dot(jnp.concatenate([hi, lo], axis=1), u2_ref[...])


def _head_rows(x2, scale):
    lane = lax.broadcasted_iota(jnp.int32, (1, 128), 1)
    zero = jnp.zeros_like(x2)
    both = jnp.concatenate([jnp.where(lane < HEAD_DIM, x2, zero), jnp.where(lane >= HEAD_DIM, x2, zero)], axis=0)
    return both * scale


def _causal_mask(i, ks, tq, tk):
    row = lax.broadcasted_iota(jnp.int32, (2 * tq, tk), 0)
    row = i * tq + jnp.where(row >= tq, row - tq, row)
    col = lax.broadcasted_iota(jnp.int32, (2 * tq, tk), 1)
    return (ks + col) < row


def _attn_fwd(qkv, *, name, tq, tk):
    S = qkv.shape[0]
    tq = tk = min(tq, tk, S)
    assert S % tq == 0 and S // tk <= 128
    tri = _tri(tk, "after")

    def body(q_ref, k_ref, v_ref, u_ref, o_ref, c_ref, qcat, oacc, cacc, call, ls_buf, tl_buf, l0_buf):
        i = pl.program_id(1)
        lane = lax.broadcasted_iota(jnp.int32, (1, 128), 1)
        qcat[...] = _head_rows(q_ref[...], SCALE)
        oacc[...] = jnp.zeros_like(oacc)
        cacc[...] = jnp.zeros_like(cacc)
        call[...] = jnp.full_like(call, NOT_VISITED)

        def scores(kb, masked, slot):
            ks = pl.multiple_of(jnp.maximum(kb, 0) * tk, tk)
            z = _dot(qcat[...], k_ref[pl.ds(ks, tk), :], NT)
            logsig, lom = _log_terms(z)
            if masked:
                msk = jnp.logical_and(_causal_mask(i, ks, tq, tk), kb >= 0)
                lom = jnp.where(msk, lom, 0.0)
                logsig = jnp.where(msk, logsig, -1e30)
            ls_buf[slot] = logsig
            tl_buf[slot] = _cumsum_mm(lom, u_ref)
            l0_buf[slot] = lom[:, 0:1]

        def weights(kb, slot):
            ks = pl.multiple_of(jnp.maximum(kb, 0) * tk, tk)
            tl = tl_buf[slot]
            c = cacc[...]
            call[...] = jnp.where(lane == kb, c, call[...])
            a = jnp.exp(ls_buf[slot] + tl + c).astype(BF16)
            oacc[...] += _dot(a, v_ref[pl.ds(ks, tk), :])
            cacc[...] = c + tl[:, 0:1] + l0_buf[slot]

        def pair(kb, masked, masked_next):
            scores(kb, masked, 0)
            scores(kb - 1, masked_next, 1)
            weights(kb, 0)
            weights(kb - 1, 1)

        pair(i, True, True)

        def live(state):
            t, cmax = state
            return jnp.logical_and(t < (i - 1) // 2, cmax > DEAD)

        def trip(state):
            t, _ = state
            pair(i - 2 - 2 * t, False, False)
            return t + 1, jnp.max(cacc[...])

        t_end, cmax = lax.while_loop(live, trip, (0, jnp.max(cacc[...])))

        left_over = jnp.logical_and(i >= 2, i % 2 == 0)
        still_live = jnp.logical_and(t_end == (i - 1) // 2, cmax > DEAD)

        @pl.when(jnp.logical_and(left_over, still_live))
        def _():
            pair(0, False, True)

        o_ref[...] = jnp.where(lane < HEAD_DIM, oacc[0:tq], oacc[tq:2 * tq])
        c_ref[...] = jnp.concatenate([call[0:tq], call[tq:2 * tq]], axis=1)

    nbytes = (_nbytes((tq, 128), BF16) + 2 * _nbytes((S, 128), BF16) + _nbytes((2 * tk, tk), BF16)
              + 8 * _nbytes((tq, 128), F32) + 14 * _nbytes((2 * tq, tk), F32))
    return pl.pallas_call(
        body, name=name, grid=(4, S // tq),
        in_specs=[pl.BlockSpec((tq, 128), lambda j, i: (i, j)),
                  pl.BlockSpec((S, 128), lambda j, i: (0, 4 + j)),
                  pl.BlockSpec((S, 128), lambda j, i: (0, 8 + j)),
                  pl.BlockSpec((2 * tk, tk), lambda j, i: (0, 0))],
        out_specs=[pl.BlockSpec((tq, 128), lambda j, i: (i, j)),
                   pl.BlockSpec((tq, 256), lambda j, i: (i, j))],
        out_shape=[jax.ShapeDtypeStruct((S, D_SB), F32), jax.ShapeDtypeStruct((S, 1024), F32)],
        scratch_shapes=[pltpu.VMEM((2 * tq, 128), BF16), pltpu.VMEM((2 * tq, 128), F32),
                        pltpu.VMEM((2 * tq, 1), F32), pltpu.VMEM((2 * tq, 128), F32),
                        pltpu.VMEM((2, 2 * tq, tk), F32), pltpu.VMEM((2, 2 * tq, tk), F32),
                        pltpu.VMEM((2, 2 * tq, 1), F32)],
        compiler_params=_params(("parallel", "arbitrary"), nbytes),
    )(qkv, qkv, qkv, tri)


def _attn_bwd(qkv, carry, do, *, name, tq, tk):
    S = qkv.shape[0]
    tq = tk = min(tq, tk, S)
    assert S % tq == 0 and S // tk <= 128
    nkb = S // tk
    nq = S // tq
    tri_after = _tri(tk, "after")
    tri_before = _tri(tk, "before")

    def body(q_ref, k_ref, v_ref, c_ref, do_ref, ua_ref, ub_ref, dq_ref, dk_ref, dv_ref,
             qcat, docat, qcat_t, docat_t, ccat, dqacc, pacc, dkt, dvt, ls_buf, tl_buf, da_buf):
        i = pl.program_id(1)
        lane = lax.broadcasted_iota(jnp.int32, (1, 128), 1)
        sub = lax.broadcasted_iota(jnp.int32, (128, 1), 0)
        q2 = q_ref[...]
        do2 = do_ref[...]
        qcat[...] = _head_rows(q2, SCALE)
        docat[...] = _head_rows(do2, 1.0).astype(BF16)
        qt = q2.astype(F32).T * SCALE
        dot_ = do2.T
        qcat_t[...] = jnp.concatenate([jnp.where(sub < HEAD_DIM, qt, 0.0), jnp.where(sub >= HEAD_DIM, qt, 0.0)],
                                      axis=1).astype(BF16)
        docat_t[...] = jnp.concatenate([jnp.where(sub < HEAD_DIM, dot_, 0.0), jnp.where(sub >= HEAD_DIM, dot_, 0.0)],
                                       axis=1).astype(BF16)
        ccat[0:tq] = c_ref[:, 0:128]
        ccat[tq:2 * tq] = c_ref[:, 128:256]

        @pl.when(i == 0)
        def _():
            dkt[...] = jnp.zeros_like(dkt)
            dvt[...] = jnp.zeros_like(dvt)

        dqacc[...] = jnp.zeros_like(dqacc)
        pacc[...] = jnp.zeros_like(pacc)

        def scores(kb, masked, slot):
            ks = pl.multiple_of(jnp.maximum(kb, 0) * tk, tk)
            z = _dot(qcat[...], k_ref[pl.ds(ks, tk), :], NT)
            logsig, lom = _log_terms(z)
            if masked:
                msk = jnp.logical_and(_causal_mask(i, ks, tq, tk), kb >= 0)
                lom = jnp.where(msk, lom, 0.0)
                logsig = jnp.where(msk, logsig, -1e30)
            ls_buf[slot] = logsig
            tl_buf[slot] = _cumsum_mm(lom, ua_ref)
            da_buf[slot] = _dot(docat[...], v_ref[pl.ds(ks, tk), :], NT)

        def grads(kb, slot):
            kbc = jnp.maximum(kb, 0)
            ks = pl.multiple_of(kbc * tk, tk)
            logsig = ls_buf[slot]
            c = jnp.sum(jnp.where(lane == kb, ccat[...], 0.0), axis=1, keepdims=True)
            a = jnp.exp(logsig + tl_buf[slot] + c)
            g = a * da_buf[slot]
            before = _cumsum_mm(g, ub_ref)
            pc = pacc[...]
            dz = g - jnp.exp(logsig) * (g + before + pc)
            dzb = dz.astype(BF16)
            dqacc[...] += _dot(dzb, k_ref[pl.ds(ks, tk), :])
            dkt[kbc] += _dot(qcat_t[...], dzb)
            dvt[kbc] += _dot(docat_t[...], a.astype(BF16))
            pacc[...] = pc + before[:, tk - 1:tk] + g[:, tk - 1:tk]

        def pair(kb, masked, masked_next):
            scores(kb, masked, 0)
            scores(kb + 1, masked_next, 1)
            grads(kb, 0)
            grads(kb + 1, 1)

        reach = jnp.max(ccat[...], axis=0, keepdims=True)
        first = jnp.min(jnp.where(reach > DEAD, lane, 128).astype(F32)).astype(jnp.int32)
        first = jnp.minimum(first, i)
        start = first - (i - first + 1) % 2

        @pl.when(jnp.logical_and(start < 0, i >= 2))
        def _():
            pair(-1, True, False)

        k0 = jnp.where(start < 0, 1, start)

        def loop(t, carry_):
            pair(k0 + 2 * t, False, False)
            return carry_

        lax.fori_loop(0, jnp.maximum((i - 1 - k0) // 2, 0), loop, 0)

        @pl.when(i == 0)
        def _():
            pair(-1, True, True)

        @pl.when(i > 0)
        def _():
            pair(i - 1, False, True)
        dq_ref[...] = (jnp.where(lane < HEAD_DIM, dqacc[0:tq], dqacc[tq:2 * tq]) * SCALE).astype(BF16)

        @pl.when(i == nq - 1)
        def _():
            for kb in range(nkb):
                dk_ref[kb * tk:(kb + 1) * tk, :] = dkt[kb].T.astype(BF16)
                dv_ref[kb * tk:(kb + 1) * tk, :] = dvt[kb].T.astype(BF16)

    nbytes = (_nbytes((tq, 128), BF16) + 2 * _nbytes((S, 128), BF16) + 2 * _nbytes((2 * tk, tk), BF16)
              + 12 * _nbytes((tq, 128), F32) + 4 * _nbytes((S, 128), F32) + 14 * _nbytes((2 * tq, tk), F32))
    blk = pl.BlockSpec((tq, 128), lambda j, i: (i, j))
    full = pl.BlockSpec((S, 128), lambda j, i: (0, j))
    tri_spec = pl.BlockSpec((2 * tk, tk), lambda j, i: (0, 0))
    dq, dk, dv = pl.pallas_call(
        body, name=name, grid=(4, nq),
        in_specs=[blk,
                  pl.BlockSpec((S, 128), lambda j, i: (0, 4 + j)),
                  pl.BlockSpec((S, 128), lambda j, i: (0, 8 + j)),
                  pl.BlockSpec((tq, 256), lambda j, i: (i, j)),
                  blk, tri_spec, tri_spec],
        out_specs=[blk, full, full],
        out_shape=[jax.ShapeDtypeStruct((S, D_SB), BF16)] * 3,
        scratch_shapes=[pltpu.VMEM((2 * tq, 128), BF16), pltpu.VMEM((2 * tq, 128), BF16),
                        pltpu.VMEM((128, 2 * tq), BF16), pltpu.VMEM((128, 2 * tq), BF16),
                        pltpu.VMEM((2 * tq, 128), F32), pltpu.VMEM((2 * tq, 128), F32), pltpu.VMEM((2 * tq, 1), F32),
                        pltpu.VMEM((nkb, 128, tk), F32), pltpu.VMEM((nkb, 128, tk), F32),
                        pltpu.VMEM((2, 2 * tq, tk), F32), pltpu.VMEM((2, 2 * tq, tk), F32),
                        pltpu.VMEM((2, 2 * tq, tk), F32)],
        compiler_params=_params(("parallel", "arbitrary"), nbytes),
    )(qkv, qkv, qkv, carry, do, tri_after, tri_before)
    return dq, dk, dv


def _group_mats():
    lanes = jnp.arange(D_MODEL) // HEAD_DIM
    gs = (lanes[:, None] == jnp.arange(128)[None, :]).astype(BF16)
    return gs, gs.T


def _group_sum_bcast(x, gs, gb):
    hi, lo = _split(x)
    s = _dot(hi, gs) + _dot(lo, gs)
    return _bcast(s, gb)


def _bcast(s, gb):
    hi, lo = _split(s)
    return _dot(hi, gb) + _dot(lo, gb)


def _pool_lane_consts():
    lane = lax.broadcasted_iota(jnp.int32, (1, D_POOL), 1)
    grp = lane // (D_POOL // 4)
    win = jnp.where(grp == 0, 2, jnp.where(grp == 1, 4, jnp.where(grp == 2, 8, 16)))
    return grp, win


def _by_group(grp, s2, s4, s8, s16):
    return jnp.where(grp == 0, s2, jnp.where(grp == 1, s4, jnp.where(grp == 2, s8, s16)))


def _mixers(i, ts, prev_ref, cur_ref, cw_ref, pw_ref, ps_ref):
    cur = cur_ref[...]
    prev = jnp.where(i == 0, 0.0, prev_ref[...])
    ext = jnp.concatenate([prev, cur], axis=0)

    def back(a, k):
        return pltpu.roll(a, k, 0)

    u = ext[:, D_CONV:2 * D_CONV] * ext[:, 2 * D_CONV:3 * D_CONV]
    p = ext[:, 3 * D_CONV:]
    cv = (cw_ref[0:1, :] * back(u, 2) + cw_ref[1:2, :] * back(u, 1) + cw_ref[2:3, :] * u)[HALO:]
    s2 = p + back(p, 1)
    s4 = s2 + back(s2, 2)
    s8 = s4 + back(s4, 4)
    s16 = s8 + back(s8, 8)
    grp, win = _pool_lane_consts()
    t1 = i * ts + 1 + lax.broadcasted_iota(jnp.int32, (ts, 1), 0)
    cnt = jnp.minimum(t1, win).astype(F32)
    pooled = _by_group(grp, s2, s4, s8, s16)[HALO:] / cnt - p[HALO:]
    yp = _dot(pooled.astype(BF16), pw_ref[...])
    return dict(b=cur[:, 0:D_CONV], u=u, cv=cv, pooled=pooled, yp=yp, cnt=cnt,
                conv_out=cur[:, 0:D_CONV] * cv, pool_out=yp * ps_ref[...])


def _halo_specs(ts, S, width):
    nb = ts // HALO
    last = S // HALO - 1
    prev = pl.BlockSpec((HALO, width), lambda i: (jnp.maximum(i * nb - 1, 0), 0))
    nxt = pl.BlockSpec((HALO, width), lambda i: (jnp.minimum((i + 1) * nb, last), 0))
    return prev, nxt


def _mixer_fwd(rest, attn, cw8, pwbd, ps, gain, *, name, ts):
    S = rest.shape[0]
    ts = min(ts, S)
    gs, gb = _group_mats()

    def body(prev_ref, cur_ref, attn_ref, cw_ref, pw_ref, ps_ref, gain_ref, gs_ref, gb_ref, o_ref):
        i = pl.program_id(0)
        f = _mixers(i, ts, prev_ref, cur_ref, cw_ref, pw_ref, ps_ref)
        mix = jnp.concatenate([attn_ref[...], f["conv_out"], f["pool_out"]], axis=1)
        ss = _group_sum_bcast(mix * mix, gs_ref[...], gb_ref[...])
        rinv = lax.rsqrt(ss * (1.0 / HEAD_DIM) + RMS_EPS)
        o_ref[...] = (mix * rinv * gain_ref[...]).astype(BF16)

    prev, _ = _halo_specs(ts, S, D_REST)
    row = lambda w: pl.BlockSpec((ts, w), lambda i: (i, 0))
    const = lambda a: pl.BlockSpec(a.shape, lambda i: (0, 0))
    nbytes = 12 * _nbytes((ts + HALO, D_REST), F32)
    return pl.pallas_call(
        body, name=name, grid=(S // ts,),
        in_specs=[prev, row(D_REST), row(D_SB), const(cw8), const(pwbd), const(ps), const(gain), const(gs), const(gb)],
        out_specs=row(D_MODEL),
        out_shape=jax.ShapeDtypeStruct((S, D_MODEL), BF16),
        compiler_params=_params(("parallel",), nbytes),
    )(rest, rest, attn, cw8, pwbd, ps, gain, gs, gb)


def _mixer_bwd1(dmixn, rest, attn, cw8, pwbd, ps, gain, *, name, ts):
    S = rest.shape[0]
    ts = min(ts, S)
    gs, gb = _group_mats()

    def body(dm_ref, prev_ref, cur_ref, attn_ref, cw_ref, pw_ref, ps_ref, gain_ref, gs_ref, gb_ref,
             da_ref, aux_ref, dg_ref, dsc_ref, dcw_ref, dpw_ref):
        i = pl.program_id(0)
        f = _mixers(i, ts, prev_ref, cur_ref, cw_ref, pw_ref, ps_ref)
        mix = jnp.concatenate([attn_ref[...], f["conv_out"], f["pool_out"]], axis=1)
        gsm, gbm = gs_ref[...], gb_ref[...]
        ss = _group_sum_bcast(mix * mix, gsm, gbm)
        rinv = lax.rsqrt(ss * (1.0 / HEAD_DIM) + RMS_EPS)
        dm = dm_ref[...]
        xn = mix * rinv
        dyg = dm * gain_ref[...]
        gm = _group_sum_bcast(dyg * xn, gsm, gbm) * (1.0 / HEAD_DIM)
        dmix = rinv * (dyg - xn * gm)
        da_ref[...] = dmix[:, 0:D_SB]
        dco = dmix[:, D_SB:D_SB + D_CONV]
        dpo = dmix[:, D_SB + D_CONV:]
        dcv = dco * f["b"]
        dyp = dpo * ps_ref[...]
        dpooled = _dot(dyp.astype(BF16), pw_ref[...], NT)
        aux_ref[...] = jnp.concatenate([dco * f["cv"], dcv, dpooled / f["cnt"], dpooled], axis=1)
        u = f["u"]
        parts = [
            _sum8(dm * xn),
            _sum8(dpo * f["yp"]),
            jnp.concatenate([_sum8(dcv * pltpu.roll(u, 2, 0)[HALO:]), _sum8(dcv * pltpu.roll(u, 1, 0)[HALO:]),
                             _sum8(dcv * u[HALO:])], axis=0),
            _dot(f["pooled"].astype(BF16), dyp.astype(BF16), TN),
        ]
        outs = [dg_ref, dsc_ref, dcw_ref, dpw_ref]

        @pl.when(i == 0)
        def _():
            for o, v in zip(outs, parts):
                o[...] = v

        @pl.when(i > 0)
        def _():
            for o, v in zip(outs, parts):
                o[...] += v

    prev, _ = _halo_specs(ts, S, D_REST)
    row = lambda w: pl.BlockSpec((ts, w), lambda i: (i, 0))
    const = lambda a: pl.BlockSpec(a.shape, lambda i: (0, 0))
    acc = lambda r_, w: pl.BlockSpec((r_, w), lambda i: (0, 0))
    nbytes = 16 * _nbytes((ts + HALO, D_REST), F32)
    return pl.pallas_call(
        body, name=name, grid=(S // ts,),
        in_specs=[row(D_MODEL), prev, row(D_REST), row(D_SB), const(cw8), const(pwbd), const(ps), const(gain),
                  const(gs), const(gb)],
        out_specs=[row(D_SB), row(D_REST), acc(8, D_MODEL), acc(8, D_POOL), acc(24, D_CONV), acc(D_POOL, D_POOL)],
        out_shape=[jax.ShapeDtypeStruct((S, D_SB), F32), jax.ShapeDtypeStruct((S, D_REST), F32),
                   jax.ShapeDtypeStruct((8, D_MODEL), F32), jax.ShapeDtypeStruct((8, D_POOL), F32),
                   jax.ShapeDtypeStruct((24, D_CONV), F32), jax.ShapeDtypeStruct((D_POOL, D_POOL), F32)],
        compiler_params=_params(("arbitrary",), nbytes),
    )(dmixn, rest, rest, attn, cw8, pwbd, ps, gain, gs, gb)


def _mixer_bwd2(aux, rest, cw8, *, name, ts):
    S = rest.shape[0]
    ts = min(ts, S)
    nblk = S // ts

    def body(cur_ref, nxt_ref, rest_ref, cw_ref, o_ref):
        i = pl.program_id(0)
        cur = cur_ref[...]
        nxt = jnp.where(i == nblk - 1, 0.0, nxt_ref[...])
        ext = jnp.concatenate([cur, nxt], axis=0)
        n = ts + HALO

        def fwd(a, k):
            return pltpu.roll(a, n - k, 0)

        dcv = ext[:, D_CONV:2 * D_CONV]
        dps = ext[:, 2 * D_CONV:3 * D_CONV]
        du = (cw_ref[2:3, :] * dcv + cw_ref[1:2, :] * fwd(dcv, 1) + cw_ref[0:1, :] * fwd(dcv, 2))[0:ts]
        f2 = dps + fwd(dps, 1)
        f4 = f2 + fwd(f2, 2)
        f8 = f4 + fwd(f4, 4)
        f16 = f8 + fwd(f8, 8)
        grp, _ = _pool_lane_consts()
        dp = _by_group(grp, f2, f4, f8, f16)[0:ts] - cur[:, 3 * D_CONV:]
        rest_v = rest_ref[...]
        c_gate = rest_v[:, D_CONV:2 * D_CONV]
        h = rest_v[:, 2 * D_CONV:3 * D_CONV]
        o_ref[...] = jnp.concatenate([cur[:, 0:D_CONV], du * h, du * c_gate, dp], axis=1).astype(BF16)

    _, nxt = _halo_specs(ts, S, D_REST)
    row = pl.BlockSpec((ts, D_REST), lambda i: (i, 0))
    return pl.pallas_call(
        body, name=name, grid=(nblk,),
        in_specs=[row, nxt, row, pl.BlockSpec(cw8.shape, lambda i: (0, 0))],
        out_specs=row,
        out_shape=jax.ShapeDtypeStruct((S, D_REST), BF16),
        compiler_params=_params(("parallel",), 10 * _nbytes((ts + HALO, D_REST), F32)),
    )(aux, aux, rest, cw8)


def _block_diag(pw):
    wide = jnp.tile(pw.reshape(256, 64), (1, 4))
    grp = jnp.arange(256) // 64
    return jnp.where(grp[:, None] == grp[None, :], wide, 0.0)


def _rows8(v, rows=8):
    return jnp.pad(v, ((0, rows - v.shape[0]), (0, 0)))


TILES = dict(tm=512, ts=256, tq=256, tk=256)


def _local_step(x, tgt, w, t=None, grad_pack=None):
    t = dict(TILES, **(t or {}))
    gp = None if grad_pack is None else grad_pack[0]
    tm, ts, tq, tk = t["tm"], t["ts"], t["tq"], t["tk"]
    big = dict(tm=1024, tn=1024)
    saved = []
    xl, xl_affine, xl16 = x, None, x.astype(BF16)
    for l in range(DEPTH):
        n = f"l{l}_"
        wq, wr = w["w_in"][l][:, :D_QKV], w["w_in"][l][:, D_QKV:]
        qkv = _matmul(xl16, wq, name=n + "proj_qkv", tm=1024, tn=D_QKV, tk=1024, out_dtype=BF16)
        rest = _matmul(xl16, wr, name=n + "proj_rest", tk=1024, **big)
        attn, carry = _attn_fwd(qkv, name=n + "attn_fwd", tq=tq, tk=tk)
        cw8 = _rows8(w["conv_w"][l])
        pwbd = _block_diag(w["pool_w"][l]).astype(BF16)
        ps = w["pool_scale"][l].reshape(1, D_POOL)
        gain = w["mix_norm_g"][l].reshape(1, D_MODEL)
        mixn = _mixer_fwd(rest, attn, cw8, pwbd, ps, gain, name=n + "mixer_fwd", ts=ts)
        x1_16, xh1, rs1 = _matmul_ln(mixn, w["w_o"][l], xl, w["ln1_g"][l], w["ln1_b"][l], name=n + "wo_ln",
                                     tm=2 * tm, tk=1024, res_affine=xl_affine)
        hpre, hid = _matmul(x1_16, w["w_up"][l], name=n + "ffn_up", tk=1024, relu2_out=True, out_dtype=BF16, **big)
        x2_16, xh2, rs2 = _matmul_ln(hid, w["w_down"][l], xh1, w["ln2_g"][l], w["ln2_b"][l], name=n + "ffn_down_ln",
                                     tm=tm, tk=D_FF, res_affine=(w["ln1_g"][l], w["ln1_b"][l]))
        saved.append(dict(xin16=xl16, wq=wq, wr=wr, qkv=qkv, rest=rest, attn=attn, carry=carry, cw8=cw8, pwbd=pwbd,
                          ps=ps, gain=gain, mixn=mixn, x1_16=x1_16, xh1=xh1, rs1=rs1, hpre=hpre, hid=hid, xh2=xh2,
                          rs2=rs2))
        xl, xl_affine, xl16 = xh2, (w["ln2_g"][l], w["ln2_b"][l]), x2_16

    top = saved[-1]
    ln2_back = _loss_ln_bwd(tgt, top["xh2"], top["rs2"], w["ln2_g"][DEPTH - 1], w["ln2_b"][DEPTH - 1],
                            name="loss_ln2_bwd", tm=tm)
    lsum = ln2_back[4]
    grads = {k: [None] * DEPTH for k in
             ("w_in", "conv_w", "pool_w", "pool_scale", "mix_norm_g", "w_o", "ln1_g", "ln1_b", "w_up", "w_down",
              "ln2_g", "ln2_b")}
    dw = dict(tk=2048, ta=True, out_dtype=BF16, **big)
    for l in reversed(range(DEPTH)):
        n = f"l{l}_"
        s = saved[l]
        dr2, dr2_16, dg2, db2 = ln2_back[:4]
        dhpre = _matmul(dr2_16, w["w_down"][l], name=n + "ffn_down_dx", tk=1024, tb=True, out_dtype=BF16,
                        epi="drelu2", e=s["hpre"], **big)
        if gp is None:
            grads["w_down"][l] = _matmul(s["hid"], dr2_16, name=n + "ffn_down_dw", **dw)
        else:
            gp = _matmul(s["hid"], dr2_16, name=n + "ffn_down_dw", out_chips=_ChipWeight(gp, grad_pack[2] + l, "rows"),
                         **dw)
        dr1, dr1_16, dg1, db1 = _matmul(dhpre, w["w_up"][l], name=n + "ffn_up_dx_ln1_bwd", tm=512, tn=1024, tk=D_FF,
                                        tb=True, epi="add", e=dr2, e_scale=ALPHA,
                                        ln_bwd=(s["xh1"], s["rs1"], w["ln1_g"][l]))
        if gp is None:
            grads["w_up"][l] = _matmul(s["x1_16"], dhpre, name=n + "ffn_up_dw", **dw)
        else:
            gp = _matmul(s["x1_16"], dhpre, name=n + "ffn_up_dw", out_chips=_ChipWeight(gp, grad_pack[1] + l, "cols"),
                         **dw)
        dmixn = _matmul(dr1_16, w["w_o"][l], name=n + "wo_dx", tk=1024, tb=True, **big)
        grads["w_o"][l] = _matmul(s["mixn"], dr1_16, name=n + "wo_dw", **dw)
        d_attn, aux, dgain, dsc, dcw, dpw = _mixer_bwd1(dmixn, s["rest"], s["attn"], s["cw8"], s["pwbd"], s["ps"],
                                                        s["gain"], name=n + "mixer_bwd1", ts=ts)
        drest = _mixer_bwd2(aux, s["rest"], s["cw8"], name=n + "mixer_bwd2", ts=ts)
        dqkv = jnp.concatenate(_attn_bwd(s["qkv"], s["carry"], d_attn, name=n + "attn_bwd", tq=tq, tk=tk), axis=1)
        below = None if l == 0 else (saved[l - 1]["xh2"], saved[l - 1]["rs2"], w["ln2_g"][l - 1])
        ln2_back = _matmul(dqkv, s["wq"], name=n + "proj_dx", tm=512, tn=1024, tk=D_QKV, tb=True, also=(drest, s["wr"]),
                           epi="add", e=dr1, e_scale=ALPHA, ln_bwd=below)
        dy = ln2_back
        dwq = _matmul(s["xin16"], dqkv, name=n + "proj_qkv_dw", tm=1024, tn=D_QKV, tk=1024, ta=True, out_dtype=BF16)
        dwr = _matmul(s["xin16"], drest, name=n + "proj_rest_dw", **dw)
        grads["w_in"][l] = jnp.concatenate([dwq, dwr], axis=1)
        grads["ln2_g"][l] = dg2.sum(0)
        grads["ln2_b"][l] = db2.sum(0)
        grads["ln1_g"][l] = dg1.sum(0)
        grads["ln1_b"][l] = db1.sum(0)
        grads["mix_norm_g"][l] = dgain.sum(0)
        grads["pool_scale"][l] = dsc.sum(0)
        grads["conv_w"][l] = dcw.reshape(3, 8, D_CONV).sum(1)
        grads["pool_w"][l] = jnp.stack([dpw[64 * g:64 * g + 64, 64 * g:64 * g + 64] for g in range(4)])
    grads = {k: jnp.stack(v) for k, v in grads.items() if v[0] is not None}
    if gp is not None:
        grads["pack"] = gp
    return lsum, dy, grads


ANY = pl.BlockSpec(memory_space=pl.ANY)


def _place():
    x, y, c = lax.axis_index("x"), lax.axis_index("y"), lax.axis_index("c")
    chips = [(1 - x, y), (x, 1 - y), (1 - x, 1 - y)]
    return x, y, c, chips


def _remote(src, dst, send_sems, recv_sems, k, to):
    return pltpu.make_async_remote_copy(src_ref=src, dst_ref=dst, send_sem=send_sems.at[k], recv_sem=recv_sems.at[k],
                                        device_id=to, device_id_type=MESH)


class _Copy:
    def __init__(self, src, dst, send_sems, recv_sems, k, to):
        self.args = (send_sems, recv_sems, k, to)
        self.copy = _remote(src, dst, *self.args)

    def like(self, src, dst):
        return _remote(src, dst, *self.args)

    def start(self):
        self.copy.start()

    def wait(self):
        self.copy.wait()

    def wait_send(self):
        self.copy.wait_send()

    def wait_recv(self):
        self.copy.wait_recv()


def _allgather_chips(pack, *, name):
    R, C = pack.shape
    H = R // 2
    Q = H // 2
    assert R % 64 == 0
    A, B = 0, 1

    def body(p_ref, o_ref, send_sems, recv_sems):
        x, y, c, _ = _place()
        my, kx, ky, kd = 2 * x + y, 2 * (1 - x) + y, 2 * x + (1 - y), 2 * (1 - x) + (1 - y)
        xn, yn, sib = (1 - x, y, c), (x, 1 - y, c), (x, y, 1 - c)

        def own(ab):
            return p_ref.at[pl.ds(c * H + ab * Q, Q), :]

        def quarter(k, hc, ab):
            return o_ref.at[k, pl.ds(hc * H + ab * Q, Q), :]

        def send(src, k, ab, sem, to):
            cp = _Copy(src, quarter(k, c, ab), send_sems, recv_sems, sem, to)
            cp.start()
            return cp

        def landed(sent, k, hc, ab):
            sent.like(quarter(k, hc, ab), quarter(k, hc, ab)).wait_recv()

        a_x = send(own(A), my, A, 0, xn)
        b_y = send(own(B), my, B, 3, yn)
        b_x = send(own(B), my, B, 1, xn)
        a_y = send(own(A), my, A, 4, yn)
        landed(b_y, ky, c, B)
        fb = send(quarter(ky, c, B), ky, B, 2, xn)
        landed(a_x, kx, c, A)
        fa = send(quarter(kx, c, A), kx, A, 5, yn)
        arrivals = [(kx, A, None), (ky, B, None), (kx, B, b_x), (ky, A, a_y), (kd, B, fb), (kd, A, fa)]
        passed = []
        for j, (k, ab, sent) in enumerate(arrivals):
            if sent is not None:
                landed(sent, k, c, ab)
            passed.append(_Copy(quarter(k, c, ab), quarter(k, c, ab), send_sems, recv_sems, 6 + j, sib))
            passed[-1].start()
        for j, (k, ab, _) in enumerate(arrivals):
            landed(passed[j], k, 1 - c, ab)
        for cp in [a_x, b_y, b_x, a_y, fb, fa] + passed:
            cp.wait_send()

    return pl.pallas_call(
        body, name=name, in_specs=[ANY], out_specs=ANY,
        out_shape=jax.ShapeDtypeStruct((N_CHIPS, R, C), pack.dtype),
        scratch_shapes=[pltpu.SemaphoreType.DMA((12,)), pltpu.SemaphoreType.DMA((12,))],
    )(pack)


def _swap_halves(gp, *, name):
    K, R, C = gp.shape
    H = R // 2

    def body(g_ref, theirs_ref, send_sems, recv_sems):
        x, y, c, _ = _place()
        cp = _Copy(g_ref.at[:, pl.ds((1 - c) * H, H), :], theirs_ref, send_sems, recv_sems, 0, (x, y, 1 - c))
        cp.start()
        cp.wait()

    return pl.pallas_call(
        body, name=name, in_specs=[ANY], out_specs=ANY, out_shape=jax.ShapeDtypeStruct((K, H, C), gp.dtype),
        scratch_shapes=[pltpu.SemaphoreType.DMA((1,)), pltpu.SemaphoreType.DMA((1,))],
    )(gp)


def _scatter_chips(part, *, name):
    K, H, C = part.shape

    def body(p_ref, o_ref, send_sems, recv_sems):
        x, y, c, chips = _place()
        copies = [_Copy(p_ref.at[2 * cx + cy], o_ref.at[j], send_sems, recv_sems, j, (cx, cy, c))
                  for j, (cx, cy) in enumerate(chips)]
        for cp in copies:
            cp.start()
        for cp in copies:
            cp.wait()

    return pl.pallas_call(
        body, name=name, in_specs=[ANY], out_specs=ANY,
        out_shape=jax.ShapeDtypeStruct((3, H, C), part.dtype),
        scratch_shapes=[pltpu.SemaphoreType.DMA((3,)), pltpu.SemaphoreType.DMA((3,))],
    )(part)


def _join_halves(both, *, name):
    H, C = both.shape[0] // 2, both.shape[1]

    def body(in_ref, o_ref, send_sems, recv_sems):
        x, y, c, _ = _place()
        mine = pl.ds(c * H, H)
        theirs = pl.ds((1 - c) * H, H)
        cp = _Copy(in_ref.at[mine, :], o_ref.at[mine, :], send_sems, recv_sems, 0, (x, y, 1 - c))
        cp.start()
        cp.wait_send()
        cp.like(in_ref.at[theirs, :], o_ref.at[theirs, :]).wait_recv()

    return pl.pallas_call(
        body, name=name, in_specs=[ANY], out_specs=ANY, input_output_aliases={0: 0},
        out_shape=jax.ShapeDtypeStruct(both.shape, both.dtype),
        scratch_shapes=[pltpu.SemaphoreType.DMA((1,)), pltpu.SemaphoreType.DMA((1,))],
    )(both)


def _allreduce_small(v, *, name):
    R, C = v.shape
    n_dev = 8

    def body(v_ref, o_ref, gat, send_sems, recv_sems):
        x, y, c, chips = _place()
        sib = (x, y, 1 - c)

        def rows(px, py, pc):
            return gat.at[4 * px + 2 * py + pc]

        gat[4 * x + 2 * y + c] = v_ref[...]
        first = [_remote(v_ref, rows(x, y, c), send_sems, recv_sems, 0, sib)]
        first += [_remote(v_ref, rows(x, y, c), send_sems, recv_sems, 1 + j, (cx, cy, c))
                  for j, (cx, cy) in enumerate(chips)]
        for cp in first:
            cp.start()
        passed = []
        for j, (cx, cy) in enumerate(chips):
            _remote(v_ref, rows(cx, cy, c), send_sems, recv_sems, 1 + j, sib).wait_recv()
            fwd = _remote(rows(cx, cy, c), rows(cx, cy, c), send_sems, recv_sems, 4 + j, sib)
            fwd.start()
            passed.append(fwd)
        _remote(v_ref, rows(x, y, 1 - c), send_sems, recv_sems, 0, sib).wait_recv()
        for j, (cx, cy) in enumerate(chips):
            _remote(v_ref, rows(cx, cy, 1 - c), send_sems, recv_sems, 4 + j, sib).wait_recv()
        for cp in first + passed:
            cp.wait_send()
        acc = gat[0]
        for d in range(1, n_dev):
            acc = acc + gat[d]
        o_ref[...] = acc

    vm = pl.BlockSpec(memory_space=pltpu.VMEM)
    return pl.pallas_call(
        body, name=name, in_specs=[vm], out_specs=vm,
        out_shape=jax.ShapeDtypeStruct((R, C), F32),
        scratch_shapes=[pltpu.VMEM((n_dev, R, C), F32), pltpu.SemaphoreType.DMA((7,)), pltpu.SemaphoreType.DMA((7,))],
    )(v)


def _add_pairs(gp, theirs, place, *, name, tr):
    K, H, C = theirs.shape
    tr = min(tr, H)
    assert H % tr == 0
    nb = H // tr

    def body(place_ref, a_ref, b_ref, o_ref):
        o_ref[...] = (a_ref[...].astype(F32) + b_ref[...].astype(F32)).astype(BF16)

    blk = pl.BlockSpec((1, tr, C), lambda k, i, p: (k, i, 0))
    mine = pl.BlockSpec((1, tr, C), lambda k, i, p: (k, i + p[1] * nb, 0))
    return pl.pallas_call(
        body, name=name,
        grid_spec=pltpu.PrefetchScalarGridSpec(num_scalar_prefetch=1, grid=(K, nb), in_specs=[mine, blk],
                                               out_specs=blk),
        out_shape=jax.ShapeDtypeStruct((K, H, C), BF16),
        compiler_params=_params(("parallel", "parallel"), 3 * _nbytes((tr, C), BF16)),
    )(place, gp, theirs)


def _add_final(gp, theirs, others, place, *, name, tr):
    K, H, C = theirs.shape
    tr = min(tr, H)
    assert H % tr == 0
    nb = H // tr

    def body(place_ref, a_ref, b_ref, o_ref_in, out_ref):
        acc = a_ref[0].astype(F32) + b_ref[0].astype(F32)
        for j in range(3):
            acc = acc + o_ref_in[j].astype(F32)
        out_ref[...] = acc

    return pl.pallas_call(
        body, name=name,
        grid_spec=pltpu.PrefetchScalarGridSpec(
            num_scalar_prefetch=1, grid=(nb,),
            in_specs=[pl.BlockSpec((1, tr, C), lambda i, p: (p[0], i + p[1] * nb, 0)),
                      pl.BlockSpec((1, tr, C), lambda i, p: (p[0], i, 0)),
                      pl.BlockSpec((3, tr, C), lambda i, p: (0, i, 0))],
            out_specs=pl.BlockSpec((tr, C), lambda i, p: (i + p[1] * nb, 0))),
        out_shape=jax.ShapeDtypeStruct((2 * H, C), F32),
        compiler_params=_params(("parallel",), 6 * _nbytes((tr, C), F32)),
    )(place, gp, theirs, others)


def _adamw(w, g, m, v, *, name, tr, row0=0):
    R, C = w.shape
    tr = min(tr, R)
    assert R % tr == 0 and row0 % tr == 0
    off = row0 // tr

    def body(w_ref, g_ref, m_ref, v_ref, go_ref, d_ref, mo_ref, vo_ref):
        gv = g_ref[...]
        m2 = ADAM_B1 * m_ref[...] + (1.0 - ADAM_B1) * gv
        v2 = ADAM_B2 * v_ref[...] + (1.0 - ADAM_B2) * jnp.square(gv)
        m_hat = m2 / (1.0 - ADAM_B1 ** ADAM_STEP)
        v_hat = v2 / (1.0 - ADAM_B2 ** ADAM_STEP)
        d_ref[...] = -ADAM_LR * (m_hat / (jnp.sqrt(v_hat) + ADAM_EPS) + ADAM_WD * w_ref[...])
        go_ref[...] = gv
        mo_ref[...] = m2
        vo_ref[...] = v2

    blk = pl.BlockSpec((tr, C), lambda i: (i, 0))
    shape = jax.ShapeDtypeStruct((R, C), F32)
    return pl.pallas_call(
        body, name=name, grid=(R // tr,),
        in_specs=[blk, pl.BlockSpec((tr, C), lambda i: (i + off, 0)), blk, blk], out_specs=[blk] * 4,
        out_shape=[shape] * 4,
        compiler_params=_params(("parallel",), 8 * _nbytes((tr, C), F32)),
    )(w, g, m, v)


BIG = ("w_up", "w_down", "w_in", "w_o")
IN_PLACE = ("w_up", "w_down")
BIG_AXIS = dict(w_in=2, w_o=1, w_up=2, w_down=1)
SMALL = ("pool_w", "pool_scale", "mix_norm_g", "ln1_g", "ln1_b", "ln2_g", "ln2_b")
CONV_ROWS = 64
SMALL_ROWS = 48


def _big_rows(shards):
    sizes = [shards[n].size // D_MODEL for n in BIG]
    starts = [sum(sizes[:i]) for i in range(len(sizes))]
    return sizes, starts


def _to_chips(a, axis):
    shape = list(a.shape)
    shape[axis:axis + 1] = [N_CHIPS, shape[axis] // N_CHIPS]
    return jnp.moveaxis(a.reshape(shape), axis, 0)


def _from_chips(a, axis):
    a = jnp.moveaxis(a, 0, axis)
    shape = list(a.shape)
    shape[axis:axis + 2] = [shape[axis] * shape[axis + 1]]
    return a.reshape(shape)


def _pad_rows(flat, rows):
    return jnp.pad(flat, (0, rows * D_MODEL - flat.shape[0])).reshape(rows, D_MODEL)


def kernel(x, w_in, conv_w, pool_w, pool_scale, mix_norm_g, w_o, ln1_g, ln1_b, w_up, w_down, ln2_g, ln2_b, loss_target, m_w_in, m_conv_w, m_pool_w, m_pool_scale, m_mix_norm_g, m_w_o, m_ln1_g, m_ln1_b, m_w_up, m_w_down, m_ln2_g, m_ln2_b, v_w_in, v_conv_w, v_pool_w, v_pool_scale, v_mix_norm_g, v_w_o, v_ln1_g, v_ln1_b, v_w_up, v_w_down, v_ln2_g, v_ln2_b):
    wts = dict(w_in=w_in, conv_w=conv_w, pool_w=pool_w, pool_scale=pool_scale, mix_norm_g=mix_norm_g, w_o=w_o,
               ln1_g=ln1_g, ln1_b=ln1_b, w_up=w_up, w_down=w_down, ln2_g=ln2_g, ln2_b=ln2_b)
    mom = dict(w_in=m_w_in, conv_w=m_conv_w, pool_w=m_pool_w, pool_scale=m_pool_scale, mix_norm_g=m_mix_norm_g,
               w_o=m_w_o, ln1_g=m_ln1_g, ln1_b=m_ln1_b, w_up=m_w_up, w_down=m_w_down, ln2_g=m_ln2_g, ln2_b=m_ln2_b)
    var = dict(w_in=v_w_in, conv_w=v_conv_w, pool_w=v_pool_w, pool_scale=v_pool_scale, mix_norm_g=v_mix_norm_g,
               w_o=v_w_o, ln1_g=v_ln1_g, ln1_b=v_ln1_b, w_up=v_w_up, w_down=v_w_down, ln2_g=v_ln2_g, ln2_b=v_ln2_b)
    chip = 2 * lax.axis_index("x") + lax.axis_index("y")
    sizes, starts = _big_rows(wts)
    big_rows = sum(sizes)

    conv_bits = lax.bitcast_convert_type(conv_w.reshape(-1), BF16).reshape(-1)
    pack = jnp.concatenate([wts[n].reshape(-1, D_MODEL).astype(BF16) for n in BIG]
                           + [_pad_rows(conv_bits, CONV_ROWS)], axis=0)
    gathered = _allgather_chips(pack, name="gather_weights")
    gathered = lax.dynamic_update_index_in_dim(gathered, pack, chip, 0)
    full = {}
    first_block = {}
    for n, size, start in zip(BIG, sizes, starts):
        if n in IN_PLACE:
            assert start % CHIP_BLOCK == 0 and size == DEPTH * CHIP_BLOCK
            first_block[n] = start // CHIP_BLOCK
            along = "cols" if BIG_AXIS[n] == 2 else "rows"
            full[n] = [_ChipWeight(gathered, first_block[n] + l, along) for l in range(DEPTH)]
        else:
            full[n] = _from_chips(gathered[:, start:start + size].reshape((N_CHIPS,) + wts[n].shape), BIG_AXIS[n])
    conv_parts = [lax.bitcast_convert_type(gathered[k, big_rows:].reshape(-1)[:2 * conv_w.size].reshape(-1, 2), F32)
                  .reshape(conv_w.shape) for k in range(N_CHIPS)]
    full["conv_w"] = jnp.concatenate(conv_parts, axis=2)
    for n in SMALL:
        full[n] = wts[n]

    gpack = jnp.zeros((N_CHIPS, big_rows, D_MODEL), BF16)
    lsum, grad_x, grads = _local_step(x[0], loss_target[0], full,
                                      grad_pack=(gpack, first_block["w_up"], first_block["w_down"]))

    others = [n for n in BIG if n not in IN_PLACE]
    rest = jnp.concatenate([_to_chips(grads[n], BIG_AXIS[n]).reshape(N_CHIPS, -1, D_MODEL) for n in others], axis=1)
    gpack = lax.dynamic_update_slice_in_dim(grads["pack"], rest.astype(BF16), starts[len(IN_PLACE)], axis=1)
    place = jnp.stack([chip, lax.axis_index("c")]).astype(jnp.int32)
    theirs = _swap_halves(gpack, name="grad_swap_cores")
    add_rows = big_rows // 8
    chip_sum = _add_pairs(gpack, theirs, place, name="grad_add_cores", tr=add_rows)
    from_chips = _scatter_chips(chip_sum, name="grad_scatter_chips")
    half_sum = _add_final(gpack, theirs, from_chips, place, name="grad_add_chips", tr=add_rows)
    gsum = _join_halves(half_sum, name="grad_join_cores")

    small_flat = jnp.concatenate([grads[n].reshape(-1) for n in SMALL] + [grads["conv_w"].reshape(-1),
                                                                          lsum.sum().reshape(1)])
    small_sum = _allreduce_small(_pad_rows(small_flat, SMALL_ROWS), name="allreduce_small").reshape(-1)
    gsmall = {}
    pos = 0
    for n in SMALL:
        gsmall[n] = small_sum[pos:pos + wts[n].size].reshape(wts[n].shape)
        pos += wts[n].size
    conv_full = small_sum[pos:pos + 4 * conv_w.size].reshape(DEPTH, 3, D_CONV)
    pos += 4 * conv_w.size
    loss = small_sum[pos]
    gsmall["conv_w"] = lax.dynamic_slice_in_dim(conv_full, chip * conv_w.shape[2], conv_w.shape[2], axis=2)

    out_g, out_d, out_m, out_v = {}, {}, {}, {}
    for n, size, start in zip(BIG, sizes, starts):
        shp = wts[n].shape
        g, row0 = gsum, start
        if shp[-1] != D_MODEL:
            g, row0 = gsum[start:start + size].reshape(-1, shp[-1]), 0
        res = _adamw(wts[n].reshape(-1, shp[-1]), g, mom[n].reshape(-1, shp[-1]), var[n].reshape(-1, shp[-1]),
                     name="adamw_" + n, tr=256, row0=row0)
        out_g[n], out_d[n], out_m[n], out_v[n] = [r.reshape(shp) for r in res]
    small_names = SMALL + ("conv_w",)
    packs = [_pad_rows(jnp.concatenate([d[n].reshape(-1) for n in small_names]), SMALL_ROWS)
             for d in (wts, gsmall, mom, var)]
    res = _adamw(*packs, name="adamw_small", tr=SMALL_ROWS)
    pos = 0
    for n in small_names:
        shp = wts[n].shape
        out_g[n], out_d[n], out_m[n], out_v[n] = [r.reshape(-1)[pos:pos + wts[n].size].reshape(shp) for r in res]
        pos += wts[n].size

    order = ("w_in", "conv_w", "pool_w", "pool_scale", "mix_norm_g", "w_o", "ln1_g", "ln1_b", "w_up", "w_down",
             "ln2_g", "ln2_b")
    return (loss, grad_x[None], *[out_g[n] for n in order], *[out_d[n] for n in order],
            *[out_m[n] for n in order], *[out_v[n] for n in order])
```
